```python
import jax, jax.numpy as jnp
from jax import lax
import numpy as np

D_MODEL = 1024
BATCH = 8
SEQ = 8192
DEPTH = 1

ATTN_WIDTH = D_MODEL // 2
HEAD_DIM = 64
N_HEADS = ATTN_WIDTH // HEAD_DIM
CONV_WIDTH_CH = D_MODEL - ATTN_WIDTH
CONV_GROUPS = CONV_WIDTH_CH // HEAD_DIM
CONV_KERNEL = 31
D_FF = 2816
Q_BLOCK = 128
N_SUBLAYERS = 3
MIX_IN = 3 * ATTN_WIDTH + 2 * CONV_WIDTH_CH
RMS_EPS = 1e-6
LN_EPS = 1e-5

kernel_name = "hybrid_stickbreak_conformer_macaron_block"


def rms_norm(x, g, eps=RMS_EPS):
    xf = x.astype(jnp.float32)
    y = xf * lax.rsqrt(jnp.mean(xf * xf, axis=-1, keepdims=True) + eps)
    return (y * g.astype(jnp.float32)).astype(x.dtype)


def layer_norm(x, g, b, eps=LN_EPS):
    xf = x.astype(jnp.float32)
    mu = jnp.mean(xf, axis=-1, keepdims=True)
    var = jnp.mean(jnp.square(xf - mu), axis=-1, keepdims=True)
    y = (xf - mu) * lax.rsqrt(var + eps)
    return (y * g.astype(jnp.float32) + b.astype(jnp.float32)).astype(x.dtype)


def modulate(h, shift, scale):
    return h * (1.0 + scale[:, None, :]) + shift[:, None, :]


def swiglu_ffn(h, w_in, w_out):
    gate, up = jnp.split(h @ w_in, 2, axis=-1)
    return (jax.nn.silu(gate) * up) @ w_out


def stick_breaking_attention(q, k, v):
    seq = q.shape[2]
    scale = HEAD_DIM ** -0.5
    qf = q.astype(jnp.float32) * scale
    kf = k.astype(jnp.float32)
    vf = v.astype(jnp.float32)
    outs = []
    for start in range(0, seq, Q_BLOCK):
        end = start + Q_BLOCK
        q_blk = qf[:, :, start:end]
        k_ctx = kf[:, :, :end]
        v_ctx = vf[:, :, :end]
        z = jnp.einsum('bhqd,bhkd->bhqk', q_blk, k_ctx)
        q_pos = jnp.arange(start, end)[:, None]
        k_pos = jnp.arange(end)[None, :]
        strict = k_pos < q_pos
        log_one_minus = jnp.where(strict, jax.nn.log_sigmoid(-z), 0.0)
        after = lax.cumsum(log_one_minus, axis=3, reverse=True) - log_one_minus
        log_w = jax.nn.log_sigmoid(z) + after
        w = jnp.where(strict, jnp.exp(log_w), 0.0)
        outs.append(jnp.einsum('bhqk,bhkd->bhqd', w, v_ctx))
    return jnp.concatenate(outs, axis=2).astype(q.dtype)


def causal_depthwise_conv(u, w, b):
    y = lax.conv_general_dilated(
        u, w[:, None, :].astype(u.dtype), window_strides=(1,),
        padding=[(CONV_KERNEL - 1, 0)],
        dimension_numbers=('NWC', 'WIO', 'NWC'),
        feature_group_count=u.shape[-1])
    return y + b


def hybrid_mixer(h, w_in_mix, g_attn_out, conv_w, conv_b, conv_ln_g, conv_ln_b, w_out_mix):
    bsz, seq, _ = h.shape
    proj = h @ w_in_mix
    q, k, v, cv, cg = jnp.split(
        proj, [ATTN_WIDTH, 2 * ATTN_WIDTH, 3 * ATTN_WIDTH, 3 * ATTN_WIDTH + CONV_WIDTH_CH], axis=-1)

    def heads(t):
        return t.reshape(bsz, seq, N_HEADS, HEAD_DIM).transpose(0, 2, 1, 3)

    a = stick_breaking_attention(heads(q), heads(k), heads(v))
    a = rms_norm(a, g_attn_out[:, None, :])
    a = a.transpose(0, 2, 1, 3).reshape(bsz, seq, ATTN_WIDTH)

    u = cv * jax.nn.sigmoid(cg)
    u = causal_depthwise_conv(u, conv_w, conv_b)
    u = jax.nn.silu(layer_norm(u, conv_ln_g, conv_ln_b))

    return jnp.concatenate([a, u], axis=-1) @ w_out_mix


def sandwich_sublayer(x, g_pre, g_post, shift, scale, gate, res_w, fn):
    h = modulate(rms_norm(x, g_pre), shift, scale)
    y = rms_norm(fn(h), g_post)
    return x + res_w * (1.0 + gate[:, None, :]) * y


def _fwd_setup_inputs(seed: int = 0) -> dict:
    key = jax.random.key(seed)
    ks = jax.random.split(key, 24)
    f32 = jnp.float32

    def nrm(k, shape, s):
        return jax.random.normal(k, shape, f32) * s

    def gain(k, n):
        return 1.0 + 0.02 * jax.random.normal(k, (n,), f32)

    return {
        "x": jax.random.normal(ks[0], (BATCH, SEQ, D_MODEL), f32),
        "c": jax.random.normal(ks[1], (BATCH, D_MODEL), f32),
        "w_ada": nrm(ks[2], (D_MODEL, 3 * N_SUBLAYERS * D_MODEL), 0.1 * D_MODEL ** -0.5),
        "b_ada": nrm(ks[3], (3 * N_SUBLAYERS * D_MODEL,), 0.02),
        "g_pre_ff1": gain(ks[4], D_MODEL),
        "g_post_ff1": gain(ks[5], D_MODEL),
        "ff1_w_in": nrm(ks[6], (D_MODEL, 2 * D_FF), D_MODEL ** -0.5),
        "ff1_w_out": nrm(ks[7], (D_FF, D_MODEL), D_FF ** -0.5),
        "g_pre_mix": gain(ks[8], D_MODEL),
        "g_post_mix": gain(ks[9], D_MODEL),
        "w_in_mix": nrm(ks[10], (D_MODEL, MIX_IN), D_MODEL ** -0.5),
        "g_attn_out": 1.0 + 0.02 * jax.random.normal(ks[11], (N_HEADS, HEAD_DIM), f32),
        "conv_w": nrm(ks[12], (CONV_KERNEL, CONV_WIDTH_CH), CONV_KERNEL ** -0.5),
        "conv_b": nrm(ks[13], (CONV_WIDTH_CH,), 0.02),
        "conv_ln_g": gain(ks[14], CONV_WIDTH_CH),
        "conv_ln_b": nrm(ks[15], (CONV_WIDTH_CH,), 0.02),
        "w_out_mix": nrm(ks[16], (D_MODEL, D_MODEL), D_MODEL ** -0.5),
        "g_pre_ff2": gain(ks[17], D_MODEL),
        "g_post_ff2": gain(ks[18], D_MODEL),
        "ff2_w_in": nrm(ks[19], (D_MODEL, 2 * D_FF), D_MODEL ** -0.5),
        "ff2_w_out": nrm(ks[20], (D_FF, D_MODEL), D_FF ** -0.5),
    }


def _fwd_reference(x, c, w_ada, b_ada, g_pre_ff1, g_post_ff1, ff1_w_in, ff1_w_out,
              g_pre_mix, g_post_mix, w_in_mix, g_attn_out, conv_w, conv_b,
              conv_ln_g, conv_ln_b, w_out_mix, g_pre_ff2, g_post_ff2,
              ff2_w_in, ff2_w_out):
    mod = (jax.nn.silu(c) @ w_ada + b_ada).reshape(c.shape[0], N_SUBLAYERS, 3, D_MODEL)
    h = x
    for _layer in range(DEPTH):
        h = sandwich_sublayer(
            h, g_pre_ff1, g_post_ff1, mod[:, 0, 0], mod[:, 0, 1], mod[:, 0, 2], 0.5,
            lambda t: swiglu_ffn(t, ff1_w_in, ff1_w_out))
        h = sandwich_sublayer(
            h, g_pre_mix, g_post_mix, mod[:, 1, 0], mod[:, 1, 1], mod[:, 1, 2], 1.0,
            lambda t: hybrid_mixer(t, w_in_mix, g_attn_out, conv_w, conv_b,
                                   conv_ln_g, conv_ln_b, w_out_mix))
        h = sandwich_sublayer(
            h, g_pre_ff2, g_post_ff2, mod[:, 2, 0], mod[:, 2, 1], mod[:, 2, 2], 0.5,
            lambda t: swiglu_ffn(t, ff2_w_in, ff2_w_out))
    return h


import jax as _jax
import jax.numpy as _jnp

TWIN_FORMAT = 'train_step'
FWD_PARAMS = ['x', 'c', 'w_ada', 'b_ada', 'g_pre_ff1', 'g_post_ff1', 'ff1_w_in', 'ff1_w_out', 'g_pre_mix', 'g_post_mix', 'w_in_mix', 'g_attn_out', 'conv_w', 'conv_b', 'conv_ln_g', 'conv_ln_b', 'w_out_mix', 'g_pre_ff2', 'g_post_ff2', 'ff2_w_in', 'ff2_w_out']
TWIN_WEIGHTS = ['w_ada', 'b_ada', 'g_pre_ff1', 'g_post_ff1', 'ff1_w_in', 'ff1_w_out', 'g_pre_mix', 'g_post_mix', 'w_in_mix', 'g_attn_out', 'conv_w', 'conv_b', 'conv_ln_g', 'conv_ln_b', 'w_out_mix', 'g_pre_ff2', 'g_post_ff2', 'ff2_w_in', 'ff2_w_out']
TWIN_DIFF_INPUT = 'x'
TWIN_INPUTS = ['x', 'c', 'w_ada', 'b_ada', 'g_pre_ff1', 'g_post_ff1', 'ff1_w_in', 'ff1_w_out', 'g_pre_mix', 'g_post_mix', 'w_in_mix', 'g_attn_out', 'conv_w', 'conv_b', 'conv_ln_g', 'conv_ln_b', 'w_out_mix', 'g_pre_ff2', 'g_post_ff2', 'ff2_w_in', 'ff2_w_out', 'loss_target', 'm_w_ada', 'm_b_ada', 'm_g_pre_ff1', 'm_g_post_ff1', 'm_ff1_w_in', 'm_ff1_w_out', 'm_g_pre_mix', 'm_g_post_mix', 'm_w_in_mix', 'm_g_attn_out', 'm_conv_w', 'm_conv_b', 'm_conv_ln_g', 'm_conv_ln_b', 'm_w_out_mix', 'm_g_pre_ff2', 'm_g_post_ff2', 'm_ff2_w_in', 'm_ff2_w_out', 'v_w_ada', 'v_b_ada', 'v_g_pre_ff1', 'v_g_post_ff1', 'v_ff1_w_in', 'v_ff1_w_out', 'v_g_pre_mix', 'v_g_post_mix', 'v_w_in_mix', 'v_g_attn_out', 'v_conv_w', 'v_conv_b', 'v_conv_ln_g', 'v_conv_ln_b', 'v_w_out_mix', 'v_g_pre_ff2', 'v_g_post_ff2', 'v_ff2_w_in', 'v_ff2_w_out']
TWIN_OUTPUTS = ['loss', 'grad_x', 'grad_w_ada', 'grad_b_ada', 'grad_g_pre_ff1', 'grad_g_post_ff1', 'grad_ff1_w_in', 'grad_ff1_w_out', 'grad_g_pre_mix', 'grad_g_post_mix', 'grad_w_in_mix', 'grad_g_attn_out', 'grad_conv_w', 'grad_conv_b', 'grad_conv_ln_g', 'grad_conv_ln_b', 'grad_w_out_mix', 'grad_g_pre_ff2', 'grad_g_post_ff2', 'grad_ff2_w_in', 'grad_ff2_w_out', 'delta_w_ada', 'delta_b_ada', 'delta_g_pre_ff1', 'delta_g_post_ff1', 'delta_ff1_w_in', 'delta_ff1_w_out', 'delta_g_pre_mix', 'delta_g_post_mix', 'delta_w_in_mix', 'delta_g_attn_out', 'delta_conv_w', 'delta_conv_b', 'delta_conv_ln_g', 'delta_conv_ln_b', 'delta_w_out_mix', 'delta_g_pre_ff2', 'delta_g_post_ff2', 'delta_ff2_w_in', 'delta_ff2_w_out', 'new_m_w_ada', 'new_m_b_ada', 'new_m_g_pre_ff1', 'new_m_g_post_ff1', 'new_m_ff1_w_in', 'new_m_ff1_w_out', 'new_m_g_pre_mix', 'new_m_g_post_mix', 'new_m_w_in_mix', 'new_m_g_attn_out', 'new_m_conv_w', 'new_m_conv_b', 'new_m_conv_ln_g', 'new_m_conv_ln_b', 'new_m_w_out_mix', 'new_m_g_pre_ff2', 'new_m_g_post_ff2', 'new_m_ff2_w_in', 'new_m_ff2_w_out', 'new_v_w_ada', 'new_v_b_ada', 'new_v_g_pre_ff1', 'new_v_g_post_ff1', 'new_v_ff1_w_in', 'new_v_ff1_w_out', 'new_v_g_pre_mix', 'new_v_g_post_mix', 'new_v_w_in_mix', 'new_v_g_attn_out', 'new_v_conv_w', 'new_v_conv_b', 'new_v_conv_ln_g', 'new_v_conv_ln_b', 'new_v_w_out_mix', 'new_v_g_pre_ff2', 'new_v_g_post_ff2', 'new_v_ff2_w_in', 'new_v_ff2_w_out']
TWIN_LEAF_KINDS = {'loss': 'loss', 'grad_x': 'grad_x', 'grad_w_ada': 'grad_w', 'grad_b_ada': 'grad_w', 'grad_g_pre_ff1': 'grad_w', 'grad_g_post_ff1': 'grad_w', 'grad_ff1_w_in': 'grad_w', 'grad_ff1_w_out': 'grad_w', 'grad_g_pre_mix': 'grad_w', 'grad_g_post_mix': 'grad_w', 'grad_w_in_mix': 'grad_w', 'grad_g_attn_out': 'grad_w', 'grad_conv_w': 'grad_w', 'grad_conv_b': 'grad_w', 'grad_conv_ln_g': 'grad_w', 'grad_conv_ln_b': 'grad_w', 'grad_w_out_mix': 'grad_w', 'grad_g_pre_ff2': 'grad_w', 'grad_g_post_ff2': 'grad_w', 'grad_ff2_w_in': 'grad_w', 'grad_ff2_w_out': 'grad_w', 'delta_w_ada': 'delta_w', 'delta_b_ada': 'delta_w', 'delta_g_pre_ff1': 'delta_w', 'delta_g_post_ff1': 'delta_w', 'delta_ff1_w_in': 'delta_w', 'delta_ff1_w_out': 'delta_w', 'delta_g_pre_mix': 'delta_w', 'delta_g_post_mix': 'delta_w', 'delta_w_in_mix': 'delta_w', 'delta_g_attn_out': 'delta_w', 'delta_conv_w': 'delta_w', 'delta_conv_b': 'delta_w', 'delta_conv_ln_g': 'delta_w', 'delta_conv_ln_b': 'delta_w', 'delta_w_out_mix': 'delta_w', 'delta_g_pre_ff2': 'delta_w', 'delta_g_post_ff2': 'delta_w', 'delta_ff2_w_in': 'delta_w', 'delta_ff2_w_out': 'delta_w', 'new_m_w_ada': 'new_m', 'new_m_b_ada': 'new_m', 'new_m_g_pre_ff1': 'new_m', 'new_m_g_post_ff1': 'new_m', 'new_m_ff1_w_in': 'new_m', 'new_m_ff1_w_out': 'new_m', 'new_m_g_pre_mix': 'new_m', 'new_m_g_post_mix': 'new_m', 'new_m_w_in_mix': 'new_m', 'new_m_g_attn_out': 'new_m', 'new_m_conv_w': 'new_m', 'new_m_conv_b': 'new_m', 'new_m_conv_ln_g': 'new_m', 'new_m_conv_ln_b': 'new_m', 'new_m_w_out_mix': 'new_m', 'new_m_g_pre_ff2': 'new_m', 'new_m_g_post_ff2': 'new_m', 'new_m_ff2_w_in': 'new_m', 'new_m_ff2_w_out': 'new_m', 'new_v_w_ada': 'new_v', 'new_v_b_ada': 'new_v', 'new_v_g_pre_ff1': 'new_v', 'new_v_g_post_ff1': 'new_v', 'new_v_ff1_w_in': 'new_v', 'new_v_ff1_w_out': 'new_v', 'new_v_g_pre_mix': 'new_v', 'new_v_g_post_mix': 'new_v', 'new_v_w_in_mix': 'new_v', 'new_v_g_attn_out': 'new_v', 'new_v_conv_w': 'new_v', 'new_v_conv_b': 'new_v', 'new_v_conv_ln_g': 'new_v', 'new_v_conv_ln_b': 'new_v', 'new_v_w_out_mix': 'new_v', 'new_v_g_pre_ff2': 'new_v', 'new_v_g_post_ff2': 'new_v', 'new_v_ff2_w_in': 'new_v', 'new_v_ff2_w_out': 'new_v'}


def _forward(args):
    return _fwd_reference(*[args[k] for k in FWD_PARAMS])


def _output_shape():
    def fwd():
        inp = _fwd_setup_inputs(0)
        return _fwd_reference(*[inp[k] for k in FWD_PARAMS])
    out = _jax.eval_shape(fwd)
    return out.shape, out.dtype

N_MICROBATCH = 1
ADAM_LR = 0.001
ADAM_B1 = 0.9
ADAM_B2 = 0.999
ADAM_EPS = 1e-08
ADAM_WD = 0.01
ADAM_STEP = 10
PER_EXAMPLE_BATCH_AXIS = {'x': 0, 'c': 0, 'loss_target': 0}
SHARED_INPUTS = []
_WEIGHT_DTYPES = {'w_ada': _jnp.float32, 'b_ada': _jnp.float32, 'g_pre_ff1': _jnp.float32, 'g_post_ff1': _jnp.float32, 'ff1_w_in': _jnp.float32, 'ff1_w_out': _jnp.float32, 'g_pre_mix': _jnp.float32, 'g_post_mix': _jnp.float32, 'w_in_mix': _jnp.float32, 'g_attn_out': _jnp.float32, 'conv_w': _jnp.float32, 'conv_b': _jnp.float32, 'conv_ln_g': _jnp.float32, 'conv_ln_b': _jnp.float32, 'w_out_mix': _jnp.float32, 'g_pre_ff2': _jnp.float32, 'g_post_ff2': _jnp.float32, 'ff2_w_in': _jnp.float32, 'ff2_w_out': _jnp.float32}
MOMENT_SCALE = {'w_ada': 6.428908e+00, 'b_ada': 2.272118e+01, 'g_pre_ff1': 5.711289e-01, 'g_post_ff1': 1.600078e+01, 'ff1_w_in': 2.350379e-01, 'ff1_w_out': 4.110108e-01, 'g_pre_mix': 5.525568e-01, 'g_post_mix': 6.462789e+01, 'w_in_mix': 3.620415e-01, 'g_attn_out': 7.117275e-01, 'conv_w': 3.917668e-01, 'conv_b': 2.415538e+00, 'conv_ln_g': 1.133753e+00, 'conv_ln_b': 1.558198e+00, 'w_out_mix': 6.392565e-01, 'g_pre_ff2': 6.055844e-01, 'g_post_ff2': 1.609000e+01, 'ff2_w_in': 2.526139e-01, 'ff2_w_out': 5.312729e-01}


def _to_microbatches(a, axis):
    t = _jnp.moveaxis(a, axis, 0)
    t = t.reshape((N_MICROBATCH, t.shape[0] // N_MICROBATCH) + t.shape[1:])
    return _jnp.moveaxis(t, 1, axis + 1)


def setup_inputs(seed: int = 0) -> dict:
    inp = _fwd_setup_inputs(seed)
    key = _jax.random.fold_in(_jax.random.key(seed), 7919)
    shape, _ = _output_shape()
    out = dict(inp)
    out["loss_target"] = _jax.random.normal(_jax.random.fold_in(key, 0), shape, _jnp.float32)
    for i, name in enumerate(TWIN_WEIGHTS):
        w = inp[name].astype(_jnp.float32)
        if MOMENT_SCALE is None:
            s = _jnp.sqrt(_jnp.mean(_jnp.square(w)) + 1e-30)
        else:
            s = MOMENT_SCALE[name]
        km, kv = _jax.random.split(_jax.random.fold_in(key, i + 1))
        out[name] = w
        out["m_" + name] = s * _jax.random.normal(km, w.shape, _jnp.float32)
        out["v_" + name] = (s * s) * _jax.random.uniform(kv, w.shape, _jnp.float32, 0.5, 1.5)
    if N_MICROBATCH > 1:
        for name, axis in PER_EXAMPLE_BATCH_AXIS.items():
            out[name] = _to_microbatches(out[name], axis)
    return {'x': out['x'], 'c': out['c'], 'w_ada': out['w_ada'], 'b_ada': out['b_ada'], 'g_pre_ff1': out['g_pre_ff1'], 'g_post_ff1': out['g_post_ff1'], 'ff1_w_in': out['ff1_w_in'], 'ff1_w_out': out['ff1_w_out'], 'g_pre_mix': out['g_pre_mix'], 'g_post_mix': out['g_post_mix'], 'w_in_mix': out['w_in_mix'], 'g_attn_out': out['g_attn_out'], 'conv_w': out['conv_w'], 'conv_b': out['conv_b'], 'conv_ln_g': out['conv_ln_g'], 'conv_ln_b': out['conv_ln_b'], 'w_out_mix': out['w_out_mix'], 'g_pre_ff2': out['g_pre_ff2'], 'g_post_ff2': out['g_post_ff2'], 'ff2_w_in': out['ff2_w_in'], 'ff2_w_out': out['ff2_w_out'], 'loss_target': out['loss_target'], 'm_w_ada': out['m_w_ada'], 'm_b_ada': out['m_b_ada'], 'm_g_pre_ff1': out['m_g_pre_ff1'], 'm_g_post_ff1': out['m_g_post_ff1'], 'm_ff1_w_in': out['m_ff1_w_in'], 'm_ff1_w_out': out['m_ff1_w_out'], 'm_g_pre_mix': out['m_g_pre_mix'], 'm_g_post_mix': out['m_g_post_mix'], 'm_w_in_mix': out['m_w_in_mix'], 'm_g_attn_out': out['m_g_attn_out'], 'm_conv_w': out['m_conv_w'], 'm_conv_b': out['m_conv_b'], 'm_conv_ln_g': out['m_conv_ln_g'], 'm_conv_ln_b': out['m_conv_ln_b'], 'm_w_out_mix': out['m_w_out_mix'], 'm_g_pre_ff2': out['m_g_pre_ff2'], 'm_g_post_ff2': out['m_g_post_ff2'], 'm_ff2_w_in': out['m_ff2_w_in'], 'm_ff2_w_out': out['m_ff2_w_out'], 'v_w_ada': out['v_w_ada'], 'v_b_ada': out['v_b_ada'], 'v_g_pre_ff1': out['v_g_pre_ff1'], 'v_g_post_ff1': out['v_g_post_ff1'], 'v_ff1_w_in': out['v_ff1_w_in'], 'v_ff1_w_out': out['v_ff1_w_out'], 'v_g_pre_mix': out['v_g_pre_mix'], 'v_g_post_mix': out['v_g_post_mix'], 'v_w_in_mix': out['v_w_in_mix'], 'v_g_attn_out': out['v_g_attn_out'], 'v_conv_w': out['v_conv_w'], 'v_conv_b': out['v_conv_b'], 'v_conv_ln_g': out['v_conv_ln_g'], 'v_conv_ln_b': out['v_conv_ln_b'], 'v_w_out_mix': out['v_w_out_mix'], 'v_g_pre_ff2': out['v_g_pre_ff2'], 'v_g_post_ff2': out['v_g_post_ff2'], 'v_ff2_w_in': out['v_ff2_w_in'], 'v_ff2_w_out': out['v_ff2_w_out']}


def _loss(weights, diff, rest, loss_target):
    with _jax.named_scope("forward"):
        args = {**rest, TWIN_DIFF_INPUT: diff, **{k: w.astype(_WEIGHT_DTYPES[k]) for k, w in weights.items()}}
        y = _forward(args)
    with _jax.named_scope("loss_head"):
        err = _jnp.square(y.astype(_jnp.float32) - loss_target)
        return 0.5 * _jnp.sum(_jnp.mean(err, axis=-1)) if err.ndim else 0.5 * err


def _adamw(w, g, m, v):
    m = ADAM_B1 * m + (1.0 - ADAM_B1) * g
    v = ADAM_B2 * v + (1.0 - ADAM_B2) * _jnp.square(g)
    m_hat = m / (1.0 - ADAM_B1 ** ADAM_STEP)
    v_hat = v / (1.0 - ADAM_B2 ** ADAM_STEP)
    delta = -ADAM_LR * (m_hat / (_jnp.sqrt(v_hat) + ADAM_EPS) + ADAM_WD * w)
    return delta, m, v


def reference(x, c, w_ada, b_ada, g_pre_ff1, g_post_ff1, ff1_w_in, ff1_w_out, g_pre_mix, g_post_mix, w_in_mix, g_attn_out, conv_w, conv_b, conv_ln_g, conv_ln_b, w_out_mix, g_pre_ff2, g_post_ff2, ff2_w_in, ff2_w_out, loss_target, m_w_ada, m_b_ada, m_g_pre_ff1, m_g_post_ff1, m_ff1_w_in, m_ff1_w_out, m_g_pre_mix, m_g_post_mix, m_w_in_mix, m_g_attn_out, m_conv_w, m_conv_b, m_conv_ln_g, m_conv_ln_b, m_w_out_mix, m_g_pre_ff2, m_g_post_ff2, m_ff2_w_in, m_ff2_w_out, v_w_ada, v_b_ada, v_g_pre_ff1, v_g_post_ff1, v_ff1_w_in, v_ff1_w_out, v_g_pre_mix, v_g_post_mix, v_w_in_mix, v_g_attn_out, v_conv_w, v_conv_b, v_conv_ln_g, v_conv_ln_b, v_w_out_mix, v_g_pre_ff2, v_g_post_ff2, v_ff2_w_in, v_ff2_w_out):
    given = dict(x=x, c=c, w_ada=w_ada, b_ada=b_ada, g_pre_ff1=g_pre_ff1, g_post_ff1=g_post_ff1, ff1_w_in=ff1_w_in, ff1_w_out=ff1_w_out, g_pre_mix=g_pre_mix, g_post_mix=g_post_mix, w_in_mix=w_in_mix, g_attn_out=g_attn_out, conv_w=conv_w, conv_b=conv_b, conv_ln_g=conv_ln_g, conv_ln_b=conv_ln_b, w_out_mix=w_out_mix, g_pre_ff2=g_pre_ff2, g_post_ff2=g_post_ff2, ff2_w_in=ff2_w_in, ff2_w_out=ff2_w_out, loss_target=loss_target, m_w_ada=m_w_ada, m_b_ada=m_b_ada, m_g_pre_ff1=m_g_pre_ff1, m_g_post_ff1=m_g_post_ff1, m_ff1_w_in=m_ff1_w_in, m_ff1_w_out=m_ff1_w_out, m_g_pre_mix=m_g_pre_mix, m_g_post_mix=m_g_post_mix, m_w_in_mix=m_w_in_mix, m_g_attn_out=m_g_attn_out, m_conv_w=m_conv_w, m_conv_b=m_conv_b, m_conv_ln_g=m_conv_ln_g, m_conv_ln_b=m_conv_ln_b, m_w_out_mix=m_w_out_mix, m_g_pre_ff2=m_g_pre_ff2, m_g_post_ff2=m_g_post_ff2, m_ff2_w_in=m_ff2_w_in, m_ff2_w_out=m_ff2_w_out, v_w_ada=v_w_ada, v_b_ada=v_b_ada, v_g_pre_ff1=v_g_pre_ff1, v_g_post_ff1=v_g_post_ff1, v_ff1_w_in=v_ff1_w_in, v_ff1_w_out=v_ff1_w_out, v_g_pre_mix=v_g_pre_mix, v_g_post_mix=v_g_post_mix, v_w_in_mix=v_w_in_mix, v_g_attn_out=v_g_attn_out, v_conv_w=v_conv_w, v_conv_b=v_conv_b, v_conv_ln_g=v_conv_ln_g, v_conv_ln_b=v_conv_ln_b, v_w_out_mix=v_w_out_mix, v_g_pre_ff2=v_g_pre_ff2, v_g_post_ff2=v_g_post_ff2, v_ff2_w_in=v_ff2_w_in, v_ff2_w_out=v_ff2_w_out)
    weights = {n: given[n] for n in TWIN_WEIGHTS}
    shared = {n: given[n] for n in SHARED_INPUTS}
    per_example = {n: given[n] for n in ['x', 'c']}
    grad_fn = _jax.value_and_grad(_loss, argnums=(0, 1))

    def one_microbatch(ex, loss_target):
        ex = dict(ex)
        diff = ex.pop(TWIN_DIFF_INPUT)
        return grad_fn(weights, diff, {**shared, **ex}, loss_target)

    if N_MICROBATCH == 1:
        loss, (grad_w, grad_x) = one_microbatch(per_example, given["loss_target"])
    else:
        def body(carry, xs):
            loss_sum, grad_sum = carry
            l_k, (gw_k, gx_k) = one_microbatch(xs[0], xs[1])
            with _jax.named_scope("update"):
                return (loss_sum + l_k, _jax.tree.map(_jnp.add, grad_sum, gw_k)), gx_k

        init = (_jnp.zeros((), _jnp.float32), _jax.tree.map(_jnp.zeros_like, weights))
        (loss, grad_w), grad_x = _jax.lax.scan(body, init, (per_example, given["loss_target"]))
    with _jax.named_scope("update"):
        delta_w, new_m, new_v = {}, {}, {}
        for n in TWIN_WEIGHTS:
            delta_w[n], new_m[n], new_v[n] = _adamw(weights[n], grad_w[n], given["m_" + n], given["v_" + n])
    return (loss, grad_x, *[grad_w[n] for n in TWIN_WEIGHTS], *[delta_w[n] for n in TWIN_WEIGHTS],
            *[new_m[n] for n in TWIN_WEIGHTS], *[new_v[n] for n in TWIN_WEIGHTS])
```

```python
import functools

import jax
import jax.numpy as jnp
from jax import lax
from jax.experimental import pallas as pl
from jax.experimental.pallas import tpu as pltpu

F32 = jnp.float32
BF16 = jnp.bfloat16
MESH = pl.DeviceIdType.MESH

HEAD_DIM = 64
CONV_KERNEL = 31
RMS_EPS = 1e-6
LN_EPS = 1e-5
ADAM_LR = 0.001
ADAM_B1 = 0.9
ADAM_B2 = 0.999
ADAM_EPS = 1e-08
ADAM_WD = 0.01
ADAM_STEP = 10

LANES = 128
HALO = 32
VMEM_LIMIT = 52 * 1024 * 1024

WEIGHTS = ['w_ada', 'b_ada', 'g_pre_ff1', 'g_post_ff1', 'ff1_w_in', 'ff1_w_out', 'g_pre_mix',
           'g_post_mix', 'w_in_mix', 'g_attn_out', 'conv_w', 'conv_b', 'conv_ln_g', 'conv_ln_b',
           'w_out_mix', 'g_pre_ff2', 'g_post_ff2', 'ff2_w_in', 'ff2_w_out']
BIG = [('ff1_w_in', 'col'), ('ff1_w_out', 'row'), ('w_in_mix', 'col'), ('w_out_mix', 'row'),
       ('ff2_w_in', 'col'), ('ff2_w_out', 'row')]


def _tile(dim, pref, mult=LANES):
    if dim <= pref:
        return dim
    best = None
    for t in range(mult, pref + 1, mult):
        if dim % t == 0:
            best = t
    assert best is not None, (dim, pref, mult)
    return best


def _cparams(sem=None):
    kw = dict(vmem_limit_bytes=VMEM_LIMIT)
    if sem is not None:
        kw['dimension_semantics'] = sem
    return pltpu.CompilerParams(**kw)


def _sigmoid(x):
    return 1.0 / (1.0 + jnp.exp(-x))


_DIMS = {'nn': (((1,), (0,)), ((), ())), 'nt': (((1,), (1,)), ((), ())), 'tn': (((0,), (0,)), ((), ()))}


def _matmul(a, b, *, mode, M, N, K, tm, tn, tk, out_dtype, name, a_spec=None, b_spec=None):
    nm, nn, nk = M // tm, N // tn, K // tk
    assert nm * tm == M and nn * tn == N and nk * tk == K, (name, M, N, K, tm, tn, tk)
    if a_spec is None:
        a_spec = (pl.BlockSpec((tk, tm), lambda i, j, k: (k, i)) if mode == 'tn'
                  else pl.BlockSpec((tm, tk), lambda i, j, k: (i, k)))
    if b_spec is None:
        b_spec = (pl.BlockSpec((tn, tk), lambda i, j, k: (j, k)) if mode == 'nt'
                  else pl.BlockSpec((tk, tn), lambda i, j, k: (k, j)))
    dims = _DIMS[mode]

    def body(a_ref, b_ref, o_ref, *scr):
        p = lax.dot_general(a_ref[...], b_ref[...], dims, preferred_element_type=F32)
        if nk == 1:
            o_ref[...] = p.astype(o_ref.dtype)
        else:
            acc = scr[0]
            k = pl.program_id(2)

            @pl.when(k == 0)
            def _():
                acc[...] = p

            @pl.when(k > 0)
            def _():
                acc[...] += p

            @pl.when(k == nk - 1)
            def _():
                o_ref[...] = acc[...].astype(o_ref.dtype)

    return pl.pallas_call(
        body, grid=(nm, nn, nk), in_specs=[a_spec, b_spec],
        out_specs=pl.BlockSpec((tm, tn), lambda i, j, k: (i, j)),
        out_shape=jax.ShapeDtypeStruct((M, N), out_dtype),
        scratch_shapes=[pltpu.VMEM((tm, tn), F32)] if nk > 1 else [],
        compiler_params=_cparams(("parallel", "parallel", "arbitrary")), name=name)(a, b)


def _ffn_in(h, w_in, *, T, D, F, name):
    tm, tn = _tile(T, 512), _tile(F, 1408)
    nf = F // tn

    def body(h_ref, wg_ref, wu_ref, gu_ref, a_ref):
        hh = h_ref[...]
        g = jnp.dot(hh, wg_ref[...], preferred_element_type=F32)
        u = jnp.dot(hh, wu_ref[...], preferred_element_type=F32)
        gu_ref[0] = g.astype(BF16)
        gu_ref[1] = u.astype(BF16)
        a_ref[...] = (g * _sigmoid(g) * u).astype(BF16)

    return pl.pallas_call(
        body, grid=(nf, T // tm),
        in_specs=[pl.BlockSpec((tm, D), lambda j, i: (i, 0)),
                  pl.BlockSpec((D, tn), lambda j, i: (0, j)),
                  pl.BlockSpec((D, tn), lambda j, i: (0, nf + j))],
        out_specs=[pl.BlockSpec((2, tm, tn), lambda j, i: (0, i, j)),
                   pl.BlockSpec((tm, tn), lambda j, i: (i, j))],
        out_shape=[jax.ShapeDtypeStruct((2, T, F), BF16), jax.ShapeDtypeStruct((T, F), BF16)],
        compiler_params=_cparams(("parallel", "parallel")), name=name)(h, w_in, w_in)


def _ffn_dact(df, w_out, gu, *, T, D, F, name):
    tm, tn = _tile(T, 512), _tile(F, 1408)

    def body(df_ref, w_ref, gu_ref, o_ref):
        da = lax.dot_general(df_ref[...], w_ref[...], _DIMS['nt'], preferred_element_type=F32)
        g = gu_ref[0].astype(F32)
        u = gu_ref[1].astype(F32)
        s = _sigmoid(g)
        o_ref[0] = (da * u * (s * (1.0 + g * (1.0 - s)))).astype(BF16)
        o_ref[1] = (da * (g * s)).astype(BF16)

    return pl.pallas_call(
        body, grid=(F // tn, T // tm),
        in_specs=[pl.BlockSpec((tm, D), lambda j, i: (i, 0)),
                  pl.BlockSpec((tn, D), lambda j, i: (j, 0)),
                  pl.BlockSpec((2, tm, tn), lambda j, i: (0, i, j))],
        out_specs=pl.BlockSpec((2, tm, tn), lambda j, i: (0, i, j)),
        out_shape=jax.ShapeDtypeStruct((2, T, F), BF16),
        compiler_params=_cparams(("parallel", "parallel")), name=name)(df, w_out, gu)


def _rowwise(fn, *, T, tm, name, tiled=(), prev=(), nxt=(), consts=(), out_tiled=(), out_acc=(), scratch=()):
    n = T // tm
    assert n * tm == T and tm % HALO == 0
    hb = tm // HALO
    in_specs = [pl.BlockSpec((tm, a.shape[1]), lambda i: (i, 0)) for a in tiled]
    in_specs += [pl.BlockSpec((HALO, a.shape[1]), lambda i: (jnp.maximum(i * hb - 1, 0), 0)) for a in prev]
    in_specs += [pl.BlockSpec((HALO, a.shape[1]), lambda i: (jnp.minimum((i + 1) * hb, T // HALO - 1), 0))
                 for a in nxt]
    in_specs += [pl.BlockSpec(a.shape, lambda i: (0, 0)) for a in consts]
    out_shape = [jax.ShapeDtypeStruct((T, c), dt) for c, dt in out_tiled]
    out_shape += [jax.ShapeDtypeStruct(s, F32) for s in out_acc]
    out_specs = [pl.BlockSpec((tm, c), lambda i: (i, 0)) for c, _ in out_tiled]
    out_specs += [pl.BlockSpec(s, lambda i: (0, 0)) for s in out_acc]
    nt, npv, nnx, nc, not_, na = len(tiled), len(prev), len(nxt), len(consts), len(out_tiled), len(out_acc)

    def body(*refs):
        pos = 0
        groups = []
        for cnt in (nt, npv, nnx, nc, not_, na, len(scratch)):
            groups.append(refs[pos:pos + cnt])
            pos += cnt
        t_r, p_r, n_r, c_r, o_r, a_r, s_r = groups
        i = pl.program_id(0)

        @pl.when(i == 0)
        def _():
            for r in a_r:
                r[...] = jnp.zeros_like(r)

        outs = fn(i, n, [r[...] for r in t_r], [r[...] for r in p_r], [r[...] for r in n_r],
                  [r[...] for r in c_r], a_r, s_r)
        for r, v in zip(o_r, outs):
            r[...] = v.astype(r.dtype)

    res = pl.pallas_call(
        body, grid=(n,), in_specs=in_specs, out_specs=out_specs, out_shape=out_shape,
        scratch_shapes=list(scratch), compiler_params=_cparams(("arbitrary",)), name=name,
    )(*tiled, *prev, *nxt, *consts)
    return res


def _colsum(v):
    return jnp.sum(v, axis=0, keepdims=True)


def _rowmean(v):
    return jnp.mean(v, axis=-1, keepdims=True)


def _pre_fwd(x, gains, mod, *, T, s, name):
    def fn(i, n, t, p, nx, c, acc, scr):
        xv, (g, m) = t[0], c
        g_pre, shift, scale = g[2 * s:2 * s + 1], m[3 * s:3 * s + 1], m[3 * s + 1:3 * s + 2]
        r = lax.rsqrt(_rowmean(xv * xv) + RMS_EPS)
        return [((xv * r) * g_pre) * (1.0 + scale) + shift]

    return _rowwise(fn, T=T, tm=_tile(T, 512, HALO), name=name, tiled=[x], consts=[gains, mod],
                    out_tiled=[(x.shape[1], BF16)])[0]


def _post_fwd(x, f, gains, mod, *, T, s, res_w, name):
    def fn(i, n, t, p, nx, c, acc, scr):
        (xv, fv), (g, m) = t, c
        g_post, gate = g[2 * s + 1:2 * s + 2], m[3 * s + 2:3 * s + 3]
        y = (fv * lax.rsqrt(_rowmean(fv * fv) + RMS_EPS)) * g_post
        return [xv + (res_w * (1.0 + gate)) * y]

    return _rowwise(fn, T=T, tm=_tile(T, 512, HALO), name=name, tiled=[x, f], consts=[gains, mod],
                    out_tiled=[(x.shape[1], F32)])[0]


def _post_fwd_loss(x, f, target, gains, mod, *, T, s, res_w, name):
    D = x.shape[1]

    def fn(i, n, t, p, nx, c, acc, scr):
        (xv, fv, tv), (g, m) = t, c
        g_post, gate = g[2 * s + 1:2 * s + 2], m[3 * s + 2:3 * s + 3]
        y = (fv * lax.rsqrt(_rowmean(fv * fv) + RMS_EPS)) * g_post
        err = (xv + (res_w * (1.0 + gate)) * y) - tv
        acc[0][...] += _colsum(err * err)
        return [err * (1.0 / D)]

    dout, sq = _rowwise(fn, T=T, tm=_tile(T, 512, HALO), name=name, tiled=[x, f, target], consts=[gains, mod],
                        out_tiled=[(D, F32)], out_acc=[(1, D)])
    return dout, sq


def _post_bwd(dout, f, gains, mod, *, T, s, res_w, name):
    D = f.shape[1]

    def fn(i, n, t, p, nx, c, acc, scr):
        (dv, fv), (g, m) = t, c
        g_post, gate = g[2 * s + 1:2 * s + 2], m[3 * s + 2:3 * s + 3]
        r2 = lax.rsqrt(_rowmean(fv * fv) + RMS_EPS)
        fh = fv * r2
        dy = dv * (res_w * (1.0 + gate))
        acc[0][...] += _colsum(dv * (res_w * (fh * g_post)))
        acc[1][...] += _colsum(dy * fh)
        gy = dy * g_post
        return [r2 * (gy - fh * _rowmean(gy * fh))]

    return _rowwise(fn, T=T, tm=_tile(T, 512, HALO), name=name, tiled=[dout, f], consts=[gains, mod],
                    out_tiled=[(D, BF16)], out_acc=[(1, D), (1, D)])


def _pre_bwd(dh, x, dout, gains, mod, *, T, s, name):
    D = x.shape[1]

    def fn(i, n, t, p, nx, c, acc, scr):
        (dhv, xv, dv), (g, m) = t, c
        g_pre, scale = g[2 * s:2 * s + 1], m[3 * s + 1:3 * s + 2]
        r = lax.rsqrt(_rowmean(xv * xv) + RMS_EPS)
        nv = xv * r
        acc[0][...] += _colsum(dhv)
        acc[1][...] += _colsum(dhv * (nv * g_pre))
        acc[2][...] += _colsum(dhv * ((1.0 + scale) * nv))
        gn = dhv * (g_pre * (1.0 + scale))
        return [r * (gn - nv * _rowmean(gn * nv)) + dv]

    return _rowwise(fn, T=T, tm=_tile(T, 512, HALO), name=name, tiled=[dh, x, dout], consts=[gains, mod],
                    out_tiled=[(D, F32)], out_acc=[(1, D), (1, D), (1, D)])


def _glu(cvg, C):
    return cvg[:, :C] * _sigmoid(cvg[:, C:])


def _conv_taps(ext_ref, w, tm, off):
    acc = None
    for k in range(CONV_KERNEL):
        term = w[k:k + 1] * ext_ref[pl.ds(off(k), tm), :]
        acc = term if acc is None else acc + term
    return acc


def _conv_norm(ext_ref, cw, cb, tm):
    yc = _conv_taps(ext_ref, cw, tm, lambda k: HALO - (CONV_KERNEL - 1) + k) + cb
    mu = _rowmean(yc)
    d = yc - mu
    rstd = lax.rsqrt(_rowmean(d * d) + LN_EPS)
    return d * rstd, rstd


def _conv_fwd(cvg, cw, cvec, *, T, C, name):
    tm = _tile(T, 512, HALO)

    def fn(i, n, t, p, nx, c, acc, scr):
        ext = scr[0]
        ext[pl.ds(0, HALO), :] = jnp.where(i == 0, 0.0, _glu(p[0], C))
        ext[pl.ds(HALO, tm), :] = _glu(t[0], C)
        yh, _ = _conv_norm(ext, c[0], c[1][0:1], tm)
        zz = yh * c[1][1:2] + c[1][2:3]
        return [zz * _sigmoid(zz)]

    return _rowwise(fn, T=T, tm=tm, name=name, tiled=[cvg], prev=[cvg], consts=[cw, cvec],
                    out_tiled=[(C, BF16)], scratch=[pltpu.VMEM((HALO + tm, C), F32)])[0]


def _conv_bwd1(cvg, duc, cw, cvec, *, T, C, name):
    tm = _tile(T, 512, HALO)

    def fn(i, n, t, p, nx, c, acc, scr):
        ext = scr[0]
        ext[pl.ds(0, HALO), :] = jnp.where(i == 0, 0.0, _glu(p[0], C))
        ext[pl.ds(HALO, tm), :] = _glu(t[0], C)
        yh, rstd = _conv_norm(ext, c[0], c[1][0:1], tm)
        ln_g = c[1][1:2]
        zz = yh * ln_g + c[1][2:3]
        s = _sigmoid(zz)
        dz = t[1] * (s * (1.0 + zz * (1.0 - s)))
        dyh = dz * ln_g
        dyc = rstd * (dyh - _rowmean(dyh) - yh * _rowmean(dyh * yh))
        acc[0][0:1, :] += _colsum(dyc)
        acc[0][1:2, :] += _colsum(dz * yh)
        acc[0][2:3, :] += _colsum(dz)
        for k in range(CONV_KERNEL):
            acc[1][k:k + 1, :] += _colsum(dyc * ext[pl.ds(HALO - (CONV_KERNEL - 1) + k, tm), :])
        return [dyc]

    return _rowwise(fn, T=T, tm=tm, name=name, tiled=[cvg, duc], prev=[cvg], consts=[cw, cvec],
                    out_tiled=[(C, F32)], out_acc=[(8, C), (HALO, C)],
                    scratch=[pltpu.VMEM((HALO + tm, C), F32)])


def _conv_bwd2(dyc, cvg, cw, *, T, C, name):
    tm = _tile(T, 512, HALO)

    def fn(i, n, t, p, nx, c, acc, scr):
        ext = scr[0]
        ext[pl.ds(0, tm), :] = t[0]
        ext[pl.ds(tm, HALO), :] = jnp.where(i == n - 1, 0.0, nx[0])
        dug = _conv_taps(ext, c[0], tm, lambda k: (CONV_KERNEL - 1) - k)
        cv, cg = t[1][:, :C], t[1][:, C:]
        s = _sigmoid(cg)
        return [dug * s, dug * cv * (s * (1.0 - s))]

    return _rowwise(fn, T=T, tm=tm, name=name, tiled=[dyc, cvg], nxt=[dyc], consts=[cw],
                    out_tiled=[(C, BF16), (C, BF16)], scratch=[pltpu.VMEM((tm + HALO, C), F32)])


def _split(v):
    hi = v.astype(BF16)
    return hi, (v - hi.astype(F32)).astype(BF16)


def _dot2(v, m):
    hi, lo = _split(v)
    return jnp.dot(hi, m, preferred_element_type=F32) + jnp.dot(lo, m, preferred_element_type=F32)


def _log_gap(z):
    return -(jnp.maximum(z, 0.0) + jnp.log(1.0 + jnp.exp(-jnp.abs(z))))


def _head_masks():
    lane = lax.broadcasted_iota(jnp.int32, (1, LANES), 1)
    return lane < HEAD_DIM, lane >= HEAD_DIM


def _attn_fwd(qkv, g_attn, *, T, AW, name):
    P = AW // LANES
    tq = _tile(T, 256)
    nq = T // tq
    scale = HEAD_DIM ** -0.5

    def body(q_ref, k_ref, v_ref, g_ref, o_ref, a_ref):
        i = pl.program_id(1)
        rows = lax.broadcasted_iota(jnp.int32, (tq, tq), 0)
        cols = lax.broadcasted_iota(jnp.int32, (tq, tq), 1)
        strict = cols < rows
        tri = jnp.where(rows >= cols, 1.0, 0.0).astype(BF16)
        q = q_ref[...]
        lo_mask, hi_mask = _head_masks()
        accs = []
        for hmask in (lo_mask, hi_mask):
            qh = jnp.where(hmask, q, jnp.zeros_like(q)) * jnp.asarray(scale, BF16)

            def block(j, carry, masked):
                acc, c = carry
                st = pl.multiple_of(j * tq, tq)
                kj = k_ref[pl.ds(st, tq), :]
                vj = v_ref[pl.ds(st, tq), :]
                z = lax.dot_general(qh, kj, _DIMS['nt'], preferred_element_type=F32)
                l = _log_gap(z)
                if masked:
                    l = jnp.where(strict, l, 0.0)
                cum = _dot2(l, tri)
                w = jnp.exp(z + cum + c)
                if masked:
                    w = jnp.where(strict, w, 0.0)
                return acc + _dot2(w, vj), c + cum[:, 0:1]

            carry = block(i, (jnp.zeros((tq, LANES), F32), jnp.zeros((tq, 1), F32)), True)
            carry = lax.fori_loop(0, i, lambda jj, cr: block(i - 1 - jj, cr, False), carry)
            accs.append(carry[0])
        o = jnp.where(lo_mask, accs[0], accs[1])
        o2 = o * o
        r0 = lax.rsqrt(jnp.sum(jnp.where(lo_mask, o2, 0.0), -1, keepdims=True) * (1.0 / HEAD_DIM) + RMS_EPS)
        r1 = lax.rsqrt(jnp.sum(jnp.where(hi_mask, o2, 0.0), -1, keepdims=True) * (1.0 / HEAD_DIM) + RMS_EPS)
        o_ref[...] = o
        a_ref[...] = ((o * jnp.where(lo_mask, r0, r1)) * g_ref[...]).astype(BF16)

    return pl.pallas_call(
        body, grid=(P, nq),
        in_specs=[pl.BlockSpec((tq, LANES), lambda p, i: (i, p)),
                  pl.BlockSpec((T, LANES), lambda p, i: (0, P + p)),
                  pl.BlockSpec((T, LANES), lambda p, i: (0, 2 * P + p)),
                  pl.BlockSpec((1, LANES), lambda p, i: (0, p))],
        out_specs=[pl.BlockSpec((tq, LANES), lambda p, i: (i, p)),
                   pl.BlockSpec((tq, LANES), lambda p, i: (i, p))],
        out_shape=[jax.ShapeDtypeStruct((T, AW), F32), jax.ShapeDtypeStruct((T, AW), BF16)],
        compiler_params=_cparams(("parallel", "arbitrary")), name=name)(qkv, qkv, qkv, g_attn)


def _attn_bwd(qkv, o, da, g_attn, *, T, AW, name):
    P = AW // LANES
    tq = _tile(T, 256)
    nq = T // tq
    scale = HEAD_DIM ** -0.5

    def body(q_ref, k_ref, v_ref, o_ref, da_ref, g_ref, dq_ref, dk_ref, dv_ref, dg_ref):
        i = pl.program_id(1)

        @pl.when(i == 0)
        def _():
            dk_ref[...] = jnp.zeros_like(dk_ref)
            dv_ref[...] = jnp.zeros_like(dv_ref)
            dg_ref[...] = jnp.zeros_like(dg_ref)

        rows = lax.broadcasted_iota(jnp.int32, (tq, tq), 0)
        cols = lax.broadcasted_iota(jnp.int32, (tq, tq), 1)
        strict = cols < rows
        tri = jnp.where(rows >= cols, 1.0, 0.0).astype(BF16)
        tri_s = jnp.where(rows > cols, 1.0, 0.0).astype(BF16)
        lo_mask, hi_mask = _head_masks()
        q = q_ref[...]
        o = o_ref[...]
        da = da_ref[...]
        g = g_ref[...]
        o2 = o * o
        r0 = lax.rsqrt(jnp.sum(jnp.where(lo_mask, o2, 0.0), -1, keepdims=True) * (1.0 / HEAD_DIM) + RMS_EPS)
        r1 = lax.rsqrt(jnp.sum(jnp.where(hi_mask, o2, 0.0), -1, keepdims=True) * (1.0 / HEAD_DIM) + RMS_EPS)
        r = jnp.where(lo_mask, r0, r1)
        oh = o * r
        gy = da * g
        gyo = gy * oh
        m0 = jnp.sum(jnp.where(lo_mask, gyo, 0.0), -1, keepdims=True) * (1.0 / HEAD_DIM)
        m1 = jnp.sum(jnp.where(hi_mask, gyo, 0.0), -1, keepdims=True) * (1.0 / HEAD_DIM)
        do = r * (gy - oh * jnp.where(lo_mask, m0, m1))
        dg_ref[...] += _colsum(da * oh)

        dqs = []
        for hmask in (lo_mask, hi_mask):
            qh = jnp.where(hmask, q, jnp.zeros_like(q)) * jnp.asarray(scale, BF16)
            do_b = jnp.where(hmask, do, 0.0).astype(BF16)
            delta = jnp.sum(do_b.astype(F32) * o, -1, keepdims=True)
            q_t = qh.astype(F32).T.astype(BF16)
            do_t = do_b.astype(F32).T.astype(BF16)

            def block(j, carry, masked):
                dq, c, gsum = carry
                st = pl.multiple_of(j * tq, tq)
                kj = k_ref[pl.ds(st, tq), :]
                vj = v_ref[pl.ds(st, tq), :]
                z = lax.dot_general(qh, kj, _DIMS['nt'], preferred_element_type=F32)
                l = _log_gap(z)
                sig = jnp.exp(z + l)
                if masked:
                    l = jnp.where(strict, l, 0.0)
                cum = _dot2(l, tri)
                w = jnp.exp(z + cum + c)
                if masked:
                    w = jnp.where(strict, w, 0.0)
                dp = lax.dot_general(do_b, vj, _DIMS['nt'], preferred_element_type=F32)
                pw = w * dp
                after = _dot2(pw, tri_s)
                dz = pw - sig * (delta - gsum - after)
                if masked:
                    dz = jnp.where(strict, dz, 0.0)
                dz_b = dz.astype(BF16)
                dk_ref[j] += jnp.dot(q_t, dz_b, preferred_element_type=F32)
                dv_ref[j] += jnp.dot(do_t, w.astype(BF16), preferred_element_type=F32)
                dq = dq + jnp.dot(dz_b, kj, preferred_element_type=F32)
                return dq, c + cum[:, 0:1], gsum + (after[:, 0:1] + pw[:, 0:1])

            zero1 = jnp.zeros((tq, 1), F32)
            carry = block(i, (jnp.zeros((tq, LANES), F32), zero1, zero1), True)
            carry = lax.fori_loop(0, i, lambda jj, cr: block(i - 1 - jj, cr, False), carry)
            dqs.append(carry[0])
        dq_ref[...] = (jnp.where(lo_mask, dqs[0], dqs[1]) * scale).astype(BF16)

    return pl.pallas_call(
        body, grid=(P, nq),
        in_specs=[pl.BlockSpec((tq, LANES), lambda p, i: (i, p)),
                  pl.BlockSpec((T, LANES), lambda p, i: (0, P + p)),
                  pl.BlockSpec((T, LANES), lambda p, i: (0, 2 * P + p)),
                  pl.BlockSpec((tq, LANES), lambda p, i: (i, p)),
                  pl.BlockSpec((tq, LANES), lambda p, i: (i, p)),
                  pl.BlockSpec((1, LANES), lambda p, i: (0, p))],
        out_specs=[pl.BlockSpec((tq, LANES), lambda p, i: (i, p)),
                   pl.BlockSpec((None, nq, LANES, tq), lambda p, i: (p, 0, 0, 0)),
                   pl.BlockSpec((None, nq, LANES, tq), lambda p, i: (p, 0, 0, 0)),
                   pl.BlockSpec((1, LANES), lambda p, i: (0, p))],
        out_shape=[jax.ShapeDtypeStruct((T, AW), BF16),
                   jax.ShapeDtypeStruct((P, nq, LANES, tq), F32),
                   jax.ShapeDtypeStruct((P, nq, LANES, tq), F32),
                   jax.ShapeDtypeStruct((1, AW), F32)],
        compiler_params=_cparams(("parallel", "arbitrary")), name=name)(qkv, qkv, qkv, o, da, g_attn)


def _ada_fwd(c_all, w_ada, b_ada, *, name):
    def body(c_ref, w_ref, b_ref, o_ref):
        cv = c_ref[...]
        sc = cv * _sigmoid(cv)
        o_ref[...] = jnp.dot(sc, w_ref[...], preferred_element_type=F32,
                             precision=lax.Precision.HIGHEST) + b_ref[...]

    return pl.pallas_call(body, out_shape=jax.ShapeDtypeStruct((c_all.shape[0], w_ada.shape[1]), F32),
                          compiler_params=_cparams(), name=name)(c_all, w_ada, b_ada)


def _ada_bwd(c_all_t, dmod, *, name):
    def body(c_ref, d_ref, o_ref):
        cv = c_ref[...]
        sc = cv * _sigmoid(cv)
        o_ref[...] = jnp.dot(sc, d_ref[...], preferred_element_type=F32, precision=lax.Precision.HIGHEST)

    return pl.pallas_call(body, out_shape=jax.ShapeDtypeStruct((c_all_t.shape[0], dmod.shape[1]), F32),
                          compiler_params=_cparams(), name=name)(c_all_t, dmod)


def _adamw(w, g, m, v, *, name):
    R, C = w.shape
    tr = _tile(R, max(8, (1 << 18) // C), 8)

    def body(w_ref, g_ref, m_ref, v_ref, d_ref, nm_ref, nv_ref):
        gv = g_ref[...]
        m2 = ADAM_B1 * m_ref[...] + (1.0 - ADAM_B1) * gv
        v2 = ADAM_B2 * v_ref[...] + (1.0 - ADAM_B2) * jnp.square(gv)
        m_hat = m2 / (1.0 - ADAM_B1 ** ADAM_STEP)
        v_hat = v2 / (1.0 - ADAM_B2 ** ADAM_STEP)
        d_ref[...] = -ADAM_LR * (m_hat / (jnp.sqrt(v_hat) + ADAM_EPS) + ADAM_WD * w_ref[...])
        nm_ref[...] = m2
        nv_ref[...] = v2

    spec = pl.BlockSpec((tr, C), lambda i: (i, 0))
    return pl.pallas_call(
        body, grid=(R // tr,), in_specs=[spec] * 4, out_specs=[spec] * 3,
        out_shape=[jax.ShapeDtypeStruct((R, C), F32)] * 3,
        compiler_params=_cparams(("parallel",)), name=name)(w, g, m, v)


def _sum_devices(a, *, name):
    def body(a_ref, o_ref):
        s = a_ref[0]
        for d in range(1, a_ref.shape[0]):
            s = s + a_ref[d]
        o_ref[...] = s

    return pl.pallas_call(body, out_shape=jax.ShapeDtypeStruct(a.shape[1:], F32),
                          compiler_params=_cparams(), name=name)(a)


def _place():
    return lax.axis_index("x"), lax.axis_index("y"), lax.axis_index("c")


def _flip(v, bit):
    return 1 - v if bit else v


def _allgather8(blk, *, name):
    R, C = blk.shape

    def body(x_ref, out_ref, send_sems, recv_sems):
        x, y, c = _place()
        me = 4 * x + 2 * y + c
        out_ref[me] = x_ref[...]
        copies = []
        for k in range(1, 8):
            peer = (_flip(x, (k >> 2) & 1), _flip(y, (k >> 1) & 1), _flip(c, k & 1))
            cp = pltpu.make_async_remote_copy(
                src_ref=x_ref, dst_ref=out_ref.at[me], send_sem=send_sems.at[k - 1],
                recv_sem=recv_sems.at[k - 1], device_id=peer, device_id_type=MESH)
            cp.start()
            copies.append(cp)
        for cp in copies:
            cp.wait()

    return pl.pallas_call(
        body, out_shape=jax.ShapeDtypeStruct((8, R, C), F32),
        in_specs=[pl.BlockSpec(memory_space=pltpu.VMEM)], out_specs=pl.BlockSpec(memory_space=pltpu.VMEM),
        scratch_shapes=[pltpu.SemaphoreType.DMA((7,)), pltpu.SemaphoreType.DMA((7,))],
        compiler_params=_cparams(), name=name)(blk)


def _aligned(v, m):
    return v if isinstance(v, int) else pl.multiple_of(v, m)


def _rows_half(ref, half):
    n = ref.shape[0] // 2
    return ref.at[pl.ds(_aligned(half * n, 16), n)]


def _region(ref, kind, slot, half):
    if kind == 'col':
        n, cs = ref.shape[0] // 2, ref.shape[1] // 4
        return ref.at[pl.ds(_aligned(half * n, 16), n), pl.ds(_aligned(slot * cs, LANES), cs)]
    rs = ref.shape[0] // 4
    return ref.at[pl.ds(_aligned(slot * rs + half * (rs // 2), 16), rs // 2)]


def _other_chips(x, y):
    return [(1 - x, y), (x, 1 - y), (1 - x, 1 - y)]


def _gather_weights(shards, kinds, *, name):
    nw = len(shards)
    full_shapes = []
    for s, kind in zip(shards, kinds):
        full_shapes.append((s.shape[0], 4 * s.shape[1]) if kind == 'col' else (4 * s.shape[0], s.shape[1]))

    def body(*refs):
        sh, full = refs[:nw], refs[nw:2 * nw]
        lsem, ssem, rsem, fssem, frsem = refs[2 * nw:]
        x, y, c = _place()
        me_slot = 2 * x + y
        chips = _other_chips(x, y)
        local, sends = [], []
        for w in range(nw):
            for h in (0, 1):
                cp = pltpu.make_async_copy(_rows_half(sh[w], h), _region(full[w], kinds[w], me_slot, h),
                                           lsem.at[w, h])
                cp.start()
                local.append(cp)
            for r, (px, py) in enumerate(chips):
                cp = pltpu.make_async_remote_copy(
                    src_ref=_rows_half(sh[w], c), dst_ref=_region(full[w], kinds[w], me_slot, c),
                    send_sem=ssem.at[w, r], recv_sem=rsem.at[w, r], device_id=(px, py, c), device_id_type=MESH)
                cp.start()
                sends.append(cp)
        for w in range(nw):
            for r, (px, py) in enumerate(chips):
                landed = _region(full[w], kinds[w], 2 * px + py, c)
                pltpu.make_async_remote_copy(
                    src_ref=landed, dst_ref=landed, send_sem=ssem.at[w, r], recv_sem=rsem.at[w, r],
                    device_id=(px, py, c), device_id_type=MESH).wait_recv()
                cp = pltpu.make_async_remote_copy(
                    src_ref=landed, dst_ref=landed, send_sem=fssem.at[w, r], recv_sem=frsem.at[w, r],
                    device_id=(x, y, 1 - c), device_id_type=MESH)
                cp.start()
                sends.append(cp)
        for w in range(nw):
            for r, (px, py) in enumerate(chips):
                passed = _region(full[w], kinds[w], 2 * px + py, 1 - c)
                pltpu.make_async_remote_copy(
                    src_ref=passed, dst_ref=passed, send_sem=fssem.at[w, r], recv_sem=frsem.at[w, r],
                    device_id=(x, y, 1 - c), device_id_type=MESH).wait_recv()
        for cp in sends:
            cp.wait_send()
        for cp in local:
            cp.wait()

    anyspec = pl.BlockSpec(memory_space=pl.ANY)
    return pl.pallas_call(
        body, out_shape=[jax.ShapeDtypeStruct(s, BF16) for s in full_shapes],
        in_specs=[anyspec] * nw, out_specs=[anyspec] * nw,
        scratch_shapes=[pltpu.SemaphoreType.DMA((nw, 2))] + [pltpu.SemaphoreType.DMA((nw, 3))] * 4,
        compiler_params=_cparams(), name=name)(*shards)


def _exchange_core_halves(grads, kinds, *, name):
    nw = len(grads)

    def body(*refs):
        g, r1 = refs[:nw], refs[nw:2 * nw]
        ssem, rsem = refs[2 * nw:]
        x, y, c = _place()
        copies = []
        for w in range(nw):
            for slot in range(4):
                cp = pltpu.make_async_remote_copy(
                    src_ref=_region(g[w], kinds[w], slot, 1 - c), dst_ref=_region(r1[w], kinds[w], slot, 1 - c),
                    send_sem=ssem.at[w, slot], recv_sem=rsem.at[w, slot], device_id=(x, y, 1 - c),
                    device_id_type=MESH)
                cp.start()
                copies.append(cp)
        for w in range(nw):
            for slot in range(4):
                mine = _region(r1[w], kinds[w], slot, c)
                pltpu.make_async_remote_copy(
                    src_ref=mine, dst_ref=mine, send_sem=ssem.at[w, slot], recv_sem=rsem.at[w, slot],
                    device_id=(x, y, 1 - c), device_id_type=MESH).wait_recv()
        for cp in copies:
            cp.wait_send()

    anyspec = pl.BlockSpec(memory_space=pl.ANY)
    return pl.pallas_call(
        body, out_shape=[jax.ShapeDtypeStruct(g.shape, F32) for g in grads],
        in_specs=[anyspec] * nw, out_specs=[anyspec] * nw,
        scratch_shapes=[pltpu.SemaphoreType.DMA((nw, 4))] * 2,
        compiler_params=_cparams(), name=name)(*grads)


def _add_core_halves(g, r1, place, kind, *, name):
    if kind == 'col':
        n, cs = g.shape[0] // 2, g.shape[1] // 4
        tr = _tile(n, 256, 16)
        nt = n // tr
        ispec = pl.BlockSpec((tr, cs), lambda s, t, pr: (pr[0] * nt + t, s))
    else:
        rs, cs = g.shape[0] // 4, g.shape[1]
        n = rs // 2
        tr, nt = n, 1
        ispec = pl.BlockSpec((tr, cs), lambda s, t, pr: (s * 2 + pr[0], 0))

    def body(pr, a_ref, b_ref, o_ref):
        o_ref[...] = a_ref[...] + b_ref[...]

    return pl.pallas_call(
        body,
        grid_spec=pltpu.PrefetchScalarGridSpec(
            num_scalar_prefetch=1, grid=(4, nt), in_specs=[ispec, ispec],
            out_specs=pl.BlockSpec((None, tr, cs), lambda s, t, pr: (s, t, 0))),
        out_shape=jax.ShapeDtypeStruct((4, n, cs), F32),
        compiler_params=_cparams(("parallel", "parallel")), name=name)(place, g, r1)


def _scatter_to_owners(hs, *, name):
    nw = len(hs)

    def body(*refs):
        h, r2 = refs[:nw], refs[nw:2 * nw]
        ssem, rsem = refs[2 * nw:]
        x, y, c = _place()
        chips = _other_chips(x, y)
        copies = []
        for w in range(nw):
            for r, (px, py) in enumerate(chips):
                cp = pltpu.make_async_remote_copy(
                    src_ref=h[w].at[2 * px + py], dst_ref=r2[w].at[r], send_sem=ssem.at[w, r],
                    recv_sem=rsem.at[w, r], device_id=(px, py, c), device_id_type=MESH)
                cp.start()
                copies.append(cp)
        for cp in copies:
            cp.wait()

    anyspec = pl.BlockSpec(memory_space=pl.ANY)
    return pl.pallas_call(
        body, out_shape=[jax.ShapeDtypeStruct((3,) + a.shape[1:], F32) for a in hs],
        in_specs=[anyspec] * nw, out_specs=[anyspec] * nw,
        scratch_shapes=[pltpu.SemaphoreType.DMA((nw, 3))] * 2,
        compiler_params=_cparams(), name=name)(*hs)


def _sum_owner(hs, r2, place, *, name):
    _, n, cs = hs.shape
    tr = _tile(n, 256, 8)
    nt = n // tr

    def body(pr, h_ref, r_ref, o_ref):
        o_ref[...] = ((h_ref[...] + r_ref[0]) + r_ref[1]) + r_ref[2]

    return pl.pallas_call(
        body,
        grid_spec=pltpu.PrefetchScalarGridSpec(
            num_scalar_prefetch=1, grid=(nt,),
            in_specs=[pl.BlockSpec((None, tr, cs), lambda t, pr: (pr[1], t, 0)),
                      pl.BlockSpec((3, tr, cs), lambda t, pr: (0, t, 0))],
            out_specs=pl.BlockSpec((None, tr, cs), lambda t, pr: (pr[0], t, 0))),
        out_shape=jax.ShapeDtypeStruct((2, n, cs), F32),
        compiler_params=_cparams(("parallel",)), name=name)(place, hs, r2)


def _share_with_sibling(fins, *, name):
    nw = len(fins)

    def body(*refs):
        fin, out = refs[:nw], refs[nw:2 * nw]
        ssem, rsem = refs[2 * nw:]
        x, y, c = _place()
        copies = []
        for w in range(nw):
            cp = pltpu.make_async_remote_copy(
                src_ref=fin[w].at[c], dst_ref=out[w].at[c], send_sem=ssem.at[w], recv_sem=rsem.at[w],
                device_id=(x, y, 1 - c), device_id_type=MESH)
            cp.start()
            copies.append(cp)
        for w in range(nw):
            theirs = out[w].at[1 - c]
            pltpu.make_async_remote_copy(
                src_ref=theirs, dst_ref=theirs, send_sem=ssem.at[w], recv_sem=rsem.at[w],
                device_id=(x, y, 1 - c), device_id_type=MESH).wait_recv()
        for cp in copies:
            cp.wait_send()

    anyspec = pl.BlockSpec(memory_space=pl.ANY)
    return pl.pallas_call(
        body, out_shape=[jax.ShapeDtypeStruct(a.shape, F32) for a in fins],
        in_specs=[anyspec] * nw, out_specs=[anyspec] * nw,
        input_output_aliases={w: w for w in range(nw)},
        scratch_shapes=[pltpu.SemaphoreType.DMA((nw,))] * 2,
        compiler_params=_cparams(), name=name)(*fins)


def _local_step(x, target, mod, gains, wfull, g_attn, conv_w, cvec):
    T, D = x.shape
    F = wfull['ff1_w_out'].shape[0]
    AW = D // 2
    C = D - AW
    NQKV = 3 * AW
    MIX = NQKV + 2 * C
    tM = _tile(T, 1024)

    def ffn_fwd(xin, s, w_in, w_out, tag):
        h = _pre_fwd(xin, gains, mod, T=T, s=s, name=f"pre_fwd_{tag}")
        gu, act = _ffn_in(h, w_in, T=T, D=D, F=F, name=f"ffn_in_{tag}")
        f = _matmul(act, w_out, mode='nn', M=T, N=D, K=F, tm=tM, tn=_tile(D, 1024), tk=_tile(F, 1408),
                    out_dtype=F32, name=f"ffn_out_{tag}")
        return h, gu, act, f

    def ffn_bwd(dout, xin, saved, s, w_in, w_out, res_w, tag):
        h, gu, act, f = saved
        df, dgate, dgpost = _post_bwd(dout, f, gains, mod, T=T, s=s, res_w=res_w, name=f"post_bwd_{tag}")
        dgu = _ffn_dact(df, w_out, gu, T=T, D=D, F=F, name=f"ffn_dact_{tag}")
        dw_out = _matmul(act, df, mode='tn', M=F, N=D, K=T, tm=_tile(F, 1408), tn=_tile(D, 1024),
                         tk=_tile(T, 512), out_dtype=F32, name=f"dw_out_{tag}")
        tk = _tile(F, 1408)
        kf = F // tk
        tn = _tile(D, 1024)
        dh = _matmul(dgu, w_in, mode='nt', M=T, N=D, K=2 * F, tm=tM, tn=tn, tk=tk, out_dtype=F32,
                     a_spec=pl.BlockSpec((None, tM, tk), lambda i, j, k: (k // kf, i, k % kf)),
                     name=f"dh_{tag}")
        tnf = _tile(F, 1408)
        nf = F // tnf
        tkt = _tile(T, 512)
        dw_in = _matmul(h, dgu, mode='tn', M=D, N=2 * F, K=T, tm=_tile(D, 1024), tn=tnf, tk=tkt, out_dtype=F32,
                        b_spec=pl.BlockSpec((None, tkt, tnf), lambda i, j, k: (j // nf, k, j % nf)),
                        name=f"dw_in_{tag}")
        dx, dshift, dscale, dgpre = _pre_bwd(dh, xin, dout, gains, mod, T=T, s=s, name=f"pre_bwd_{tag}")
        return dx, dw_in, dw_out, (dshift, dscale, dgate), dgpre, dgpost

    s1 = ffn_fwd(x, 0, wfull['ff1_w_in'], wfull['ff1_w_out'], "ff1")
    x1 = _post_fwd(x, s1[3], gains, mod, T=T, s=0, res_w=0.5, name="post_fwd_ff1")

    h2 = _pre_fwd(x1, gains, mod, T=T, s=1, name="pre_fwd_mix")
    w_in_mix, w_out_mix = wfull['w_in_mix'], wfull['w_out_mix']
    tnq = _tile(AW, 512)
    qkv = _matmul(h2, w_in_mix, mode='nn', M=T, N=NQKV, K=D, tm=tM, tn=tnq, tk=D, out_dtype=BF16, name="proj_qkv")
    tnc = _tile(C, 512)
    off = NQKV // tnc
    cvg = _matmul(h2, w_in_mix, mode='nn', M=T, N=2 * C, K=D, tm=tM, tn=tnc, tk=D, out_dtype=F32,
                  b_spec=pl.BlockSpec((D, tnc), lambda i, j, k: (0, off + j)), name="proj_conv")
    o_attn, a_attn = _attn_fwd(qkv, g_attn, T=T, AW=AW, name="attn_fwd")
    uc = _conv_fwd(cvg, conv_w, cvec, T=T, C=C, name="conv_fwd")
    mixcat = jnp.concatenate([a_attn, uc], axis=1)
    f_mix = _matmul(mixcat, w_out_mix, mode='nn', M=T, N=D, K=D, tm=tM, tn=_tile(D, 1024), tk=D, out_dtype=F32,
                    name="mix_out")
    x2 = _post_fwd(x1, f_mix, gains, mod, T=T, s=1, res_w=1.0, name="post_fwd_mix")

    s3 = ffn_fwd(x2, 2, wfull['ff2_w_in'], wfull['ff2_w_out'], "ff2")
    dout, sq = _post_fwd_loss(x2, s3[3], target, gains, mod, T=T, s=2, res_w=0.5, name="post_fwd_loss")

    dx2, dw_in2, dw_out2, dmod2, dgpre2, dgpost2 = ffn_bwd(
        dout, x2, s3, 2, wfull['ff2_w_in'], wfull['ff2_w_out'], 0.5, "ff2")

    df_mix, dgate_m, dgpost_m = _post_bwd(dx2, f_mix, gains, mod, T=T, s=1, res_w=1.0, name="post_bwd_mix")
    dmixcat = _matmul(df_mix, w_out_mix, mode='nt', M=T, N=D, K=D, tm=tM, tn=_tile(D, 1024), tk=D, out_dtype=F32,
                      name="d_mixcat")
    dw_out_mix = _matmul(mixcat, df_mix, mode='tn', M=D, N=D, K=T, tm=_tile(D, 1024), tn=_tile(D, 1024),
                         tk=_tile(T, 512), out_dtype=F32, name="dw_out_mix")
    da_attn, duc = dmixcat[:, :AW], dmixcat[:, AW:]
    dq, dk_t, dv_t, dg_attn = _attn_bwd(qkv, o_attn, da_attn, g_attn, T=T, AW=AW, name="attn_bwd")
    dk = jnp.transpose(dk_t, (1, 3, 0, 2)).reshape(T, AW).astype(BF16)
    dv = jnp.transpose(dv_t, (1, 3, 0, 2)).reshape(T, AW).astype(BF16)
    dyc, csum, dconv_w = _conv_bwd1(cvg, duc, conv_w, cvec, T=T, C=C, name="conv_bwd1")
    dcv, dcg = _conv_bwd2(dyc, cvg, conv_w, T=T, C=C, name="conv_bwd2")
    dproj = jnp.concatenate([dq, dk, dv, dcv, dcg], axis=1)
    tkm = _tile(MIX, 1280)
    dh2 = _matmul(dproj, w_in_mix, mode='nt', M=T, N=D, K=MIX, tm=tM, tn=_tile(D, 1024), tk=tkm, out_dtype=F32,
                  name="dh_mix")
    dw_in_mix = _matmul(h2, dproj, mode='tn', M=D, N=MIX, K=T, tm=_tile(D, 1024), tn=_tile(MIX, 1280),
                        tk=_tile(T, 512), out_dtype=F32, name="dw_in_mix")
    dx1, dshift_m, dscale_m, dgpre_m = _pre_bwd(dh2, x1, dx2, gains, mod, T=T, s=1, name="pre_bwd_mix")

    dx0, dw_in1, dw_out1, dmod1, dgpre1, dgpost1 = ffn_bwd(
        dx1, x, s1, 0, wfull['ff1_w_in'], wfull['ff1_w_out'], 0.5, "ff1")

    big = {'ff1_w_in': dw_in1, 'ff1_w_out': dw_out1, 'w_in_mix': dw_in_mix, 'w_out_mix': dw_out_mix,
           'ff2_w_in': dw_in2, 'ff2_w_out': dw_out2}
    dgains = [dgpre1, dgpost1, dgpre_m, dgpost_m, dgpre2, dgpost2]
    dmod = list(dmod1) + [dshift_m, dscale_m, dgate_m] + list(dmod2)
    return sq, dx0, big, dgains, dmod, dg_attn, csum, dconv_w


def _pack_rows(pieces, width):
    rows = jnp.concatenate([p.reshape(-1) for p in pieces]).reshape(-1, width)
    pad = (-rows.shape[0]) % 8
    return jnp.pad(rows, ((0, pad), (0, 0)))


def kernel(x, c, w_ada, b_ada, g_pre_ff1, g_post_ff1, ff1_w_in, ff1_w_out, g_pre_mix, g_post_mix, w_in_mix, g_attn_out, conv_w, conv_b, conv_ln_g, conv_ln_b, w_out_mix, g_pre_ff2, g_post_ff2, ff2_w_in, ff2_w_out, loss_target, m_w_ada, m_b_ada, m_g_pre_ff1, m_g_post_ff1, m_ff1_w_in, m_ff1_w_out, m_g_pre_mix, m_g_post_mix, m_w_in_mix, m_g_attn_out, m_conv_w, m_conv_b, m_conv_ln_g, m_conv_ln_b, m_w_out_mix, m_g_pre_ff2, m_g_post_ff2, m_ff2_w_in, m_ff2_w_out, v_w_ada, v_b_ada, v_g_pre_ff1, v_g_post_ff1, v_ff1_w_in, v_ff1_w_out, v_g_pre_mix, v_g_post_mix, v_w_in_mix, v_g_attn_out, v_conv_w, v_conv_b, v_conv_ln_g, v_conv_ln_b, v_w_out_mix, v_g_pre_ff2, v_g_post_ff2, v_ff2_w_in, v_ff2_w_out):
    W = dict(w_ada=w_ada, b_ada=b_ada, g_pre_ff1=g_pre_ff1, g_post_ff1=g_post_ff1, ff1_w_in=ff1_w_in,
             ff1_w_out=ff1_w_out, g_pre_mix=g_pre_mix, g_post_mix=g_post_mix, w_in_mix=w_in_mix,
             g_attn_out=g_attn_out, conv_w=conv_w, conv_b=conv_b, conv_ln_g=conv_ln_g, conv_ln_b=conv_ln_b,
             w_out_mix=w_out_mix, g_pre_ff2=g_pre_ff2, g_post_ff2=g_post_ff2, ff2_w_in=ff2_w_in,
             ff2_w_out=ff2_w_out)
    Mo = dict(w_ada=m_w_ada, b_ada=m_b_ada, g_pre_ff1=m_g_pre_ff1, g_post_ff1=m_g_post_ff1, ff1_w_in=m_ff1_w_in,
              ff1_w_out=m_ff1_w_out, g_pre_mix=m_g_pre_mix, g_post_mix=m_g_post_mix, w_in_mix=m_w_in_mix,
              g_attn_out=m_g_attn_out, conv_w=m_conv_w, conv_b=m_conv_b, conv_ln_g=m_conv_ln_g,
              conv_ln_b=m_conv_ln_b, w_out_mix=m_w_out_mix, g_pre_ff2=m_g_pre_ff2, g_post_ff2=m_g_post_ff2,
              ff2_w_in=m_ff2_w_in, ff2_w_out=m_ff2_w_out)
    Vo = dict(w_ada=v_w_ada, b_ada=v_b_ada, g_pre_ff1=v_g_pre_ff1, g_post_ff1=v_g_post_ff1, ff1_w_in=v_ff1_w_in,
              ff1_w_out=v_ff1_w_out, g_pre_mix=v_g_pre_mix, g_post_mix=v_g_post_mix, w_in_mix=v_w_in_mix,
              g_attn_out=v_g_attn_out, conv_w=v_conv_w, conv_b=v_conv_b, conv_ln_g=v_conv_ln_g,
              conv_ln_b=v_conv_ln_b, w_out_mix=v_w_out_mix, g_pre_ff2=v_g_pre_ff2, g_post_ff2=v_g_post_ff2,
              ff2_w_in=v_ff2_w_in, ff2_w_out=v_ff2_w_out)

    T, D = x.shape[1], x.shape[2]
    AW = D // 2
    C = D - AW
    xi, yi, ci = _place()
    me = 4 * xi + 2 * yi + ci
    chip = 2 * xi + yi
    place = jnp.stack([ci, chip]).astype(jnp.int32)

    c_all = _allgather8(jnp.tile(c, (8, 1)), name="gather_c")[:, 0, :]
    ncol = w_ada.shape[1]
    b_cols = lax.dynamic_index_in_dim(b_ada.reshape(4, ncol), chip, keepdims=True).reshape(1, ncol)
    modp = _ada_fwd(c_all, w_ada, b_cols, name="ada_fwd")
    mod_g = _allgather8(modp, name="gather_mod")
    mod_all = jnp.transpose(mod_g[0::2], (1, 0, 2)).reshape(8, 4 * ncol)
    mod = lax.dynamic_index_in_dim(mod_all, me, keepdims=False).reshape(9, D)

    names = [n for n, _ in BIG]
    kinds = [k for _, k in BIG]
    full = _gather_weights([W[n].astype(BF16) for n in names], kinds, name="gather_weights")
    wfull = dict(zip(names, full))
    cs = conv_w.shape[1]
    cw_all = _allgather8(jnp.pad(conv_w, ((0, HALO - CONV_KERNEL), (0, (-cs) % LANES))), name="gather_conv_w")
    conv_w_full = jnp.transpose(cw_all[0::2, :, :cs], (1, 0, 2)).reshape(HALO, 4 * cs)

    gains = _pack_rows([g_pre_ff1, g_post_ff1, g_pre_mix, g_post_mix, g_pre_ff2, g_post_ff2], D)
    cvec = _pack_rows([conv_b, conv_ln_g, conv_ln_b], C)
    g_attn = g_attn_out.reshape(1, AW)

    sq, dx, big, dgains, dmod, dg_attn, csum, dconv_w = _local_step(
        x[0], loss_target[0], mod, gains, wfull, g_attn, conv_w_full, cvec)

    loss_row = jnp.zeros((1, D), F32).at[0, 0].set(jnp.sum(sq) * (0.5 / D))
    small = _pack_rows(dgains + dmod + [dg_attn, csum[0:3], dconv_w, loss_row], D)
    small_all = _allgather8(small, name="gather_small")
    tot = _sum_devices(small_all, name="sum_small")
    n_g, n_m = 6, 9
    r0 = n_g + n_m
    flat = tot.reshape(-1)
    p = r0 * D
    g_attn_grad = flat[p:p + AW]
    p += AW
    gconv_b, gln_g, gln_b = flat[p:p + C], flat[p + C:p + 2 * C], flat[p + 2 * C:p + 3 * C]
    p += 3 * C
    gconv_w_full = flat[p:p + HALO * C].reshape(HALO, C)[:CONV_KERNEL]
    p += HALO * C
    loss = flat[p]
    gconv_w = lax.dynamic_slice_in_dim(gconv_w_full, chip * cs, cs, axis=1)
    grad_small = {'g_pre_ff1': tot[0], 'g_post_ff1': tot[1], 'g_pre_mix': tot[2], 'g_post_mix': tot[3],
                  'g_pre_ff2': tot[4], 'g_post_ff2': tot[5], 'b_ada': tot[n_g:r0].reshape(-1),
                  'g_attn_out': g_attn_grad.reshape(g_attn_out.shape), 'conv_w': gconv_w, 'conv_b': gconv_b,
                  'conv_ln_g': gln_g, 'conv_ln_b': gln_b}

    dmod_all = small_all[:, n_g:r0, :].reshape(8, 9 * D)
    dmod_cols = lax.dynamic_slice_in_dim(dmod_all, chip * ncol, ncol, axis=1)
    grad_w_ada = _ada_bwd(jnp.transpose(c_all), dmod_cols, name="ada_bwd")

    glist = [big[n] for n in names]
    r1 = _exchange_core_halves(glist, kinds, name="grad_core_exchange")
    hs = [_add_core_halves(g, r, place, k, name=f"grad_core_add_{n}") for g, r, k, n in zip(glist, r1, kinds, names)]
    r2 = _scatter_to_owners(hs, name="grad_scatter")
    fins = [_sum_owner(h, r, place, name=f"grad_owner_sum_{n}") for h, r, n in zip(hs, r2, names)]
    shared = _share_with_sibling(fins, name="grad_share")
    grads = dict(grad_small)
    grads['w_ada'] = grad_w_ada
    for n, a in zip(names, shared):
        grads[n] = a.reshape(W[n].shape)

    delta, new_m, new_v = {}, {}, {}
    for n in ['w_ada'] + names:
        delta[n], new_m[n], new_v[n] = _adamw(W[n], grads[n], Mo[n], Vo[n], name=f"adamw_{n}")
    smalls = [n for n in WEIGHTS if n not in delta]
    sizes = [W[n].size for n in smalls]
    tot_sz = sum(sizes)
    padn = (-tot_sz) % (8 * LANES)

    def pack(d):
        return jnp.pad(jnp.concatenate([d[n].reshape(-1) for n in smalls]), (0, padn)).reshape(-1, LANES)

    d_s, m_s, v_s = _adamw(pack(W), pack(grads), pack(Mo), pack(Vo), name="adamw_small")
    pos = 0
    for n, sz in zip(smalls, sizes):
        for dst, src in ((delta, d_s), (new_m, m_s), (new_v, v_s)):
            dst[n] = src.reshape(-1)[pos:pos + sz].reshape(W[n].shape)
        pos += sz

    return (loss, dx[None], *[grads[n] for n in WEIGHTS], *[delta[n] for n in WEIGHTS],
            *[new_m[n] for n in WEIGHTS], *[new_v[n] for n in WEIGHTS])
```

```python
import functools

import jax
import jax.numpy as jnp
from jax import lax
from jax.experimental import pallas as pl
from jax.experimental.pallas import tpu as pltpu

F32 = jnp.float32
BF16 = jnp.bfloat16
MESH = pl.DeviceIdType.MESH

HEAD_DIM = 64
CONV_KERNEL = 31
RMS_EPS = 1e-6
LN_EPS = 1e-5
ADAM_LR = 0.001
ADAM_B1 = 0.9
ADAM_B2 = 0.999
ADAM_EPS = 1e-08
ADAM_WD = 0.01
ADAM_STEP = 10

LANES = 128
HALO = 32
VMEM_LIMIT = 52 * 1024 * 1024

WEIGHTS = ['w_ada', 'b_ada', 'g_pre_ff1', 'g_post_ff1', 'ff1_w_in', 'ff1_w_out', 'g_pre_mix',
           'g_post_mix', 'w_in_mix', 'g_attn_out', 'conv_w', 'conv_b', 'conv_ln_g', 'conv_ln_b',
           'w_out_mix', 'g_pre_ff2', 'g_post_ff2', 'ff2_w_in', 'ff2_w_out']
BIG = [('ff1_w_in', 'col'), ('ff1_w_out', 'row'), ('w_in_mix', 'col'), ('w_out_mix', 'row'),
       ('ff2_w_in', 'col'), ('ff2_w_out', 'row')]


def _tile(dim, pref, mult=LANES):
    if dim <= pref:
        return dim
    best = None
    for t in range(mult, pref + 1, mult):
        if dim % t == 0:
            best = t
    assert best is not None, (dim, pref, mult)
    return best


def _cparams(sem=None):
    kw = dict(vmem_limit_bytes=VMEM_LIMIT)
    if sem is not None:
        kw['dimension_semantics'] = sem
    return pltpu.CompilerParams(**kw)


def _sigmoid(x):
    return 1.0 / (1.0 + jnp.exp(-x))


_DIMS = {'nn': (((1,), (0,)), ((), ())), 'nt': (((1,), (1,)), ((), ())), 'tn': (((0,), (0,)), ((), ()))}


def _matmul(a, b, *, mode, M, N, K, tm, tn, tk, out_dtype, name, a_spec=None, b_spec=None):
    nm, nn, nk = M // tm, N // tn, K // tk
    assert nm * tm == M and nn * tn == N and nk * tk == K, (name, M, N, K, tm, tn, tk)
    if a_spec is None:
        a_spec = (pl.BlockSpec((tk, tm), lambda i, j, k: (k, i)) if mode == 'tn'
                  else pl.BlockSpec((tm, tk), lambda i, j, k: (i, k)))
    if b_spec is None:
        b_spec = (pl.BlockSpec((tn, tk), lambda i, j, k: (j, k)) if mode == 'nt'
                  else pl.BlockSpec((tk, tn), lambda i, j, k: (k, j)))
    dims = _DIMS[mode]

    def body(a_ref, b_ref, o_ref, *scr):
        p = lax.dot_general(a_ref[...], b_ref[...], dims, preferred_element_type=F32)
        if nk == 1:
            o_ref[...] = p.astype(o_ref.dtype)
        else:
            acc = scr[0]
            k = pl.program_id(2)

            @pl.when(k == 0)
            def _():
                acc[...] = p

            @pl.when(k > 0)
            def _():
                acc[...] += p

            @pl.when(k == nk - 1)
            def _():
                o_ref[...] = acc[...].astype(o_ref.dtype)

    return pl.pallas_call(
        body, grid=(nm, nn, nk), in_specs=[a_spec, b_spec],
        out_specs=pl.BlockSpec((tm, tn), lambda i, j, k: (i, j)),
        out_shape=jax.ShapeDtypeStruct((M, N), out_dtype),
        scratch_shapes=[pltpu.VMEM((tm, tn), F32)] if nk > 1 else [],
        compiler_params=_cparams(("parallel", "parallel", "arbitrary")), name=name)(a, b)


def _ffn_in(h, w_in, *, T, D, F, name):
    tm, tn = _tile(T, 512), _tile(F, 1408)
    nf = F // tn

    def body(h_ref, wg_ref, wu_ref, gu_ref, a_ref):
        hh = h_ref[...]
        g = jnp.dot(hh, wg_ref[...], preferred_element_type=F32)
        u = jnp.dot(hh, wu_ref[...], preferred_element_type=F32)
        gu_ref[0] = g.astype(BF16)
        gu_ref[1] = u.astype(BF16)
        a_ref[...] = (g * _sigmoid(g) * u).astype(BF16)

    return pl.pallas_call(
        body, grid=(nf, T // tm),
        in_specs=[pl.BlockSpec((tm, D), lambda j, i: (i, 0)),
                  pl.BlockSpec((D, tn), lambda j, i: (0, j)),
                  pl.BlockSpec((D, tn), lambda j, i: (0, nf + j))],
        out_specs=[pl.BlockSpec((2, tm, tn), lambda j, i: (0, i, j)),
                   pl.BlockSpec((tm, tn), lambda j, i: (i, j))],
        out_shape=[jax.ShapeDtypeStruct((2, T, F), BF16), jax.ShapeDtypeStruct((T, F), BF16)],
        compiler_params=_cparams(("parallel", "parallel")), name=name)(h, w_in, w_in)


def _ffn_dact(df, w_out, gu, *, T, D, F, name):
    tm, tn = _tile(T, 512), _tile(F, 1408)

    def body(df_ref, w_ref, gu_ref, o_ref):
        da = lax.dot_general(df_ref[...], w_ref[...], _DIMS['nt'], preferred_element_type=F32)
        g = gu_ref[0].astype(F32)
        u = gu_ref[1].astype(F32)
        s = _sigmoid(g)
        o_ref[0] = (da * u * (s * (1.0 + g * (1.0 - s)))).astype(BF16)
        o_ref[1] = (da * (g * s)).astype(BF16)

    return pl.pallas_call(
        body, grid=(F // tn, T // tm),
        in_specs=[pl.BlockSpec((tm, D), lambda j, i: (i, 0)),
                  pl.BlockSpec((tn, D), lambda j, i: (j, 0)),
                  pl.BlockSpec((2, tm, tn), lambda j, i: (0, i, j))],
        out_specs=pl.BlockSpec((2, tm, tn), lambda j, i: (0, i, j)),
        out_shape=jax.ShapeDtypeStruct((2, T, F), BF16),
        compiler_params=_cparams(("parallel", "parallel")), name=name)(df, w_out, gu)


def _rowwise(fn, *, T, tm, name, tiled=(), prev=(), nxt=(), consts=(), out_tiled=(), out_acc=(), scratch=()):
    n = T // tm
    assert n * tm == T and tm % HALO == 0
    hb = tm // HALO
    in_specs = [pl.BlockSpec((tm, a.shape[1]), lambda i: (i, 0)) for a in tiled]
    in_specs += [pl.BlockSpec((HALO, a.shape[1]), lambda i: (jnp.maximum(i * hb - 1, 0), 0)) for a in prev]
    in_specs += [pl.BlockSpec((HALO, a.shape[1]), lambda i: (jnp.minimum((i + 1) * hb, T // HALO - 1), 0))
                 for a in nxt]
    in_specs += [pl.BlockSpec(a.shape, lambda i: (0, 0)) for a in consts]
    out_shape = [jax.ShapeDtypeStruct((T, c), dt) for c, dt in out_tiled]
    out_shape += [jax.ShapeDtypeStruct(s, F32) for s in out_acc]
    out_specs = [pl.BlockSpec((tm, c), lambda i: (i, 0)) for c, _ in out_tiled]
    out_specs += [pl.BlockSpec(s, lambda i: (0, 0)) for s in out_acc]
    nt, npv, nnx, nc, not_, na = len(tiled), len(prev), len(nxt), len(consts), len(out_tiled), len(out_acc)

    def body(*refs):
        pos = 0
        groups = []
        for cnt in (nt, npv, nnx, nc, not_, na, len(scratch)):
            groups.append(refs[pos:pos + cnt])
            pos += cnt
        t_r, p_r, n_r, c_r, o_r, a_r, s_r = groups
        i = pl.program_id(0)

        @pl.when(i == 0)
        def _():
            for r in a_r:
                r[...] = jnp.zeros_like(r)

        outs = fn(i, n, [r[...] for r in t_r], [r[...] for r in p_r], [r[...] for r in n_r],
                  [r[...] for r in c_r], a_r, s_r)
        for r, v in zip(o_r, outs):
            r[...] = v.astype(r.dtype)

    res = pl.pallas_call(
        body, grid=(n,), in_specs=in_specs, out_specs=out_specs, out_shape=out_shape,
        scratch_shapes=list(scratch), compiler_params=_cparams(("arbitrary",)), name=name,
    )(*tiled, *prev, *nxt, *consts)
    return res


def _colsum(v):
    return jnp.sum(v, axis=0, keepdims=True)


def _rowmean(v):
    return jnp.mean(v, axis=-1, keepdims=True)


def _pre_fwd(x, gains, mod, *, T, s, name):
    def fn(i, n, t, p, nx, c, acc, scr):
        xv, (g, m) = t[0], c
        g_pre, shift, scale = g[2 * s:2 * s + 1], m[3 * s:3 * s + 1], m[3 * s + 1:3 * s + 2]
        r = lax.rsqrt(_rowmean(xv * xv) + RMS_EPS)
        return [((xv * r) * g_pre) * (1.0 + scale) + shift]

    return _rowwise(fn, T=T, tm=_tile(T, 512, HALO), name=name, tiled=[x], consts=[gains, mod],
                    out_tiled=[(x.shape[1], BF16)])[0]


def _post_fwd(x, f, gains, mod, *, T, s, res_w, name):
    def fn(i, n, t, p, nx, c, acc, scr):
        (xv, fv), (g, m) = t, c
        g_post, gate = g[2 * s + 1:2 * s + 2], m[3 * s + 2:3 * s + 3]
        y = (fv * lax.rsqrt(_rowmean(fv * fv) + RMS_EPS)) * g_post
        return [xv + (res_w * (1.0 + gate)) * y]

    return _rowwise(fn, T=T, tm=_tile(T, 512, HALO), name=name, tiled=[x, f], consts=[gains, mod],
                    out_tiled=[(x.shape[1], F32)])[0]


def _post_fwd_loss(x, f, target, gains, mod, *, T, s, res_w, name):
    D = x.shape[1]

    def fn(i, n, t, p, nx, c, acc, scr):
        (xv, fv, tv), (g, m) = t, c
        g_post, gate = g[2 * s + 1:2 * s + 2], m[3 * s + 2:3 * s + 3]
        y = (fv * lax.rsqrt(_rowmean(fv * fv) + RMS_EPS)) * g_post
        err = (xv + (res_w * (1.0 + gate)) * y) - tv
        acc[0][...] += _colsum(err * err)
        return [err * (1.0 / D)]

    dout, sq = _rowwise(fn, T=T, tm=_tile(T, 512, HALO), name=name, tiled=[x, f, target], consts=[gains, mod],
                        out_tiled=[(D, F32)], out_acc=[(1, D)])
    return dout, sq


def _post_bwd(dout, f, gains, mod, *, T, s, res_w, name):
    D = f.shape[1]

    def fn(i, n, t, p, nx, c, acc, scr):
        (dv, fv), (g, m) = t, c
        g_post, gate = g[2 * s + 1:2 * s + 2], m[3 * s + 2:3 * s + 3]
        r2 = lax.rsqrt(_rowmean(fv * fv) + RMS_EPS)
        fh = fv * r2
        dy = dv * (res_w * (1.0 + gate))
        acc[0][...] += _colsum(dv * (res_w * (fh * g_post)))
        acc[1][...] += _colsum(dy * fh)
        gy = dy * g_post
        return [r2 * (gy - fh * _rowmean(gy * fh))]

    return _rowwise(fn, T=T, tm=_tile(T, 512, HALO), name=name, tiled=[dout, f], consts=[gains, mod],
                    out_tiled=[(D, BF16)], out_acc=[(1, D), (1, D)])


def _pre_bwd(dh, x, dout, gains, mod, *, T, s, name):
    D = x.shape[1]

    def fn(i, n, t, p, nx, c, acc, scr):
        (dhv, xv, dv), (g, m) = t, c
        g_pre, scale = g[2 * s:2 * s + 1], m[3 * s + 1:3 * s + 2]
        r = lax.rsqrt(_rowmean(xv * xv) + RMS_EPS)
        nv = xv * r
        acc[0][...] += _colsum(dhv)
        acc[1][...] += _colsum(dhv * (nv * g_pre))
        acc[2][...] += _colsum(dhv * ((1.0 + scale) * nv))
        gn = dhv * (g_pre * (1.0 + scale))
        return [r * (gn - nv * _rowmean(gn * nv)) + dv]

    return _rowwise(fn, T=T, tm=_tile(T, 512, HALO), name=name, tiled=[dh, x, dout], consts=[gains, mod],
                    out_tiled=[(D, F32)], out_acc=[(1, D), (1, D), (1, D)])


def _glu(cvg, C):
    return cvg[:, :C] * _sigmoid(cvg[:, C:])


def _conv_taps(ext_ref, w, tm, off):
    acc = None
    for k in range(CONV_KERNEL):
        term = w[k:k + 1] * ext_ref[pl.ds(off(k), tm), :]
        acc = term if acc is None else acc + term
    return acc


def _conv_norm(ext_ref, cw, cb, tm):
    yc = _conv_taps(ext_ref, cw, tm, lambda k: HALO - (CONV_KERNEL - 1) + k) + cb
    mu = _rowmean(yc)
    d = yc - mu
    rstd = lax.rsqrt(_rowmean(d * d) + LN_EPS)
    return d * rstd, rstd


def _conv_fwd(cvg, cw, cvec, *, T, C, name):
    tm = _tile(T, 512, HALO)

    def fn(i, n, t, p, nx, c, acc, scr):
        ext = scr[0]
        ext[pl.ds(0, HALO), :] = jnp.where(i == 0, 0.0, _glu(p[0], C))
        ext[pl.ds(HALO, tm), :] = _glu(t[0], C)
        yh, _ = _conv_norm(ext, c[0], c[1][0:1], tm)
        zz = yh * c[1][1:2] + c[1][2:3]
        return [zz * _sigmoid(zz)]

    return _rowwise(fn, T=T, tm=tm, name=name, tiled=[cvg], prev=[cvg], consts=[cw, cvec],
                    out_tiled=[(C, BF16)], scratch=[pltpu.VMEM((HALO + tm, C), F32)])[0]


def _conv_bwd1(cvg, duc, cw, cvec, *, T, C, name):
    tm = _tile(T, 512, HALO)

    def fn(i, n, t, p, nx, c, acc, scr):
        ext = scr[0]
        ext[pl.ds(0, HALO), :] = jnp.where(i == 0, 0.0, _glu(p[0], C))
        ext[pl.ds(HALO, tm), :] = _glu(t[0], C)
        yh, rstd = _conv_norm(ext, c[0], c[1][0:1], tm)
        ln_g = c[1][1:2]
        zz = yh * ln_g + c[1][2:3]
        s = _sigmoid(zz)
        dz = t[1] * (s * (1.0 + zz * (1.0 - s)))
        dyh = dz * ln_g
        dyc = rstd * (dyh - _rowmean(dyh) - yh * _rowmean(dyh * yh))
        acc[0][0:1, :] += _colsum(dyc)
        acc[0][1:2, :] += _colsum(dz * yh)
        acc[0][2:3, :] += _colsum(dz)
        for k in range(CONV_KERNEL):
            acc[1][k:k + 1, :] += _colsum(dyc * ext[pl.ds(HALO - (CONV_KERNEL - 1) + k, tm), :])
        return [dyc]

    return _rowwise(fn, T=T, tm=tm, name=name, tiled=[cvg, duc], prev=[cvg], consts=[cw, cvec],
                    out_tiled=[(C, F32)], out_acc=[(8, C), (HALO, C)],
                    scratch=[pltpu.VMEM((HALO + tm, C), F32)])


def _conv_bwd2(dyc, cvg, cw, *, T, C, name):
    tm = _tile(T, 512, HALO)

    def fn(i, n, t, p, nx, c, acc, scr):
        ext = scr[0]
        ext[pl.ds(0, tm), :] = t[0]
        ext[pl.ds(tm, HALO), :] = jnp.where(i == n - 1, 0.0, nx[0])
        dug = _conv_taps(ext, c[0], tm, lambda k: (CONV_KERNEL - 1) - k)
        cv, cg = t[1][:, :C], t[1][:, C:]
        s = _sigmoid(cg)
        return [dug * s, dug * cv * (s * (1.0 - s))]

    return _rowwise(fn, T=T, tm=tm, name=name, tiled=[dyc, cvg], nxt=[dyc], consts=[cw],
                    out_tiled=[(C, BF16), (C, BF16)], scratch=[pltpu.VMEM((tm + HALO, C), F32)])


def _split(v):
    hi = v.astype(BF16)
    return hi, (v - hi.astype(F32)).astype(BF16)


def _dot2(v, m):
    hi, lo = _split(v)
    return jnp.dot(hi, m, preferred_element_type=F32) + jnp.dot(lo, m, preferred_element_type=F32)


def _log_gap(z):
    return -(jnp.maximum(z, 0.0) + jnp.log(1.0 + jnp.exp(-jnp.abs(z))))


def _head_masks():
    lane = lax.broadcasted_iota(jnp.int32, (1, LANES), 1)
    return lane < HEAD_DIM, lane >= HEAD_DIM


LOG_WEIGHT_FLOOR = -110.0


def _key_norm_bound(k_ref, masks, T):
    ch = _tile(T, 512)

    def chunk(r, m):
        kk = k_ref[pl.ds(pl.multiple_of(r * ch, ch), ch), :].astype(F32)
        k2 = kk * kk
        return tuple(jnp.maximum(m[h], jnp.max(jnp.sum(jnp.where(masks[h], k2, 0.0), -1, keepdims=True),
                                               axis=0, keepdims=True)) for h in (0, 1))

    m0, m1 = lax.fori_loop(0, T // ch, chunk, (jnp.zeros((1, 1), F32), jnp.zeros((1, 1), F32)))
    row = lax.broadcasted_iota(jnp.int32, (8, LANES), 0)
    return jnp.where(row == 0, jnp.sqrt(m0), jnp.sqrt(m1))


def _score_bound(qh, kn):
    qf = qh.astype(F32)
    return jnp.sqrt(jnp.sum(qf * qf, -1, keepdims=True)) * (kn * 1.01) + 0.01


def _some_weight_left(carries, bounds):
    m = jnp.maximum(jnp.max(carries[0] + bounds[0]), jnp.max(carries[1] + bounds[1]))
    return m > LOG_WEIGHT_FLOOR


def _attn_fwd(qkv, g_attn, *, T, AW, name):
    P = AW // LANES
    tq = _tile(T, 256)
    nq = T // tq
    scale = HEAD_DIM ** -0.5

    def body(q_ref, k_ref, v_ref, g_ref, o_ref, a_ref, kn_ref):
        i = pl.program_id(1)
        lo_mask, hi_mask = masks = _head_masks()

        @pl.when(i == 0)
        def _():
            kn_ref[...] = _key_norm_bound(k_ref, masks, T)

        rows = lax.broadcasted_iota(jnp.int32, (tq, tq), 0)
        cols = lax.broadcasted_iota(jnp.int32, (tq, tq), 1)
        strict = cols < rows
        tri = jnp.where(rows >= cols, 1.0, 0.0).astype(BF16)
        q = q_ref[...]
        qhs = [jnp.where(m, q, jnp.zeros_like(q)) * jnp.asarray(scale, BF16) for m in masks]
        zbs = [_score_bound(qhs[h], kn_ref[h:h + 1, 0:1]) for h in (0, 1)]

        def block(j, carry, masked):
            st = pl.multiple_of(j * tq, tq)
            kj = k_ref[pl.ds(st, tq), :]
            vj = v_ref[pl.ds(st, tq), :]
            new = []
            for h in (0, 1):
                acc, c = carry[h]
                z = lax.dot_general(qhs[h], kj, _DIMS['nt'], preferred_element_type=F32)
                l = _log_gap(z)
                if masked:
                    l = jnp.where(strict, l, 0.0)
                cum = _dot2(l, tri)
                w = jnp.exp(z + cum + c)
                if masked:
                    w = jnp.where(strict, w, 0.0)
                new.append((acc + _dot2(w, vj), c + cum[:, 0:1]))
            return tuple(new)

        zero = (jnp.zeros((tq, LANES), F32), jnp.zeros((tq, 1), F32))
        carry = block(i, (zero, zero), True)

        def live(st):
            jj, cr = st
            return jnp.logical_and(jj < i, _some_weight_left([cr[0][1], cr[1][1]], zbs))

        _, carry = lax.while_loop(live, lambda st: (st[0] + 1, block(i - 1 - st[0], st[1], False)),
                                  (jnp.int32(0), carry))
        o = jnp.where(lo_mask, carry[0][0], carry[1][0])
        o2 = o * o
        r0 = lax.rsqrt(jnp.sum(jnp.where(lo_mask, o2, 0.0), -1, keepdims=True) * (1.0 / HEAD_DIM) + RMS_EPS)
        r1 = lax.rsqrt(jnp.sum(jnp.where(hi_mask, o2, 0.0), -1, keepdims=True) * (1.0 / HEAD_DIM) + RMS_EPS)
        o_ref[...] = o
        a_ref[...] = ((o * jnp.where(lo_mask, r0, r1)) * g_ref[...]).astype(BF16)

    return pl.pallas_call(
        body, grid=(P, nq),
        in_specs=[pl.BlockSpec((tq, LANES), lambda p, i: (i, p)),
                  pl.BlockSpec((T, LANES), lambda p, i: (0, P + p)),
                  pl.BlockSpec((T, LANES), lambda p, i: (0, 2 * P + p)),
                  pl.BlockSpec((1, LANES), lambda p, i: (0, p))],
        out_specs=[pl.BlockSpec((tq, LANES), lambda p, i: (i, p)),
                   pl.BlockSpec((tq, LANES), lambda p, i: (i, p))],
        out_shape=[jax.ShapeDtypeStruct((T, AW), F32), jax.ShapeDtypeStruct((T, AW), BF16)],
        scratch_shapes=[pltpu.VMEM((8, LANES), F32)],
        compiler_params=_cparams(("parallel", "arbitrary")), name=name)(qkv, qkv, qkv, g_attn)


def _attn_bwd(qkv, o, da, g_attn, *, T, AW, name):
    P = AW // LANES
    tq = _tile(T, 256)
    nq = T // tq
    scale = HEAD_DIM ** -0.5

    def body(q_ref, k_ref, v_ref, o_ref, da_ref, g_ref, dq_ref, dk_ref, dv_ref, dg_ref, kn_ref):
        i = pl.program_id(1)
        lo_mask, hi_mask = masks = _head_masks()

        @pl.when(i == 0)
        def _():
            dk_ref[...] = jnp.zeros_like(dk_ref)
            dv_ref[...] = jnp.zeros_like(dv_ref)
            dg_ref[...] = jnp.zeros_like(dg_ref)
            kn_ref[...] = _key_norm_bound(k_ref, masks, T)

        rows = lax.broadcasted_iota(jnp.int32, (tq, tq), 0)
        cols = lax.broadcasted_iota(jnp.int32, (tq, tq), 1)
        strict = cols < rows
        tri = jnp.where(rows >= cols, 1.0, 0.0).astype(BF16)
        tri_s = jnp.where(rows > cols, 1.0, 0.0).astype(BF16)
        q = q_ref[...]
        o = o_ref[...]
        da = da_ref[...]
        g = g_ref[...]
        o2 = o * o
        r0 = lax.rsqrt(jnp.sum(jnp.where(lo_mask, o2, 0.0), -1, keepdims=True) * (1.0 / HEAD_DIM) + RMS_EPS)
        r1 = lax.rsqrt(jnp.sum(jnp.where(hi_mask, o2, 0.0), -1, keepdims=True) * (1.0 / HEAD_DIM) + RMS_EPS)
        r = jnp.where(lo_mask, r0, r1)
        oh = o * r
        gy = da * g
        gyo = gy * oh
        m0 = jnp.sum(jnp.where(lo_mask, gyo, 0.0), -1, keepdims=True) * (1.0 / HEAD_DIM)
        m1 = jnp.sum(jnp.where(hi_mask, gyo, 0.0), -1, keepdims=True) * (1.0 / HEAD_DIM)
        do = r * (gy - oh * jnp.where(lo_mask, m0, m1))
        dg_ref[...] += _colsum(da * oh)

        qhs = [jnp.where(m, q, jnp.zeros_like(q)) * jnp.asarray(scale, BF16) for m in masks]
        zbs = [_score_bound(qhs[h], kn_ref[h:h + 1, 0:1]) for h in (0, 1)]
        do_bs = [jnp.where(m, do, 0.0).astype(BF16) for m in masks]
        deltas = [jnp.sum(d.astype(F32) * o, -1, keepdims=True) for d in do_bs]
        q_ts = [qh.astype(F32).T.astype(BF16) for qh in qhs]
        do_ts = [d.astype(F32).T.astype(BF16) for d in do_bs]

        def block(j, carry, masked):
            st = pl.multiple_of(j * tq, tq)
            kj = k_ref[pl.ds(st, tq), :]
            vj = v_ref[pl.ds(st, tq), :]
            new = []
            dk = dv = None
            for h in (0, 1):
                dq, c, gsum = carry[h]
                z = lax.dot_general(qhs[h], kj, _DIMS['nt'], preferred_element_type=F32)
                l = _log_gap(z)
                sig = jnp.exp(z + l)
                if masked:
                    l = jnp.where(strict, l, 0.0)
                cum = _dot2(l, tri)
                w = jnp.exp(z + cum + c)
                if masked:
                    w = jnp.where(strict, w, 0.0)
                dp = lax.dot_general(do_bs[h], vj, _DIMS['nt'], preferred_element_type=F32)
                pw = w * dp
                after = _dot2(pw, tri_s)
                dz = pw - sig * (deltas[h] - gsum - after)
                if masked:
                    dz = jnp.where(strict, dz, 0.0)
                dz_b = dz.astype(BF16)
                dk_h = jnp.dot(q_ts[h], dz_b, preferred_element_type=F32)
                dv_h = jnp.dot(do_ts[h], w.astype(BF16), preferred_element_type=F32)
                dk = dk_h if dk is None else dk + dk_h
                dv = dv_h if dv is None else dv + dv_h
                dq = dq + jnp.dot(dz_b, kj, preferred_element_type=F32)
                new.append((dq, c + cum[:, 0:1], gsum + (after[:, 0:1] + pw[:, 0:1])))
            dk_ref[j] += dk
            dv_ref[j] += dv
            return tuple(new)

        zero1 = jnp.zeros((tq, 1), F32)
        zero = (jnp.zeros((tq, LANES), F32), zero1, zero1)
        carry = block(i, (zero, zero), True)

        def live(st):
            jj, cr = st
            return jnp.logical_and(jj < i, _some_weight_left([cr[0][1], cr[1][1]], zbs))

        _, carry = lax.while_loop(live, lambda st: (st[0] + 1, block(i - 1 - st[0], st[1], False)),
                                  (jnp.int32(0), carry))
        dq_ref[...] = (jnp.where(lo_mask, carry[0][0], carry[1][0]) * scale).astype(BF16)

    return pl.pallas_call(
        body, grid=(P, nq),
        in_specs=[pl.BlockSpec((tq, LANES), lambda p, i: (i, p)),
                  pl.BlockSpec((T, LANES), lambda p, i: (0, P + p)),
                  pl.BlockSpec((T, LANES), lambda p, i: (0, 2 * P + p)),
                  pl.BlockSpec((tq, LANES), lambda p, i: (i, p)),
                  pl.BlockSpec((tq, LANES), lambda p, i: (i, p)),
                  pl.BlockSpec((1, LANES), lambda p, i: (0, p))],
        out_specs=[pl.BlockSpec((tq, LANES), lambda p, i: (i, p)),
                   pl.BlockSpec((None, nq, LANES, tq), lambda p, i: (p, 0, 0, 0)),
                   pl.BlockSpec((None, nq, LANES, tq), lambda p, i: (p, 0, 0, 0)),
                   pl.BlockSpec((1, LANES), lambda p, i: (0, p))],
        out_shape=[jax.ShapeDtypeStruct((T, AW), BF16),
                   jax.ShapeDtypeStruct((P, nq, LANES, tq), F32),
                   jax.ShapeDtypeStruct((P, nq, LANES, tq), F32),
                   jax.ShapeDtypeStruct((1, AW), F32)],
        scratch_shapes=[pltpu.VMEM((8, LANES), F32)],
        compiler_params=_cparams(("parallel", "arbitrary")), name=name)(qkv, qkv, qkv, o, da, g_attn)


def _ada_fwd(c_all, w_ada, b_ada, *, name):
    def body(c_ref, w_ref, b_ref, o_ref):
        cv = c_ref[...]
        sc = cv * _sigmoid(cv)
        o_ref[...] = jnp.dot(sc, w_ref[...], preferred_element_type=F32,
                             precision=lax.Precision.HIGHEST) + b_ref[...]

    return pl.pallas_call(body, out_shape=jax.ShapeDtypeStruct((c_all.shape[0], w_ada.shape[1]), F32),
                          compiler_params=_cparams(), name=name)(c_all, w_ada, b_ada)


def _ada_bwd(c_all_t, dmod, *, name):
    def body(c_ref, d_ref, o_ref):
        cv = c_ref[...]
        sc = cv * _sigmoid(cv)
        o_ref[...] = jnp.dot(sc, d_ref[...], preferred_element_type=F32, precision=lax.Precision.HIGHEST)

    return pl.pallas_call(body, out_shape=jax.ShapeDtypeStruct((c_all_t.shape[0], dmod.shape[1]), F32),
                          compiler_params=_cparams(), name=name)(c_all_t, dmod)


def _adamw(w, g, m, v, *, name):
    R, C = w.shape
    tr = _tile(R, max(8, (1 << 18) // C), 8)

    def body(w_ref, g_ref, m_ref, v_ref, d_ref, nm_ref, nv_ref):
        gv = g_ref[...]
        m2 = ADAM_B1 * m_ref[...] + (1.0 - ADAM_B1) * gv
        v2 = ADAM_B2 * v_ref[...] + (1.0 - ADAM_B2) * jnp.square(gv)
        m_hat = m2 / (1.0 - ADAM_B1 ** ADAM_STEP)
        v_hat = v2 / (1.0 - ADAM_B2 ** ADAM_STEP)
        d_ref[...] = -ADAM_LR * (m_hat / (jnp.sqrt(v_hat) + ADAM_EPS) + ADAM_WD * w_ref[...])
        nm_ref[...] = m2
        nv_ref[...] = v2

    spec = pl.BlockSpec((tr, C), lambda i: (i, 0))
    return pl.pallas_call(
        body, grid=(R // tr,), in_specs=[spec] * 4, out_specs=[spec] * 3,
        out_shape=[jax.ShapeDtypeStruct((R, C), F32)] * 3,
        compiler_params=_cparams(("parallel",)), name=name)(w, g, m, v)


def _sum_devices(a, *, name):
    def body(a_ref, o_ref):
        s = a_ref[0]
        for d in range(1, a_ref.shape[0]):
            s = s + a_ref[d]
        o_ref[...] = s

    return pl.pallas_call(body, out_shape=jax.ShapeDtypeStruct(a.shape[1:], F32),
                          compiler_params=_cparams(), name=name)(a)


def _place():
    return lax.axis_index("x"), lax.axis_index("y"), lax.axis_index("c")


def _flip(v, bit):
    return 1 - v if bit else v


def _allgather8(blk, *, name):
    R, C = blk.shape

    def body(x_ref, out_ref, send_sems, recv_sems):
        x, y, c = _place()
        me = 4 * x + 2 * y + c
        out_ref[me] = x_ref[...]
        copies = []
        for k in range(1, 8):
            peer = (_flip(x, (k >> 2) & 1), _flip(y, (k >> 1) & 1), _flip(c, k & 1))
            cp = pltpu.make_async_remote_copy(
                src_ref=x_ref, dst_ref=out_ref.at[me], send_sem=send_sems.at[k - 1],
                recv_sem=recv_sems.at[k - 1], device_id=peer, device_id_type=MESH)
            cp.start()
            copies.append(cp)
        for cp in copies:
            cp.wait()

    return pl.pallas_call(
        body, out_shape=jax.ShapeDtypeStruct((8, R, C), F32),
        in_specs=[pl.BlockSpec(memory_space=pltpu.VMEM)], out_specs=pl.BlockSpec(memory_space=pltpu.VMEM),
        scratch_shapes=[pltpu.SemaphoreType.DMA((7,)), pltpu.SemaphoreType.DMA((7,))],
        compiler_params=_cparams(), name=name)(blk)


def _aligned(v, m):
    return v if isinstance(v, int) else pl.multiple_of(v, m)


def _rows_half(ref, half):
    n = ref.shape[0] // 2
    return ref.at[pl.ds(_aligned(half * n, 16), n)]


def _region(ref, kind, slot, half):
    if kind == 'col':
        n, cs = ref.shape[0] // 2, ref.shape[1] // 4
        return ref.at[pl.ds(_aligned(half * n, 16), n), pl.ds(_aligned(slot * cs, LANES), cs)]
    rs = ref.shape[0] // 4
    return ref.at[pl.ds(_aligned(slot * rs + half * (rs // 2), 16), rs // 2)]


def _other_chips(x, y):
    return [(1 - x, y), (x, 1 - y), (1 - x, 1 - y)]


def _gather_weights(shards, kinds, *, name):
    nw = len(shards)
    full_shapes = []
    for s, kind in zip(shards, kinds):
        full_shapes.append((s.shape[0], 4 * s.shape[1]) if kind == 'col' else (4 * s.shape[0], s.shape[1]))

    def body(*refs):
        sh, full = refs[:nw], refs[nw:2 * nw]
        lsem, ssem, rsem, fssem, frsem = refs[2 * nw:]
        x, y, c = _place()
        me_slot = 2 * x + y
        chips = _other_chips(x, y)
        local, sends = [], []
        for w in range(nw):
            for h in (0, 1):
                cp = pltpu.make_async_copy(_rows_half(sh[w], h), _region(full[w], kinds[w], me_slot, h),
                                           lsem.at[w, h])
                cp.start()
                local.append(cp)
            for r, (px, py) in enumerate(chips):
                cp = pltpu.make_async_remote_copy(
                    src_ref=_rows_half(sh[w], c), dst_ref=_region(full[w], kinds[w], me_slot, c),
                    send_sem=ssem.at[w, r], recv_sem=rsem.at[w, r], device_id=(px, py, c), device_id_type=MESH)
                cp.start()
                sends.append(cp)
        for w in range(nw):
            for r, (px, py) in enumerate(chips):
                landed = _region(full[w], kinds[w], 2 * px + py, c)
                pltpu.make_async_remote_copy(
                    src_ref=landed, dst_ref=landed, send_sem=ssem.at[w, r], recv_sem=rsem.at[w, r],
                    device_id=(px, py, c), device_id_type=MESH).wait_recv()
                cp = pltpu.make_async_remote_copy(
                    src_ref=landed, dst_ref=landed, send_sem=fssem.at[w, r], recv_sem=frsem.at[w, r],
                    device_id=(x, y, 1 - c), device_id_type=MESH)
                cp.start()
                sends.append(cp)
        for w in range(nw):
            for r, (px, py) in enumerate(chips):
                passed = _region(full[w], kinds[w], 2 * px + py, 1 - c)
                pltpu.make_async_remote_copy(
                    src_ref=passed, dst_ref=passed, send_sem=fssem.at[w, r], recv_sem=frsem.at[w, r],
                    device_id=(x, y, 1 - c), device_id_type=MESH).wait_recv()
        for cp in sends:
            cp.wait_send()
        for cp in local:
            cp.wait()

    anyspec = pl.BlockSpec(memory_space=pl.ANY)
    return pl.pallas_call(
        body, out_shape=[jax.ShapeDtypeStruct(s, BF16) for s in full_shapes],
        in_specs=[anyspec] * nw, out_specs=[anyspec] * nw,
        scratch_shapes=[pltpu.SemaphoreType.DMA((nw, 2))] + [pltpu.SemaphoreType.DMA((nw, 3))] * 4,
        compiler_params=_cparams(), name=name)(*shards)


def _exchange_core_halves(grads, kinds, *, name):
    nw = len(grads)

    def body(*refs):
        g, r1 = refs[:nw], refs[nw:2 * nw]
        ssem, rsem = refs[2 * nw:]
        x, y, c = _place()
        copies = []
        for w in range(nw):
            for slot in range(4):
                cp = pltpu.make_async_remote_copy(
                    src_ref=_region(g[w], kinds[w], slot, 1 - c), dst_ref=_region(r1[w], kinds[w], slot, 1 - c),
                    send_sem=ssem.at[w, slot], recv_sem=rsem.at[w, slot], device_id=(x, y, 1 - c),
                    device_id_type=MESH)
                cp.start()
                copies.append(cp)
        for w in range(nw):
            for slot in range(4):
                mine = _region(r1[w], kinds[w], slot, c)
                pltpu.make_async_remote_copy(
                    src_ref=mine, dst_ref=mine, send_sem=ssem.at[w, slot], recv_sem=rsem.at[w, slot],
                    device_id=(x, y, 1 - c), device_id_type=MESH).wait_recv()
        for cp in copies:
            cp.wait_send()

    anyspec = pl.BlockSpec(memory_space=pl.ANY)
    return pl.pallas_call(
        body, out_shape=[jax.ShapeDtypeStruct(g.shape, F32) for g in grads],
        in_specs=[anyspec] * nw, out_specs=[anyspec] * nw,
        scratch_shapes=[pltpu.SemaphoreType.DMA((nw, 4))] * 2,
        compiler_params=_cparams(), name=name)(*grads)


def _add_core_halves(g, r1, place, kind, *, name):
    if kind == 'col':
        n, cs = g.shape[0] // 2, g.shape[1] // 4
        tr = _tile(n, 256, 16)
        nt = n // tr
        ispec = pl.BlockSpec((tr, cs), lambda s, t, pr: (pr[0] * nt + t, s))
    else:
        rs, cs = g.shape[0] // 4, g.shape[1]
        n = rs // 2
        tr, nt = n, 1
        ispec = pl.BlockSpec((tr, cs), lambda s, t, pr: (s * 2 + pr[0], 0))

    def body(pr, a_ref, b_ref, o_ref):
        o_ref[...] = (a_ref[...] + b_ref[...]).astype(BF16)

    return pl.pallas_call(
        body,
        grid_spec=pltpu.PrefetchScalarGridSpec(
            num_scalar_prefetch=1, grid=(4, nt), in_specs=[ispec, ispec],
            out_specs=pl.BlockSpec((None, tr, cs), lambda s, t, pr: (s, t, 0))),
        out_shape=jax.ShapeDtypeStruct((4, n, cs), BF16),
        compiler_params=_cparams(("parallel", "parallel")), name=name)(place, g, r1)


def _scatter_to_owners(hs, *, name):
    nw = len(hs)

    def body(*refs):
        h, r2 = refs[:nw], refs[nw:2 * nw]
        ssem, rsem = refs[2 * nw:]
        x, y, c = _place()
        chips = _other_chips(x, y)
        copies = []
        for w in range(nw):
            for r, (px, py) in enumerate(chips):
                cp = pltpu.make_async_remote_copy(
                    src_ref=h[w].at[2 * px + py], dst_ref=r2[w].at[r], send_sem=ssem.at[w, r],
                    recv_sem=rsem.at[w, r], device_id=(px, py, c), device_id_type=MESH)
                cp.start()
                copies.append(cp)
        for cp in copies:
            cp.wait()

    anyspec = pl.BlockSpec(memory_space=pl.ANY)
    return pl.pallas_call(
        body, out_shape=[jax.ShapeDtypeStruct((3,) + a.shape[1:], a.dtype) for a in hs],
        in_specs=[anyspec] * nw, out_specs=[anyspec] * nw,
        scratch_shapes=[pltpu.SemaphoreType.DMA((nw, 3))] * 2,
        compiler_params=_cparams(), name=name)(*hs)


def _sum_owner(hs, r2, place, *, name):
    _, n, cs = hs.shape
    tr = _tile(n, 256, 16)
    nt = n // tr

    def body(pr, h_ref, r_ref, o_ref):
        o_ref[...] = ((h_ref[...].astype(F32) + r_ref[0].astype(F32)) + r_ref[1].astype(F32)) + r_ref[2].astype(F32)

    return pl.pallas_call(
        body,
        grid_spec=pltpu.PrefetchScalarGridSpec(
            num_scalar_prefetch=1, grid=(nt,),
            in_specs=[pl.BlockSpec((None, tr, cs), lambda t, pr: (pr[1], t, 0)),
                      pl.BlockSpec((3, tr, cs), lambda t, pr: (0, t, 0))],
            out_specs=pl.BlockSpec((None, tr, cs), lambda t, pr: (pr[0], t, 0))),
        out_shape=jax.ShapeDtypeStruct((2, n, cs), F32),
        compiler_params=_cparams(("parallel",)), name=name)(place, hs, r2)


def _share_with_sibling(fins, *, name):
    nw = len(fins)

    def body(*refs):
        fin, out = refs[:nw], refs[nw:2 * nw]
        ssem, rsem = refs[2 * nw:]
        x, y, c = _place()
        copies = []
        for w in range(nw):
            cp = pltpu.make_async_remote_copy(
                src_ref=fin[w].at[c], dst_ref=out[w].at[c], send_sem=ssem.at[w], recv_sem=rsem.at[w],
                device_id=(x, y, 1 - c), device_id_type=MESH)
            cp.start()
            copies.append(cp)
        for w in range(nw):
            theirs = out[w].at[1 - c]
            pltpu.make_async_remote_copy(
                src_ref=theirs, dst_ref=theirs, send_sem=ssem.at[w], recv_sem=rsem.at[w],
                device_id=(x, y, 1 - c), device_id_type=MESH).wait_recv()
        for cp in copies:
            cp.wait_send()

    anyspec = pl.BlockSpec(memory_space=pl.ANY)
    return pl.pallas_call(
        body, out_shape=[jax.ShapeDtypeStruct(a.shape, F32) for a in fins],
        in_specs=[anyspec] * nw, out_specs=[anyspec] * nw,
        input_output_aliases={w: w for w in range(nw)},
        scratch_shapes=[pltpu.SemaphoreType.DMA((nw,))] * 2,
        compiler_params=_cparams(), name=name)(*fins)


def _local_step(x, target, mod, gains, wfull, g_attn, conv_w, cvec):
    T, D = x.shape
    F = wfull['ff1_w_out'].shape[0]
    AW = D // 2
    C = D - AW
    NQKV = 3 * AW
    MIX = NQKV + 2 * C
    tM = _tile(T, 1024)

    def ffn_fwd(xin, s, w_in, w_out, tag):
        h = _pre_fwd(xin, gains, mod, T=T, s=s, name=f"pre_fwd_{tag}")
        gu, act = _ffn_in(h, w_in, T=T, D=D, F=F, name=f"ffn_in_{tag}")
        f = _matmul(act, w_out, mode='nn', M=T, N=D, K=F, tm=tM, tn=_tile(D, 1024), tk=_tile(F, 1408),
                    out_dtype=F32, name=f"ffn_out_{tag}")
        return h, gu, act, f

    def ffn_bwd(dout, xin, saved, s, w_in, w_out, res_w, tag):
        h, gu, act, f = saved
        df, dgate, dgpost = _post_bwd(dout, f, gains, mod, T=T, s=s, res_w=res_w, name=f"post_bwd_{tag}")
        dgu = _ffn_dact(df, w_out, gu, T=T, D=D, F=F, name=f"ffn_dact_{tag}")
        dw_out = _matmul(act, df, mode='tn', M=F, N=D, K=T, tm=_tile(F, 1408), tn=_tile(D, 1024),
                         tk=_tile(T, 512), out_dtype=F32, name=f"dw_out_{tag}")
        tk = _tile(F, 1408)
        kf = F // tk
        tn = _tile(D, 1024)
        dh = _matmul(dgu, w_in, mode='nt', M=T, N=D, K=2 * F, tm=tM, tn=tn, tk=tk, out_dtype=F32,
                     a_spec=pl.BlockSpec((None, tM, tk), lambda i, j, k: (k // kf, i, k % kf)),
                     name=f"dh_{tag}")
        tnf = _tile(F, 1408)
        nf = F // tnf
        tkt = _tile(T, 512)
        dw_in = _matmul(h, dgu, mode='tn', M=D, N=2 * F, K=T, tm=_tile(D, 1024), tn=tnf, tk=tkt, out_dtype=F32,
                        b_spec=pl.BlockSpec((None, tkt, tnf), lambda i, j, k: (j // nf, k, j % nf)),
                        name=f"dw_in_{tag}")
        dx, dshift, dscale, dgpre = _pre_bwd(dh, xin, dout, gains, mod, T=T, s=s, name=f"pre_bwd_{tag}")
        return dx, dw_in, dw_out, (dshift, dscale, dgate), dgpre, dgpost

    s1 = ffn_fwd(x, 0, wfull['ff1_w_in'], wfull['ff1_w_out'], "ff1")
    x1 = _post_fwd(x, s1[3], gains, mod, T=T, s=0, res_w=0.5, name="post_fwd_ff1")

    h2 = _pre_fwd(x1, gains, mod, T=T, s=1, name="pre_fwd_mix")
    w_in_mix, w_out_mix = wfull['w_in_mix'], wfull['w_out_mix']
    tnq = _tile(AW, 512)
    qkv = _matmul(h2, w_in_mix, mode='nn', M=T, N=NQKV, K=D, tm=tM, tn=tnq, tk=D, out_dtype=BF16, name="proj_qkv")
    tnc = _tile(C, 512)
    off = NQKV // tnc
    cvg = _matmul(h2, w_in_mix, mode='nn', M=T, N=2 * C, K=D, tm=tM, tn=tnc, tk=D, out_dtype=F32,
                  b_spec=pl.BlockSpec((D, tnc), lambda i, j, k: (0, off + j)), name="proj_conv")
    o_attn, a_attn = _attn_fwd(qkv, g_attn, T=T, AW=AW, name="attn_fwd")
    uc = _conv_fwd(cvg, conv_w, cvec, T=T, C=C, name="conv_fwd")
    mixcat = jnp.concatenate([a_attn, uc], axis=1)
    f_mix = _matmul(mixcat, w_out_mix, mode='nn', M=T, N=D, K=D, tm=tM, tn=_tile(D, 1024), tk=D, out_dtype=F32,
                    name="mix_out")
    x2 = _post_fwd(x1, f_mix, gains, mod, T=T, s=1, res_w=1.0, name="post_fwd_mix")

    s3 = ffn_fwd(x2, 2, wfull['ff2_w_in'], wfull['ff2_w_out'], "ff2")
    dout, sq = _post_fwd_loss(x2, s3[3], target, gains, mod, T=T, s=2, res_w=0.5, name="post_fwd_loss")

    dx2, dw_in2, dw_out2, dmod2, dgpre2, dgpost2 = ffn_bwd(
        dout, x2, s3, 2, wfull['ff2_w_in'], wfull['ff2_w_out'], 0.5, "ff2")

    df_mix, dgate_m, dgpost_m = _post_bwd(dx2, f_mix, gains, mod, T=T, s=1, res_w=1.0, name="post_bwd_mix")
    dmixcat = _matmul(df_mix, w_out_mix, mode='nt', M=T, N=D, K=D, tm=tM, tn=_tile(D, 1024), tk=D, out_dtype=F32,
                      name="d_mixcat")
    dw_out_mix = _matmul(mixcat, df_mix, mode='tn', M=D, N=D, K=T, tm=_tile(D, 1024), tn=_tile(D, 1024),
                         tk=_tile(T, 512), out_dtype=F32, name="dw_out_mix")
    da_attn, duc = dmixcat[:, :AW], dmixcat[:, AW:]
    dq, dk_t, dv_t, dg_attn = _attn_bwd(qkv, o_attn, da_attn, g_attn, T=T, AW=AW, name="attn_bwd")
    dk = jnp.transpose(dk_t, (1, 3, 0, 2)).reshape(T, AW).astype(BF16)
    dv = jnp.transpose(dv_t, (1, 3, 0, 2)).reshape(T, AW).astype(BF16)
    dyc, csum, dconv_w = _conv_bwd1(cvg, duc, conv_w, cvec, T=T, C=C, name="conv_bwd1")
    dcv, dcg = _conv_bwd2(dyc, cvg, conv_w, T=T, C=C, name="conv_bwd2")
    dproj = jnp.concatenate([dq, dk, dv, dcv, dcg], axis=1)
    tkm = _tile(MIX, 1280)
    dh2 = _matmul(dproj, w_in_mix, mode='nt', M=T, N=D, K=MIX, tm=tM, tn=_tile(D, 1024), tk=tkm, out_dtype=F32,
                  name="dh_mix")
    dw_in_mix = _matmul(h2, dproj, mode='tn', M=D, N=MIX, K=T, tm=_tile(D, 1024), tn=_tile(MIX, 1280),
                        tk=_tile(T, 512), out_dtype=F32, name="dw_in_mix")
    dx1, dshift_m, dscale_m, dgpre_m = _pre_bwd(dh2, x1, dx2, gains, mod, T=T, s=1, name="pre_bwd_mix")

    dx0, dw_in1, dw_out1, dmod1, dgpre1, dgpost1 = ffn_bwd(
        dx1, x, s1, 0, wfull['ff1_w_in'], wfull['ff1_w_out'], 0.5, "ff1")

    big = {'ff1_w_in': dw_in1, 'ff1_w_out': dw_out1, 'w_in_mix': dw_in_mix, 'w_out_mix': dw_out_mix,
           'ff2_w_in': dw_in2, 'ff2_w_out': dw_out2}
    dgains = [dgpre1, dgpost1, dgpre_m, dgpost_m, dgpre2, dgpost2]
    dmod = list(dmod1) + [dshift_m, dscale_m, dgate_m] + list(dmod2)
    return sq, dx0, big, dgains, dmod, dg_attn, csum, dconv_w


def _pack_rows(pieces, width):
    rows = jnp.concatenate([p.reshape(-1) for p in pieces]).reshape(-1, width)
    pad = (-rows.shape[0]) % 8
    return jnp.pad(rows, ((0, pad), (0, 0)))


def kernel(x, c, w_ada, b_ada, g_pre_ff1, g_post_ff1, ff1_w_in, ff1_w_out, g_pre_mix, g_post_mix, w_in_mix, g_attn_out, conv_w, conv_b, conv_ln_g, conv_ln_b, w_out_mix, g_pre_ff2, g_post_ff2, ff2_w_in, ff2_w_out, loss_target, m_w_ada, m_b_ada, m_g_pre_ff1, m_g_post_ff1, m_ff1_w_in, m_ff1_w_out, m_g_pre_mix, m_g_post_mix, m_w_in_mix, m_g_attn_out, m_conv_w, m_conv_b, m_conv_ln_g, m_conv_ln_b, m_w_out_mix, m_g_pre_ff2, m_g_post_ff2, m_ff2_w_in, m_ff2_w_out, v_w_ada, v_b_ada, v_g_pre_ff1, v_g_post_ff1, v_ff1_w_in, v_ff1_w_out, v_g_pre_mix, v_g_post_mix, v_w_in_mix, v_g_attn_out, v_conv_w, v_conv_b, v_conv_ln_g, v_conv_ln_b, v_w_out_mix, v_g_pre_ff2, v_g_post_ff2, v_ff2_w_in, v_ff2_w_out):
    W = dict(w_ada=w_ada, b_ada=b_ada, g_pre_ff1=g_pre_ff1, g_post_ff1=g_post_ff1, ff1_w_in=ff1_w_in,
             ff1_w_out=ff1_w_out, g_pre_mix=g_pre_mix, g_post_mix=g_post_mix, w_in_mix=w_in_mix,
             g_attn_out=g_attn_out, conv_w=conv_w, conv_b=conv_b, conv_ln_g=conv_ln_g, conv_ln_b=conv_ln_b,
             w_out_mix=w_out_mix, g_pre_ff2=g_pre_ff2, g_post_ff2=g_post_ff2, ff2_w_in=ff2_w_in,
             ff2_w_out=ff2_w_out)
    Mo = dict(w_ada=m_w_ada, b_ada=m_b_ada, g_pre_ff1=m_g_pre_ff1, g_post_ff1=m_g_post_ff1, ff1_w_in=m_ff1_w_in,
              ff1_w_out=m_ff1_w_out, g_pre_mix=m_g_pre_mix, g_post_mix=m_g_post_mix, w_in_mix=m_w_in_mix,
              g_attn_out=m_g_attn_out, conv_w=m_conv_w, conv_b=m_conv_b, conv_ln_g=m_conv_ln_g,
              conv_ln_b=m_conv_ln_b, w_out_mix=m_w_out_mix, g_pre_ff2=m_g_pre_ff2, g_post_ff2=m_g_post_ff2,
              ff2_w_in=m_ff2_w_in, ff2_w_out=m_ff2_w_out)
    Vo = dict(w_ada=v_w_ada, b_ada=v_b_ada, g_pre_ff1=v_g_pre_ff1, g_post_ff1=v_g_post_ff1, ff1_w_in=v_ff1_w_in,
              ff1_w_out=v_ff1_w_out, g_pre_mix=v_g_pre_mix, g_post_mix=v_g_post_mix, w_in_mix=v_w_in_mix,
              g_attn_out=v_g_attn_out, conv_w=v_conv_w, conv_b=v_conv_b, conv_ln_g=v_conv_ln_g,
              conv_ln_b=v_conv_ln_b, w_out_mix=v_w_out_mix, g_pre_ff2=v_g_pre_ff2, g_post_ff2=v_g_post_ff2,
              ff2_w_in=v_ff2_w_in, ff2_w_out=v_ff2_w_out)

    T, D = x.shape[1], x.shape[2]
    AW = D // 2
    C = D - AW
    xi, yi, ci = _place()
    me = 4 * xi + 2 * yi + ci
    chip = 2 * xi + yi
    place = jnp.stack([ci, chip]).astype(jnp.int32)

    c_all = _allgather8(jnp.tile(c, (8, 1)), name="gather_c")[:, 0, :]
    ncol = w_ada.shape[1]
    b_cols = lax.dynamic_index_in_dim(b_ada.reshape(4, ncol), chip, keepdims=True).reshape(1, ncol)
    modp = _ada_fwd(c_all, w_ada, b_cols, name="ada_fwd")
    mod_g = _allgather8(modp, name="gather_mod")
    mod_all = jnp.transpose(mod_g[0::2], (1, 0, 2)).reshape(8, 4 * ncol)
    mod = lax.dynamic_index_in_dim(mod_all, me, keepdims=False).reshape(9, D)

    names = [n for n, _ in BIG]
    kinds = [k for _, k in BIG]
    full = _gather_weights([W[n].astype(BF16) for n in names], kinds, name="gather_weights")
    wfull = dict(zip(names, full))
    cs = conv_w.shape[1]
    cw_all = _allgather8(jnp.pad(conv_w, ((0, HALO - CONV_KERNEL), (0, (-cs) % LANES))), name="gather_conv_w")
    conv_w_full = jnp.transpose(cw_all[0::2, :, :cs], (1, 0, 2)).reshape(HALO, 4 * cs)

    gains = _pack_rows([g_pre_ff1, g_post_ff1, g_pre_mix, g_post_mix, g_pre_ff2, g_post_ff2], D)
    cvec = _pack_rows([conv_b, conv_ln_g, conv_ln_b], C)
    g_attn = g_attn_out.reshape(1, AW)

    sq, dx, big, dgains, dmod, dg_attn, csum, dconv_w = _local_step(
        x[0], loss_target[0], mod, gains, wfull, g_attn, conv_w_full, cvec)

    loss_row = jnp.zeros((1, D), F32).at[0, 0].set(jnp.sum(sq) * (0.5 / D))
    small = _pack_rows(dgains + dmod + [dg_attn, csum[0:3], dconv_w, loss_row], D)
    small_all = _allgather8(small, name="gather_small")
    tot = _sum_devices(small_all, name="sum_small")
    n_g, n_m = 6, 9
    r0 = n_g + n_m
    flat = tot.reshape(-1)
    p = r0 * D
    g_attn_grad = flat[p:p + AW]
    p += AW
    gconv_b, gln_g, gln_b = flat[p:p + C], flat[p + C:p + 2 * C], flat[p + 2 * C:p + 3 * C]
    p += 3 * C
    gconv_w_full = flat[p:p + HALO * C].reshape(HALO, C)[:CONV_KERNEL]
    p += HALO * C
    loss = flat[p]
    gconv_w = lax.dynamic_slice_in_dim(gconv_w_full, chip * cs, cs, axis=1)
    grad_small = {'g_pre_ff1': tot[0], 'g_post_ff1': tot[1], 'g_pre_mix': tot[2], 'g_post_mix': tot[3],
                  'g_pre_ff2': tot[4], 'g_post_ff2': tot[5], 'b_ada': tot[n_g:r0].reshape(-1),
                  'g_attn_out': g_attn_grad.reshape(g_attn_out.shape), 'conv_w': gconv_w, 'conv_b': gconv_b,
                  'conv_ln_g': gln_g, 'conv_ln_b': gln_b}

    dmod_all = small_all[:, n_g:r0, :].reshape(8, 9 * D)
    dmod_cols = lax.dynamic_slice_in_dim(dmod_all, chip * ncol, ncol, axis=1)
    grad_w_ada = _ada_bwd(jnp.transpose(c_all), dmod_cols, name="ada_bwd")

    glist = [big[n] for n in names]
    r1 = _exchange_core_halves(glist, kinds, name="grad_core_exchange")
    hs = [_add_core_halves(g, r, place, k, name=f"grad_core_add_{n}") for g, r, k, n in zip(glist, r1, kinds, names)]
    r2 = _scatter_to_owners(hs, name="grad_scatter")
    fins = [_sum_owner(h, r, place, name=f"grad_owner_sum_{n}") for h, r, n in zip(hs, r2, names)]
    shared = _share_with_sibling(fins, name="grad_share")
    grads = dict(grad_small)
    grads['w_ada'] = grad_w_ada
    for n, a in zip(names, shared):
        grads[n] = a.reshape(W[n].shape)

    delta, new_m, new_v = {}, {}, {}
    for n in ['w_ada'] + names:
        delta[n], new_m[n], new_v[n] = _adamw(W[n], grads[n], Mo[n], Vo[n], name=f"adamw_{n}")
    smalls = [n for n in WEIGHTS if n not in delta]
    sizes = [W[n].size for n in smalls]
    tot_sz = sum(sizes)
    padn = (-tot_sz) % (8 * LANES)

    def pack(d):
        return jnp.pad(jnp.concatenate([d[n].reshape(-1) for n in smalls]), (0, padn)).reshape(-1, LANES)

    d_s, m_s, v_s = _adamw(pack(W), pack(grads), pack(Mo), pack(Vo), name="adamw_small")
    pos = 0
    for n, sz in zip(smalls, sizes):
        for dst, src in ((delta, d_s), (new_m, m_s), (new_v, v_s)):
            dst[n] = src.reshape(-1)[pos:pos + sz].reshape(W[n].shape)
        pos += sz

    return (loss, dx[None], *[grads[n] for n in WEIGHTS], *[delta[n] for n in WEIGHTS],
            *[new_m[n] for n in WEIGHTS], *[new_v[n] for n in WEIGHTS])
```

```python
import functools

import jax
import jax.numpy as jnp
from jax import lax
from jax.experimental import pallas as pl
from jax.experimental.pallas import tpu as pltpu

F32 = jnp.float32
BF16 = jnp.bfloat16
MESH = pl.DeviceIdType.MESH

HEAD_DIM = 64
CONV_KERNEL = 31
RMS_EPS = 1e-6
LN_EPS = 1e-5
ADAM_LR = 0.001
ADAM_B1 = 0.9
ADAM_B2 = 0.999
ADAM_EPS = 1e-08
ADAM_WD = 0.01
ADAM_STEP = 10

LANES = 128
HALO = 32
VMEM_LIMIT = 52 * 1024 * 1024

WEIGHTS = ['w_ada', 'b_ada', 'g_pre_ff1', 'g_post_ff1', 'ff1_w_in', 'ff1_w_out', 'g_pre_mix',
           'g_post_mix', 'w_in_mix', 'g_attn_out', 'conv_w', 'conv_b', 'conv_ln_g', 'conv_ln_b',
           'w_out_mix', 'g_pre_ff2', 'g_post_ff2', 'ff2_w_in', 'ff2_w_out']
BIG = [('ff1_w_in', 'col'), ('ff1_w_out', 'row'), ('w_in_mix', 'col'), ('w_out_mix', 'row'),
       ('ff2_w_in', 'col'), ('ff2_w_out', 'row')]


def _tile(dim, pref, mult=LANES):
    if dim <= pref:
        return dim
    best = None
    for t in range(mult, pref + 1, mult):
        if dim % t == 0:
            best = t
    assert best is not None, (dim, pref, mult)
    return best


def _cparams(sem=None):
    kw = dict(vmem_limit_bytes=VMEM_LIMIT)
    if sem is not None:
        kw['dimension_semantics'] = sem
    return pltpu.CompilerParams(**kw)


def _sigmoid(x):
    return 1.0 / (1.0 + jnp.exp(-x))


_DIMS = {'nn': (((1,), (0,)), ((), ())), 'nt': (((1,), (1,)), ((), ())), 'tn': (((0,), (0,)), ((), ()))}


def _matmul(a, b, *, mode, M, N, K, tm, tn, tk, out_dtype, name, a_spec=None, b_spec=None):
    nm, nn, nk = M // tm, N // tn, K // tk
    assert nm * tm == M and nn * tn == N and nk * tk == K, (name, M, N, K, tm, tn, tk)
    if a_spec is None:
        a_spec = (pl.BlockSpec((tk, tm), lambda i, j, k: (k, i)) if mode == 'tn'
                  else pl.BlockSpec((tm, tk), lambda i, j, k: (i, k)))
    if b_spec is None:
        b_spec = (pl.BlockSpec((tn, tk), lambda i, j, k: (j, k)) if mode == 'nt'
                  else pl.BlockSpec((tk, tn), lambda i, j, k: (k, j)))
    dims = _DIMS[mode]
    assert nk == 1 or out_dtype == F32, name

    def body(a_ref, b_ref, o_ref):
        def prod():
            return lax.dot_general(a_ref[...], b_ref[...], dims, preferred_element_type=F32)

        if nk == 1:
            o_ref[...] = prod().astype(o_ref.dtype)
        else:
            k = pl.program_id(2)

            @pl.when(k == 0)
            def _():
                o_ref[...] = prod()

            @pl.when(k > 0)
            def _():
                o_ref[...] += prod()

    return pl.pallas_call(
        body, grid=(nm, nn, nk), in_specs=[a_spec, b_spec],
        out_specs=pl.BlockSpec((tm, tn), lambda i, j, k: (i, j)),
        out_shape=jax.ShapeDtypeStruct((M, N), out_dtype),
        compiler_params=_cparams(("parallel", "parallel", "arbitrary")), name=name)(a, b)


def _ffn_in(h, w_in, *, T, D, F, name):
    tm, tn = _tile(T, 512), _tile(F, 1408)
    nf = F // tn

    def body(h_ref, wg_ref, wu_ref, jac_ref, a_ref):
        hh = h_ref[...]
        g = jnp.dot(hh, wg_ref[...], preferred_element_type=F32)
        u = jnp.dot(hh, wu_ref[...], preferred_element_type=F32)
        s = _sigmoid(g)
        sg = g * s
        jac_ref[0] = (u * (s * (1.0 + g * (1.0 - s)))).astype(BF16)
        jac_ref[1] = sg.astype(BF16)
        a_ref[...] = (sg * u).astype(BF16)

    return pl.pallas_call(
        body, grid=(nf, T // tm),
        in_specs=[pl.BlockSpec((tm, D), lambda j, i: (i, 0)),
                  pl.BlockSpec((D, tn), lambda j, i: (0, j)),
                  pl.BlockSpec((D, tn), lambda j, i: (0, nf + j))],
        out_specs=[pl.BlockSpec((2, tm, tn), lambda j, i: (0, i, j)),
                   pl.BlockSpec((tm, tn), lambda j, i: (i, j))],
        out_shape=[jax.ShapeDtypeStruct((2, T, F), BF16), jax.ShapeDtypeStruct((T, F), BF16)],
        compiler_params=_cparams(("parallel", "parallel")), name=name)(h, w_in, w_in)


def _ffn_dact(df, w_out, jac, *, T, D, F, name):
    tm, tn = _tile(T, 512), _tile(F, 1408)

    def body(df_ref, w_ref, jac_ref, o_ref):
        da = lax.dot_general(df_ref[...], w_ref[...], _DIMS['nt'], preferred_element_type=F32)
        o_ref[0] = (da * jac_ref[0].astype(F32)).astype(BF16)
        o_ref[1] = (da * jac_ref[1].astype(F32)).astype(BF16)

    return pl.pallas_call(
        body, grid=(F // tn, T // tm),
        in_specs=[pl.BlockSpec((tm, D), lambda j, i: (i, 0)),
                  pl.BlockSpec((tn, D), lambda j, i: (j, 0)),
                  pl.BlockSpec((2, tm, tn), lambda j, i: (0, i, j))],
        out_specs=pl.BlockSpec((2, tm, tn), lambda j, i: (0, i, j)),
        out_shape=jax.ShapeDtypeStruct((2, T, F), BF16),
        compiler_params=_cparams(("parallel", "parallel")), name=name)(df, w_out, jac)


def _rowwise(fn, *, T, tm, name, tiled=(), prev=(), nxt=(), consts=(), out_tiled=(), out_acc=(), scratch=(),
             by_ref=False):
    n = T // tm
    assert n * tm == T and tm % HALO == 0
    hb = tm // HALO
    in_specs = [pl.BlockSpec((tm, a.shape[1]), lambda i: (i, 0)) for a in tiled]
    in_specs += [pl.BlockSpec((HALO, a.shape[1]), lambda i: (jnp.maximum(i * hb - 1, 0), 0)) for a in prev]
    in_specs += [pl.BlockSpec((HALO, a.shape[1]), lambda i: (jnp.minimum((i + 1) * hb, T // HALO - 1), 0))
                 for a in nxt]
    in_specs += [pl.BlockSpec(a.shape, lambda i: (0, 0)) for a in consts]
    out_shape = [jax.ShapeDtypeStruct((T, c), dt) for c, dt in out_tiled]
    out_shape += [jax.ShapeDtypeStruct(s, F32) for s in out_acc]
    out_specs = [pl.BlockSpec((tm, c), lambda i: (i, 0)) for c, _ in out_tiled]
    out_specs += [pl.BlockSpec(s, lambda i: (0, 0)) for s in out_acc]
    nt, npv, nnx, nc, not_, na = len(tiled), len(prev), len(nxt), len(consts), len(out_tiled), len(out_acc)

    def body(*refs):
        pos = 0
        groups = []
        for cnt in (nt, npv, nnx, nc, not_, na, len(scratch)):
            groups.append(refs[pos:pos + cnt])
            pos += cnt
        t_r, p_r, n_r, c_r, o_r, a_r, s_r = groups
        i = pl.program_id(0)

        @pl.when(i == 0)
        def _():
            for r in a_r:
                r[...] = jnp.zeros_like(r)

        if by_ref:
            fn(i, n, t_r, p_r, n_r, c_r, o_r, a_r, s_r)
            return
        outs = fn(i, n, [r[...] for r in t_r], [r[...] for r in p_r], [r[...] for r in n_r],
                  [r[...] for r in c_r], a_r, s_r)
        for r, v in zip(o_r, outs):
            r[...] = v.astype(r.dtype)

    res = pl.pallas_call(
        body, grid=(n,), in_specs=in_specs, out_specs=out_specs, out_shape=out_shape,
        scratch_shapes=list(scratch), compiler_params=_cparams(("arbitrary",)), name=name,
    )(*tiled, *prev, *nxt, *consts)
    return res


def _colsum(v):
    return jnp.sum(v, axis=0, keepdims=True)


def _rowmean(v):
    return jnp.mean(v, axis=-1, keepdims=True)


def _pre_fwd(x, gains, mod, *, T, s, name):
    def fn(i, n, t, p, nx, c, acc, scr):
        xv, (g, m) = t[0], c
        g_pre, shift, scale = g[2 * s:2 * s + 1], m[3 * s:3 * s + 1], m[3 * s + 1:3 * s + 2]
        r = lax.rsqrt(_rowmean(xv * xv) + RMS_EPS)
        return [((xv * r) * g_pre) * (1.0 + scale) + shift]

    return _rowwise(fn, T=T, tm=_tile(T, 512, HALO), name=name, tiled=[x], consts=[gains, mod],
                    out_tiled=[(x.shape[1], BF16)])[0]


def _post_fwd(x, f, gains, mod, *, T, s, res_w, name):
    def fn(i, n, t, p, nx, c, acc, scr):
        (xv, fv), (g, m) = t, c
        g_post, gate = g[2 * s + 1:2 * s + 2], m[3 * s + 2:3 * s + 3]
        y = (fv * lax.rsqrt(_rowmean(fv * fv) + RMS_EPS)) * g_post
        return [xv + (res_w * (1.0 + gate)) * y]

    return _rowwise(fn, T=T, tm=_tile(T, 512, HALO), name=name, tiled=[x, f], consts=[gains, mod],
                    out_tiled=[(x.shape[1], F32)])[0]


def _post_fwd_loss(x, f, target, gains, mod, *, T, s, res_w, name):
    D = x.shape[1]

    def fn(i, n, t, p, nx, c, acc, scr):
        (xv, fv, tv), (g, m) = t, c
        g_post, gate = g[2 * s + 1:2 * s + 2], m[3 * s + 2:3 * s + 3]
        y = (fv * lax.rsqrt(_rowmean(fv * fv) + RMS_EPS)) * g_post
        err = (xv + (res_w * (1.0 + gate)) * y) - tv
        acc[0][...] += _colsum(err * err)
        return [err * (1.0 / D)]

    dout, sq = _rowwise(fn, T=T, tm=_tile(T, 512, HALO), name=name, tiled=[x, f, target], consts=[gains, mod],
                        out_tiled=[(D, F32)], out_acc=[(1, D)])
    return dout, sq


def _post_bwd(dout, f, gains, mod, *, T, s, res_w, name):
    D = f.shape[1]

    def fn(i, n, t, p, nx, c, acc, scr):
        (dv, fv), (g, m) = t, c
        g_post, gate = g[2 * s + 1:2 * s + 2], m[3 * s + 2:3 * s + 3]
        r2 = lax.rsqrt(_rowmean(fv * fv) + RMS_EPS)
        fh = fv * r2
        dy = dv * (res_w * (1.0 + gate))
        acc[0][...] += _colsum(dv * (res_w * (fh * g_post)))
        acc[1][...] += _colsum(dy * fh)
        gy = dy * g_post
        return [r2 * (gy - fh * _rowmean(gy * fh))]

    return _rowwise(fn, T=T, tm=_tile(T, 512, HALO), name=name, tiled=[dout, f], consts=[gains, mod],
                    out_tiled=[(D, BF16)], out_acc=[(1, D), (1, D)])


def _pre_bwd(dh, x, dout, gains, mod, *, T, s, name):
    D = x.shape[1]

    def fn(i, n, t, p, nx, c, acc, scr):
        (dhv, xv, dv), (g, m) = t, c
        g_pre, scale = g[2 * s:2 * s + 1], m[3 * s + 1:3 * s + 2]
        r = lax.rsqrt(_rowmean(xv * xv) + RMS_EPS)
        nv = xv * r
        acc[0][...] += _colsum(dhv)
        acc[1][...] += _colsum(dhv * (nv * g_pre))
        acc[2][...] += _colsum(dhv * ((1.0 + scale) * nv))
        gn = dhv * (g_pre * (1.0 + scale))
        return [r * (gn - nv * _rowmean(gn * nv)) + dv]

    return _rowwise(fn, T=T, tm=_tile(T, 512, HALO), name=name, tiled=[dh, x, dout], consts=[gains, mod],
                    out_tiled=[(D, F32)], out_acc=[(1, D), (1, D), (1, D)])


SUBLANES = 8
CONV_CHUNK = 64


def _glu(cvg, C):
    return cvg[:, :C] * _sigmoid(cvg[:, C:])


def _fill_rotations(ext, rot, rows):
    for r in range(SUBLANES):
        rot[r] = ext[pl.ds(r, rows), :]


def _conv_taps(rot, w, r0, rows, off):
    acc = None
    for k in range(CONV_KERNEL):
        a, r = divmod(off(k), SUBLANES)
        term = w[k:k + 1] * rot[r, pl.ds(pl.multiple_of(r0 + a * SUBLANES, SUBLANES), rows), :]
        acc = term if acc is None else acc + term
    return acc


def _causal_off(k):
    return HALO - (CONV_KERNEL - 1) + k


def _conv_norm(rot, cw, cb, r0, rows):
    yc = _conv_taps(rot, cw, r0, rows, _causal_off) + cb
    mu = _rowmean(yc)
    d = yc - mu
    rstd = lax.rsqrt(_rowmean(d * d) + LN_EPS)
    return d * rstd, rstd


def _stage_glu(i, t, p, ext, rot, tm, C):
    ext[pl.ds(0, HALO), :] = jnp.where(i == 0, 0.0, _glu(p[0][...], C))
    ext[pl.ds(HALO, tm), :] = _glu(t[0][...], C)
    ext[pl.ds(HALO + tm, SUBLANES), :] = jnp.zeros((SUBLANES, C), F32)
    _fill_rotations(ext, rot, tm + HALO)


def _conv_scratch(tm, C):
    return [pltpu.VMEM((HALO + tm + SUBLANES, C), F32), pltpu.VMEM((SUBLANES, HALO + tm, C), F32)]


def _conv_fwd(cvg, cw, cvec, *, T, C, name):
    tm = _tile(T, 512, HALO)
    ch = min(CONV_CHUNK, tm)

    def fn(i, n, t, p, nx, c, o, acc, scr):
        ext, rot = scr
        _stage_glu(i, t, p, ext, rot, tm, C)
        w, vec = c[0][...], c[1][...]

        def chunk(ci, carry):
            r0 = pl.multiple_of(ci * ch, ch)
            yh, _ = _conv_norm(rot, w, vec[0:1], r0, ch)
            zz = yh * vec[1:2] + vec[2:3]
            o[0][pl.ds(r0, ch), :] = (zz * _sigmoid(zz)).astype(BF16)
            return carry

        lax.fori_loop(0, tm // ch, chunk, 0)

    return _rowwise(fn, T=T, tm=tm, name=name, tiled=[cvg], prev=[cvg], consts=[cw, cvec],
                    out_tiled=[(C, BF16)], scratch=_conv_scratch(tm, C), by_ref=True)[0]


def _conv_bwd1(cvg, duc, cw, cvec, *, T, C, name):
    tm = _tile(T, 512, HALO)
    ch = min(CONV_CHUNK, tm)

    def fn(i, n, t, p, nx, c, o, acc, scr):
        ext, rot, w8 = scr

        @pl.when(i == 0)
        def _():
            w8[...] = jnp.zeros_like(w8)

        _stage_glu(i, t, p, ext, rot, tm, C)
        w, vec = c[0][...], c[1][...]
        ln_g = vec[1:2]

        def chunk(ci, carry):
            r0 = pl.multiple_of(ci * ch, ch)
            yh, rstd = _conv_norm(rot, w, vec[0:1], r0, ch)
            zz = yh * ln_g + vec[2:3]
            s = _sigmoid(zz)
            dz = t[1][pl.ds(r0, ch), :] * (s * (1.0 + zz * (1.0 - s)))
            dyh = dz * ln_g
            dyc = rstd * (dyh - _rowmean(dyh) - yh * _rowmean(dyh * yh))
            o[0][pl.ds(r0, ch), :] = dyc
            acc[0][0:1, :] += _colsum(dyc)
            acc[0][1:2, :] += _colsum(dz * yh)
            acc[0][2:3, :] += _colsum(dz)
            for k in range(CONV_KERNEL):
                a, r = divmod(_causal_off(k), SUBLANES)
                prod = dyc * rot[r, pl.ds(pl.multiple_of(r0 + a * SUBLANES, SUBLANES), ch), :]
                part = prod[0:SUBLANES]
                for g in range(1, ch // SUBLANES):
                    part = part + prod[g * SUBLANES:(g + 1) * SUBLANES]
                w8[pl.ds(k * SUBLANES, SUBLANES), :] += part
            return carry

        lax.fori_loop(0, tm // ch, chunk, 0)

        @pl.when(i == n - 1)
        def _():
            for k in range(CONV_KERNEL):
                acc[1][k:k + 1, :] = _colsum(w8[pl.ds(k * SUBLANES, SUBLANES), :])

    return _rowwise(fn, T=T, tm=tm, name=name, tiled=[cvg, duc], prev=[cvg], consts=[cw, cvec],
                    out_tiled=[(C, F32)], out_acc=[(8, C), (HALO, C)],
                    scratch=_conv_scratch(tm, C) + [pltpu.VMEM((HALO * SUBLANES, C), F32)], by_ref=True)


def _conv_bwd2(dyc, cvg, cw, *, T, C, name):
    tm = _tile(T, 512, HALO)
    ch = min(CONV_CHUNK, tm)

    def fn(i, n, t, p, nx, c, o, acc, scr):
        ext, rot = scr
        ext[pl.ds(0, tm), :] = t[0][...]
        ext[pl.ds(tm, HALO), :] = jnp.where(i == n - 1, 0.0, nx[0][...])
        _fill_rotations(ext, rot, tm + HALO - SUBLANES)
        w = c[0][...]

        def chunk(ci, carry):
            r0 = pl.multiple_of(ci * ch, ch)
            dug = _conv_taps(rot, w, r0, ch, lambda k: (CONV_KERNEL - 1) - k)
            cv = t[1][pl.ds(r0, ch), pl.ds(0, C)]
            s = _sigmoid(t[1][pl.ds(r0, ch), pl.ds(C, C)])
            o[0][pl.ds(r0, ch), :] = (dug * s).astype(BF16)
            o[1][pl.ds(r0, ch), :] = (dug * cv * (s * (1.0 - s))).astype(BF16)
            return carry

        lax.fori_loop(0, tm // ch, chunk, 0)

    return _rowwise(fn, T=T, tm=tm, name=name, tiled=[dyc, cvg], nxt=[dyc], consts=[cw],
                    out_tiled=[(C, BF16), (C, BF16)],
                    scratch=[pltpu.VMEM((tm + HALO, C), F32), pltpu.VMEM((SUBLANES, tm + HALO - SUBLANES, C), F32)],
                    by_ref=True)


def _split(v):
    hi = v.astype(BF16)
    return hi, (v - hi.astype(F32)).astype(BF16)


def _dot2(v, m):
    hi, lo = _split(v)
    return jnp.dot(hi, m, preferred_element_type=F32) + jnp.dot(lo, m, preferred_element_type=F32)


def _log_gap(z):
    return -(jnp.maximum(z, 0.0) + jnp.log(1.0 + jnp.exp(-jnp.abs(z))))


def _head_masks():
    lane = lax.broadcasted_iota(jnp.int32, (1, LANES), 1)
    return lane < HEAD_DIM, lane >= HEAD_DIM


LOG_WEIGHT_FLOOR = -110.0


def _key_norm_bound(k_ref, masks, T):
    ch = _tile(T, 512)

    def chunk(r, m):
        kk = k_ref[pl.ds(pl.multiple_of(r * ch, ch), ch), :].astype(F32)
        k2 = kk * kk
        return tuple(jnp.maximum(m[h], jnp.max(jnp.sum(jnp.where(masks[h], k2, 0.0), -1, keepdims=True),
                                               axis=0, keepdims=True)) for h in (0, 1))

    m0, m1 = lax.fori_loop(0, T // ch, chunk, (jnp.zeros((1, 1), F32), jnp.zeros((1, 1), F32)))
    row = lax.broadcasted_iota(jnp.int32, (8, LANES), 0)
    return jnp.where(row == 0, jnp.sqrt(m0), jnp.sqrt(m1))


def _score_bound(qh, kn):
    qf = qh.astype(F32)
    return jnp.sqrt(jnp.sum(qf * qf, -1, keepdims=True)) * (kn * 1.01) + 0.01


def _some_weight_left(carries, bounds):
    m = jnp.maximum(jnp.max(carries[0] + bounds[0]), jnp.max(carries[1] + bounds[1]))
    return m > LOG_WEIGHT_FLOOR


def _attn_fwd(qkv, g_attn, *, T, AW, name):
    P = AW // LANES
    tq = _tile(T, 256)
    nq = T // tq
    scale = HEAD_DIM ** -0.5

    def body(q_ref, k_ref, v_ref, g_ref, o_ref, a_ref, kn_ref):
        i = pl.program_id(1)
        lo_mask, hi_mask = masks = _head_masks()

        @pl.when(i == 0)
        def _():
            kn_ref[...] = _key_norm_bound(k_ref, masks, T)

        rows = lax.broadcasted_iota(jnp.int32, (tq, tq), 0)
        cols = lax.broadcasted_iota(jnp.int32, (tq, tq), 1)
        strict = cols < rows
        tri = jnp.where(rows >= cols, 1.0, 0.0).astype(BF16)
        q = q_ref[...]
        qhs = [jnp.where(m, q, jnp.zeros_like(q)) * jnp.asarray(scale, BF16) for m in masks]
        zbs = [_score_bound(qhs[h], kn_ref[h:h + 1, 0:1]) for h in (0, 1)]

        def block(j, carry, masked):
            st = pl.multiple_of(j * tq, tq)
            kj = k_ref[pl.ds(st, tq), :]
            vj = v_ref[pl.ds(st, tq), :]
            new = []
            for h in (0, 1):
                acc, c = carry[h]
                z = lax.dot_general(qhs[h], kj, _DIMS['nt'], preferred_element_type=F32)
                l = _log_gap(z)
                if masked:
                    l = jnp.where(strict, l, 0.0)
                cum = _dot2(l, tri)
                w = jnp.exp(z + cum + c)
                if masked:
                    w = jnp.where(strict, w, 0.0)
                new.append((acc + _dot2(w, vj), c + cum[:, 0:1]))
            return tuple(new)

        zero = (jnp.zeros((tq, LANES), F32), jnp.zeros((tq, 1), F32))
        carry = block(i, (zero, zero), True)

        def live(st):
            jj, cr = st
            return jnp.logical_and(jj < i, _some_weight_left([cr[0][1], cr[1][1]], zbs))

        _, carry = lax.while_loop(live, lambda st: (st[0] + 1, block(i - 1 - st[0], st[1], False)),
                                  (jnp.int32(0), carry))
        o = jnp.where(lo_mask, carry[0][0], carry[1][0])
        o2 = o * o
        r0 = lax.rsqrt(jnp.sum(jnp.where(lo_mask, o2, 0.0), -1, keepdims=True) * (1.0 / HEAD_DIM) + RMS_EPS)
        r1 = lax.rsqrt(jnp.sum(jnp.where(hi_mask, o2, 0.0), -1, keepdims=True) * (1.0 / HEAD_DIM) + RMS_EPS)
        o_ref[...] = o
        a_ref[...] = ((o * jnp.where(lo_mask, r0, r1)) * g_ref[...]).astype(BF16)

    return pl.pallas_call(
        body, grid=(P, nq),
        in_specs=[pl.BlockSpec((tq, LANES), lambda p, i: (i, p)),
                  pl.BlockSpec((T, LANES), lambda p, i: (0, P + p)),
                  pl.BlockSpec((T, LANES), lambda p, i: (0, 2 * P + p)),
                  pl.BlockSpec((1, LANES), lambda p, i: (0, p))],
        out_specs=[pl.BlockSpec((tq, LANES), lambda p, i: (i, p)),
                   pl.BlockSpec((tq, LANES), lambda p, i: (i, p))],
        out_shape=[jax.ShapeDtypeStruct((T, AW), F32), jax.ShapeDtypeStruct((T, AW), BF16)],
        scratch_shapes=[pltpu.VMEM((8, LANES), F32)],
        compiler_params=_cparams(("parallel", "arbitrary")), name=name)(qkv, qkv, qkv, g_attn)


def _attn_bwd(qkv, o, da, g_attn, *, T, AW, name):
    P = AW // LANES
    tq = _tile(T, 256)
    nq = T // tq
    scale = HEAD_DIM ** -0.5

    def body(q_ref, k_ref, v_ref, o_ref, da_ref, g_ref, dq_ref, dk_ref, dv_ref, dg_ref, kn_ref):
        i = pl.program_id(1)
        lo_mask, hi_mask = masks = _head_masks()

        @pl.when(i == 0)
        def _():
            dk_ref[...] = jnp.zeros_like(dk_ref)
            dv_ref[...] = jnp.zeros_like(dv_ref)
            dg_ref[...] = jnp.zeros_like(dg_ref)
            kn_ref[...] = _key_norm_bound(k_ref, masks, T)

        rows = lax.broadcasted_iota(jnp.int32, (tq, tq), 0)
        cols = lax.broadcasted_iota(jnp.int32, (tq, tq), 1)
        strict = cols < rows
        tri = jnp.where(rows >= cols, 1.0, 0.0).astype(BF16)
        tri_s = jnp.where(rows > cols, 1.0, 0.0).astype(BF16)
        q = q_ref[...]
        o = o_ref[...]
        da = da_ref[...]
        g = g_ref[...]
        o2 = o * o
        r0 = lax.rsqrt(jnp.sum(jnp.where(lo_mask, o2, 0.0), -1, keepdims=True) * (1.0 / HEAD_DIM) + RMS_EPS)
        r1 = lax.rsqrt(jnp.sum(jnp.where(hi_mask, o2, 0.0), -1, keepdims=True) * (1.0 / HEAD_DIM) + RMS_EPS)
        r = jnp.where(lo_mask, r0, r1)
        oh = o * r
        gy = da * g
        gyo = gy * oh
        m0 = jnp.sum(jnp.where(lo_mask, gyo, 0.0), -1, keepdims=True) * (1.0 / HEAD_DIM)
        m1 = jnp.sum(jnp.where(hi_mask, gyo, 0.0), -1, keepdims=True) * (1.0 / HEAD_DIM)
        do = r * (gy - oh * jnp.where(lo_mask, m0, m1))
        dg_ref[...] += _colsum(da * oh)

        qhs = [jnp.where(m, q, jnp.zeros_like(q)) * jnp.asarray(scale, BF16) for m in masks]
        zbs = [_score_bound(qhs[h], kn_ref[h:h + 1, 0:1]) for h in (0, 1)]
        do_bs = [jnp.where(m, do, 0.0).astype(BF16) for m in masks]
        deltas = [jnp.sum(d.astype(F32) * o, -1, keepdims=True) for d in do_bs]
        q_ts = [qh.astype(F32).T.astype(BF16) for qh in qhs]
        do_ts = [d.astype(F32).T.astype(BF16) for d in do_bs]

        def block(j, carry, masked):
            st = pl.multiple_of(j * tq, tq)
            kj = k_ref[pl.ds(st, tq), :]
            vj = v_ref[pl.ds(st, tq), :]
            new = []
            dk = dv = None
            for h in (0, 1):
                dq, c, gsum = carry[h]
                z = lax.dot_general(qhs[h], kj, _DIMS['nt'], preferred_element_type=F32)
                l = _log_gap(z)
                sig = jnp.exp(z + l)
                if masked:
                    l = jnp.where(strict, l, 0.0)
                cum = _dot2(l, tri)
                w = jnp.exp(z + cum + c)
                if masked:
                    w = jnp.where(strict, w, 0.0)
                dp = lax.dot_general(do_bs[h], vj, _DIMS['nt'], preferred_element_type=F32)
                pw = w * dp
                after = _dot2(pw, tri_s)
                dz = pw - sig * (deltas[h] - gsum - after)
                if masked:
                    dz = jnp.where(strict, dz, 0.0)
                dz_b = dz.astype(BF16)
                dk_h = jnp.dot(q_ts[h], dz_b, preferred_element_type=F32)
                dv_h = jnp.dot(do_ts[h], w.astype(BF16), preferred_element_type=F32)
                dk = dk_h if dk is None else dk + dk_h
                dv = dv_h if dv is None else dv + dv_h
                dq = dq + jnp.dot(dz_b, kj, preferred_element_type=F32)
                new.append((dq, c + cum[:, 0:1], gsum + (after[:, 0:1] + pw[:, 0:1])))
            dk_ref[j] += dk
            dv_ref[j] += dv
            return tuple(new)

        zero1 = jnp.zeros((tq, 1), F32)
        zero = (jnp.zeros((tq, LANES), F32), zero1, zero1)
        carry = block(i, (zero, zero), True)

        def live(st):
            jj, cr = st
            return jnp.logical_and(jj < i, _some_weight_left([cr[0][1], cr[1][1]], zbs))

        _, carry = lax.while_loop(live, lambda st: (st[0] + 1, block(i - 1 - st[0], st[1], False)),
                                  (jnp.int32(0), carry))
        dq_ref[...] = (jnp.where(lo_mask, carry[0][0], carry[1][0]) * scale).astype(BF16)

    return pl.pallas_call(
        body, grid=(P, nq),
        in_specs=[pl.BlockSpec((tq, LANES), lambda p, i: (i, p)),
                  pl.BlockSpec((T, LANES), lambda p, i: (0, P + p)),
                  pl.BlockSpec((T, LANES), lambda p, i: (0, 2 * P + p)),
                  pl.BlockSpec((tq, LANES), lambda p, i: (i, p)),
                  pl.BlockSpec((tq, LANES), lambda p, i: (i, p)),
                  pl.BlockSpec((1, LANES), lambda p, i: (0, p))],
        out_specs=[pl.BlockSpec((tq, LANES), lambda p, i: (i, p)),
                   pl.BlockSpec((None, nq, LANES, tq), lambda p, i: (p, 0, 0, 0)),
                   pl.BlockSpec((None, nq, LANES, tq), lambda p, i: (p, 0, 0, 0)),
                   pl.BlockSpec((1, LANES), lambda p, i: (0, p))],
        out_shape=[jax.ShapeDtypeStruct((T, AW), BF16),
                   jax.ShapeDtypeStruct((P, nq, LANES, tq), F32),
                   jax.ShapeDtypeStruct((P, nq, LANES, tq), F32),
                   jax.ShapeDtypeStruct((1, AW), F32)],
        scratch_shapes=[pltpu.VMEM((8, LANES), F32)],
        compiler_params=_cparams(("parallel", "arbitrary")), name=name)(qkv, qkv, qkv, o, da, g_attn)


def _ada_fwd(c_all, w_ada, b_ada, *, name):
    def body(c_ref, w_ref, b_ref, o_ref):
        cv = c_ref[...]
        sc = cv * _sigmoid(cv)
        o_ref[...] = jnp.dot(sc, w_ref[...], preferred_element_type=F32,
                             precision=lax.Precision.HIGHEST) + b_ref[...]

    return pl.pallas_call(body, out_shape=jax.ShapeDtypeStruct((c_all.shape[0], w_ada.shape[1]), F32),
                          compiler_params=_cparams(), name=name)(c_all, w_ada, b_ada)


def _ada_bwd(c_all_t, dmod, *, name):
    def body(c_ref, d_ref, o_ref):
        cv = c_ref[...]
        sc = cv * _sigmoid(cv)
        o_ref[...] = jnp.dot(sc, d_ref[...], preferred_element_type=F32, precision=lax.Precision.HIGHEST)

    return pl.pallas_call(body, out_shape=jax.ShapeDtypeStruct((c_all_t.shape[0], dmod.shape[1]), F32),
                          compiler_params=_cparams(), name=name)(c_all_t, dmod)


def _adamw(w, g, m, v, *, name):
    R, C = w.shape
    tr = _tile(R, max(8, (1 << 18) // C), 8)

    def body(w_ref, g_ref, m_ref, v_ref, d_ref, nm_ref, nv_ref):
        gv = g_ref[...]
        m2 = ADAM_B1 * m_ref[...] + (1.0 - ADAM_B1) * gv
        v2 = ADAM_B2 * v_ref[...] + (1.0 - ADAM_B2) * jnp.square(gv)
        m_hat = m2 / (1.0 - ADAM_B1 ** ADAM_STEP)
        v_hat = v2 / (1.0 - ADAM_B2 ** ADAM_STEP)
        d_ref[...] = -ADAM_LR * (m_hat / (jnp.sqrt(v_hat) + ADAM_EPS) + ADAM_WD * w_ref[...])
        nm_ref[...] = m2
        nv_ref[...] = v2

    spec = pl.BlockSpec((tr, C), lambda i: (i, 0))
    return pl.pallas_call(
        body, grid=(R // tr,), in_specs=[spec] * 4, out_specs=[spec] * 3,
        out_shape=[jax.ShapeDtypeStruct((R, C), F32)] * 3,
        compiler_params=_cparams(("parallel",)), name=name)(w, g, m, v)


def _sum_devices(a, *, name):
    def body(a_ref, o_ref):
        s = a_ref[0]
        for d in range(1, a_ref.shape[0]):
            s = s + a_ref[d]
        o_ref[...] = s

    return pl.pallas_call(body, out_shape=jax.ShapeDtypeStruct(a.shape[1:], F32),
                          compiler_params=_cparams(), name=name)(a)


def _place():
    return lax.axis_index("x"), lax.axis_index("y"), lax.axis_index("c")


def _flip(v, bit):
    return 1 - v if bit else v


def _allgather8(blk, *, name):
    R, C = blk.shape

    def body(x_ref, out_ref, send_sems, recv_sems):
        x, y, c = _place()
        me = 4 * x + 2 * y + c
        out_ref[me] = x_ref[...]
        copies = []
        for k in range(1, 8):
            peer = (_flip(x, (k >> 2) & 1), _flip(y, (k >> 1) & 1), _flip(c, k & 1))
            cp = pltpu.make_async_remote_copy(
                src_ref=x_ref, dst_ref=out_ref.at[me], send_sem=send_sems.at[k - 1],
                recv_sem=recv_sems.at[k - 1], device_id=peer, device_id_type=MESH)
            cp.start()
            copies.append(cp)
        for cp in copies:
            cp.wait()

    return pl.pallas_call(
        body, out_shape=jax.ShapeDtypeStruct((8, R, C), F32),
        in_specs=[pl.BlockSpec(memory_space=pltpu.VMEM)], out_specs=pl.BlockSpec(memory_space=pltpu.VMEM),
        scratch_shapes=[pltpu.SemaphoreType.DMA((7,)), pltpu.SemaphoreType.DMA((7,))],
        compiler_params=_cparams(), name=name)(blk)


def _aligned(v, m):
    return v if isinstance(v, int) else pl.multiple_of(v, m)


def _rows_half(ref, half):
    n = ref.shape[0] // 2
    return ref.at[pl.ds(_aligned(half * n, 16), n)]


def _region(ref, kind, slot, half):
    if kind == 'col':
        n, cs = ref.shape[0] // 2, ref.shape[1] // 4
        return ref.at[pl.ds(_aligned(half * n, 16), n), pl.ds(_aligned(slot * cs, LANES), cs)]
    rs = ref.shape[0] // 4
    return ref.at[pl.ds(_aligned(slot * rs + half * (rs // 2), 16), rs // 2)]


def _other_chips(x, y):
    return [(1 - x, y), (x, 1 - y), (1 - x, 1 - y)]


def _gather_weights(shards, kinds, *, name):
    nw = len(shards)
    full_shapes = []
    for s, kind in zip(shards, kinds):
        full_shapes.append((s.shape[0], 4 * s.shape[1]) if kind == 'col' else (4 * s.shape[0], s.shape[1]))

    def body(*refs):
        sh, full = refs[:nw], refs[nw:2 * nw]
        lsem, ssem, rsem, fssem, frsem = refs[2 * nw:]
        x, y, c = _place()
        me_slot = 2 * x + y
        chips = _other_chips(x, y)
        local, sends = [], []
        for w in range(nw):
            for h in (0, 1):
                cp = pltpu.make_async_copy(_rows_half(sh[w], h), _region(full[w], kinds[w], me_slot, h),
                                           lsem.at[w, h])
                cp.start()
                local.append(cp)
            for r, (px, py) in enumerate(chips):
                cp = pltpu.make_async_remote_copy(
                    src_ref=_rows_half(sh[w], c), dst_ref=_region(full[w], kinds[w], me_slot, c),
                    send_sem=ssem.at[w, r], recv_sem=rsem.at[w, r], device_id=(px, py, c), device_id_type=MESH)
                cp.start()
                sends.append(cp)
        for w in range(nw):
            for r, (px, py) in enumerate(chips):
                landed = _region(full[w], kinds[w], 2 * px + py, c)
                pltpu.make_async_remote_copy(
                    src_ref=landed, dst_ref=landed, send_sem=ssem.at[w, r], recv_sem=rsem.at[w, r],
                    device_id=(px, py, c), device_id_type=MESH).wait_recv()
                cp = pltpu.make_async_remote_copy(
                    src_ref=landed, dst_ref=landed, send_sem=fssem.at[w, r], recv_sem=frsem.at[w, r],
                    device_id=(x, y, 1 - c), device_id_type=MESH)
                cp.start()
                sends.append(cp)
        for w in range(nw):
            for r, (px, py) in enumerate(chips):
                passed = _region(full[w], kinds[w], 2 * px + py, 1 - c)
                pltpu.make_async_remote_copy(
                    src_ref=passed, dst_ref=passed, send_sem=fssem.at[w, r], recv_sem=frsem.at[w, r],
                    device_id=(x, y, 1 - c), device_id_type=MESH).wait_recv()
        for cp in sends:
            cp.wait_send()
        for cp in local:
            cp.wait()

    anyspec = pl.BlockSpec(memory_space=pl.ANY)
    return pl.pallas_call(
        body, out_shape=[jax.ShapeDtypeStruct(s, BF16) for s in full_shapes],
        in_specs=[anyspec] * nw, out_specs=[anyspec] * nw,
        scratch_shapes=[pltpu.SemaphoreType.DMA((nw, 2))] + [pltpu.SemaphoreType.DMA((nw, 3))] * 4,
        compiler_params=_cparams(), name=name)(*shards)


def _exchange_core_halves(grads, kinds, *, name):
    nw = len(grads)

    def body(*refs):
        g, r1 = refs[:nw], refs[nw:2 * nw]
        ssem, rsem = refs[2 * nw:]
        x, y, c = _place()
        copies = []
        for w in range(nw):
            for slot in range(4):
                cp = pltpu.make_async_remote_copy(
                    src_ref=_region(g[w], kinds[w], slot, 1 - c), dst_ref=_region(r1[w], kinds[w], slot, 1 - c),
                    send_sem=ssem.at[w, slot], recv_sem=rsem.at[w, slot], device_id=(x, y, 1 - c),
                    device_id_type=MESH)
                cp.start()
                copies.append(cp)
        for w in range(nw):
            for slot in range(4):
                mine = _region(r1[w], kinds[w], slot, c)
                pltpu.make_async_remote_copy(
                    src_ref=mine, dst_ref=mine, send_sem=ssem.at[w, slot], recv_sem=rsem.at[w, slot],
                    device_id=(x, y, 1 - c), device_id_type=MESH).wait_recv()
        for cp in copies:
            cp.wait_send()

    anyspec = pl.BlockSpec(memory_space=pl.ANY)
    return pl.pallas_call(
        body, out_shape=[jax.ShapeDtypeStruct(g.shape, F32) for g in grads],
        in_specs=[anyspec] * nw, out_specs=[anyspec] * nw,
        scratch_shapes=[pltpu.SemaphoreType.DMA((nw, 4))] * 2,
        compiler_params=_cparams(), name=name)(*grads)


def _add_core_halves(g, r1, place, kind, *, name):
    if kind == 'col':
        n, cs = g.shape[0] // 2, g.shape[1] // 4
        tr = _tile(n, 256, 16)
        nt = n // tr
        ispec = pl.BlockSpec((tr, cs), lambda s, t, pr: (pr[0] * nt + t, s))
    else:
        rs, cs = g.shape[0] // 4, g.shape[1]
        n = rs // 2
        tr, nt = n, 1
        ispec = pl.BlockSpec((tr, cs), lambda s, t, pr: (s * 2 + pr[0], 0))

    def body(pr, a_ref, b_ref, o_ref):
        o_ref[...] = (a_ref[...] + b_ref[...]).astype(BF16)

    return pl.pallas_call(
        body,
        grid_spec=pltpu.PrefetchScalarGridSpec(
            num_scalar_prefetch=1, grid=(4, nt), in_specs=[ispec, ispec],
            out_specs=pl.BlockSpec((None, tr, cs), lambda s, t, pr: (s, t, 0))),
        out_shape=jax.ShapeDtypeStruct((4, n, cs), BF16),
        compiler_params=_cparams(("parallel", "parallel")), name=name)(place, g, r1)


def _scatter_to_owners(hs, *, name):
    nw = len(hs)

    def body(*refs):
        h, r2 = refs[:nw], refs[nw:2 * nw]
        ssem, rsem = refs[2 * nw:]
        x, y, c = _place()
        chips = _other_chips(x, y)
        copies = []
        for w in range(nw):
            for r, (px, py) in enumerate(chips):
                cp = pltpu.make_async_remote_copy(
                    src_ref=h[w].at[2 * px + py], dst_ref=r2[w].at[r], send_sem=ssem.at[w, r],
                    recv_sem=rsem.at[w, r], device_id=(px, py, c), device_id_type=MESH)
                cp.start()
                copies.append(cp)
        for cp in copies:
            cp.wait()

    anyspec = pl.BlockSpec(memory_space=pl.ANY)
    return pl.pallas_call(
        body, out_shape=[jax.ShapeDtypeStruct((3,) + a.shape[1:], a.dtype) for a in hs],
        in_specs=[anyspec] * nw, out_specs=[anyspec] * nw,
        scratch_shapes=[pltpu.SemaphoreType.DMA((nw, 3))] * 2,
        compiler_params=_cparams(), name=name)(*hs)


def _sum_owner(hs, r2, place, *, name):
    _, n, cs = hs.shape
    tr = _tile(n, 256, 16)
    nt = n // tr

    def body(pr, h_ref, r_ref, o_ref):
        o_ref[...] = ((h_ref[...].astype(F32) + r_ref[0].astype(F32)) + r_ref[1].astype(F32)) + r_ref[2].astype(F32)

    return pl.pallas_call(
        body,
        grid_spec=pltpu.PrefetchScalarGridSpec(
            num_scalar_prefetch=1, grid=(nt,),
            in_specs=[pl.BlockSpec((None, tr, cs), lambda t, pr: (pr[1], t, 0)),
                      pl.BlockSpec((3, tr, cs), lambda t, pr: (0, t, 0))],
            out_specs=pl.BlockSpec((None, tr, cs), lambda t, pr: (pr[0], t, 0))),
        out_shape=jax.ShapeDtypeStruct((2, n, cs), F32),
        compiler_params=_cparams(("parallel",)), name=name)(place, hs, r2)


def _share_with_sibling(fins, *, name):
    nw = len(fins)

    def body(*refs):
        fin, out = refs[:nw], refs[nw:2 * nw]
        ssem, rsem = refs[2 * nw:]
        x, y, c = _place()
        copies = []
        for w in range(nw):
            cp = pltpu.make_async_remote_copy(
                src_ref=fin[w].at[c], dst_ref=out[w].at[c], send_sem=ssem.at[w], recv_sem=rsem.at[w],
                device_id=(x, y, 1 - c), device_id_type=MESH)
            cp.start()
            copies.append(cp)
        for w in range(nw):
            theirs = out[w].at[1 - c]
            pltpu.make_async_remote_copy(
                src_ref=theirs, dst_ref=theirs, send_sem=ssem.at[w], recv_sem=rsem.at[w],
                device_id=(x, y, 1 - c), device_id_type=MESH).wait_recv()
        for cp in copies:
            cp.wait_send()

    anyspec = pl.BlockSpec(memory_space=pl.ANY)
    return pl.pallas_call(
        body, out_shape=[jax.ShapeDtypeStruct(a.shape, F32) for a in fins],
        in_specs=[anyspec] * nw, out_specs=[anyspec] * nw,
        input_output_aliases={w: w for w in range(nw)},
        scratch_shapes=[pltpu.SemaphoreType.DMA((nw,))] * 2,
        compiler_params=_cparams(), name=name)(*fins)


def _local_step(x, target, mod, gains, wfull, g_attn, conv_w, cvec):
    T, D = x.shape
    F = wfull['ff1_w_out'].shape[0]
    AW = D // 2
    C = D - AW
    NQKV = 3 * AW
    MIX = NQKV + 2 * C
    tM = _tile(T, 1024)
    tkT = _tile(T, 1024)

    def ffn_fwd(xin, s, w_in, w_out, tag):
        h = _pre_fwd(xin, gains, mod, T=T, s=s, name=f"pre_fwd_{tag}")
        jac, act = _ffn_in(h, w_in, T=T, D=D, F=F, name=f"ffn_in_{tag}")
        f = _matmul(act, w_out, mode='nn', M=T, N=D, K=F, tm=tM, tn=_tile(D, 1024), tk=F,
                    out_dtype=F32, name=f"ffn_out_{tag}")
        return h, jac, act, f

    def ffn_bwd(dout, xin, saved, s, w_in, w_out, res_w, tag):
        h, jac, act, f = saved
        df, dgate, dgpost = _post_bwd(dout, f, gains, mod, T=T, s=s, res_w=res_w, name=f"post_bwd_{tag}")
        dgu = _ffn_dact(df, w_out, jac, T=T, D=D, F=F, name=f"ffn_dact_{tag}")
        dw_out = _matmul(act, df, mode='tn', M=F, N=D, K=T, tm=_tile(F, 1408), tn=_tile(D, 1024),
                         tk=tkT, out_dtype=F32, name=f"dw_out_{tag}")
        tk = F
        kf = F // tk
        tn = _tile(D, 1024)
        dh = _matmul(dgu, w_in, mode='nt', M=T, N=D, K=2 * F, tm=tM, tn=tn, tk=tk, out_dtype=F32,
                     a_spec=pl.BlockSpec((None, tM, tk), lambda i, j, k: (k // kf, i, k % kf)),
                     name=f"dh_{tag}")
        tnf = _tile(F, 1408)
        nf = F // tnf
        tkt = tkT
        dw_in =_matmul(h, dgu, mode='tn', M=D, N=2 * F, K=T, tm=_tile(D, 1024), tn=tnf, tk=tkt, out_dtype=F32,
                        b_spec=pl.BlockSpec((None, tkt, tnf), lambda i, j, k: (j // nf, k, j % nf)),
                        name=f"dw_in_{tag}")
        dx, dshift, dscale, dgpre = _pre_bwd(dh, xin, dout, gains, mod, T=T, s=s, name=f"pre_bwd_{tag}")
        return dx, dw_in, dw_out, (dshift, dscale, dgate), dgpre, dgpost

    s1 = ffn_fwd(x, 0, wfull['ff1_w_in'], wfull['ff1_w_out'], "ff1")
    x1 = _post_fwd(x, s1[3], gains, mod, T=T, s=0, res_w=0.5, name="post_fwd_ff1")

    h2 = _pre_fwd(x1, gains, mod, T=T, s=1, name="pre_fwd_mix")
    w_in_mix, w_out_mix = wfull['w_in_mix'], wfull['w_out_mix']
    tnq = _tile(AW, 512)
    qkv = _matmul(h2, w_in_mix, mode='nn', M=T, N=NQKV, K=D, tm=tM, tn=tnq, tk=D, out_dtype=BF16, name="proj_qkv")
    tnc = _tile(C, 512)
    off = NQKV // tnc
    cvg = _matmul(h2, w_in_mix, mode='nn', M=T, N=2 * C, K=D, tm=tM, tn=tnc, tk=D, out_dtype=F32,
                  b_spec=pl.BlockSpec((D, tnc), lambda i, j, k: (0, off + j)), name="proj_conv")
    o_attn, a_attn = _attn_fwd(qkv, g_attn, T=T, AW=AW, name="attn_fwd")
    uc = _conv_fwd(cvg, conv_w, cvec, T=T, C=C, name="conv_fwd")
    mixcat = jnp.concatenate([a_attn, uc], axis=1)
    f_mix = _matmul(mixcat, w_out_mix, mode='nn', M=T, N=D, K=D, tm=tM, tn=_tile(D, 1024), tk=D, out_dtype=F32,
                    name="mix_out")
    x2 = _post_fwd(x1, f_mix, gains, mod, T=T, s=1, res_w=1.0, name="post_fwd_mix")

    s3 = ffn_fwd(x2, 2, wfull['ff2_w_in'], wfull['ff2_w_out'], "ff2")
    dout, sq = _post_fwd_loss(x2, s3[3], target, gains, mod, T=T, s=2, res_w=0.5, name="post_fwd_loss")

    dx2, dw_in2, dw_out2, dmod2, dgpre2, dgpost2 = ffn_bwd(
        dout, x2, s3, 2, wfull['ff2_w_in'], wfull['ff2_w_out'], 0.5, "ff2")

    df_mix, dgate_m, dgpost_m = _post_bwd(dx2, f_mix, gains, mod, T=T, s=1, res_w=1.0, name="post_bwd_mix")
    dmixcat = _matmul(df_mix, w_out_mix, mode='nt', M=T, N=D, K=D, tm=tM, tn=_tile(D, 1024), tk=D, out_dtype=F32,
                      name="d_mixcat")
    dw_out_mix = _matmul(mixcat, df_mix, mode='tn', M=D, N=D, K=T, tm=_tile(D, 1024), tn=_tile(D, 1024),
                         tk=tkT, out_dtype=F32, name="dw_out_mix")
    da_attn, duc = dmixcat[:, :AW], dmixcat[:, AW:]
    dq, dk_t, dv_t, dg_attn = _attn_bwd(qkv, o_attn, da_attn, g_attn, T=T, AW=AW, name="attn_bwd")
    dk = jnp.transpose(dk_t, (1, 3, 0, 2)).reshape(T, AW).astype(BF16)
    dv = jnp.transpose(dv_t, (1, 3, 0, 2)).reshape(T, AW).astype(BF16)
    dyc, csum, dconv_w = _conv_bwd1(cvg, duc, conv_w, cvec, T=T, C=C, name="conv_bwd1")
    dcv, dcg = _conv_bwd2(dyc, cvg, conv_w, T=T, C=C, name="conv_bwd2")
    dproj = jnp.concatenate([dq, dk, dv, dcv, dcg], axis=1)
    tkm = MIX
    dh2 =_matmul(dproj, w_in_mix, mode='nt', M=T, N=D, K=MIX, tm=tM, tn=_tile(D, 1024), tk=tkm, out_dtype=F32,
                  name="dh_mix")
    dw_in_mix = _matmul(h2, dproj, mode='tn', M=D, N=MIX, K=T, tm=_tile(D, 1024), tn=_tile(MIX, 1280),
                        tk=tkT, out_dtype=F32, name="dw_in_mix")
    dx1, dshift_m, dscale_m, dgpre_m = _pre_bwd(dh2, x1, dx2, gains, mod, T=T, s=1, name="pre_bwd_mix")

    dx0, dw_in1, dw_out1, dmod1, dgpre1, dgpost1 = ffn_bwd(
        dx1, x, s1, 0, wfull['ff1_w_in'], wfull['ff1_w_out'], 0.5, "ff1")

    big = {'ff1_w_in': dw_in1, 'ff1_w_out': dw_out1, 'w_in_mix': dw_in_mix, 'w_out_mix': dw_out_mix,
           'ff2_w_in': dw_in2, 'ff2_w_out': dw_out2}
    dgains = [dgpre1, dgpost1, dgpre_m, dgpost_m, dgpre2, dgpost2]
    dmod = list(dmod1) + [dshift_m, dscale_m, dgate_m] + list(dmod2)
    return sq, dx0, big, dgains, dmod, dg_attn, csum, dconv_w


def _pack_rows(pieces, width):
    rows = jnp.concatenate([p.reshape(-1) for p in pieces]).reshape(-1, width)
    pad = (-rows.shape[0]) % 8
    return jnp.pad(rows, ((0, pad), (0, 0)))


def kernel(x, c, w_ada, b_ada, g_pre_ff1, g_post_ff1, ff1_w_in, ff1_w_out, g_pre_mix, g_post_mix, w_in_mix, g_attn_out, conv_w, conv_b, conv_ln_g, conv_ln_b, w_out_mix, g_pre_ff2, g_post_ff2, ff2_w_in, ff2_w_out, loss_target, m_w_ada, m_b_ada, m_g_pre_ff1, m_g_post_ff1, m_ff1_w_in, m_ff1_w_out, m_g_pre_mix, m_g_post_mix, m_w_in_mix, m_g_attn_out, m_conv_w, m_conv_b, m_conv_ln_g, m_conv_ln_b, m_w_out_mix, m_g_pre_ff2, m_g_post_ff2, m_ff2_w_in, m_ff2_w_out, v_w_ada, v_b_ada, v_g_pre_ff1, v_g_post_ff1, v_ff1_w_in, v_ff1_w_out, v_g_pre_mix, v_g_post_mix, v_w_in_mix, v_g_attn_out, v_conv_w, v_conv_b, v_conv_ln_g, v_conv_ln_b, v_w_out_mix, v_g_pre_ff2, v_g_post_ff2, v_ff2_w_in, v_ff2_w_out):
    W = dict(w_ada=w_ada, b_ada=b_ada, g_pre_ff1=g_pre_ff1, g_post_ff1=g_post_ff1, ff1_w_in=ff1_w_in,
             ff1_w_out=ff1_w_out, g_pre_mix=g_pre_mix, g_post_mix=g_post_mix, w_in_mix=w_in_mix,
             g_attn_out=g_attn_out, conv_w=conv_w, conv_b=conv_b, conv_ln_g=conv_ln_g, conv_ln_b=conv_ln_b,
             w_out_mix=w_out_mix, g_pre_ff2=g_pre_ff2, g_post_ff2=g_post_ff2, ff2_w_in=ff2_w_in,
             ff2_w_out=ff2_w_out)
    Mo = dict(w_ada=m_w_ada, b_ada=m_b_ada, g_pre_ff1=m_g_pre_ff1, g_post_ff1=m_g_post_ff1, ff1_w_in=m_ff1_w_in,
              ff1_w_out=m_ff1_w_out, g_pre_mix=m_g_pre_mix, g_post_mix=m_g_post_mix, w_in_mix=m_w_in_mix,
              g_attn_out=m_g_attn_out, conv_w=m_conv_w, conv_b=m_conv_b, conv_ln_g=m_conv_ln_g,
              conv_ln_b=m_conv_ln_b, w_out_mix=m_w_out_mix, g_pre_ff2=m_g_pre_ff2, g_post_ff2=m_g_post_ff2,
              ff2_w_in=m_ff2_w_in, ff2_w_out=m_ff2_w_out)
    Vo = dict(w_ada=v_w_ada, b_ada=v_b_ada, g_pre_ff1=v_g_pre_ff1, g_post_ff1=v_g_post_ff1, ff1_w_in=v_ff1_w_in,
              ff1_w_out=v_ff1_w_out, g_pre_mix=v_g_pre_mix, g_post_mix=v_g_post_mix, w_in_mix=v_w_in_mix,
              g_attn_out=v_g_attn_out, conv_w=v_conv_w, conv_b=v_conv_b, conv_ln_g=v_conv_ln_g,
              conv_ln_b=v_conv_ln_b, w_out_mix=v_w_out_mix, g_pre_ff2=v_g_pre_ff2, g_post_ff2=v_g_post_ff2,
              ff2_w_in=v_ff2_w_in, ff2_w_out=v_ff2_w_out)

    T, D = x.shape[1], x.shape[2]
    AW = D // 2
    C = D - AW
    xi, yi, ci = _place()
    me = 4 * xi + 2 * yi + ci
    chip = 2 * xi + yi
    place = jnp.stack([ci, chip]).astype(jnp.int32)

    c_all = _allgather8(jnp.tile(c, (8, 1)), name="gather_c")[:, 0, :]
    ncol = w_ada.shape[1]
    b_cols = lax.dynamic_index_in_dim(b_ada.reshape(4, ncol), chip, keepdims=True).reshape(1, ncol)
    modp = _ada_fwd(c_all, w_ada, b_cols, name="ada_fwd")
    mod_g = _allgather8(modp, name="gather_mod")
    mod_all = jnp.transpose(mod_g[0::2], (1, 0, 2)).reshape(8, 4 * ncol)
    mod = lax.dynamic_index_in_dim(mod_all, me, keepdims=False).reshape(9, D)

    names = [n for n, _ in BIG]
    kinds = [k for _, k in BIG]
    full = _gather_weights([W[n].astype(BF16) for n in names], kinds, name="gather_weights")
    wfull = dict(zip(names, full))
    cs = conv_w.shape[1]
    cw_all = _allgather8(jnp.pad(conv_w, ((0, HALO - CONV_KERNEL), (0, (-cs) % LANES))), name="gather_conv_w")
    conv_w_full = jnp.transpose(cw_all[0::2, :, :cs], (1, 0, 2)).reshape(HALO, 4 * cs)

    gains = _pack_rows([g_pre_ff1, g_post_ff1, g_pre_mix, g_post_mix, g_pre_ff2, g_post_ff2], D)
    cvec = _pack_rows([conv_b, conv_ln_g, conv_ln_b], C)
    g_attn = g_attn_out.reshape(1, AW)

    sq, dx, big, dgains, dmod, dg_attn, csum, dconv_w = _local_step(
        x[0], loss_target[0], mod, gains, wfull, g_attn, conv_w_full, cvec)

    loss_row = jnp.zeros((1, D), F32).at[0, 0].set(jnp.sum(sq) * (0.5 / D))
    small = _pack_rows(dgains + dmod + [dg_attn, csum[0:3], dconv_w, loss_row], D)
    small_all = _allgather8(small, name="gather_small")
    tot = _sum_devices(small_all, name="sum_small")
    n_g, n_m = 6, 9
    r0 = n_g + n_m
    flat = tot.reshape(-1)
    p = r0 * D
    g_attn_grad = flat[p:p + AW]
    p += AW
    gconv_b, gln_g, gln_b = flat[p:p + C], flat[p + C:p + 2 * C], flat[p + 2 * C:p + 3 * C]
    p += 3 * C
    gconv_w_full = flat[p:p + HALO * C].reshape(HALO, C)[:CONV_KERNEL]
    p += HALO * C
    loss = flat[p]
    gconv_w = lax.dynamic_slice_in_dim(gconv_w_full, chip * cs, cs, axis=1)
    grad_small = {'g_pre_ff1': tot[0], 'g_post_ff1': tot[1], 'g_pre_mix': tot[2], 'g_post_mix': tot[3],
                  'g_pre_ff2': tot[4], 'g_post_ff2': tot[5], 'b_ada': tot[n_g:r0].reshape(-1),
                  'g_attn_out': g_attn_grad.reshape(g_attn_out.shape), 'conv_w': gconv_w, 'conv_b': gconv_b,
                  'conv_ln_g': gln_g, 'conv_ln_b': gln_b}

    dmod_all = small_all[:, n_g:r0, :].reshape(8, 9 * D)
    dmod_cols = lax.dynamic_slice_in_dim(dmod_all, chip * ncol, ncol, axis=1)
    grad_w_ada = _ada_bwd(jnp.transpose(c_all), dmod_cols, name="ada_bwd")

    glist = [big[n] for n in names]
    r1 = _exchange_core_halves(glist, kinds, name="grad_core_exchange")
    hs = [_add_core_halves(g, r, place, k, name=f"grad_core_add_{n}") for g, r, k, n in zip(glist, r1, kinds, names)]
    r2 = _scatter_to_owners(hs, name="grad_scatter")
    fins = [_sum_owner(h, r, place, name=f"grad_owner_sum_{n}") for h, r, n in zip(hs, r2, names)]
    shared = _share_with_sibling(fins, name="grad_share")
    grads = dict(grad_small)
    grads['w_ada'] = grad_w_ada
    for n, a in zip(names, shared):
        grads[n] = a.reshape(W[n].shape)

    delta, new_m, new_v = {}, {}, {}
    for n in ['w_ada'] + names:
        delta[n], new_m[n], new_v[n] = _adamw(W[n], grads[n], Mo[n], Vo[n], name=f"adamw_{n}")
    smalls = [n for n in WEIGHTS if n not in delta]
    sizes = [W[n].size for n in smalls]
    tot_sz = sum(sizes)
    padn = (-tot_sz) % (8 * LANES)

    def pack(d):
        return jnp.pad(jnp.concatenate([d[n].reshape(-1) for n in smalls]), (0, padn)).reshape(-1, LANES)

    d_s, m_s, v_s = _adamw(pack(W), pack(grads), pack(Mo), pack(Vo), name="adamw_small")
    pos = 0
    for n, sz in zip(smalls, sizes):
        for dst, src in ((delta, d_s), (new_m, m_s), (new_v, v_s)):
            dst[n] = src.reshape(-1)[pos:pos + sz].reshape(W[n].shape)
        pos += sz

    return (loss, dx[None], *[grads[n] for n in WEIGHTS], *[delta[n] for n in WEIGHTS],
            *[new_m[n] for n in WEIGHTS], *[new_v[n] for n in WEIGHTS])
```

```python
import functools

import jax
import jax.numpy as jnp
from jax import lax
from jax.experimental import pallas as pl
from jax.experimental.pallas import tpu as pltpu

F32 = jnp.float32
BF16 = jnp.bfloat16
MESH = pl.DeviceIdType.MESH

HEAD_DIM = 64
CONV_KERNEL = 31
RMS_EPS = 1e-6
LN_EPS = 1e-5
ADAM_LR = 0.001
ADAM_B1 = 0.9
ADAM_B2 = 0.999
ADAM_EPS = 1e-08
ADAM_WD = 0.01
ADAM_STEP = 10

LANES = 128
HALO = 32
VMEM_LIMIT = 52 * 1024 * 1024

WEIGHTS = ['w_ada', 'b_ada', 'g_pre_ff1', 'g_post_ff1', 'ff1_w_in', 'ff1_w_out', 'g_pre_mix',
           'g_post_mix', 'w_in_mix', 'g_attn_out', 'conv_w', 'conv_b', 'conv_ln_g', 'conv_ln_b',
           'w_out_mix', 'g_pre_ff2', 'g_post_ff2', 'ff2_w_in', 'ff2_w_out']
BIG = [('ff1_w_in', 'col'), ('ff1_w_out', 'row'), ('w_in_mix', 'col'), ('w_out_mix', 'row'),
       ('ff2_w_in', 'col'), ('ff2_w_out', 'row')]


def _tile(dim, pref, mult=LANES):
    if dim <= pref:
        return dim
    best = None
    for t in range(mult, pref + 1, mult):
        if dim % t == 0:
            best = t
    assert best is not None, (dim, pref, mult)
    return best


def _cparams(sem=None):
    kw = dict(vmem_limit_bytes=VMEM_LIMIT)
    if sem is not None:
        kw['dimension_semantics'] = sem
    return pltpu.CompilerParams(**kw)


def _sigmoid(x):
    return 1.0 / (1.0 + jnp.exp(-x))


_DIMS = {'nn': (((1,), (0,)), ((), ())), 'nt': (((1,), (1,)), ((), ())), 'tn': (((0,), (0,)), ((), ()))}


def _matmul(a, b, *, mode, M, N, K, tm, tn, tk, out_dtype, name, a_spec=None, b_spec=None, comm=None):
    nm, nn, nk = M // tm, N // tn, K // tk
    assert nm * tm == M and nn * tn == N and nk * tk == K, (name, M, N, K, tm, tn, tk)
    if a_spec is None:
        a_spec = (pl.BlockSpec((tk, tm), lambda i, j, k: (k, i)) if mode == 'tn'
                  else pl.BlockSpec((tm, tk), lambda i, j, k: (i, k)))
    if b_spec is None:
        b_spec = (pl.BlockSpec((tn, tk), lambda i, j, k: (j, k)) if mode == 'nt'
                  else pl.BlockSpec((tk, tn), lambda i, j, k: (k, j)))
    dims = _DIMS[mode]
    assert nk == 1 or out_dtype == F32, name
    ci_specs, co_specs, co_shapes, csems = _comm_specs(comm)
    nci, nco = len(ci_specs), len(co_specs)

    def body(a_ref, b_ref, *rest):
        o_ref = rest[nci]
        i, j, k = pl.program_id(0), pl.program_id(1), pl.program_id(2)
        first = jnp.logical_and(jnp.logical_and(i == 0, j == 0), k == 0)
        last = jnp.logical_and(jnp.logical_and(i == nm - 1, j == nn - 1), k == nk - 1)
        at_entry, at_exit = _comm_hooks(comm, first, last, (rest[:nci], rest[nci + 1:nci + 1 + nco], rest[nci + 1 + nco:]))
        at_entry()

        def prod():
            return lax.dot_general(a_ref[...], b_ref[...], dims, preferred_element_type=F32)

        if nk == 1:
            o_ref[...] = prod().astype(o_ref.dtype)
        else:
            @pl.when(k == 0)
            def _():
                o_ref[...] = prod()

            @pl.when(k > 0)
            def _():
                o_ref[...] += prod()
        at_exit()

    sem = ("parallel", "parallel", "arbitrary") if comm is None else ("arbitrary",) * 3
    res = pl.pallas_call(
        body, grid=(nm, nn, nk), in_specs=[a_spec, b_spec] + ci_specs,
        out_specs=[pl.BlockSpec((tm, tn), lambda i, j, k: (i, j))] + co_specs,
        out_shape=[jax.ShapeDtypeStruct((M, N), out_dtype)] + co_shapes, scratch_shapes=csems,
        compiler_params=_cparams(sem), name=name)(a, b, *([] if comm is None else comm.ins))
    return res[0] if comm is None else (res[0], res[1:])


def _grid2_hooks(comm, n0, n1, refs):
    j, i = pl.program_id(0), pl.program_id(1)
    return _comm_hooks(comm, jnp.logical_and(j == 0, i == 0), jnp.logical_and(j == n0 - 1, i == n1 - 1), refs)


def _ffn_in(h, w_in, *, T, D, F, name, comm=None):
    tm, tn = _tile(T, 512), _tile(F, 1408)
    nf, nt = F // tn, T // tm
    ci_specs, co_specs, co_shapes, csems = _comm_specs(comm)
    nci, nco = len(ci_specs), len(co_specs)

    def body(h_ref, wg_ref, wu_ref, *rest):
        jac_ref, a_ref = rest[nci], rest[nci + 1]
        at_entry, at_exit = _grid2_hooks(comm, nf, nt, (rest[:nci], rest[nci + 2:nci + 2 + nco], rest[nci + 2 + nco:]))
        at_entry()
        hh = h_ref[...]
        g = jnp.dot(hh, wg_ref[...], preferred_element_type=F32)
        u = jnp.dot(hh, wu_ref[...], preferred_element_type=F32)
        s = _sigmoid(g)
        sg = g * s
        jac_ref[0] = (u * (s * (1.0 + g * (1.0 - s)))).astype(BF16)
        jac_ref[1] = sg.astype(BF16)
        a_ref[...] = (sg * u).astype(BF16)
        at_exit()

    res = pl.pallas_call(
        body, grid=(nf, nt),
        in_specs=[pl.BlockSpec((tm, D), lambda j, i: (i, 0)),
                  pl.BlockSpec((D, tn), lambda j, i: (0, j)),
                  pl.BlockSpec((D, tn), lambda j, i: (0, nf + j))] + ci_specs,
        out_specs=[pl.BlockSpec((2, tm, tn), lambda j, i: (0, i, j)),
                   pl.BlockSpec((tm, tn), lambda j, i: (i, j))] + co_specs,
        out_shape=[jax.ShapeDtypeStruct((2, T, F), BF16), jax.ShapeDtypeStruct((T, F), BF16)] + co_shapes,
        scratch_shapes=csems,
        compiler_params=_cparams(("parallel", "parallel") if comm is None else ("arbitrary", "arbitrary")),
        name=name)(h, w_in, w_in, *([] if comm is None else comm.ins))
    return (res[0], res[1]) if comm is None else (res[0], res[1], res[2:])


def _ffn_dact(df, w_out, jac, *, T, D, F, name, comm=None):
    tm, tn = _tile(T, 512), _tile(F, 1408)
    nf, nt = F // tn, T // tm
    ci_specs, co_specs, co_shapes, csems = _comm_specs(comm)
    nci, nco = len(ci_specs), len(co_specs)

    def body(df_ref, w_ref, jac_ref, *rest):
        o_ref = rest[nci]
        at_entry, at_exit = _grid2_hooks(comm, nf, nt, (rest[:nci], rest[nci + 1:nci + 1 + nco], rest[nci + 1 + nco:]))
        at_entry()
        da = lax.dot_general(df_ref[...], w_ref[...], _DIMS['nt'], preferred_element_type=F32)
        o_ref[0] = (da * jac_ref[0].astype(F32)).astype(BF16)
        o_ref[1] = (da * jac_ref[1].astype(F32)).astype(BF16)
        at_exit()

    res = pl.pallas_call(
        body, grid=(nf, nt),
        in_specs=[pl.BlockSpec((tm, D), lambda j, i: (i, 0)),
                  pl.BlockSpec((tn, D), lambda j, i: (j, 0)),
                  pl.BlockSpec((2, tm, tn), lambda j, i: (0, i, j))] + ci_specs,
        out_specs=[pl.BlockSpec((2, tm, tn), lambda j, i: (0, i, j))] + co_specs,
        out_shape=[jax.ShapeDtypeStruct((2, T, F), BF16)] + co_shapes, scratch_shapes=csems,
        compiler_params=_cparams(("parallel", "parallel") if comm is None else ("arbitrary", "arbitrary")),
        name=name)(df, w_out, jac, *([] if comm is None else comm.ins))
    return res[0] if comm is None else (res[0], res[1:])


def _rowwise(fn, *, T, tm, name, tiled=(), prev=(), nxt=(), consts=(), out_tiled=(), out_acc=(), scratch=(),
             by_ref=False, comm=None):
    n = T // tm
    assert n * tm == T and tm % HALO == 0
    hb = tm // HALO
    in_specs = [pl.BlockSpec((tm, a.shape[1]), lambda i: (i, 0)) for a in tiled]
    in_specs += [pl.BlockSpec((HALO, a.shape[1]), lambda i: (jnp.maximum(i * hb - 1, 0), 0)) for a in prev]
    in_specs += [pl.BlockSpec((HALO, a.shape[1]), lambda i: (jnp.minimum((i + 1) * hb, T // HALO - 1), 0))
                 for a in nxt]
    in_specs += [pl.BlockSpec(a.shape, lambda i: (0, 0)) for a in consts]
    out_shape = [jax.ShapeDtypeStruct((T, c), dt) for c, dt in out_tiled]
    out_shape += [jax.ShapeDtypeStruct(s, F32) for s in out_acc]
    out_specs = [pl.BlockSpec((tm, c), lambda i: (i, 0)) for c, _ in out_tiled]
    out_specs += [pl.BlockSpec(s, lambda i: (0, 0)) for s in out_acc]
    nt, npv, nnx, nc, not_, na = len(tiled), len(prev), len(nxt), len(consts), len(out_tiled), len(out_acc)
    ci_specs, co_specs, co_shapes, csems = _comm_specs(comm)

    def body(*refs):
        pos = 0
        groups = []
        for cnt in (nt, npv, nnx, nc, len(ci_specs), not_, na, len(co_specs), len(scratch), len(csems)):
            groups.append(refs[pos:pos + cnt])
            pos += cnt
        t_r, p_r, n_r, c_r, ci_r, o_r, a_r, co_r, s_r, cs_r = groups
        i = pl.program_id(0)
        at_entry, at_exit = _comm_hooks(comm, i == 0, i == n - 1, (ci_r, co_r, cs_r))
        at_entry()

        @pl.when(i == 0)
        def _():
            for r in a_r:
                r[...] = jnp.zeros_like(r)

        if by_ref:
            fn(i, n, t_r, p_r, n_r, c_r, o_r, a_r, s_r)
        else:
            outs = fn(i, n, [r[...] for r in t_r], [r[...] for r in p_r], [r[...] for r in n_r],
                      [r[...] for r in c_r], a_r, s_r)
            for r, v in zip(o_r, outs):
                r[...] = v.astype(r.dtype)
        at_exit()

    res = pl.pallas_call(
        body, grid=(n,), in_specs=in_specs + ci_specs, out_specs=out_specs + co_specs,
        out_shape=out_shape + co_shapes, scratch_shapes=list(scratch) + csems,
        compiler_params=_cparams(("arbitrary",)), name=name,
    )(*tiled, *prev, *nxt, *consts, *([] if comm is None else comm.ins))
    return res if comm is None else (res[:not_ + na], res[not_ + na:])


def _colsum(v):
    return jnp.sum(v, axis=0, keepdims=True)


def _rowmean(v):
    return jnp.mean(v, axis=-1, keepdims=True)


def _pre_fwd(x, gains, mod, *, T, s, name):
    def fn(i, n, t, p, nx, c, acc, scr):
        xv, (g, m) = t[0], c
        g_pre, shift, scale = g[2 * s:2 * s + 1], m[3 * s:3 * s + 1], m[3 * s + 1:3 * s + 2]
        r = lax.rsqrt(_rowmean(xv * xv) + RMS_EPS)
        return [((xv * r) * g_pre) * (1.0 + scale) + shift]

    return _rowwise(fn, T=T, tm=_tile(T, 512, HALO), name=name, tiled=[x], consts=[gains, mod],
                    out_tiled=[(x.shape[1], BF16)])[0]


def _post_fwd(x, f, gains, mod, *, T, s, res_w, name):
    def fn(i, n, t, p, nx, c, acc, scr):
        (xv, fv), (g, m) = t, c
        g_post, gate = g[2 * s + 1:2 * s + 2], m[3 * s + 2:3 * s + 3]
        y = (fv * lax.rsqrt(_rowmean(fv * fv) + RMS_EPS)) * g_post
        return [xv + (res_w * (1.0 + gate)) * y]

    return _rowwise(fn, T=T, tm=_tile(T, 512, HALO), name=name, tiled=[x, f], consts=[gains, mod],
                    out_tiled=[(x.shape[1], F32)])[0]


def _post_fwd_loss(x, f, target, gains, mod, *, T, s, res_w, name):
    D = x.shape[1]

    def fn(i, n, t, p, nx, c, acc, scr):
        (xv, fv, tv), (g, m) = t, c
        g_post, gate = g[2 * s + 1:2 * s + 2], m[3 * s + 2:3 * s + 3]
        y = (fv * lax.rsqrt(_rowmean(fv * fv) + RMS_EPS)) * g_post
        err = (xv + (res_w * (1.0 + gate)) * y) - tv
        acc[0][...] += _colsum(err * err)
        return [err * (1.0 / D)]

    dout, sq = _rowwise(fn, T=T, tm=_tile(T, 512, HALO), name=name, tiled=[x, f, target], consts=[gains, mod],
                        out_tiled=[(D, F32)], out_acc=[(1, D)])
    return dout, sq


def _post_bwd(dout, f, gains, mod, *, T, s, res_w, name):
    D = f.shape[1]

    def fn(i, n, t, p, nx, c, acc, scr):
        (dv, fv), (g, m) = t, c
        g_post, gate = g[2 * s + 1:2 * s + 2], m[3 * s + 2:3 * s + 3]
        r2 = lax.rsqrt(_rowmean(fv * fv) + RMS_EPS)
        fh = fv * r2
        dy = dv * (res_w * (1.0 + gate))
        acc[0][...] += _colsum(dv * (res_w * (fh * g_post)))
        acc[1][...] += _colsum(dy * fh)
        gy = dy * g_post
        return [r2 * (gy - fh * _rowmean(gy * fh))]

    return _rowwise(fn, T=T, tm=_tile(T, 512, HALO), name=name, tiled=[dout, f], consts=[gains, mod],
                    out_tiled=[(D, BF16)], out_acc=[(1, D), (1, D)])


def _pre_bwd(dh, x, dout, gains, mod, *, T, s, name, comm=None):
    D = x.shape[1]

    def fn(i, n, t, p, nx, c, acc, scr):
        (dhv, xv, dv), (g, m) = t, c
        g_pre, scale = g[2 * s:2 * s + 1], m[3 * s + 1:3 * s + 2]
        r = lax.rsqrt(_rowmean(xv * xv) + RMS_EPS)
        nv = xv * r
        acc[0][...] += _colsum(dhv)
        acc[1][...] += _colsum(dhv * (nv * g_pre))
        acc[2][...] += _colsum(dhv * ((1.0 + scale) * nv))
        gn = dhv * (g_pre * (1.0 + scale))
        return [r * (gn - nv * _rowmean(gn * nv)) + dv]

    return _rowwise(fn, T=T, tm=_tile(T, 512, HALO), name=name, tiled=[dh, x, dout], consts=[gains, mod],
                    out_tiled=[(D, F32)], out_acc=[(1, D), (1, D), (1, D)], comm=comm)


SUBLANES = 8
CONV_CHUNK = 64


def _glu(cvg, C):
    return cvg[:, :C] * _sigmoid(cvg[:, C:])


def _fill_rotations(ext, rot, rows):
    for r in range(SUBLANES):
        rot[r] = ext[pl.ds(r, rows), :]


def _conv_taps(rot, w, r0, rows, off):
    acc = None
    for k in range(CONV_KERNEL):
        a, r = divmod(off(k), SUBLANES)
        term = w[k:k + 1] * rot[r, pl.ds(pl.multiple_of(r0 + a * SUBLANES, SUBLANES), rows), :]
        acc = term if acc is None else acc + term
    return acc


def _causal_off(k):
    return HALO - (CONV_KERNEL - 1) + k


def _conv_norm(rot, cw, cb, r0, rows):
    yc = _conv_taps(rot, cw, r0, rows, _causal_off) + cb
    mu = _rowmean(yc)
    d = yc - mu
    rstd = lax.rsqrt(_rowmean(d * d) + LN_EPS)
    return d * rstd, rstd


def _stage_glu(i, t, p, ext, rot, tm, C):
    ext[pl.ds(0, HALO), :] = jnp.where(i == 0, 0.0, _glu(p[0][...], C))
    ext[pl.ds(HALO, tm), :] = _glu(t[0][...], C)
    ext[pl.ds(HALO + tm, SUBLANES), :] = jnp.zeros((SUBLANES, C), F32)
    _fill_rotations(ext, rot, tm + HALO)


def _conv_scratch(tm, C):
    return [pltpu.VMEM((HALO + tm + SUBLANES, C), F32), pltpu.VMEM((SUBLANES, HALO + tm, C), F32)]


def _conv_fwd(cvg, cw, cvec, *, T, C, name):
    tm = _tile(T, 512, HALO)
    ch = min(CONV_CHUNK, tm)

    def fn(i, n, t, p, nx, c, o, acc, scr):
        ext, rot = scr
        _stage_glu(i, t, p, ext, rot, tm, C)
        w, vec = c[0][...], c[1][...]

        def chunk(ci, carry):
            r0 = pl.multiple_of(ci * ch, ch)
            yh, _ = _conv_norm(rot, w, vec[0:1], r0, ch)
            zz = yh * vec[1:2] + vec[2:3]
            o[0][pl.ds(r0, ch), :] = (zz * _sigmoid(zz)).astype(BF16)
            return carry

        lax.fori_loop(0, tm // ch, chunk, 0)

    return _rowwise(fn, T=T, tm=tm, name=name, tiled=[cvg], prev=[cvg], consts=[cw, cvec],
                    out_tiled=[(C, BF16)], scratch=_conv_scratch(tm, C), by_ref=True)[0]


def _conv_bwd1(cvg, duc, cw, cvec, *, T, C, name, comm=None):
    tm = _tile(T, 512, HALO)
    ch = min(CONV_CHUNK, tm)

    def fn(i, n, t, p, nx, c, o, acc, scr):
        ext, rot, w8 = scr

        @pl.when(i == 0)
        def _():
            w8[...] = jnp.zeros_like(w8)

        _stage_glu(i, t, p, ext, rot, tm, C)
        w, vec = c[0][...], c[1][...]
        ln_g = vec[1:2]

        def chunk(ci, carry):
            r0 = pl.multiple_of(ci * ch, ch)
            yh, rstd = _conv_norm(rot, w, vec[0:1], r0, ch)
            zz = yh * ln_g + vec[2:3]
            s = _sigmoid(zz)
            dz = t[1][pl.ds(r0, ch), :] * (s * (1.0 + zz * (1.0 - s)))
            dyh = dz * ln_g
            dyc = rstd * (dyh - _rowmean(dyh) - yh * _rowmean(dyh * yh))
            o[0][pl.ds(r0, ch), :] = dyc
            acc[0][0:1, :] += _colsum(dyc)
            acc[0][1:2, :] += _colsum(dz * yh)
            acc[0][2:3, :] += _colsum(dz)
            for k in range(CONV_KERNEL):
                a, r = divmod(_causal_off(k), SUBLANES)
                prod = dyc * rot[r, pl.ds(pl.multiple_of(r0 + a * SUBLANES, SUBLANES), ch), :]
                part = prod[0:SUBLANES]
                for g in range(1, ch // SUBLANES):
                    part = part + prod[g * SUBLANES:(g + 1) * SUBLANES]
                w8[pl.ds(k * SUBLANES, SUBLANES), :] += part
            return carry

        lax.fori_loop(0, tm // ch, chunk, 0)

        @pl.when(i == n - 1)
        def _():
            for k in range(CONV_KERNEL):
                acc[1][k:k + 1, :] = _colsum(w8[pl.ds(k * SUBLANES, SUBLANES), :])

    return _rowwise(fn, T=T, tm=tm, name=name, tiled=[cvg, duc], prev=[cvg], consts=[cw, cvec],
                    out_tiled=[(C, F32)], out_acc=[(8, C), (HALO, C)],
                    scratch=_conv_scratch(tm, C) + [pltpu.VMEM((HALO * SUBLANES, C), F32)], by_ref=True, comm=comm)


def _conv_bwd2(dyc, cvg, cw, *, T, C, name):
    tm = _tile(T, 512, HALO)
    ch = min(CONV_CHUNK, tm)

    def fn(i, n, t, p, nx, c, o, acc, scr):
        ext, rot = scr
        ext[pl.ds(0, tm), :] = t[0][...]
        ext[pl.ds(tm, HALO), :] = jnp.where(i == n - 1, 0.0, nx[0][...])
        _fill_rotations(ext, rot, tm + HALO - SUBLANES)
        w = c[0][...]

        def chunk(ci, carry):
            r0 = pl.multiple_of(ci * ch, ch)
            dug = _conv_taps(rot, w, r0, ch, lambda k: (CONV_KERNEL - 1) - k)
            cv = t[1][pl.ds(r0, ch), pl.ds(0, C)]
            s = _sigmoid(t[1][pl.ds(r0, ch), pl.ds(C, C)])
            o[0][pl.ds(r0, ch), :] = (dug * s).astype(BF16)
            o[1][pl.ds(r0, ch), :] = (dug * cv * (s * (1.0 - s))).astype(BF16)
            return carry

        lax.fori_loop(0, tm // ch, chunk, 0)

    return _rowwise(fn, T=T, tm=tm, name=name, tiled=[dyc, cvg], nxt=[dyc], consts=[cw],
                    out_tiled=[(C, BF16), (C, BF16)],
                    scratch=[pltpu.VMEM((tm + HALO, C), F32), pltpu.VMEM((SUBLANES, tm + HALO - SUBLANES, C), F32)],
                    by_ref=True)


def _split(v):
    hi = v.astype(BF16)
    return hi, (v - hi.astype(F32)).astype(BF16)


def _dot2(v, m):
    hi, lo = _split(v)
    return jnp.dot(hi, m, preferred_element_type=F32) + jnp.dot(lo, m, preferred_element_type=F32)


def _log_gap(z):
    return -(jnp.maximum(z, 0.0) + jnp.log(1.0 + jnp.exp(-jnp.abs(z))))


def _head_masks():
    lane = lax.broadcasted_iota(jnp.int32, (1, LANES), 1)
    return lane < HEAD_DIM, lane >= HEAD_DIM


LOG_WEIGHT_FLOOR = -110.0


def _key_norm_bound(k_ref, masks, T):
    ch = _tile(T, 512)

    def chunk(r, m):
        kk = k_ref[pl.ds(pl.multiple_of(r * ch, ch), ch), :].astype(F32)
        k2 = kk * kk
        return tuple(jnp.maximum(m[h], jnp.max(jnp.sum(jnp.where(masks[h], k2, 0.0), -1, keepdims=True),
                                               axis=0, keepdims=True)) for h in (0, 1))

    m0, m1 = lax.fori_loop(0, T // ch, chunk, (jnp.zeros((1, 1), F32), jnp.zeros((1, 1), F32)))
    row = lax.broadcasted_iota(jnp.int32, (8, LANES), 0)
    return jnp.where(row == 0, jnp.sqrt(m0), jnp.sqrt(m1))


def _score_bound(qh, kn):
    qf = qh.astype(F32)
    return jnp.sqrt(jnp.sum(qf * qf, -1, keepdims=True)) * (kn * 1.01) + 0.01


def _some_weight_left(carries, bounds):
    m = jnp.maximum(jnp.max(carries[0] + bounds[0]), jnp.max(carries[1] + bounds[1]))
    return m > LOG_WEIGHT_FLOOR


def _attn_fwd(qkv, g_attn, *, T, AW, name):
    P = AW // LANES
    tq = _tile(T, 256)
    nq = T // tq
    scale = HEAD_DIM ** -0.5

    def body(q_ref, k_ref, v_ref, g_ref, o_ref, a_ref, kn_ref):
        i = pl.program_id(1)
        lo_mask, hi_mask = masks = _head_masks()

        @pl.when(i == 0)
        def _():
            kn_ref[...] = _key_norm_bound(k_ref, masks, T)

        rows = lax.broadcasted_iota(jnp.int32, (tq, tq), 0)
        cols = lax.broadcasted_iota(jnp.int32, (tq, tq), 1)
        strict = cols < rows
        tri = jnp.where(rows >= cols, 1.0, 0.0).astype(BF16)
        q = q_ref[...]
        qhs = [jnp.where(m, q, jnp.zeros_like(q)) * jnp.asarray(scale, BF16) for m in masks]
        zbs = [_score_bound(qhs[h], kn_ref[h:h + 1, 0:1]) for h in (0, 1)]

        def block(j, carry, mask=None):
            st = pl.multiple_of(j * tq, tq)
            kj = k_ref[pl.ds(st, tq), :]
            vj = v_ref[pl.ds(st, tq), :]
            new = []
            for h in (0, 1):
                acc, c = carry[h]
                z = lax.dot_general(qhs[h], kj, _DIMS['nt'], preferred_element_type=F32)
                l = _log_gap(z)
                if mask is not None:
                    l = jnp.where(mask, l, 0.0)
                cum = _dot2(l, tri)
                w = jnp.exp(z + cum + c)
                if mask is not None:
                    w = jnp.where(mask, w, 0.0)
                new.append((acc + _dot2(w, vj), c + cum[:, 0:1]))
            return tuple(new)

        zero = (jnp.zeros((tq, LANES), F32), jnp.zeros((tq, 1), F32))
        carry = block(i, (zero, zero), strict)
        carry = block(jnp.maximum(i - 1, 0), carry, jnp.logical_and(i > 0, cols >= 0))

        def live(st):
            jj, cr = st
            return jnp.logical_and(jj < i - 1, _some_weight_left([cr[0][1], cr[1][1]], zbs))

        _, carry = lax.while_loop(live, lambda st: (st[0] + 1, block(i - 2 - st[0], st[1])),
                                  (jnp.int32(0), carry))
        o = jnp.where(lo_mask, carry[0][0], carry[1][0])
        o2 = o * o
        r0 = lax.rsqrt(jnp.sum(jnp.where(lo_mask, o2, 0.0), -1, keepdims=True) * (1.0 / HEAD_DIM) + RMS_EPS)
        r1 = lax.rsqrt(jnp.sum(jnp.where(hi_mask, o2, 0.0), -1, keepdims=True) * (1.0 / HEAD_DIM) + RMS_EPS)
        o_ref[...] = o
        a_ref[...] = ((o * jnp.where(lo_mask, r0, r1)) * g_ref[...]).astype(BF16)

    return pl.pallas_call(
        body, grid=(P, nq),
        in_specs=[pl.BlockSpec((tq, LANES), lambda p, i: (i, p)),
                  pl.BlockSpec((T, LANES), lambda p, i: (0, P + p)),
                  pl.BlockSpec((T, LANES), lambda p, i: (0, 2 * P + p)),
                  pl.BlockSpec((1, LANES), lambda p, i: (0, p))],
        out_specs=[pl.BlockSpec((tq, LANES), lambda p, i: (i, p)),
                   pl.BlockSpec((tq, LANES), lambda p, i: (i, p))],
        out_shape=[jax.ShapeDtypeStruct((T, AW), F32), jax.ShapeDtypeStruct((T, AW), BF16)],
        scratch_shapes=[pltpu.VMEM((8, LANES), F32)],
        compiler_params=_cparams(("parallel", "arbitrary")), name=name)(qkv, qkv, qkv, g_attn)


def _attn_bwd(qkv, o, da, g_attn, *, T, AW, name):
    P = AW // LANES
    tq = _tile(T, 256)
    nq = T // tq
    scale = HEAD_DIM ** -0.5

    def body(q_ref, k_ref, v_ref, o_ref, da_ref, g_ref, dq_ref, dk_ref, dv_ref, dg_ref, kn_ref):
        i = pl.program_id(1)
        lo_mask, hi_mask = masks = _head_masks()

        @pl.when(i == 0)
        def _():
            dk_ref[...] = jnp.zeros_like(dk_ref)
            dv_ref[...] = jnp.zeros_like(dv_ref)
            dg_ref[...] = jnp.zeros_like(dg_ref)
            kn_ref[...] = _key_norm_bound(k_ref, masks, T)

        rows = lax.broadcasted_iota(jnp.int32, (tq, tq), 0)
        cols = lax.broadcasted_iota(jnp.int32, (tq, tq), 1)
        strict = cols < rows
        tri = jnp.where(rows >= cols, 1.0, 0.0).astype(BF16)
        tri_s = jnp.where(rows > cols, 1.0, 0.0).astype(BF16)
        q = q_ref[...]
        o = o_ref[...]
        da = da_ref[...]
        g = g_ref[...]
        o2 = o * o
        r0 = lax.rsqrt(jnp.sum(jnp.where(lo_mask, o2, 0.0), -1, keepdims=True) * (1.0 / HEAD_DIM) + RMS_EPS)
        r1 = lax.rsqrt(jnp.sum(jnp.where(hi_mask, o2, 0.0), -1, keepdims=True) * (1.0 / HEAD_DIM) + RMS_EPS)
        r = jnp.where(lo_mask, r0, r1)
        oh = o * r
        gy = da * g
        gyo = gy * oh
        m0 = jnp.sum(jnp.where(lo_mask, gyo, 0.0), -1, keepdims=True) * (1.0 / HEAD_DIM)
        m1 = jnp.sum(jnp.where(hi_mask, gyo, 0.0), -1, keepdims=True) * (1.0 / HEAD_DIM)
        do = r * (gy - oh * jnp.where(lo_mask, m0, m1))
        dg_ref[...] += _colsum(da * oh)

        qhs = [jnp.where(m, q, jnp.zeros_like(q)) * jnp.asarray(scale, BF16) for m in masks]
        zbs = [_score_bound(qhs[h], kn_ref[h:h + 1, 0:1]) for h in (0, 1)]
        do_bs = [jnp.where(m, do, 0.0).astype(BF16) for m in masks]
        deltas = [jnp.sum(d.astype(F32) * o, -1, keepdims=True) for d in do_bs]
        q_ts = [qh.astype(F32).T.astype(BF16) for qh in qhs]
        do_ts = [d.astype(F32).T.astype(BF16) for d in do_bs]

        def block(j, carry, mask=None):
            masked = mask is not None
            st = pl.multiple_of(j * tq, tq)
            kj = k_ref[pl.ds(st, tq), :]
            vj = v_ref[pl.ds(st, tq), :]
            new = []
            dk = dv = None
            for h in (0, 1):
                dq, c, gsum = carry[h]
                z = lax.dot_general(qhs[h], kj, _DIMS['nt'], preferred_element_type=F32)
                l = _log_gap(z)
                sig = jnp.exp(z + l)
                if masked:
                    l = jnp.where(mask, l, 0.0)
                cum = _dot2(l, tri)
                w = jnp.exp(z + cum + c)
                if masked:
                    w = jnp.where(mask, w, 0.0)
                dp = lax.dot_general(do_bs[h], vj, _DIMS['nt'], preferred_element_type=F32)
                pw = w * dp
                after = _dot2(pw, tri_s)
                dz = pw - sig * (deltas[h] - gsum - after)
                if masked:
                    dz = jnp.where(mask, dz, 0.0)
                dz_b = dz.astype(BF16)
                dk_h = jnp.dot(q_ts[h], dz_b, preferred_element_type=F32)
                dv_h = jnp.dot(do_ts[h], w.astype(BF16), preferred_element_type=F32)
                dk = dk_h if dk is None else dk + dk_h
                dv = dv_h if dv is None else dv + dv_h
                dq = dq + jnp.dot(dz_b, kj, preferred_element_type=F32)
                new.append((dq, c + cum[:, 0:1], gsum + (after[:, 0:1] + pw[:, 0:1])))
            dk_ref[j] += dk
            dv_ref[j] += dv
            return tuple(new)

        zero1 = jnp.zeros((tq, 1), F32)
        zero = (jnp.zeros((tq, LANES), F32), zero1, zero1)
        carry = block(i, (zero, zero), strict)
        carry = block(jnp.maximum(i - 1, 0), carry, jnp.logical_and(i > 0, cols >= 0))

        def live(st):
            jj, cr = st
            return jnp.logical_and(jj < i - 1, _some_weight_left([cr[0][1], cr[1][1]], zbs))

        _, carry = lax.while_loop(live, lambda st: (st[0] + 1, block(i - 2 - st[0], st[1])),
                                  (jnp.int32(0), carry))
        dq_ref[...] = (jnp.where(lo_mask, carry[0][0], carry[1][0]) * scale).astype(BF16)

    return pl.pallas_call(
        body, grid=(P, nq),
        in_specs=[pl.BlockSpec((tq, LANES), lambda p, i: (i, p)),
                  pl.BlockSpec((T, LANES), lambda p, i: (0, P + p)),
                  pl.BlockSpec((T, LANES), lambda p, i: (0, 2 * P + p)),
                  pl.BlockSpec((tq, LANES), lambda p, i: (i, p)),
                  pl.BlockSpec((tq, LANES), lambda p, i: (i, p)),
                  pl.BlockSpec((1, LANES), lambda p, i: (0, p))],
        out_specs=[pl.BlockSpec((tq, LANES), lambda p, i: (i, p)),
                   pl.BlockSpec((None, nq, LANES, tq), lambda p, i: (p, 0, 0, 0)),
                   pl.BlockSpec((None, nq, LANES, tq), lambda p, i: (p, 0, 0, 0)),
                   pl.BlockSpec((1, LANES), lambda p, i: (0, p))],
        out_shape=[jax.ShapeDtypeStruct((T, AW), BF16),
                   jax.ShapeDtypeStruct((P, nq, LANES, tq), F32),
                   jax.ShapeDtypeStruct((P, nq, LANES, tq), F32),
                   jax.ShapeDtypeStruct((1, AW), F32)],
        scratch_shapes=[pltpu.VMEM((8, LANES), F32)],
        compiler_params=_cparams(("parallel", "arbitrary")), name=name)(qkv, qkv, qkv, o, da, g_attn)


def _ada_fwd(c_all, w_ada, b_ada, *, name):
    def body(c_ref, w_ref, b_ref, o_ref):
        cv = c_ref[...]
        sc = cv * _sigmoid(cv)
        o_ref[...] = jnp.dot(sc, w_ref[...], preferred_element_type=F32,
                             precision=lax.Precision.HIGHEST) + b_ref[...]

    return pl.pallas_call(body, out_shape=jax.ShapeDtypeStruct((c_all.shape[0], w_ada.shape[1]), F32),
                          compiler_params=_cparams(), name=name)(c_all, w_ada, b_ada)


def _ada_bwd(c_all_t, dmod, *, name):
    def body(c_ref, d_ref, o_ref):
        cv = c_ref[...]
        sc = cv * _sigmoid(cv)
        o_ref[...] = jnp.dot(sc, d_ref[...], preferred_element_type=F32, precision=lax.Precision.HIGHEST)

    return pl.pallas_call(body, out_shape=jax.ShapeDtypeStruct((c_all_t.shape[0], dmod.shape[1]), F32),
                          compiler_params=_cparams(), name=name)(c_all_t, dmod)


def _adamw(w, g, m, v, *, name):
    R, C = w.shape
    tr = _tile(R, max(8, (1 << 18) // C), 8)

    def body(w_ref, g_ref, m_ref, v_ref, d_ref, nm_ref, nv_ref):
        gv = g_ref[...]
        m2 = ADAM_B1 * m_ref[...] + (1.0 - ADAM_B1) * gv
        v2 = ADAM_B2 * v_ref[...] + (1.0 - ADAM_B2) * jnp.square(gv)
        m_hat = m2 / (1.0 - ADAM_B1 ** ADAM_STEP)
        v_hat = v2 / (1.0 - ADAM_B2 ** ADAM_STEP)
        d_ref[...] = -ADAM_LR * (m_hat / (jnp.sqrt(v_hat) + ADAM_EPS) + ADAM_WD * w_ref[...])
        nm_ref[...] = m2
        nv_ref[...] = v2

    spec = pl.BlockSpec((tr, C), lambda i: (i, 0))
    return pl.pallas_call(
        body, grid=(R // tr,), in_specs=[spec] * 4, out_specs=[spec] * 3,
        out_shape=[jax.ShapeDtypeStruct((R, C), F32)] * 3,
        compiler_params=_cparams(("parallel",)), name=name)(w, g, m, v)


def _sum_devices(a, *, name):
    def body(a_ref, o_ref):
        s = a_ref[0]
        for d in range(1, a_ref.shape[0]):
            s = s + a_ref[d]
        o_ref[...] = s

    return pl.pallas_call(body, out_shape=jax.ShapeDtypeStruct(a.shape[1:], F32),
                          compiler_params=_cparams(), name=name)(a)


def _place():
    return lax.axis_index("x"), lax.axis_index("y"), lax.axis_index("c")


def _flip(v, bit):
    return 1 - v if bit else v


def _allgather8(blk, *, name):
    R, C = blk.shape

    def body(x_ref, out_ref, send_sems, recv_sems):
        x, y, c = _place()
        me = 4 * x + 2 * y + c
        out_ref[me] = x_ref[...]
        copies = []
        for k in range(1, 8):
            peer = (_flip(x, (k >> 2) & 1), _flip(y, (k >> 1) & 1), _flip(c, k & 1))
            cp = pltpu.make_async_remote_copy(
                src_ref=x_ref, dst_ref=out_ref.at[me], send_sem=send_sems.at[k - 1],
                recv_sem=recv_sems.at[k - 1], device_id=peer, device_id_type=MESH)
            cp.start()
            copies.append(cp)
        for cp in copies:
            cp.wait()

    return pl.pallas_call(
        body, out_shape=jax.ShapeDtypeStruct((8, R, C), F32),
        in_specs=[pl.BlockSpec(memory_space=pltpu.VMEM)], out_specs=pl.BlockSpec(memory_space=pltpu.VMEM),
        scratch_shapes=[pltpu.SemaphoreType.DMA((7,)), pltpu.SemaphoreType.DMA((7,))],
        compiler_params=_cparams(), name=name)(blk)


def _aligned(v, m):
    return v if isinstance(v, int) else pl.multiple_of(v, m)


def _rows_half(ref, half):
    n = ref.shape[0] // 2
    return ref.at[pl.ds(_aligned(half * n, 16), n)]


def _region(ref, kind, slot, half):
    if kind == 'col':
        n, cs = ref.shape[0] // 2, ref.shape[1] // 4
        return ref.at[pl.ds(_aligned(half * n, 16), n), pl.ds(_aligned(slot * cs, LANES), cs)]
    rs = ref.shape[0] // 4
    return ref.at[pl.ds(_aligned(slot * rs + half * (rs // 2), 16), rs // 2)]


def _other_chips(x, y):
    return [(1 - x, y), (x, 1 - y), (1 - x, 1 - y)]


class _Comm:
    def __init__(self, ins, outs, sems, start, finish):
        self.ins, self.outs, self.sems, self.start, self.finish = list(ins), list(outs), list(sems), start, finish


def _comm_specs(comm):
    if comm is None:
        return [], [], [], []
    anyspec = pl.BlockSpec(memory_space=pl.ANY)
    return [anyspec] * len(comm.ins), [anyspec] * len(comm.outs), list(comm.outs), list(comm.sems)


def _comm_hooks(comm, first, last, refs):
    if comm is None:
        return (lambda: None), (lambda: None)

    def at_entry():
        pl.when(first)(lambda: comm.start(*refs))

    def at_exit():
        pl.when(last)(lambda: comm.finish(*refs))

    return at_entry, at_exit


def _comm_alone(comm, *, name):
    ni, no = len(comm.ins), len(comm.outs)

    def body(*refs):
        parts = (refs[:ni], refs[ni:ni + no], refs[ni + no:])
        comm.start(*parts)
        comm.finish(*parts)

    i_specs, o_specs, o_shapes, sems = _comm_specs(comm)
    return pl.pallas_call(body, out_shape=o_shapes, in_specs=i_specs, out_specs=o_specs, scratch_shapes=sems,
                          compiler_params=_cparams(), name=name)(*comm.ins)


def _gather_comm(shards, kinds):
    nw = len(shards)
    full_shapes = []
    for s, kind in zip(shards, kinds):
        full_shapes.append((s.shape[0], 4 * s.shape[1]) if kind == 'col' else (4 * s.shape[0], s.shape[1]))

    def copies(sh, full, sems):
        lsem, ssem, rsem, fssem, frsem = sems
        x, y, c = _place()
        me_slot = 2 * x + y
        chips = _other_chips(x, y)
        local, ici, landed, fwd, passed = [], [], [], [], []
        for w in range(nw):
            for h in (0, 1):
                local.append(pltpu.make_async_copy(_rows_half(sh[w], h), _region(full[w], kinds[w], me_slot, h),
                                                   lsem.at[w, h]))
            for r, (px, py) in enumerate(chips):
                ici.append(pltpu.make_async_remote_copy(
                    src_ref=_rows_half(sh[w], c), dst_ref=_region(full[w], kinds[w], me_slot, c),
                    send_sem=ssem.at[w, r], recv_sem=rsem.at[w, r], device_id=(px, py, c), device_id_type=MESH))
                mine = _region(full[w], kinds[w], 2 * px + py, c)
                landed.append(pltpu.make_async_remote_copy(
                    src_ref=mine, dst_ref=mine, send_sem=ssem.at[w, r], recv_sem=rsem.at[w, r],
                    device_id=(px, py, c), device_id_type=MESH))
                fwd.append(pltpu.make_async_remote_copy(
                    src_ref=mine, dst_ref=mine, send_sem=fssem.at[w, r], recv_sem=frsem.at[w, r],
                    device_id=(x, y, 1 - c), device_id_type=MESH))
                theirs = _region(full[w], kinds[w], 2 * px + py, 1 - c)
                passed.append(pltpu.make_async_remote_copy(
                    src_ref=theirs, dst_ref=theirs, send_sem=fssem.at[w, r], recv_sem=frsem.at[w, r],
                    device_id=(x, y, 1 - c), device_id_type=MESH))
        return local, ici, landed, fwd, passed

    def start(sh, full, sems):
        local, ici, _, _, _ = copies(sh, full, sems)
        for cp in local + ici:
            cp.start()

    def finish(sh, full, sems):
        local, ici, landed, fwd, passed = copies(sh, full, sems)
        for got, cp in zip(landed, fwd):
            got.wait_recv()
            cp.start()
        for got in passed:
            got.wait_recv()
        for cp in ici + fwd:
            cp.wait_send()
        for cp in local:
            cp.wait()

    return _Comm(shards, [jax.ShapeDtypeStruct(s, BF16) for s in full_shapes],
                 [pltpu.SemaphoreType.DMA((nw, 2))] + [pltpu.SemaphoreType.DMA((nw, 3))] * 4, start, finish)


def _exchange_comm(grads, kinds):
    nw = len(grads)

    def copies(g, r1, sems):
        ssem, rsem = sems
        x, y, c = _place()
        out, back = [], []
        for w in range(nw):
            for slot in range(4):
                out.append(pltpu.make_async_remote_copy(
                    src_ref=_region(g[w], kinds[w], slot, 1 - c), dst_ref=_region(r1[w], kinds[w], slot, 1 - c),
                    send_sem=ssem.at[w, slot], recv_sem=rsem.at[w, slot], device_id=(x, y, 1 - c),
                    device_id_type=MESH))
                mine = _region(r1[w], kinds[w], slot, c)
                back.append(pltpu.make_async_remote_copy(
                    src_ref=mine, dst_ref=mine, send_sem=ssem.at[w, slot], recv_sem=rsem.at[w, slot],
                    device_id=(x, y, 1 - c), device_id_type=MESH))
        return out, back

    def start(g, r1, sems):
        for cp in copies(g, r1, sems)[0]:
            cp.start()

    def finish(g, r1, sems):
        out, back = copies(g, r1, sems)
        for got in back:
            got.wait_recv()
        for cp in out:
            cp.wait_send()

    return _Comm(grads, [jax.ShapeDtypeStruct(g.shape, F32) for g in grads],
                 [pltpu.SemaphoreType.DMA((nw, 4))] * 2, start, finish)


def _add_core_halves(g, r1, place, kind, *, name):
    if kind == 'col':
        n, cs = g.shape[0] // 2, g.shape[1] // 4
        tr = _tile(n, 256, 16)
        nt = n // tr
        ispec = pl.BlockSpec((tr, cs), lambda s, t, pr: (pr[0] * nt + t, s))
    else:
        rs, cs = g.shape[0] // 4, g.shape[1]
        n = rs // 2
        tr, nt = n, 1
        ispec = pl.BlockSpec((tr, cs), lambda s, t, pr: (s * 2 + pr[0], 0))

    def body(pr, a_ref, b_ref, o_ref):
        o_ref[...] = (a_ref[...] + b_ref[...]).astype(BF16)

    return pl.pallas_call(
        body,
        grid_spec=pltpu.PrefetchScalarGridSpec(
            num_scalar_prefetch=1, grid=(4, nt), in_specs=[ispec, ispec],
            out_specs=pl.BlockSpec((None, tr, cs), lambda s, t, pr: (s, t, 0))),
        out_shape=jax.ShapeDtypeStruct((4, n, cs), BF16),
        compiler_params=_cparams(("parallel", "parallel")), name=name)(place, g, r1)


def _scatter_comm(hs):
    nw = len(hs)

    def copies(h, r2, sems):
        ssem, rsem = sems
        x, y, c = _place()
        return [pltpu.make_async_remote_copy(
            src_ref=h[w].at[2 * px + py], dst_ref=r2[w].at[r], send_sem=ssem.at[w, r],
            recv_sem=rsem.at[w, r], device_id=(px, py, c), device_id_type=MESH)
            for w in range(nw) for r, (px, py) in enumerate(_other_chips(x, y))]

    def start(h, r2, sems):
        for cp in copies(h, r2, sems):
            cp.start()

    def finish(h, r2, sems):
        for cp in copies(h, r2, sems):
            cp.wait()

    return _Comm(hs, [jax.ShapeDtypeStruct((3,) + a.shape[1:], a.dtype) for a in hs],
                 [pltpu.SemaphoreType.DMA((nw, 3))] * 2, start, finish)


def _sum_owner(hs, r2, place, *, name):
    _, n, cs = hs.shape
    tr = _tile(n, 256, 16)
    nt = n // tr

    def body(pr, h_ref, r_ref, o_ref):
        o_ref[...] = ((h_ref[...].astype(F32) + r_ref[0].astype(F32)) + r_ref[1].astype(F32)) + r_ref[2].astype(F32)

    return pl.pallas_call(
        body,
        grid_spec=pltpu.PrefetchScalarGridSpec(
            num_scalar_prefetch=1, grid=(nt,),
            in_specs=[pl.BlockSpec((None, tr, cs), lambda t, pr: (pr[1], t, 0)),
                      pl.BlockSpec((3, tr, cs), lambda t, pr: (0, t, 0))],
            out_specs=pl.BlockSpec((None, tr, cs), lambda t, pr: (pr[0], t, 0))),
        out_shape=jax.ShapeDtypeStruct((2, n, cs), F32),
        compiler_params=_cparams(("parallel",)), name=name)(place, hs, r2)


def _share_with_sibling(fins, *, name):
    nw = len(fins)

    def body(*refs):
        fin, out = refs[:nw], refs[nw:2 * nw]
        ssem, rsem = refs[2 * nw:]
        x, y, c = _place()
        copies = []
        for w in range(nw):
            cp = pltpu.make_async_remote_copy(
                src_ref=fin[w].at[c], dst_ref=out[w].at[c], send_sem=ssem.at[w], recv_sem=rsem.at[w],
                device_id=(x, y, 1 - c), device_id_type=MESH)
            cp.start()
            copies.append(cp)
        for w in range(nw):
            theirs = out[w].at[1 - c]
            pltpu.make_async_remote_copy(
                src_ref=theirs, dst_ref=theirs, send_sem=ssem.at[w], recv_sem=rsem.at[w],
                device_id=(x, y, 1 - c), device_id_type=MESH).wait_recv()
        for cp in copies:
            cp.wait_send()

    anyspec = pl.BlockSpec(memory_space=pl.ANY)
    return pl.pallas_call(
        body, out_shape=[jax.ShapeDtypeStruct(a.shape, F32) for a in fins],
        in_specs=[anyspec] * nw, out_specs=[anyspec] * nw,
        input_output_aliases={w: w for w in range(nw)},
        scratch_shapes=[pltpu.SemaphoreType.DMA((nw,))] * 2,
        compiler_params=_cparams(), name=name)(*fins)


class _Plan:
    def __init__(self, wfull):
        self.w = dict(wfull)
        self.grads = {}

    def comm(self, site):
        return None

    def done(self, site, results):
        pass

    def ready(self, group, names, arrays):
        self.grads.update(zip(names, arrays))


def _riding(plan, site, call):
    comm = plan.comm(site)
    res = call(comm)
    if comm is None:
        return res
    *main, extra = res
    plan.done(site, extra)
    return main[0] if len(main) == 1 else tuple(main)


class _DistPlan(_Plan):
    EXCHANGE_AT = {'pre_bwd_ff2': 'ff2', 'pre_bwd_mix': 'mix', 'dh_ff1': 'ff1'}
    SCATTER_AT = {'conv_bwd1': 'ff2', 'ffn_dact_ff1': 'mix', 'pre_bwd_ff1': 'ff1'}

    def __init__(self, shards, place):
        self.shards, self.place, self.kind = shards, place, dict(BIG)
        self.w, self.group, self.hs, self.fin = {}, {}, {}, {}

    def _gather(self, names):
        return _gather_comm([self.shards[n] for n in names], [self.kind[n] for n in names])

    def gather_now(self, names, *, name):
        self.w.update(zip(names, _comm_alone(self._gather(names), name=name)))

    def ready(self, group, names, arrays):
        self.group[group] = (names, arrays)

    def comm(self, site):
        if site == 'ffn_in_ff1':
            self.rest = [n for n in self.shards if n not in self.w]
            return self._gather(self.rest)
        if site in self.EXCHANGE_AT:
            names, arrays = self.group[self.EXCHANGE_AT[site]]
            return _exchange_comm(arrays, [self.kind[n] for n in names])
        if site in self.SCATTER_AT:
            return _scatter_comm([self.hs[n] for n in self.group[self.SCATTER_AT[site]][0]])
        return None

    def done(self, site, results):
        if site == 'ffn_in_ff1':
            self.w.update(zip(self.rest, results))
        elif site in self.EXCHANGE_AT:
            names, arrays = self.group[self.EXCHANGE_AT[site]]
            for n, g, r in zip(names, arrays, results):
                self.hs[n] = _add_core_halves(g, r, self.place, self.kind[n], name=f"grad_core_add_{n}")
        else:
            for n, r in zip(self.group[self.SCATTER_AT[site]][0], results):
                self.fin[n] = _sum_owner(self.hs[n], r, self.place, name=f"grad_owner_sum_{n}")

    def finish(self):
        names = list(self.shards)
        return dict(zip(names, _share_with_sibling([self.fin[n] for n in names], name="grad_share")))


def _local_step(x, target, mod, gains, plan, g_attn, conv_w, cvec):
    T, D = x.shape
    F = plan.w['ff1_w_out'].shape[0]
    AW = D // 2
    C = D - AW
    NQKV = 3 * AW
    MIX = NQKV + 2 * C
    tM = _tile(T, 1024)
    tkT = _tile(T, 1024)

    def ffn_fwd(xin, s, tag):
        h = _pre_fwd(xin, gains, mod, T=T, s=s, name=f"pre_fwd_{tag}")
        w_in, w_out = plan.w[f"{tag}_w_in"], plan.w[f"{tag}_w_out"]
        jac, act = _riding(plan, f"ffn_in_{tag}",
                           lambda cm: _ffn_in(h, w_in, T=T, D=D, F=F, name=f"ffn_in_{tag}", comm=cm))
        f = _matmul(act, w_out, mode='nn', M=T, N=D, K=F, tm=tM, tn=_tile(D, 1024), tk=F,
                    out_dtype=F32, name=f"ffn_out_{tag}")
        return h, jac, act, f

    def ffn_bwd(dout, xin, saved, s, res_w, tag):
        h, jac, act, f = saved
        w_in, w_out = plan.w[f"{tag}_w_in"], plan.w[f"{tag}_w_out"]
        df, dgate, dgpost = _post_bwd(dout, f, gains, mod, T=T, s=s, res_w=res_w, name=f"post_bwd_{tag}")
        dgu = _riding(plan, f"ffn_dact_{tag}",
                      lambda cm: _ffn_dact(df, w_out, jac, T=T, D=D, F=F, name=f"ffn_dact_{tag}", comm=cm))
        dw_out = _matmul(act, df, mode='tn', M=F, N=D, K=T, tm=_tile(F, 1408), tn=_tile(D, 1024),
                         tk=tkT, out_dtype=F32, name=f"dw_out_{tag}")
        tnf = _tile(F, 1408)
        nf = F // tnf
        dw_in = _matmul(h, dgu, mode='tn', M=D, N=2 * F, K=T, tm=_tile(D, 1024), tn=tnf, tk=tkT, out_dtype=F32,
                        b_spec=pl.BlockSpec((None, tkT, tnf), lambda i, j, k: (j // nf, k, j % nf)),
                        name=f"dw_in_{tag}")
        plan.ready(tag, [f"{tag}_w_in", f"{tag}_w_out"], [dw_in, dw_out])
        dh = _riding(plan, f"dh_{tag}", lambda cm: _matmul(
            dgu, w_in, mode='nt', M=T, N=D, K=2 * F, tm=tM, tn=_tile(D, 1024), tk=F, out_dtype=F32,
            a_spec=pl.BlockSpec((None, tM, F), lambda i, j, k: (k, i, 0)), name=f"dh_{tag}", comm=cm))
        dx, dshift, dscale, dgpre = _riding(plan, f"pre_bwd_{tag}", lambda cm: _pre_bwd(
            dh, xin, dout, gains, mod, T=T, s=s, name=f"pre_bwd_{tag}", comm=cm))
        return dx, (dshift, dscale, dgate), dgpre, dgpost

    s1 = ffn_fwd(x, 0, "ff1")
    x1 = _post_fwd(x, s1[3], gains, mod, T=T, s=0, res_w=0.5, name="post_fwd_ff1")

    h2 = _pre_fwd(x1, gains, mod, T=T, s=1, name="pre_fwd_mix")
    w_in_mix, w_out_mix = plan.w['w_in_mix'], plan.w['w_out_mix']
    tnq = _tile(AW, 512)
    qkv = _matmul(h2, w_in_mix, mode='nn', M=T, N=NQKV, K=D, tm=tM, tn=tnq, tk=D, out_dtype=BF16, name="proj_qkv")
    tnc = _tile(C, 512)
    off = NQKV // tnc
    cvg = _matmul(h2, w_in_mix, mode='nn', M=T, N=2 * C, K=D, tm=tM, tn=tnc, tk=D, out_dtype=F32,
                  b_spec=pl.BlockSpec((D, tnc), lambda i, j, k: (0, off + j)), name="proj_conv")
    o_attn, a_attn = _attn_fwd(qkv, g_attn, T=T, AW=AW, name="attn_fwd")
    uc = _conv_fwd(cvg, conv_w, cvec, T=T, C=C, name="conv_fwd")
    mixcat = jnp.concatenate([a_attn, uc], axis=1)
    f_mix = _matmul(mixcat, w_out_mix, mode='nn', M=T, N=D, K=D, tm=tM, tn=_tile(D, 1024), tk=D, out_dtype=F32,
                    name="mix_out")
    x2 = _post_fwd(x1, f_mix, gains, mod, T=T, s=1, res_w=1.0, name="post_fwd_mix")

    s3 = ffn_fwd(x2, 2, "ff2")
    dout, sq = _post_fwd_loss(x2, s3[3], target, gains, mod, T=T, s=2, res_w=0.5, name="post_fwd_loss")

    dx2, dmod2, dgpre2, dgpost2 = ffn_bwd(dout, x2, s3, 2, 0.5, "ff2")

    df_mix, dgate_m, dgpost_m = _post_bwd(dx2, f_mix, gains, mod, T=T, s=1, res_w=1.0, name="post_bwd_mix")
    dmixcat = _matmul(df_mix, w_out_mix, mode='nt', M=T, N=D, K=D, tm=tM, tn=_tile(D, 1024), tk=D, out_dtype=F32,
                      name="d_mixcat")
    dw_out_mix = _matmul(mixcat, df_mix, mode='tn', M=D, N=D, K=T, tm=_tile(D, 1024), tn=_tile(D, 1024),
                         tk=tkT, out_dtype=F32, name="dw_out_mix")
    da_attn, duc = dmixcat[:, :AW], dmixcat[:, AW:]
    dq, dk_t, dv_t, dg_attn = _attn_bwd(qkv, o_attn, da_attn, g_attn, T=T, AW=AW, name="attn_bwd")
    dk = jnp.transpose(dk_t, (1, 3, 0, 2)).reshape(T, AW).astype(BF16)
    dv = jnp.transpose(dv_t, (1, 3, 0, 2)).reshape(T, AW).astype(BF16)
    dyc, csum, dconv_w = _riding(plan, "conv_bwd1", lambda cm: _conv_bwd1(
        cvg, duc, conv_w, cvec, T=T, C=C, name="conv_bwd1", comm=cm))
    dcv, dcg = _conv_bwd2(dyc, cvg, conv_w, T=T, C=C, name="conv_bwd2")
    dproj = jnp.concatenate([dq, dk, dv, dcv, dcg], axis=1)
    dh2 = _matmul(dproj, w_in_mix, mode='nt', M=T, N=D, K=MIX, tm=tM, tn=_tile(D, 1024), tk=MIX, out_dtype=F32,
                  name="dh_mix")
    dw_in_mix = _matmul(h2, dproj, mode='tn', M=D, N=MIX, K=T, tm=_tile(D, 1024), tn=_tile(MIX, 1280),
                        tk=tkT, out_dtype=F32, name="dw_in_mix")
    plan.ready("mix", ['w_in_mix', 'w_out_mix'], [dw_in_mix, dw_out_mix])
    dx1, dshift_m, dscale_m, dgpre_m = _riding(plan, "pre_bwd_mix", lambda cm: _pre_bwd(
        dh2, x1, dx2, gains, mod, T=T, s=1, name="pre_bwd_mix", comm=cm))

    dx0, dmod1, dgpre1, dgpost1 = ffn_bwd(dx1, x, s1, 0, 0.5, "ff1")

    dgains = [dgpre1, dgpost1, dgpre_m, dgpost_m, dgpre2, dgpost2]
    dmod = list(dmod1) + [dshift_m, dscale_m, dgate_m] + list(dmod2)
    return sq, dx0, dgains, dmod, dg_attn, csum, dconv_w


def _pack_rows(pieces, width):
    rows = jnp.concatenate([p.reshape(-1) for p in pieces]).reshape(-1, width)
    pad = (-rows.shape[0]) % 8
    return jnp.pad(rows, ((0, pad), (0, 0)))


def kernel(x, c, w_ada, b_ada, g_pre_ff1, g_post_ff1, ff1_w_in, ff1_w_out, g_pre_mix, g_post_mix, w_in_mix, g_attn_out, conv_w, conv_b, conv_ln_g, conv_ln_b, w_out_mix, g_pre_ff2, g_post_ff2, ff2_w_in, ff2_w_out, loss_target, m_w_ada, m_b_ada, m_g_pre_ff1, m_g_post_ff1, m_ff1_w_in, m_ff1_w_out, m_g_pre_mix, m_g_post_mix, m_w_in_mix, m_g_attn_out, m_conv_w, m_conv_b, m_conv_ln_g, m_conv_ln_b, m_w_out_mix, m_g_pre_ff2, m_g_post_ff2, m_ff2_w_in, m_ff2_w_out, v_w_ada, v_b_ada, v_g_pre_ff1, v_g_post_ff1, v_ff1_w_in, v_ff1_w_out, v_g_pre_mix, v_g_post_mix, v_w_in_mix, v_g_attn_out, v_conv_w, v_conv_b, v_conv_ln_g, v_conv_ln_b, v_w_out_mix, v_g_pre_ff2, v_g_post_ff2, v_ff2_w_in, v_ff2_w_out):
    W = dict(w_ada=w_ada, b_ada=b_ada, g_pre_ff1=g_pre_ff1, g_post_ff1=g_post_ff1, ff1_w_in=ff1_w_in,
             ff1_w_out=ff1_w_out, g_pre_mix=g_pre_mix, g_post_mix=g_post_mix, w_in_mix=w_in_mix,
             g_attn_out=g_attn_out, conv_w=conv_w, conv_b=conv_b, conv_ln_g=conv_ln_g, conv_ln_b=conv_ln_b,
             w_out_mix=w_out_mix, g_pre_ff2=g_pre_ff2, g_post_ff2=g_post_ff2, ff2_w_in=ff2_w_in,
             ff2_w_out=ff2_w_out)
    Mo = dict(w_ada=m_w_ada, b_ada=m_b_ada, g_pre_ff1=m_g_pre_ff1, g_post_ff1=m_g_post_ff1, ff1_w_in=m_ff1_w_in,
              ff1_w_out=m_ff1_w_out, g_pre_mix=m_g_pre_mix, g_post_mix=m_g_post_mix, w_in_mix=m_w_in_mix,
              g_attn_out=m_g_attn_out, conv_w=m_conv_w, conv_b=m_conv_b, conv_ln_g=m_conv_ln_g,
              conv_ln_b=m_conv_ln_b, w_out_mix=m_w_out_mix, g_pre_ff2=m_g_pre_ff2, g_post_ff2=m_g_post_ff2,
              ff2_w_in=m_ff2_w_in, ff2_w_out=m_ff2_w_out)
    Vo = dict(w_ada=v_w_ada, b_ada=v_b_ada, g_pre_ff1=v_g_pre_ff1, g_post_ff1=v_g_post_ff1, ff1_w_in=v_ff1_w_in,
              ff1_w_out=v_ff1_w_out, g_pre_mix=v_g_pre_mix, g_post_mix=v_g_post_mix, w_in_mix=v_w_in_mix,
              g_attn_out=v_g_attn_out, conv_w=v_conv_w, conv_b=v_conv_b, conv_ln_g=v_conv_ln_g,
              conv_ln_b=v_conv_ln_b, w_out_mix=v_w_out_mix, g_pre_ff2=v_g_pre_ff2, g_post_ff2=v_g_post_ff2,
              ff2_w_in=v_ff2_w_in, ff2_w_out=v_ff2_w_out)

    T, D = x.shape[1], x.shape[2]
    AW = D // 2
    C = D - AW
    xi, yi, ci = _place()
    me = 4 * xi + 2 * yi + ci
    chip = 2 * xi + yi
    place = jnp.stack([ci, chip]).astype(jnp.int32)

    c_all = _allgather8(jnp.tile(c, (8, 1)), name="gather_c")[:, 0, :]
    ncol = w_ada.shape[1]
    b_cols = lax.dynamic_index_in_dim(b_ada.reshape(4, ncol), chip, keepdims=True).reshape(1, ncol)
    modp = _ada_fwd(c_all, w_ada, b_cols, name="ada_fwd")
    mod_g = _allgather8(modp, name="gather_mod")
    mod_all = jnp.transpose(mod_g[0::2], (1, 0, 2)).reshape(8, 4 * ncol)
    mod = lax.dynamic_index_in_dim(mod_all, me, keepdims=False).reshape(9, D)

    names = [n for n, _ in BIG]
    plan = _DistPlan({n: W[n].astype(BF16) for n in names}, place)
    plan.gather_now(['ff1_w_in', 'ff1_w_out'], name="gather_weights_ff1")
    cs = conv_w.shape[1]
    cw_all = _allgather8(jnp.pad(conv_w, ((0, HALO - CONV_KERNEL), (0, (-cs) % LANES))), name="gather_conv_w")
    conv_w_full = jnp.transpose(cw_all[0::2, :, :cs], (1, 0, 2)).reshape(HALO, 4 * cs)

    gains = _pack_rows([g_pre_ff1, g_post_ff1, g_pre_mix, g_post_mix, g_pre_ff2, g_post_ff2], D)
    cvec = _pack_rows([conv_b, conv_ln_g, conv_ln_b], C)
    g_attn = g_attn_out.reshape(1, AW)

    sq, dx, dgains, dmod, dg_attn, csum, dconv_w = _local_step(
        x[0], loss_target[0], mod, gains, plan, g_attn, conv_w_full, cvec)

    loss_row = jnp.zeros((1, D), F32).at[0, 0].set(jnp.sum(sq) * (0.5 / D))
    small = _pack_rows(dgains + dmod + [dg_attn, csum[0:3], dconv_w, loss_row], D)
    small_all = _allgather8(small, name="gather_small")
    tot = _sum_devices(small_all, name="sum_small")
    n_g, n_m = 6, 9
    r0 = n_g + n_m
    flat = tot.reshape(-1)
    p = r0 * D
    g_attn_grad = flat[p:p + AW]
    p += AW
    gconv_b, gln_g, gln_b = flat[p:p + C], flat[p + C:p + 2 * C], flat[p + 2 * C:p + 3 * C]
    p += 3 * C
    gconv_w_full = flat[p:p + HALO * C].reshape(HALO, C)[:CONV_KERNEL]
    p += HALO * C
    loss = flat[p]
    gconv_w = lax.dynamic_slice_in_dim(gconv_w_full, chip * cs, cs, axis=1)
    grad_small = {'g_pre_ff1': tot[0], 'g_post_ff1': tot[1], 'g_pre_mix': tot[2], 'g_post_mix': tot[3],
                  'g_pre_ff2': tot[4], 'g_post_ff2': tot[5], 'b_ada': tot[n_g:r0].reshape(-1),
                  'g_attn_out': g_attn_grad.reshape(g_attn_out.shape), 'conv_w': gconv_w, 'conv_b': gconv_b,
                  'conv_ln_g': gln_g, 'conv_ln_b': gln_b}

    dmod_all = small_all[:, n_g:r0, :].reshape(8, 9 * D)
    dmod_cols = lax.dynamic_slice_in_dim(dmod_all, chip * ncol, ncol, axis=1)
    grad_w_ada = _ada_bwd(jnp.transpose(c_all), dmod_cols, name="ada_bwd")

    grads = dict(grad_small)
    grads['w_ada'] = grad_w_ada
    for n, a in plan.finish().items():
        grads[n] = a.reshape(W[n].shape)

    delta, new_m, new_v = {}, {}, {}
    for n in ['w_ada'] + names:
        delta[n], new_m[n], new_v[n] = _adamw(W[n], grads[n], Mo[n], Vo[n], name=f"adamw_{n}")
    smalls = [n for n in WEIGHTS if n not in delta]
    sizes = [W[n].size for n in smalls]
    tot_sz = sum(sizes)
    padn = (-tot_sz) % (8 * LANES)

    def pack(d):
        return jnp.pad(jnp.concatenate([d[n].reshape(-1) for n in smalls]), (0, padn)).reshape(-1, LANES)

    d_s, m_s, v_s = _adamw(pack(W), pack(grads), pack(Mo), pack(Vo), name="adamw_small")
    pos = 0
    for n, sz in zip(smalls, sizes):
        for dst, src in ((delta, d_s), (new_m, m_s), (new_v, v_s)):
            dst[n] = src.reshape(-1)[pos:pos + sz].reshape(W[n].shape)
        pos += sz

    return (loss, dx[None], *[grads[n] for n in WEIGHTS], *[delta[n] for n in WEIGHTS],
            *[new_m[n] for n in WEIGHTS], *[new_v[n] for n in WEIGHTS])
```

```python
import functools

import jax
import jax.numpy as jnp
from jax import lax
from jax.experimental import pallas as pl
from jax.experimental.pallas import tpu as pltpu

F32 = jnp.float32
BF16 = jnp.bfloat16
MESH = pl.DeviceIdType.MESH

HEAD_DIM = 64
CONV_KERNEL = 31
RMS_EPS = 1e-6
LN_EPS = 1e-5
ADAM_LR = 0.001
ADAM_B1 = 0.9
ADAM_B2 = 0.999
ADAM_EPS = 1e-08
ADAM_WD = 0.01
ADAM_STEP = 10

LANES = 128
HALO = 32
VMEM_LIMIT = 52 * 1024 * 1024

WEIGHTS = ['w_ada', 'b_ada', 'g_pre_ff1', 'g_post_ff1', 'ff1_w_in', 'ff1_w_out', 'g_pre_mix',
           'g_post_mix', 'w_in_mix', 'g_attn_out', 'conv_w', 'conv_b', 'conv_ln_g', 'conv_ln_b',
           'w_out_mix', 'g_pre_ff2', 'g_post_ff2', 'ff2_w_in', 'ff2_w_out']
BIG = [('ff1_w_in', 'col'), ('ff1_w_out', 'row'), ('w_in_mix', 'col'), ('w_out_mix', 'row'),
       ('ff2_w_in', 'col'), ('ff2_w_out', 'row')]


def _tile(dim, pref, mult=LANES):
    if dim <= pref:
        return dim
    best = None
    for t in range(mult, pref + 1, mult):
        if dim % t == 0:
            best = t
    assert best is not None, (dim, pref, mult)
    return best


def _cparams(sem=None):
    kw = dict(vmem_limit_bytes=VMEM_LIMIT)
    if sem is not None:
        kw['dimension_semantics'] = sem
    return pltpu.CompilerParams(**kw)


def _sigmoid(x):
    return 1.0 / (1.0 + jnp.exp(-x))


_DIMS = {'nn': (((1,), (0,)), ((), ())), 'nt': (((1,), (1,)), ((), ())), 'tn': (((0,), (0,)), ((), ()))}


def _matmul(a, b, *, mode, M, N, K, tm, tn, tk, out_dtype, name, a_spec=None, b_spec=None, comm=None):
    nm, nn, nk = M // tm, N // tn, K // tk
    assert nm * tm == M and nn * tn == N and nk * tk == K, (name, M, N, K, tm, tn, tk)
    if a_spec is None:
        a_spec = (pl.BlockSpec((tk, tm), lambda i, j, k: (k, i)) if mode == 'tn'
                  else pl.BlockSpec((tm, tk), lambda i, j, k: (i, k)))
    if b_spec is None:
        b_spec = (pl.BlockSpec((tn, tk), lambda i, j, k: (j, k)) if mode == 'nt'
                  else pl.BlockSpec((tk, tn), lambda i, j, k: (k, j)))
    dims = _DIMS[mode]
    assert nk == 1 or out_dtype == F32, name
    ci_specs, co_specs, co_shapes, csems = _comm_specs(comm)
    nci, nco = len(ci_specs), len(co_specs)

    def body(a_ref, b_ref, *rest):
        o_ref = rest[nci]
        i, j, k = pl.program_id(0), pl.program_id(1), pl.program_id(2)
        first = jnp.logical_and(jnp.logical_and(i == 0, j == 0), k == 0)
        last = jnp.logical_and(jnp.logical_and(i == nm - 1, j == nn - 1), k == nk - 1)
        at_entry, at_exit = _comm_hooks(comm, first, last, (rest[:nci], rest[nci + 1:nci + 1 + nco], rest[nci + 1 + nco:]))
        at_entry()

        def prod():
            return lax.dot_general(a_ref[...], b_ref[...], dims, preferred_element_type=F32)

        if nk == 1:
            o_ref[...] = prod().astype(o_ref.dtype)
        else:
            @pl.when(k == 0)
            def _():
                o_ref[...] = prod()

            @pl.when(k > 0)
            def _():
                o_ref[...] += prod()
        at_exit()

    sem = ("parallel", "parallel", "arbitrary") if comm is None else ("arbitrary",) * 3
    res = pl.pallas_call(
        body, grid=(nm, nn, nk), in_specs=[a_spec, b_spec] + ci_specs,
        out_specs=[pl.BlockSpec((tm, tn), lambda i, j, k: (i, j))] + co_specs,
        out_shape=[jax.ShapeDtypeStruct((M, N), out_dtype)] + co_shapes, scratch_shapes=csems,
        compiler_params=_cparams(sem), name=name)(a, b, *([] if comm is None else comm.ins))
    return res[0] if comm is None else (res[0], res[1:])


def _grid2_hooks(comm, n0, n1, refs):
    j, i = pl.program_id(0), pl.program_id(1)
    return _comm_hooks(comm, jnp.logical_and(j == 0, i == 0), jnp.logical_and(j == n0 - 1, i == n1 - 1), refs)


def _ffn_in(h, w_in, *, T, D, F, name, comm=None):
    tm, tn = _tile(T, 512), _tile(F, 1408)
    nf, nt = F // tn, T // tm
    ci_specs, co_specs, co_shapes, csems = _comm_specs(comm)
    nci, nco = len(ci_specs), len(co_specs)

    def body(h_ref, wg_ref, wu_ref, *rest):
        jac_ref, a_ref = rest[nci], rest[nci + 1]
        at_entry, at_exit = _grid2_hooks(comm, nf, nt, (rest[:nci], rest[nci + 2:nci + 2 + nco], rest[nci + 2 + nco:]))
        at_entry()
        hh = h_ref[...]
        g = jnp.dot(hh, wg_ref[...], preferred_element_type=F32)
        u = jnp.dot(hh, wu_ref[...], preferred_element_type=F32)
        s = _sigmoid(g)
        sg = g * s
        jac_ref[0] = (u * (s * (1.0 + g * (1.0 - s)))).astype(BF16)
        jac_ref[1] = sg.astype(BF16)
        a_ref[...] = (sg * u).astype(BF16)
        at_exit()

    res = pl.pallas_call(
        body, grid=(nf, nt),
        in_specs=[pl.BlockSpec((tm, D), lambda j, i: (i, 0)),
                  pl.BlockSpec((D, tn), lambda j, i: (0, j)),
                  pl.BlockSpec((D, tn), lambda j, i: (0, nf + j))] + ci_specs,
        out_specs=[pl.BlockSpec((2, tm, tn), lambda j, i: (0, i, j)),
                   pl.BlockSpec((tm, tn), lambda j, i: (i, j))] + co_specs,
        out_shape=[jax.ShapeDtypeStruct((2, T, F), BF16), jax.ShapeDtypeStruct((T, F), BF16)] + co_shapes,
        scratch_shapes=csems,
        compiler_params=_cparams(("parallel", "parallel") if comm is None else ("arbitrary", "arbitrary")),
        name=name)(h, w_in, w_in, *([] if comm is None else comm.ins))
    return (res[0], res[1]) if comm is None else (res[0], res[1], res[2:])


def _ffn_dact(df, w_out, jac, *, T, D, F, name, comm=None):
    tm, tn = _tile(T, 512), _tile(F, 1408)
    nf, nt = F // tn, T // tm
    ci_specs, co_specs, co_shapes, csems = _comm_specs(comm)
    nci, nco = len(ci_specs), len(co_specs)

    def body(df_ref, w_ref, jac_ref, *rest):
        o_ref = rest[nci]
        at_entry, at_exit = _grid2_hooks(comm, nf, nt, (rest[:nci], rest[nci + 1:nci + 1 + nco], rest[nci + 1 + nco:]))
        at_entry()
        da = lax.dot_general(df_ref[...], w_ref[...], _DIMS['nt'], preferred_element_type=F32)
        o_ref[0] = (da * jac_ref[0].astype(F32)).astype(BF16)
        o_ref[1] = (da * jac_ref[1].astype(F32)).astype(BF16)
        at_exit()

    res = pl.pallas_call(
        body, grid=(nf, nt),
        in_specs=[pl.BlockSpec((tm, D), lambda j, i: (i, 0)),
                  pl.BlockSpec((tn, D), lambda j, i: (j, 0)),
                  pl.BlockSpec((2, tm, tn), lambda j, i: (0, i, j))] + ci_specs,
        out_specs=[pl.BlockSpec((2, tm, tn), lambda j, i: (0, i, j))] + co_specs,
        out_shape=[jax.ShapeDtypeStruct((2, T, F), BF16)] + co_shapes, scratch_shapes=csems,
        compiler_params=_cparams(("parallel", "parallel") if comm is None else ("arbitrary", "arbitrary")),
        name=name)(df, w_out, jac, *([] if comm is None else comm.ins))
    return res[0] if comm is None else (res[0], res[1:])


def _rowwise(fn, *, T, tm, name, tiled=(), prev=(), nxt=(), consts=(), out_tiled=(), out_acc=(), scratch=(),
             by_ref=False, comm=None, into=None):
    n = T // tm
    assert n * tm == T and tm % HALO == 0
    hb = tm // HALO
    cols = [a if isinstance(a, tuple) else (a, a.shape[1], 0) for a in tiled]
    tiled = [a for a, _, _ in cols]
    in_specs = [pl.BlockSpec((tm, w), functools.partial(lambda cb, i: (i, cb), cb)) for _, w, cb in cols]
    in_specs += [pl.BlockSpec((HALO, a.shape[1]), lambda i: (jnp.maximum(i * hb - 1, 0), 0)) for a in prev]
    in_specs += [pl.BlockSpec((HALO, a.shape[1]), lambda i: (jnp.minimum((i + 1) * hb, T // HALO - 1), 0))
                 for a in nxt]
    in_specs += [pl.BlockSpec(a.shape, lambda i: (0, 0)) for a in consts]
    out_shape = [jax.ShapeDtypeStruct((T, c), dt) for c, dt in out_tiled]
    out_shape += [jax.ShapeDtypeStruct(s, F32) for s in out_acc]
    out_specs = [pl.BlockSpec((tm, c), lambda i: (i, 0)) for c, _ in out_tiled]
    out_specs += [pl.BlockSpec(s, lambda i: (0, 0)) for s in out_acc]
    nt, npv, nnx, nc, not_, na = len(tiled), len(prev), len(nxt), len(consts), len(out_tiled), len(out_acc)
    ci_specs, co_specs, co_shapes, csems = _comm_specs(comm)
    extra_in, aliases = [], {}
    if into is not None:
        arr, cb = into
        width = out_tiled[0][0]
        out_shape[0] = jax.ShapeDtypeStruct(arr.shape, arr.dtype)
        out_specs[0] = pl.BlockSpec((tm, width), lambda i: (i, cb))
        extra_in = [arr]
        aliases = {nt + npv + nnx + nc: 0}
    n_extra = len(extra_in)

    def body(*refs):
        pos = 0
        groups = []
        for cnt in (nt, npv, nnx, nc, n_extra, len(ci_specs), not_, na, len(co_specs), len(scratch), len(csems)):
            groups.append(refs[pos:pos + cnt])
            pos += cnt
        t_r, p_r, n_r, c_r, _, ci_r, o_r, a_r, co_r, s_r, cs_r = groups
        i = pl.program_id(0)
        at_entry, at_exit = _comm_hooks(comm, i == 0, i == n - 1, (ci_r, co_r, cs_r))
        at_entry()

        @pl.when(i == 0)
        def _():
            for r in a_r:
                r[...] = jnp.zeros_like(r)

        if by_ref:
            fn(i, n, t_r, p_r, n_r, c_r, o_r, a_r, s_r)
        else:
            outs = fn(i, n, [r[...] for r in t_r], [r[...] for r in p_r], [r[...] for r in n_r],
                      [r[...] for r in c_r], a_r, s_r)
            for r, v in zip(o_r, outs):
                r[...] = v.astype(r.dtype)
        at_exit()

    res = pl.pallas_call(
        body, grid=(n,), in_specs=in_specs + [pl.BlockSpec(memory_space=pl.ANY)] * n_extra + ci_specs,
        out_specs=out_specs + co_specs, out_shape=out_shape + co_shapes, scratch_shapes=list(scratch) + csems,
        input_output_aliases=aliases, compiler_params=_cparams(("arbitrary",)), name=name,
    )(*tiled, *prev, *nxt, *consts, *extra_in, *([] if comm is None else comm.ins))
    return res if comm is None else (res[:not_ + na], res[not_ + na:])


def _colsum(v):
    return jnp.sum(v, axis=0, keepdims=True)


def _rowmean(v):
    return jnp.mean(v, axis=-1, keepdims=True)


def _pre_fwd(x, gains, mod, *, T, s, name):
    def fn(i, n, t, p, nx, c, acc, scr):
        xv, (g, m) = t[0], c
        g_pre, shift, scale = g[2 * s:2 * s + 1], m[3 * s:3 * s + 1], m[3 * s + 1:3 * s + 2]
        r = lax.rsqrt(_rowmean(xv * xv) + RMS_EPS)
        return [((xv * r) * g_pre) * (1.0 + scale) + shift]

    return _rowwise(fn, T=T, tm=_tile(T, 512, HALO), name=name, tiled=[x], consts=[gains, mod],
                    out_tiled=[(x.shape[1], BF16)])[0]


def _post_fwd(x, f, gains, mod, *, T, s, res_w, name):
    def fn(i, n, t, p, nx, c, acc, scr):
        (xv, fv), (g, m) = t, c
        g_post, gate = g[2 * s + 1:2 * s + 2], m[3 * s + 2:3 * s + 3]
        y = (fv * lax.rsqrt(_rowmean(fv * fv) + RMS_EPS)) * g_post
        return [xv + (res_w * (1.0 + gate)) * y]

    return _rowwise(fn, T=T, tm=_tile(T, 512, HALO), name=name, tiled=[x, f], consts=[gains, mod],
                    out_tiled=[(x.shape[1], F32)])[0]


def _post_fwd_loss(x, f, target, gains, mod, *, T, s, res_w, name):
    D = x.shape[1]

    def fn(i, n, t, p, nx, c, acc, scr):
        (xv, fv, tv), (g, m) = t, c
        g_post, gate = g[2 * s + 1:2 * s + 2], m[3 * s + 2:3 * s + 3]
        y = (fv * lax.rsqrt(_rowmean(fv * fv) + RMS_EPS)) * g_post
        err = (xv + (res_w * (1.0 + gate)) * y) - tv
        acc[0][...] += _colsum(err * err)
        return [err * (1.0 / D)]

    dout, sq = _rowwise(fn, T=T, tm=_tile(T, 512, HALO), name=name, tiled=[x, f, target], consts=[gains, mod],
                        out_tiled=[(D, F32)], out_acc=[(1, D)])
    return dout, sq


def _post_bwd(dout, f, gains, mod, *, T, s, res_w, name):
    D = f.shape[1]

    def fn(i, n, t, p, nx, c, acc, scr):
        (dv, fv), (g, m) = t, c
        g_post, gate = g[2 * s + 1:2 * s + 2], m[3 * s + 2:3 * s + 3]
        r2 = lax.rsqrt(_rowmean(fv * fv) + RMS_EPS)
        fh = fv * r2
        dy = dv * (res_w * (1.0 + gate))
        acc[0][...] += _colsum(dv * (res_w * (fh * g_post)))
        acc[1][...] += _colsum(dy * fh)
        gy = dy * g_post
        return [r2 * (gy - fh * _rowmean(gy * fh))]

    return _rowwise(fn, T=T, tm=_tile(T, 512, HALO), name=name, tiled=[dout, f], consts=[gains, mod],
                    out_tiled=[(D, BF16)], out_acc=[(1, D), (1, D)])


def _pre_bwd(dh, x, dout, gains, mod, *, T, s, name, comm=None):
    D = x.shape[1]

    def fn(i, n, t, p, nx, c, acc, scr):
        (dhv, xv, dv), (g, m) = t, c
        g_pre, scale = g[2 * s:2 * s + 1], m[3 * s + 1:3 * s + 2]
        r = lax.rsqrt(_rowmean(xv * xv) + RMS_EPS)
        nv = xv * r
        acc[0][...] += _colsum(dhv)
        acc[1][...] += _colsum(dhv * (nv * g_pre))
        acc[2][...] += _colsum(dhv * ((1.0 + scale) * nv))
        gn = dhv * (g_pre * (1.0 + scale))
        return [r * (gn - nv * _rowmean(gn * nv)) + dv]

    return _rowwise(fn, T=T, tm=_tile(T, 512, HALO), name=name, tiled=[dh, x, dout], consts=[gains, mod],
                    out_tiled=[(D, F32)], out_acc=[(1, D), (1, D), (1, D)], comm=comm)


SUBLANES = 8
CONV_CHUNK = 64


def _glu(cvg, C):
    return cvg[:, :C] * _sigmoid(cvg[:, C:])


def _fill_rotations(ext, rot, rows):
    for r in range(SUBLANES):
        rot[r] = ext[pl.ds(r, rows), :]


def _conv_taps(rot, w, r0, rows, off):
    acc = None
    for k in range(CONV_KERNEL):
        a, r = divmod(off(k), SUBLANES)
        term = w[k:k + 1] * rot[r, pl.ds(pl.multiple_of(r0 + a * SUBLANES, SUBLANES), rows), :]
        acc = term if acc is None else acc + term
    return acc


def _causal_off(k):
    return HALO - (CONV_KERNEL - 1) + k


def _conv_norm(rot, cw, cb, r0, rows):
    yc = _conv_taps(rot, cw, r0, rows, _causal_off) + cb
    mu = _rowmean(yc)
    d = yc - mu
    rstd = lax.rsqrt(_rowmean(d * d) + LN_EPS)
    return d * rstd, rstd


def _stage_glu(i, t, p, ext, rot, tm, C):
    ext[pl.ds(0, HALO), :] = jnp.where(i == 0, 0.0, _glu(p[0][...], C))
    ext[pl.ds(HALO, tm), :] = _glu(t[0][...], C)
    ext[pl.ds(HALO + tm, SUBLANES), :] = jnp.zeros((SUBLANES, C), F32)
    _fill_rotations(ext, rot, tm + HALO)


def _conv_scratch(tm, C):
    return [pltpu.VMEM((HALO + tm + SUBLANES, C), F32), pltpu.VMEM((SUBLANES, HALO + tm, C), F32)]


def _conv_fwd(cvg, cw, cvec, *, T, C, name, into=None):
    tm = _tile(T, 512, HALO)
    ch = min(CONV_CHUNK, tm)

    def fn(i, n, t, p, nx, c, o, acc, scr):
        ext, rot = scr
        _stage_glu(i, t, p, ext, rot, tm, C)
        w, vec = c[0][...], c[1][...]

        def chunk(ci, carry):
            r0 = pl.multiple_of(ci * ch, ch)
            yh, _ = _conv_norm(rot, w, vec[0:1], r0, ch)
            zz = yh * vec[1:2] + vec[2:3]
            o[0][pl.ds(r0, ch), :] = (zz * _sigmoid(zz)).astype(BF16)
            return carry

        lax.fori_loop(0, tm // ch, chunk, 0)

    return _rowwise(fn, T=T, tm=tm, name=name, tiled=[cvg], prev=[cvg], consts=[cw, cvec],
                    out_tiled=[(C, BF16)], scratch=_conv_scratch(tm, C), by_ref=True, into=into)[0]


def _conv_bwd1(cvg, duc, cw, cvec, *, T, C, name, comm=None):
    tm = _tile(T, 512, HALO)
    ch = min(CONV_CHUNK, tm)

    def fn(i, n, t, p, nx, c, o, acc, scr):
        ext, rot, w8 = scr

        @pl.when(i == 0)
        def _():
            w8[...] = jnp.zeros_like(w8)

        _stage_glu(i, t, p, ext, rot, tm, C)
        w, vec = c[0][...], c[1][...]
        ln_g = vec[1:2]

        def chunk(ci, carry):
            r0 = pl.multiple_of(ci * ch, ch)
            yh, rstd = _conv_norm(rot, w, vec[0:1], r0, ch)
            zz = yh * ln_g + vec[2:3]
            s = _sigmoid(zz)
            dz = t[1][pl.ds(r0, ch), :] * (s * (1.0 + zz * (1.0 - s)))
            dyh = dz * ln_g
            dyc = rstd * (dyh - _rowmean(dyh) - yh * _rowmean(dyh * yh))
            o[0][pl.ds(r0, ch), :] = dyc
            acc[0][0:1, :] += _colsum(dyc)
            acc[0][1:2, :] += _colsum(dz * yh)
            acc[0][2:3, :] += _colsum(dz)
            for k in range(CONV_KERNEL):
                a, r = divmod(_causal_off(k), SUBLANES)
                prod = dyc * rot[r, pl.ds(pl.multiple_of(r0 + a * SUBLANES, SUBLANES), ch), :]
                part = prod[0:SUBLANES]
                for g in range(1, ch // SUBLANES):
                    part = part + prod[g * SUBLANES:(g + 1) * SUBLANES]
                w8[pl.ds(k * SUBLANES, SUBLANES), :] += part
            return carry

        lax.fori_loop(0, tm // ch, chunk, 0)

        @pl.when(i == n - 1)
        def _():
            for k in range(CONV_KERNEL):
                acc[1][k:k + 1, :] = _colsum(w8[pl.ds(k * SUBLANES, SUBLANES), :])

    return _rowwise(fn, T=T, tm=tm, name=name, tiled=[cvg, duc], prev=[cvg], consts=[cw, cvec],
                    out_tiled=[(C, F32)], out_acc=[(8, C), (HALO, C)],
                    scratch=_conv_scratch(tm, C) + [pltpu.VMEM((HALO * SUBLANES, C), F32)], by_ref=True, comm=comm)


def _conv_bwd2(dyc, cvg, cw, *, T, C, name):
    tm = _tile(T, 512, HALO)
    ch = min(CONV_CHUNK, tm)

    def fn(i, n, t, p, nx, c, o, acc, scr):
        ext, rot = scr
        ext[pl.ds(0, tm), :] = t[0][...]
        ext[pl.ds(tm, HALO), :] = jnp.where(i == n - 1, 0.0, nx[0][...])
        _fill_rotations(ext, rot, tm + HALO - SUBLANES)
        w = c[0][...]

        def chunk(ci, carry):
            r0 = pl.multiple_of(ci * ch, ch)
            dug = _conv_taps(rot, w, r0, ch, lambda k: (CONV_KERNEL - 1) - k)
            cv = t[1][pl.ds(r0, ch), pl.ds(0, C)]
            s = _sigmoid(t[1][pl.ds(r0, ch), pl.ds(C, C)])
            o[0][pl.ds(r0, ch), :] = (dug * s).astype(BF16)
            o[1][pl.ds(r0, ch), :] = (dug * cv * (s * (1.0 - s))).astype(BF16)
            return carry

        lax.fori_loop(0, tm // ch, chunk, 0)

    return _rowwise(fn, T=T, tm=tm, name=name, tiled=[dyc, cvg], nxt=[dyc], consts=[cw],
                    out_tiled=[(C, BF16), (C, BF16)],
                    scratch=[pltpu.VMEM((tm + HALO, C), F32), pltpu.VMEM((SUBLANES, tm + HALO - SUBLANES, C), F32)],
                    by_ref=True)


def _split(v):
    hi = v.astype(BF16)
    return hi, (v - hi.astype(F32)).astype(BF16)


def _dot2(v, m):
    hi, lo = _split(v)
    return jnp.dot(hi, m, preferred_element_type=F32) + jnp.dot(lo, m, preferred_element_type=F32)


def _log_gap(z):
    return -(jnp.maximum(z, 0.0) + jnp.log(1.0 + jnp.exp(-jnp.abs(z))))


def _head_masks():
    lane = lax.broadcasted_iota(jnp.int32, (1, LANES), 1)
    return lane < HEAD_DIM, lane >= HEAD_DIM


LOG_WEIGHT_FLOOR = -110.0


def _key_norm_bound(k_ref, masks, T):
    ch = _tile(T, 512)

    def chunk(r, m):
        kk = k_ref[pl.ds(pl.multiple_of(r * ch, ch), ch), :].astype(F32)
        k2 = kk * kk
        return tuple(jnp.maximum(m[h], jnp.max(jnp.sum(jnp.where(masks[h], k2, 0.0), -1, keepdims=True),
                                               axis=0, keepdims=True)) for h in (0, 1))

    m0, m1 = lax.fori_loop(0, T // ch, chunk, (jnp.zeros((1, 1), F32), jnp.zeros((1, 1), F32)))
    row = lax.broadcasted_iota(jnp.int32, (8, LANES), 0)
    return jnp.where(row == 0, jnp.sqrt(m0), jnp.sqrt(m1))


def _score_bound(qh, kn):
    qf = qh.astype(F32)
    return jnp.sqrt(jnp.sum(qf * qf, -1, keepdims=True)) * (kn * 1.01) + 0.01


def _some_weight_left(carries, bounds):
    m = jnp.maximum(jnp.max(carries[0] + bounds[0]), jnp.max(carries[1] + bounds[1]))
    return m > LOG_WEIGHT_FLOOR


def _attn_fwd(qkv, g_attn, *, T, AW, a_cols, name, comm=None):
    P = AW // LANES
    tq = _tile(T, 256)
    nq = T // tq
    scale = HEAD_DIM ** -0.5
    ci_specs, co_specs, co_shapes, csems = _comm_specs(comm)
    nci, nco = len(ci_specs), len(co_specs)

    def body(q_ref, k_ref, v_ref, g_ref, *rest):
        o_ref, a_ref, kn_ref = rest[nci], rest[nci + 1], rest[nci + 2 + nco]
        at_entry, at_exit = _grid2_hooks(comm, P, nq, (rest[:nci], rest[nci + 2:nci + 2 + nco], rest[nci + 3 + nco:]))
        at_entry()
        i = pl.program_id(1)
        lo_mask, hi_mask = masks = _head_masks()

        @pl.when(i == 0)
        def _():
            kn_ref[...] = _key_norm_bound(k_ref, masks, T)

        rows = lax.broadcasted_iota(jnp.int32, (tq, tq), 0)
        cols = lax.broadcasted_iota(jnp.int32, (tq, tq), 1)
        strict = cols < rows
        tri = jnp.where(rows >= cols, 1.0, 0.0).astype(BF16)
        q = q_ref[...]
        qhs = [jnp.where(m, q, jnp.zeros_like(q)) * jnp.asarray(scale, BF16) for m in masks]
        zbs = [_score_bound(qhs[h], kn_ref[h:h + 1, 0:1]) for h in (0, 1)]

        def block(j, carry, mask=None):
            st = pl.multiple_of(j * tq, tq)
            kj = k_ref[pl.ds(st, tq), :]
            vj = v_ref[pl.ds(st, tq), :]
            new = []
            for h in (0, 1):
                acc, c = carry[h]
                z = lax.dot_general(qhs[h], kj, _DIMS['nt'], preferred_element_type=F32)
                l = _log_gap(z)
                if mask is not None:
                    l = jnp.where(mask, l, 0.0)
                cum = _dot2(l, tri)
                w = jnp.exp(z + cum + c)
                if mask is not None:
                    w = jnp.where(mask, w, 0.0)
                new.append((acc + _dot2(w, vj), c + cum[:, 0:1]))
            return tuple(new)

        zero = (jnp.zeros((tq, LANES), F32), jnp.zeros((tq, 1), F32))
        carry = block(i, (zero, zero), strict)
        carry = block(jnp.maximum(i - 1, 0), carry, jnp.logical_and(i > 0, cols >= 0))

        def live(st):
            jj, cr = st
            return jnp.logical_and(jj < i - 1, _some_weight_left([cr[0][1], cr[1][1]], zbs))

        _, carry = lax.while_loop(live, lambda st: (st[0] + 1, block(i - 2 - st[0], st[1])),
                                  (jnp.int32(0), carry))
        o = jnp.where(lo_mask, carry[0][0], carry[1][0])
        o2 = o * o
        r0 = lax.rsqrt(jnp.sum(jnp.where(lo_mask, o2, 0.0), -1, keepdims=True) * (1.0 / HEAD_DIM) + RMS_EPS)
        r1 = lax.rsqrt(jnp.sum(jnp.where(hi_mask, o2, 0.0), -1, keepdims=True) * (1.0 / HEAD_DIM) + RMS_EPS)
        o_ref[...] = o
        a_ref[...] = ((o * jnp.where(lo_mask, r0, r1)) * g_ref[...]).astype(BF16)
        at_exit()

    res = pl.pallas_call(
        body, grid=(P, nq),
        in_specs=[pl.BlockSpec((tq, LANES), lambda p, i: (i, p)),
                  pl.BlockSpec((T, LANES), lambda p, i: (0, P + p)),
                  pl.BlockSpec((T, LANES), lambda p, i: (0, 2 * P + p)),
                  pl.BlockSpec((1, LANES), lambda p, i: (0, p))] + ci_specs,
        out_specs=[pl.BlockSpec((tq, LANES), lambda p, i: (i, p)),
                   pl.BlockSpec((tq, LANES), lambda p, i: (i, p))] + co_specs,
        out_shape=[jax.ShapeDtypeStruct((T, AW), F32), jax.ShapeDtypeStruct((T, a_cols), BF16)] + co_shapes,
        scratch_shapes=[pltpu.VMEM((8, LANES), F32)] + csems,
        compiler_params=_cparams(("parallel", "arbitrary") if comm is None else ("arbitrary", "arbitrary")),
        name=name)(qkv, qkv, qkv, g_attn, *([] if comm is None else comm.ins))
    return (res[0], res[1]) if comm is None else (res[0], res[1], res[2:])


def _attn_bwd(qkv, o, da, g_attn, *, T, AW, name):
    P = AW // LANES
    tq = _tile(T, 256)
    nq = T // tq
    scale = HEAD_DIM ** -0.5

    def body(q_ref, k_ref, v_ref, o_ref, da_ref, g_ref, dq_ref, dk_out, dv_out, dg_ref, kn_ref, dk_ref, dv_ref):
        i = pl.program_id(1)
        lo_mask, hi_mask = masks = _head_masks()

        @pl.when(i == 0)
        def _():
            dk_ref[...] = jnp.zeros_like(dk_ref)
            dv_ref[...] = jnp.zeros_like(dv_ref)
            dg_ref[...] = jnp.zeros_like(dg_ref)
            kn_ref[...] = _key_norm_bound(k_ref, masks, T)

        rows = lax.broadcasted_iota(jnp.int32, (tq, tq), 0)
        cols = lax.broadcasted_iota(jnp.int32, (tq, tq), 1)
        strict = cols < rows
        tri = jnp.where(rows >= cols, 1.0, 0.0).astype(BF16)
        tri_s = jnp.where(rows > cols, 1.0, 0.0).astype(BF16)
        q = q_ref[...]
        o = o_ref[...]
        da = da_ref[...]
        g = g_ref[...]
        o2 = o * o
        r0 = lax.rsqrt(jnp.sum(jnp.where(lo_mask, o2, 0.0), -1, keepdims=True) * (1.0 / HEAD_DIM) + RMS_EPS)
        r1 = lax.rsqrt(jnp.sum(jnp.where(hi_mask, o2, 0.0), -1, keepdims=True) * (1.0 / HEAD_DIM) + RMS_EPS)
        r = jnp.where(lo_mask, r0, r1)
        oh = o * r
        gy = da * g
        gyo = gy * oh
        m0 = jnp.sum(jnp.where(lo_mask, gyo, 0.0), -1, keepdims=True) * (1.0 / HEAD_DIM)
        m1 = jnp.sum(jnp.where(hi_mask, gyo, 0.0), -1, keepdims=True) * (1.0 / HEAD_DIM)
        do = r * (gy - oh * jnp.where(lo_mask, m0, m1))
        dg_ref[...] += _colsum(da * oh)

        qhs = [jnp.where(m, q, jnp.zeros_like(q)) * jnp.asarray(scale, BF16) for m in masks]
        zbs = [_score_bound(qhs[h], kn_ref[h:h + 1, 0:1]) for h in (0, 1)]
        do_bs = [jnp.where(m, do, 0.0).astype(BF16) for m in masks]
        deltas = [jnp.sum(d.astype(F32) * o, -1, keepdims=True) for d in do_bs]
        q_ts = [qh.astype(F32).T.astype(BF16) for qh in qhs]
        do_ts = [d.astype(F32).T.astype(BF16) for d in do_bs]

        def block(j, carry, mask=None):
            masked = mask is not None
            st = pl.multiple_of(j * tq, tq)
            kj = k_ref[pl.ds(st, tq), :]
            vj = v_ref[pl.ds(st, tq), :]
            new = []
            dk = dv = None
            for h in (0, 1):
                dq, c, gsum = carry[h]
                z = lax.dot_general(qhs[h], kj, _DIMS['nt'], preferred_element_type=F32)
                l = _log_gap(z)
                sig = jnp.exp(z + l)
                if masked:
                    l = jnp.where(mask, l, 0.0)
                cum = _dot2(l, tri)
                w = jnp.exp(z + cum + c)
                if masked:
                    w = jnp.where(mask, w, 0.0)
                dp = lax.dot_general(do_bs[h], vj, _DIMS['nt'], preferred_element_type=F32)
                pw = w * dp
                after = _dot2(pw, tri_s)
                dz = pw - sig * (deltas[h] - gsum - after)
                if masked:
                    dz = jnp.where(mask, dz, 0.0)
                dz_b = dz.astype(BF16)
                dk_h = jnp.dot(q_ts[h], dz_b, preferred_element_type=F32)
                dv_h = jnp.dot(do_ts[h], w.astype(BF16), preferred_element_type=F32)
                dk = dk_h if dk is None else dk + dk_h
                dv = dv_h if dv is None else dv + dv_h
                dq = dq + jnp.dot(dz_b, kj, preferred_element_type=F32)
                new.append((dq, c + cum[:, 0:1], gsum + (after[:, 0:1] + pw[:, 0:1])))
            dk_ref[j] += dk
            dv_ref[j] += dv
            return tuple(new)

        zero1 = jnp.zeros((tq, 1), F32)
        zero = (jnp.zeros((tq, LANES), F32), zero1, zero1)
        carry = block(i, (zero, zero), strict)
        carry = block(jnp.maximum(i - 1, 0), carry, jnp.logical_and(i > 0, cols >= 0))

        def live(st):
            jj, cr = st
            return jnp.logical_and(jj < i - 1, _some_weight_left([cr[0][1], cr[1][1]], zbs))

        _, carry = lax.while_loop(live, lambda st: (st[0] + 1, block(i - 2 - st[0], st[1])),
                                  (jnp.int32(0), carry))
        dq_ref[...] = (jnp.where(lo_mask, carry[0][0], carry[1][0]) * scale).astype(BF16)

        @pl.when(i == nq - 1)
        def _():
            def turn(j, carry_):
                st = pl.multiple_of(j * tq, tq)
                dk_out[pl.ds(st, tq), :] = dk_ref[j].T.astype(BF16)
                dv_out[pl.ds(st, tq), :] = dv_ref[j].T.astype(BF16)
                return carry_

            lax.fori_loop(0, nq, turn, 0)

    return pl.pallas_call(
        body, grid=(P, nq),
        in_specs=[pl.BlockSpec((tq, LANES), lambda p, i: (i, p)),
                  pl.BlockSpec((T, LANES), lambda p, i: (0, P + p)),
                  pl.BlockSpec((T, LANES), lambda p, i: (0, 2 * P + p)),
                  pl.BlockSpec((tq, LANES), lambda p, i: (i, p)),
                  pl.BlockSpec((tq, LANES), lambda p, i: (i, p)),
                  pl.BlockSpec((1, LANES), lambda p, i: (0, p))],
        out_specs=[pl.BlockSpec((tq, LANES), lambda p, i: (i, p)),
                   pl.BlockSpec((T, LANES), lambda p, i: (0, p)),
                   pl.BlockSpec((T, LANES), lambda p, i: (0, p)),
                   pl.BlockSpec((1, LANES), lambda p, i: (0, p))],
        out_shape=[jax.ShapeDtypeStruct((T, AW), BF16)] * 3 + [jax.ShapeDtypeStruct((1, AW), F32)],
        scratch_shapes=[pltpu.VMEM((8, LANES), F32), pltpu.VMEM((nq, LANES, tq), F32),
                        pltpu.VMEM((nq, LANES, tq), F32)],
        compiler_params=_cparams(("parallel", "arbitrary")), name=name)(qkv, qkv, qkv, o, da, g_attn)


def _ada_fwd(c_all, w_ada, b_ada, *, name):
    def body(c_ref, w_ref, b_ref, o_ref):
        cv = c_ref[...]
        sc = cv * _sigmoid(cv)
        o_ref[...] = jnp.dot(sc, w_ref[...], preferred_element_type=F32,
                             precision=lax.Precision.HIGHEST) + b_ref[...]

    return pl.pallas_call(body, out_shape=jax.ShapeDtypeStruct((c_all.shape[0], w_ada.shape[1]), F32),
                          compiler_params=_cparams(), name=name)(c_all, w_ada, b_ada)


def _ada_bwd(c_all_t, dmod, *, name):
    def body(c_ref, d_ref, o_ref):
        cv = c_ref[...]
        sc = cv * _sigmoid(cv)
        o_ref[...] = jnp.dot(sc, d_ref[...], preferred_element_type=F32, precision=lax.Precision.HIGHEST)

    return pl.pallas_call(body, out_shape=jax.ShapeDtypeStruct((c_all_t.shape[0], dmod.shape[1]), F32),
                          compiler_params=_cparams(), name=name)(c_all_t, dmod)


def _adamw(w, g, m, v, *, name):
    R, C = w.shape
    tr = _tile(R, max(8, (1 << 18) // C), 8)

    def body(w_ref, g_ref, m_ref, v_ref, d_ref, nm_ref, nv_ref):
        gv = g_ref[...]
        m2 = ADAM_B1 * m_ref[...] + (1.0 - ADAM_B1) * gv
        v2 = ADAM_B2 * v_ref[...] + (1.0 - ADAM_B2) * jnp.square(gv)
        m_hat = m2 / (1.0 - ADAM_B1 ** ADAM_STEP)
        v_hat = v2 / (1.0 - ADAM_B2 ** ADAM_STEP)
        d_ref[...] = -ADAM_LR * (m_hat / (jnp.sqrt(v_hat) + ADAM_EPS) + ADAM_WD * w_ref[...])
        nm_ref[...] = m2
        nv_ref[...] = v2

    spec = pl.BlockSpec((tr, C), lambda i: (i, 0))
    return pl.pallas_call(
        body, grid=(R // tr,), in_specs=[spec] * 4, out_specs=[spec] * 3,
        out_shape=[jax.ShapeDtypeStruct((R, C), F32)] * 3,
        compiler_params=_cparams(("parallel",)), name=name)(w, g, m, v)


def _sum_devices(a, *, name):
    def body(a_ref, o_ref):
        s = a_ref[0]
        for d in range(1, a_ref.shape[0]):
            s = s + a_ref[d]
        o_ref[...] = s

    return pl.pallas_call(body, out_shape=jax.ShapeDtypeStruct(a.shape[1:], F32),
                          compiler_params=_cparams(), name=name)(a)


def _place():
    return lax.axis_index("x"), lax.axis_index("y"), lax.axis_index("c")


def _flip(v, bit):
    return 1 - v if bit else v


def _allgather8(blk, *, name):
    R, C = blk.shape

    def body(x_ref, out_ref, send_sems, recv_sems):
        x, y, c = _place()
        me = 4 * x + 2 * y + c
        out_ref[me] = x_ref[...]
        copies = []
        for k in range(1, 8):
            peer = (_flip(x, (k >> 2) & 1), _flip(y, (k >> 1) & 1), _flip(c, k & 1))
            cp = pltpu.make_async_remote_copy(
                src_ref=x_ref, dst_ref=out_ref.at[me], send_sem=send_sems.at[k - 1],
                recv_sem=recv_sems.at[k - 1], device_id=peer, device_id_type=MESH)
            cp.start()
            copies.append(cp)
        for cp in copies:
            cp.wait()

    return pl.pallas_call(
        body, out_shape=jax.ShapeDtypeStruct((8, R, C), F32),
        in_specs=[pl.BlockSpec(memory_space=pltpu.VMEM)], out_specs=pl.BlockSpec(memory_space=pltpu.VMEM),
        scratch_shapes=[pltpu.SemaphoreType.DMA((7,)), pltpu.SemaphoreType.DMA((7,))],
        compiler_params=_cparams(), name=name)(blk)


def _aligned(v, m):
    return v if isinstance(v, int) else pl.multiple_of(v, m)


def _rows_half(ref, half):
    n = ref.shape[0] // 2
    return ref.at[pl.ds(_aligned(half * n, 16), n)]


def _region(ref, kind, slot, half):
    if kind == 'col':
        n, cs = ref.shape[0] // 2, ref.shape[1] // 4
        return ref.at[pl.ds(_aligned(half * n, 16), n), pl.ds(_aligned(slot * cs, LANES), cs)]
    rs = ref.shape[0] // 4
    return ref.at[pl.ds(_aligned(slot * rs + half * (rs // 2), 16), rs // 2)]


def _other_chips(x, y):
    return [(1 - x, y), (x, 1 - y), (1 - x, 1 - y)]


class _Comm:
    def __init__(self, ins, outs, sems, start, finish):
        self.ins, self.outs, self.sems, self.start, self.finish = list(ins), list(outs), list(sems), start, finish


def _comm_specs(comm):
    if comm is None:
        return [], [], [], []
    anyspec = pl.BlockSpec(memory_space=pl.ANY)
    return [anyspec] * len(comm.ins), [anyspec] * len(comm.outs), list(comm.outs), list(comm.sems)


def _comm_hooks(comm, first, last, refs):
    if comm is None:
        return (lambda: None), (lambda: None)

    def at_entry():
        pl.when(first)(lambda: comm.start(*refs))

    def at_exit():
        pl.when(last)(lambda: comm.finish(*refs))

    return at_entry, at_exit


def _comm_alone(comm, *, name):
    ni, no = len(comm.ins), len(comm.outs)

    def body(*refs):
        parts = (refs[:ni], refs[ni:ni + no], refs[ni + no:])
        comm.start(*parts)
        comm.finish(*parts)

    i_specs, o_specs, o_shapes, sems = _comm_specs(comm)
    return pl.pallas_call(body, out_shape=o_shapes, in_specs=i_specs, out_specs=o_specs, scratch_shapes=sems,
                          compiler_params=_cparams(), name=name)(*comm.ins)


def _gather_comm(shards, kinds):
    nw = len(shards)
    full_shapes = []
    for s, kind in zip(shards, kinds):
        full_shapes.append((s.shape[0], 4 * s.shape[1]) if kind == 'col' else (4 * s.shape[0], s.shape[1]))

    def copies(sh, full, sems):
        lsem, ssem, rsem, fssem, frsem = sems
        x, y, c = _place()
        me_slot = 2 * x + y
        chips = _other_chips(x, y)
        local, ici, landed, fwd, passed = [], [], [], [], []
        for w in range(nw):
            for h in (0, 1):
                local.append(pltpu.make_async_copy(_rows_half(sh[w], h), _region(full[w], kinds[w], me_slot, h),
                                                   lsem.at[w, h]))
            for r, (px, py) in enumerate(chips):
                ici.append(pltpu.make_async_remote_copy(
                    src_ref=_rows_half(sh[w], c), dst_ref=_region(full[w], kinds[w], me_slot, c),
                    send_sem=ssem.at[w, r], recv_sem=rsem.at[w, r], device_id=(px, py, c), device_id_type=MESH))
                mine = _region(full[w], kinds[w], 2 * px + py, c)
                landed.append(pltpu.make_async_remote_copy(
                    src_ref=mine, dst_ref=mine, send_sem=ssem.at[w, r], recv_sem=rsem.at[w, r],
                    device_id=(px, py, c), device_id_type=MESH))
                fwd.append(pltpu.make_async_remote_copy(
                    src_ref=mine, dst_ref=mine, send_sem=fssem.at[w, r], recv_sem=frsem.at[w, r],
                    device_id=(x, y, 1 - c), device_id_type=MESH))
                theirs = _region(full[w], kinds[w], 2 * px + py, 1 - c)
                passed.append(pltpu.make_async_remote_copy(
                    src_ref=theirs, dst_ref=theirs, send_sem=fssem.at[w, r], recv_sem=frsem.at[w, r],
                    device_id=(x, y, 1 - c), device_id_type=MESH))
        return local, ici, landed, fwd, passed

    def start(sh, full, sems):
        local, ici, _, _, _ = copies(sh, full, sems)
        for cp in local + ici:
            cp.start()

    def finish(sh, full, sems):
        local, ici, landed, fwd, passed = copies(sh, full, sems)
        for got, cp in zip(landed, fwd):
            got.wait_recv()
            cp.start()
        for got in passed:
            got.wait_recv()
        for cp in ici + fwd:
            cp.wait_send()
        for cp in local:
            cp.wait()

    return _Comm(shards, [jax.ShapeDtypeStruct(s, BF16) for s in full_shapes],
                 [pltpu.SemaphoreType.DMA((nw, 2))] + [pltpu.SemaphoreType.DMA((nw, 3))] * 4, start, finish)


def _exchange_comm(grads, kinds):
    nw = len(grads)

    def copies(g, r1, sems):
        ssem, rsem = sems
        x, y, c = _place()
        out, back = [], []
        for w in range(nw):
            for slot in range(4):
                out.append(pltpu.make_async_remote_copy(
                    src_ref=_region(g[w], kinds[w], slot, 1 - c), dst_ref=_region(r1[w], kinds[w], slot, 1 - c),
                    send_sem=ssem.at[w, slot], recv_sem=rsem.at[w, slot], device_id=(x, y, 1 - c),
                    device_id_type=MESH))
                mine = _region(r1[w], kinds[w], slot, c)
                back.append(pltpu.make_async_remote_copy(
                    src_ref=mine, dst_ref=mine, send_sem=ssem.at[w, slot], recv_sem=rsem.at[w, slot],
                    device_id=(x, y, 1 - c), device_id_type=MESH))
        return out, back

    def start(g, r1, sems):
        for cp in copies(g, r1, sems)[0]:
            cp.start()

    def finish(g, r1, sems):
        out, back = copies(g, r1, sems)
        for got in back:
            got.wait_recv()
        for cp in out:
            cp.wait_send()

    return _Comm(grads, [jax.ShapeDtypeStruct(g.shape, F32) for g in grads],
                 [pltpu.SemaphoreType.DMA((nw, 4))] * 2, start, finish)


def _add_core_halves(g, r1, place, kind, *, name):
    if kind == 'col':
        n, cs = g.shape[0] // 2, g.shape[1] // 4
        tr = _tile(n, 256, 16)
        nt = n // tr
        ispec = pl.BlockSpec((tr, cs), lambda s, t, pr: (pr[0] * nt + t, s))
    else:
        rs, cs = g.shape[0] // 4, g.shape[1]
        n = rs // 2
        tr, nt = n, 1
        ispec = pl.BlockSpec((tr, cs), lambda s, t, pr: (s * 2 + pr[0], 0))

    def body(pr, a_ref, b_ref, o_ref):
        o_ref[...] = (a_ref[...] + b_ref[...]).astype(BF16)

    return pl.pallas_call(
        body,
        grid_spec=pltpu.PrefetchScalarGridSpec(
            num_scalar_prefetch=1, grid=(4, nt), in_specs=[ispec, ispec],
            out_specs=pl.BlockSpec((None, tr, cs), lambda s, t, pr: (s, t, 0))),
        out_shape=jax.ShapeDtypeStruct((4, n, cs), BF16),
        compiler_params=_cparams(("parallel", "parallel")), name=name)(place, g, r1)


def _scatter_comm(hs):
    nw = len(hs)

    def copies(h, r2, sems):
        ssem, rsem = sems
        x, y, c = _place()
        return [pltpu.make_async_remote_copy(
            src_ref=h[w].at[2 * px + py], dst_ref=r2[w].at[r], send_sem=ssem.at[w, r],
            recv_sem=rsem.at[w, r], device_id=(px, py, c), device_id_type=MESH)
            for w in range(nw) for r, (px, py) in enumerate(_other_chips(x, y))]

    def start(h, r2, sems):
        for cp in copies(h, r2, sems):
            cp.start()

    def finish(h, r2, sems):
        for cp in copies(h, r2, sems):
            cp.wait()

    return _Comm(hs, [jax.ShapeDtypeStruct((3,) + a.shape[1:], a.dtype) for a in hs],
                 [pltpu.SemaphoreType.DMA((nw, 3))] * 2, start, finish)


def _sum_owner(hs, r2, place, *, name):
    _, n, cs = hs.shape
    tr = _tile(n, 256, 16)
    nt = n // tr

    def body(pr, h_ref, r_ref, o_ref):
        o_ref[...] = ((h_ref[...].astype(F32) + r_ref[0].astype(F32)) + r_ref[1].astype(F32)) + r_ref[2].astype(F32)

    return pl.pallas_call(
        body,
        grid_spec=pltpu.PrefetchScalarGridSpec(
            num_scalar_prefetch=1, grid=(nt,),
            in_specs=[pl.BlockSpec((None, tr, cs), lambda t, pr: (pr[1], t, 0)),
                      pl.BlockSpec((3, tr, cs), lambda t, pr: (0, t, 0))],
            out_specs=pl.BlockSpec((None, tr, cs), lambda t, pr: (pr[0], t, 0))),
        out_shape=jax.ShapeDtypeStruct((2, n, cs), F32),
        compiler_params=_cparams(("parallel",)), name=name)(place, hs, r2)


def _share_with_sibling(fins, *, name):
    nw = len(fins)

    def body(*refs):
        fin, out = refs[:nw], refs[nw:2 * nw]
        ssem, rsem = refs[2 * nw:]
        x, y, c = _place()
        copies = []
        for w in range(nw):
            cp = pltpu.make_async_remote_copy(
                src_ref=fin[w].at[c], dst_ref=out[w].at[c], send_sem=ssem.at[w], recv_sem=rsem.at[w],
                device_id=(x, y, 1 - c), device_id_type=MESH)
            cp.start()
            copies.append(cp)
        for w in range(nw):
            theirs = out[w].at[1 - c]
            pltpu.make_async_remote_copy(
                src_ref=theirs, dst_ref=theirs, send_sem=ssem.at[w], recv_sem=rsem.at[w],
                device_id=(x, y, 1 - c), device_id_type=MESH).wait_recv()
        for cp in copies:
            cp.wait_send()

    anyspec = pl.BlockSpec(memory_space=pl.ANY)
    return pl.pallas_call(
        body, out_shape=[jax.ShapeDtypeStruct(a.shape, F32) for a in fins],
        in_specs=[anyspec] * nw, out_specs=[anyspec] * nw,
        input_output_aliases={w: w for w in range(nw)},
        scratch_shapes=[pltpu.SemaphoreType.DMA((nw,))] * 2,
        compiler_params=_cparams(), name=name)(*fins)


class _Plan:
    def __init__(self, wfull):
        self.w = dict(wfull)
        self.grads = {}

    def comm(self, site):
        return None

    def done(self, site, results):
        pass

    def ready(self, group, names, arrays):
        self.grads.update(zip(names, arrays))


def _riding(plan, site, call):
    comm = plan.comm(site)
    res = call(comm)
    if comm is None:
        return res
    *main, extra = res
    plan.done(site, extra)
    return main[0] if len(main) == 1 else tuple(main)


def _merge_comms(comms):
    if len(comms) == 1:
        return comms[0]

    def parts(refs, field):
        out, pos = [], 0
        for c in comms:
            n = len(getattr(c, field))
            out.append(refs[pos:pos + n])
            pos += n
        return out

    def run(which):
        def fn(ins, outs, sems):
            for c, i, o, s in zip(comms, parts(ins, 'ins'), parts(outs, 'outs'), parts(sems, 'sems')):
                getattr(c, which)(i, o, s)
        return fn

    return _Comm(sum((c.ins for c in comms), []), sum((c.outs for c in comms), []),
                 sum((c.sems for c in comms), []), run('start'), run('finish'))


class _DistPlan(_Plan):
    RIDES = {
        'ffn_in_ff1': [('gather', ['ff1_w_out', 'w_in_mix', 'w_out_mix'])],
        'attn_fwd': [('gather', ['ff2_w_in', 'ff2_w_out'])],
        'pre_bwd_ff2': [('exchange', ['ff2_w_in', 'ff2_w_out'])],
        'conv_bwd1': [('scatter', ['ff2_w_in', 'ff2_w_out'])],
        'pre_bwd_mix': [('exchange', ['w_in_mix', 'w_out_mix'])],
        'ffn_dact_ff1': [('scatter', ['w_in_mix', 'w_out_mix'])],
        'dw_out_ff1': [('exchange', ['ff1_w_in'])],
        'dh_ff1': [('scatter', ['ff1_w_in']), ('exchange', ['ff1_w_out'])],
    }
    AFTER = [('scatter', ['ff1_w_out'])]

    def __init__(self, shards, place):
        self.shards, self.place, self.kind = shards, place, dict(BIG)
        self.w, self.grads, self.hs, self.fin = {}, {}, {}, {}

    def _make(self, kind, names):
        if kind == 'gather':
            return _gather_comm([self.shards[n] for n in names], [self.kind[n] for n in names])
        if kind == 'exchange':
            return _exchange_comm([self.grads[n] for n in names], [self.kind[n] for n in names])
        return _scatter_comm([self.hs[n] for n in names])

    def _take(self, kind, names, results):
        for n, r in zip(names, results):
            if kind == 'gather':
                self.w[n] = r
            elif kind == 'exchange':
                self.hs[n] = _add_core_halves(self.grads[n], r, self.place, self.kind[n], name=f"grad_core_add_{n}")
            else:
                self.fin[n] = _sum_owner(self.hs[n], r, self.place, name=f"grad_owner_sum_{n}")

    def gather_now(self, names, *, name):
        self._take('gather', names, _comm_alone(self._make('gather', names), name=name))

    def comm(self, site):
        rides = self.RIDES.get(site)
        return None if rides is None else _merge_comms([self._make(k, names) for k, names in rides])

    def done(self, site, results):
        pos = 0
        for kind, names in self.RIDES[site]:
            self._take(kind, names, results[pos:pos + len(names)])
            pos += len(names)

    def finish(self):
        for kind, names in self.AFTER:
            self._take(kind, names, _comm_alone(self._make(kind, names), name=f"grad_{kind}_{names[0]}"))
        names = list(self.shards)
        return dict(zip(names, _share_with_sibling([self.fin[n] for n in names], name="grad_share")))


def _local_step(x, target, mod, gains, plan, g_attn, conv_w, cvec):
    T, D = x.shape
    F = plan.w['ff1_w_in'].shape[1] // 2
    AW = D // 2
    C = D - AW
    NQKV = 3 * AW
    MIX = NQKV + 2 * C
    tM = _tile(T, 1024)
    tkT = _tile(T, 1024)

    def ffn_fwd(xin, s, tag):
        h = _pre_fwd(xin, gains, mod, T=T, s=s, name=f"pre_fwd_{tag}")
        w_in = plan.w[f"{tag}_w_in"]
        jac, act = _riding(plan, f"ffn_in_{tag}",
                           lambda cm: _ffn_in(h, w_in, T=T, D=D, F=F, name=f"ffn_in_{tag}", comm=cm))
        f = _matmul(act, plan.w[f"{tag}_w_out"], mode='nn', M=T, N=D, K=F, tm=tM, tn=_tile(D, 1024), tk=F,
                    out_dtype=F32, name=f"ffn_out_{tag}")
        return h, jac, act, f

    def ffn_bwd(dout, xin, saved, s, res_w, tag):
        h, jac, act, f = saved
        w_in, w_out = plan.w[f"{tag}_w_in"], plan.w[f"{tag}_w_out"]
        df, dgate, dgpost = _post_bwd(dout, f, gains, mod, T=T, s=s, res_w=res_w, name=f"post_bwd_{tag}")
        dgu = _riding(plan, f"ffn_dact_{tag}",
                      lambda cm: _ffn_dact(df, w_out, jac, T=T, D=D, F=F, name=f"ffn_dact_{tag}", comm=cm))
        tnf = _tile(F, 1408)
        nf = F // tnf
        dw_in = _matmul(h, dgu, mode='tn', M=D, N=2 * F, K=T, tm=_tile(D, 1024), tn=tnf, tk=tkT, out_dtype=F32,
                        b_spec=pl.BlockSpec((None, tkT, tnf), lambda i, j, k: (j // nf, k, j % nf)),
                        name=f"dw_in_{tag}")
        plan.ready(tag, [f"{tag}_w_in"], [dw_in])
        dw_out = _riding(plan, f"dw_out_{tag}", lambda cm: _matmul(
            act, df, mode='tn', M=F, N=D, K=T, tm=_tile(F, 1408), tn=_tile(D, 1024), tk=tkT, out_dtype=F32,
            name=f"dw_out_{tag}", comm=cm))
        plan.ready(tag, [f"{tag}_w_out"], [dw_out])
        dh = _riding(plan, f"dh_{tag}", lambda cm: _matmul(
            dgu, w_in, mode='nt', M=T, N=D, K=2 * F, tm=tM, tn=_tile(D, 1024), tk=F, out_dtype=F32,
            a_spec=pl.BlockSpec((None, tM, F), lambda i, j, k: (k, i, 0)), name=f"dh_{tag}", comm=cm))
        dx, dshift, dscale, dgpre = _riding(plan, f"pre_bwd_{tag}", lambda cm: _pre_bwd(
            dh, xin, dout, gains, mod, T=T, s=s, name=f"pre_bwd_{tag}", comm=cm))
        return dx, (dshift, dscale, dgate), dgpre, dgpost

    s1 = ffn_fwd(x, 0, "ff1")
    x1 = _post_fwd(x, s1[3], gains, mod, T=T, s=0, res_w=0.5, name="post_fwd_ff1")

    h2 = _pre_fwd(x1, gains, mod, T=T, s=1, name="pre_fwd_mix")
    w_in_mix, w_out_mix = plan.w['w_in_mix'], plan.w['w_out_mix']
    tnq = _tile(AW, 512)
    qkv = _matmul(h2, w_in_mix, mode='nn', M=T, N=NQKV, K=D, tm=tM, tn=tnq, tk=D, out_dtype=BF16, name="proj_qkv")
    tnc = _tile(C, 512)
    off = NQKV // tnc
    cvg = _matmul(h2, w_in_mix, mode='nn', M=T, N=2 * C, K=D, tm=tM, tn=tnc, tk=D, out_dtype=F32,
                  b_spec=pl.BlockSpec((D, tnc), lambda i, j, k: (0, off + j)), name="proj_conv")
    o_attn, a_attn = _riding(plan, "attn_fwd", lambda cm: _attn_fwd(
        qkv, g_attn, T=T, AW=AW, a_cols=D, name="attn_fwd", comm=cm))
    mixcat = _conv_fwd(cvg, conv_w, cvec, T=T, C=C, name="conv_fwd", into=(a_attn, AW // C))
    f_mix = _matmul(mixcat, w_out_mix, mode='nn', M=T, N=D, K=D, tm=tM, tn=_tile(D, 1024), tk=D, out_dtype=F32,
                    name="mix_out")
    x2 = _post_fwd(x1, f_mix, gains, mod, T=T, s=1, res_w=1.0, name="post_fwd_mix")

    s3 = ffn_fwd(x2, 2, "ff2")
    dout, sq = _post_fwd_loss(x2, s3[3], target, gains, mod, T=T, s=2, res_w=0.5, name="post_fwd_loss")

    dx2, dmod2, dgpre2, dgpost2 = ffn_bwd(dout, x2, s3, 2, 0.5, "ff2")

    df_mix, dgate_m, dgpost_m = _post_bwd(dx2, f_mix, gains, mod, T=T, s=1, res_w=1.0, name="post_bwd_mix")
    dmixcat = _matmul(df_mix, w_out_mix, mode='nt', M=T, N=D, K=D, tm=tM, tn=_tile(D, 1024), tk=D, out_dtype=F32,
                      name="d_mixcat")
    dw_out_mix = _matmul(mixcat, df_mix, mode='tn', M=D, N=D, K=T, tm=_tile(D, 1024), tn=_tile(D, 1024),
                         tk=tkT, out_dtype=F32, name="dw_out_mix")
    dq, dk, dv, dg_attn = _attn_bwd(qkv, o_attn, dmixcat, g_attn, T=T, AW=AW, name="attn_bwd")
    dyc, csum, dconv_w = _riding(plan, "conv_bwd1", lambda cm: _conv_bwd1(
        cvg, (dmixcat, C, AW // C), conv_w, cvec, T=T, C=C, name="conv_bwd1", comm=cm))
    dcv, dcg = _conv_bwd2(dyc, cvg, conv_w, T=T, C=C, name="conv_bwd2")
    dproj = jnp.concatenate([dq, dk, dv, dcv, dcg], axis=1)
    dh2 = _matmul(dproj, w_in_mix, mode='nt', M=T, N=D, K=MIX, tm=tM, tn=_tile(D, 1024), tk=MIX, out_dtype=F32,
                  name="dh_mix")
    dw_in_mix = _matmul(h2, dproj, mode='tn', M=D, N=MIX, K=T, tm=_tile(D, 1024), tn=_tile(MIX, 1280),
                        tk=tkT, out_dtype=F32, name="dw_in_mix")
    plan.ready("mix", ['w_in_mix', 'w_out_mix'], [dw_in_mix, dw_out_mix])
    dx1, dshift_m, dscale_m, dgpre_m = _riding(plan, "pre_bwd_mix", lambda cm: _pre_bwd(
        dh2, x1, dx2, gains, mod, T=T, s=1, name="pre_bwd_mix", comm=cm))

    dx0, dmod1, dgpre1, dgpost1 = ffn_bwd(dx1, x, s1, 0, 0.5, "ff1")

    dgains = [dgpre1, dgpost1, dgpre_m, dgpost_m, dgpre2, dgpost2]
    dmod = list(dmod1) + [dshift_m, dscale_m, dgate_m] + list(dmod2)
    return sq, dx0, dgains, dmod, dg_attn, csum, dconv_w


def _pack_rows(pieces, width):
    rows = jnp.concatenate([p.reshape(-1) for p in pieces]).reshape(-1, width)
    pad = (-rows.shape[0]) % 8
    return jnp.pad(rows, ((0, pad), (0, 0)))


def kernel(x, c, w_ada, b_ada, g_pre_ff1, g_post_ff1, ff1_w_in, ff1_w_out, g_pre_mix, g_post_mix, w_in_mix, g_attn_out, conv_w, conv_b, conv_ln_g, conv_ln_b, w_out_mix, g_pre_ff2, g_post_ff2, ff2_w_in, ff2_w_out, loss_target, m_w_ada, m_b_ada, m_g_pre_ff1, m_g_post_ff1, m_ff1_w_in, m_ff1_w_out, m_g_pre_mix, m_g_post_mix, m_w_in_mix, m_g_attn_out, m_conv_w, m_conv_b, m_conv_ln_g, m_conv_ln_b, m_w_out_mix, m_g_pre_ff2, m_g_post_ff2, m_ff2_w_in, m_ff2_w_out, v_w_ada, v_b_ada, v_g_pre_ff1, v_g_post_ff1, v_ff1_w_in, v_ff1_w_out, v_g_pre_mix, v_g_post_mix, v_w_in_mix, v_g_attn_out, v_conv_w, v_conv_b, v_conv_ln_g, v_conv_ln_b, v_w_out_mix, v_g_pre_ff2, v_g_post_ff2, v_ff2_w_in, v_ff2_w_out):
    W = dict(w_ada=w_ada, b_ada=b_ada, g_pre_ff1=g_pre_ff1, g_post_ff1=g_post_ff1, ff1_w_in=ff1_w_in,
             ff1_w_out=ff1_w_out, g_pre_mix=g_pre_mix, g_post_mix=g_post_mix, w_in_mix=w_in_mix,
             g_attn_out=g_attn_out, conv_w=conv_w, conv_b=conv_b, conv_ln_g=conv_ln_g, conv_ln_b=conv_ln_b,
             w_out_mix=w_out_mix, g_pre_ff2=g_pre_ff2, g_post_ff2=g_post_ff2, ff2_w_in=ff2_w_in,
             ff2_w_out=ff2_w_out)
    Mo = dict(w_ada=m_w_ada, b_ada=m_b_ada, g_pre_ff1=m_g_pre_ff1, g_post_ff1=m_g_post_ff1, ff1_w_in=m_ff1_w_in,
              ff1_w_out=m_ff1_w_out, g_pre_mix=m_g_pre_mix, g_post_mix=m_g_post_mix, w_in_mix=m_w_in_mix,
              g_attn_out=m_g_attn_out, conv_w=m_conv_w, conv_b=m_conv_b, conv_ln_g=m_conv_ln_g,
              conv_ln_b=m_conv_ln_b, w_out_mix=m_w_out_mix, g_pre_ff2=m_g_pre_ff2, g_post_ff2=m_g_post_ff2,
              ff2_w_in=m_ff2_w_in, ff2_w_out=m_ff2_w_out)
    Vo = dict(w_ada=v_w_ada, b_ada=v_b_ada, g_pre_ff1=v_g_pre_ff1, g_post_ff1=v_g_post_ff1, ff1_w_in=v_ff1_w_in,
              ff1_w_out=v_ff1_w_out, g_pre_mix=v_g_pre_mix, g_post_mix=v_g_post_mix, w_in_mix=v_w_in_mix,
              g_attn_out=v_g_attn_out, conv_w=v_conv_w, conv_b=v_conv_b, conv_ln_g=v_conv_ln_g,
              conv_ln_b=v_conv_ln_b, w_out_mix=v_w_out_mix, g_pre_ff2=v_g_pre_ff2, g_post_ff2=v_g_post_ff2,
              ff2_w_in=v_ff2_w_in, ff2_w_out=v_ff2_w_out)

    T, D = x.shape[1], x.shape[2]
    AW = D // 2
    C = D - AW
    xi, yi, ci = _place()
    me = 4 * xi + 2 * yi + ci
    chip = 2 * xi + yi
    place = jnp.stack([ci, chip]).astype(jnp.int32)

    c_all = _allgather8(jnp.tile(c, (8, 1)), name="gather_c")[:, 0, :]
    ncol = w_ada.shape[1]
    b_cols = lax.dynamic_index_in_dim(b_ada.reshape(4, ncol), chip, keepdims=True).reshape(1, ncol)
    modp = _ada_fwd(c_all, w_ada, b_cols, name="ada_fwd")
    mod_g = _allgather8(modp, name="gather_mod")
    mod_all = jnp.transpose(mod_g[0::2], (1, 0, 2)).reshape(8, 4 * ncol)
    mod = lax.dynamic_index_in_dim(mod_all, me, keepdims=False).reshape(9, D)

    names = [n for n, _ in BIG]
    plan = _DistPlan({n: W[n].astype(BF16) for n in names}, place)
    plan.gather_now(['ff1_w_in'], name="gather_ff1_w_in")
    cs = conv_w.shape[1]
    cw_all = _allgather8(jnp.pad(conv_w, ((0, HALO - CONV_KERNEL), (0, (-cs) % LANES))), name="gather_conv_w")
    conv_w_full = jnp.transpose(cw_all[0::2, :, :cs], (1, 0, 2)).reshape(HALO, 4 * cs)

    gains = _pack_rows([g_pre_ff1, g_post_ff1, g_pre_mix, g_post_mix, g_pre_ff2, g_post_ff2], D)
    cvec = _pack_rows([conv_b, conv_ln_g, conv_ln_b], C)
    g_attn = g_attn_out.reshape(1, AW)

    sq, dx, dgains, dmod, dg_attn, csum, dconv_w = _local_step(
        x[0], loss_target[0], mod, gains, plan, g_attn, conv_w_full, cvec)

    loss_row = jnp.zeros((1, D), F32).at[0, 0].set(jnp.sum(sq) * (0.5 / D))
    small = _pack_rows(dgains + dmod + [dg_attn, csum[0:3], dconv_w, loss_row], D)
    small_all = _allgather8(small, name="gather_small")
    tot = _sum_devices(small_all, name="sum_small")
    n_g, n_m = 6, 9
    r0 = n_g + n_m
    flat = tot.reshape(-1)
    p = r0 * D
    g_attn_grad = flat[p:p + AW]
    p += AW
    gconv_b, gln_g, gln_b = flat[p:p + C], flat[p + C:p + 2 * C], flat[p + 2 * C:p + 3 * C]
    p += 3 * C
    gconv_w_full = flat[p:p + HALO * C].reshape(HALO, C)[:CONV_KERNEL]
    p += HALO * C
    loss = flat[p]
    gconv_w = lax.dynamic_slice_in_dim(gconv_w_full, chip * cs, cs, axis=1)
    grad_small = {'g_pre_ff1': tot[0], 'g_post_ff1': tot[1], 'g_pre_mix': tot[2], 'g_post_mix': tot[3],
                  'g_pre_ff2': tot[4], 'g_post_ff2': tot[5], 'b_ada': tot[n_g:r0].reshape(-1),
                  'g_attn_out': g_attn_grad.reshape(g_attn_out.shape), 'conv_w': gconv_w, 'conv_b': gconv_b,
                  'conv_ln_g': gln_g, 'conv_ln_b': gln_b}

    dmod_all = small_all[:, n_g:r0, :].reshape(8, 9 * D)
    dmod_cols = lax.dynamic_slice_in_dim(dmod_all, chip * ncol, ncol, axis=1)
    grad_w_ada = _ada_bwd(jnp.transpose(c_all), dmod_cols, name="ada_bwd")

    grads = dict(grad_small)
    grads['w_ada'] = grad_w_ada
    for n, a in plan.finish().items():
        grads[n] = a.reshape(W[n].shape)

    delta, new_m, new_v = {}, {}, {}
    for n in ['w_ada'] + names:
        delta[n], new_m[n], new_v[n] = _adamw(W[n], grads[n], Mo[n], Vo[n], name=f"adamw_{n}")
    smalls = [n for n in WEIGHTS if n not in delta]
    sizes = [W[n].size for n in smalls]
    tot_sz = sum(sizes)
    padn = (-tot_sz) % (8 * LANES)

    def pack(d):
        return jnp.pad(jnp.concatenate([d[n].reshape(-1) for n in smalls]), (0, padn)).reshape(-1, LANES)

    d_s, m_s, v_s = _adamw(pack(W), pack(grads), pack(Mo), pack(Vo), name="adamw_small")
    pos = 0
    for n, sz in zip(smalls, sizes):
        for dst, src in ((delta, d_s), (new_m, m_s), (new_v, v_s)):
            dst[n] = src.reshape(-1)[pos:pos + sz].reshape(W[n].shape)
        pos += sz

    return (loss, dx[None], *[grads[n] for n in WEIGHTS], *[delta[n] for n in WEIGHTS],
            *[new_m[n] for n in WEIGHTS], *[new_v[n] for n in WEIGHTS])
```

```python
import functools

import jax
import jax.numpy as jnp
from jax import lax
from jax.experimental import pallas as pl
from jax.experimental.pallas import tpu as pltpu

F32 = jnp.float32
BF16 = jnp.bfloat16
MESH = pl.DeviceIdType.MESH

HEAD_DIM = 64
CONV_KERNEL = 31
RMS_EPS = 1e-6
LN_EPS = 1e-5
ADAM_LR = 0.001
ADAM_B1 = 0.9
ADAM_B2 = 0.999
ADAM_EPS = 1e-08
ADAM_WD = 0.01
ADAM_STEP = 10

LANES = 128
HALO = 32
VMEM_LIMIT = 52 * 1024 * 1024

WEIGHTS = ['w_ada', 'b_ada', 'g_pre_ff1', 'g_post_ff1', 'ff1_w_in', 'ff1_w_out', 'g_pre_mix',
           'g_post_mix', 'w_in_mix', 'g_attn_out', 'conv_w', 'conv_b', 'conv_ln_g', 'conv_ln_b',
           'w_out_mix', 'g_pre_ff2', 'g_post_ff2', 'ff2_w_in', 'ff2_w_out']
BIG = [('ff1_w_in', 'col'), ('ff1_w_out', 'row'), ('w_in_mix', 'col'), ('w_out_mix', 'row'),
       ('ff2_w_in', 'col'), ('ff2_w_out', 'row')]


def _tile(dim, pref, mult=LANES):
    if dim <= pref:
        return dim
    best = None
    for t in range(mult, pref + 1, mult):
        if dim % t == 0:
            best = t
    assert best is not None, (dim, pref, mult)
    return best


def _cparams(sem=None):
    kw = dict(vmem_limit_bytes=VMEM_LIMIT)
    if sem is not None:
        kw['dimension_semantics'] = sem
    return pltpu.CompilerParams(**kw)


def _sigmoid(x):
    return 1.0 / (1.0 + jnp.exp(-x))


_DIMS = {'nn': (((1,), (0,)), ((), ())), 'nt': (((1,), (1,)), ((), ())), 'tn': (((0,), (0,)), ((), ()))}


def _matmul(a, b, *, mode, M, N, K, tm, tn, tk, out_dtype, name, a_spec=None, b_spec=None, comm=None):
    nm, nn, nk = M // tm, N // tn, K // tk
    assert nm * tm == M and nn * tn == N and nk * tk == K, (name, M, N, K, tm, tn, tk)
    if a_spec is None:
        a_spec = (pl.BlockSpec((tk, tm), lambda i, j, k: (k, i)) if mode == 'tn'
                  else pl.BlockSpec((tm, tk), lambda i, j, k: (i, k)))
    if b_spec is None:
        b_spec = (pl.BlockSpec((tn, tk), lambda i, j, k: (j, k)) if mode == 'nt'
                  else pl.BlockSpec((tk, tn), lambda i, j, k: (k, j)))
    dims = _DIMS[mode]
    assert nk == 1 or out_dtype == F32, name
    ci_specs, co_specs, co_shapes, csems = _comm_specs(comm)
    nci, nco = len(ci_specs), len(co_specs)

    def body(a_ref, b_ref, *rest):
        o_ref = rest[nci]
        i, j, k = pl.program_id(0), pl.program_id(1), pl.program_id(2)
        first = jnp.logical_and(jnp.logical_and(i == 0, j == 0), k == 0)
        last = jnp.logical_and(jnp.logical_and(i == nm - 1, j == nn - 1), k == nk - 1)
        at_entry, at_exit = _comm_hooks(comm, first, last, (rest[:nci], rest[nci + 1:nci + 1 + nco], rest[nci + 1 + nco:]))
        at_entry()

        def prod():
            return lax.dot_general(a_ref[...], b_ref[...], dims, preferred_element_type=F32)

        if nk == 1:
            o_ref[...] = prod().astype(o_ref.dtype)
        else:
            @pl.when(k == 0)
            def _():
                o_ref[...] = prod()

            @pl.when(k > 0)
            def _():
                o_ref[...] += prod()
        at_exit()

    sem = ("parallel", "parallel", "arbitrary") if comm is None else ("arbitrary",) * 3
    res = pl.pallas_call(
        body, grid=(nm, nn, nk), in_specs=[a_spec, b_spec] + ci_specs,
        out_specs=[pl.BlockSpec((tm, tn), lambda i, j, k: (i, j))] + co_specs,
        out_shape=[jax.ShapeDtypeStruct((M, N), out_dtype)] + co_shapes, scratch_shapes=csems,
        compiler_params=_cparams(sem), name=name)(a, b, *([] if comm is None else comm.ins))
    return res[0] if comm is None else (res[0], res[1:])


def _grid2_hooks(comm, n0, n1, refs):
    j, i = pl.program_id(0), pl.program_id(1)
    return _comm_hooks(comm, jnp.logical_and(j == 0, i == 0), jnp.logical_and(j == n0 - 1, i == n1 - 1), refs)


def _ffn_in(h, w_in, *, T, D, F, name, comm=None):
    tm, tn = _tile(T, 256), _tile(F, 2816)
    nf, nt = F // tn, T // tm
    ci_specs, co_specs, co_shapes, csems = _comm_specs(comm)
    nci, nco = len(ci_specs), len(co_specs)

    def body(h_ref, wg_ref, wu_ref, *rest):
        jac_ref, a_ref = rest[nci], rest[nci + 1]
        at_entry, at_exit = _grid2_hooks(comm, nf, nt, (rest[:nci], rest[nci + 2:nci + 2 + nco], rest[nci + 2 + nco:]))
        at_entry()
        hh = h_ref[...]
        g = jnp.dot(hh, wg_ref[...], preferred_element_type=F32)
        u = jnp.dot(hh, wu_ref[...], preferred_element_type=F32)
        s = _sigmoid(g)
        sg = g * s
        jac_ref[0] = (u * (s * (1.0 + g * (1.0 - s)))).astype(BF16)
        jac_ref[1] = sg.astype(BF16)
        a_ref[...] = (sg * u).astype(BF16)
        at_exit()

    res = pl.pallas_call(
        body, grid=(nf, nt),
        in_specs=[pl.BlockSpec((tm, D), lambda j, i: (i, 0)),
                  pl.BlockSpec((D, tn), lambda j, i: (0, j)),
                  pl.BlockSpec((D, tn), lambda j, i: (0, nf + j))] + ci_specs,
        out_specs=[pl.BlockSpec((2, tm, tn), lambda j, i: (0, i, j)),
                   pl.BlockSpec((tm, tn), lambda j, i: (i, j))] + co_specs,
        out_shape=[jax.ShapeDtypeStruct((2, T, F), BF16), jax.ShapeDtypeStruct((T, F), BF16)] + co_shapes,
        scratch_shapes=csems,
        compiler_params=_cparams(("parallel", "parallel") if comm is None else ("arbitrary", "arbitrary")),
        name=name)(h, w_in, w_in, *([] if comm is None else comm.ins))
    return (res[0], res[1]) if comm is None else (res[0], res[1], res[2:])


def _ffn_dact(df, w_out, jac, *, T, D, F, name, comm=None):
    tm, tn = _tile(T, 512), _tile(F, 1408)
    nf, nt = F // tn, T // tm
    ci_specs, co_specs, co_shapes, csems = _comm_specs(comm)
    nci, nco = len(ci_specs), len(co_specs)

    def body(df_ref, w_ref, jac_ref, *rest):
        o_ref = rest[nci]
        at_entry, at_exit = _grid2_hooks(comm, nf, nt, (rest[:nci], rest[nci + 1:nci + 1 + nco], rest[nci + 1 + nco:]))
        at_entry()
        da = lax.dot_general(df_ref[...], w_ref[...], _DIMS['nt'], preferred_element_type=F32)
        o_ref[0] = (da * jac_ref[0].astype(F32)).astype(BF16)
        o_ref[1] = (da * jac_ref[1].astype(F32)).astype(BF16)
        at_exit()

    res = pl.pallas_call(
        body, grid=(nf, nt),
        in_specs=[pl.BlockSpec((tm, D), lambda j, i: (i, 0)),
                  pl.BlockSpec((tn, D), lambda j, i: (j, 0)),
                  pl.BlockSpec((2, tm, tn), lambda j, i: (0, i, j))] + ci_specs,
        out_specs=[pl.BlockSpec((2, tm, tn), lambda j, i: (0, i, j))] + co_specs,
        out_shape=[jax.ShapeDtypeStruct((2, T, F), BF16)] + co_shapes, scratch_shapes=csems,
        compiler_params=_cparams(("parallel", "parallel") if comm is None else ("arbitrary", "arbitrary")),
        name=name)(df, w_out, jac, *([] if comm is None else comm.ins))
    return res[0] if comm is None else (res[0], res[1:])


def _rowwise(fn, *, T, tm, name, tiled=(), prev=(), nxt=(), consts=(), out_tiled=(), out_acc=(), scratch=(),
             by_ref=False, comm=None, into=None):
    n = T // tm
    assert n * tm == T and tm % HALO == 0
    hb = tm // HALO
    cols = [a if isinstance(a, tuple) else (a, a.shape[1], 0) for a in tiled]
    tiled = [a for a, _, _ in cols]
    in_specs = [pl.BlockSpec((tm, w), functools.partial(lambda cb, i: (i, cb), cb)) for _, w, cb in cols]
    in_specs += [pl.BlockSpec((HALO, a.shape[1]), lambda i: (jnp.maximum(i * hb - 1, 0), 0)) for a in prev]
    in_specs += [pl.BlockSpec((HALO, a.shape[1]), lambda i: (jnp.minimum((i + 1) * hb, T // HALO - 1), 0))
                 for a in nxt]
    in_specs += [pl.BlockSpec(a.shape, lambda i: (0, 0)) for a in consts]
    out_shape = [jax.ShapeDtypeStruct((T, c), dt) for c, dt in out_tiled]
    out_shape += [jax.ShapeDtypeStruct(s, F32) for s in out_acc]
    out_specs = [pl.BlockSpec((tm, c), lambda i: (i, 0)) for c, _ in out_tiled]
    out_specs += [pl.BlockSpec(s, lambda i: (0, 0)) for s in out_acc]
    nt, npv, nnx, nc, not_, na = len(tiled), len(prev), len(nxt), len(consts), len(out_tiled), len(out_acc)
    ci_specs, co_specs, co_shapes, csems = _comm_specs(comm)
    extra_in, aliases = [], {}
    if into is not None:
        arr, cb = into
        width = out_tiled[0][0]
        out_shape[0] = jax.ShapeDtypeStruct(arr.shape, arr.dtype)
        out_specs[0] = pl.BlockSpec((tm, width), lambda i: (i, cb))
        extra_in = [arr]
        aliases = {nt + npv + nnx + nc: 0}
    n_extra = len(extra_in)

    def body(*refs):
        pos = 0
        groups = []
        for cnt in (nt, npv, nnx, nc, n_extra, len(ci_specs), not_, na, len(co_specs), len(scratch), len(csems)):
            groups.append(refs[pos:pos + cnt])
            pos += cnt
        t_r, p_r, n_r, c_r, _, ci_r, o_r, a_r, co_r, s_r, cs_r = groups
        i = pl.program_id(0)
        at_entry, at_exit = _comm_hooks(comm, i == 0, i == n - 1, (ci_r, co_r, cs_r))
        at_entry()

        @pl.when(i == 0)
        def _():
            for r in a_r:
                r[...] = jnp.zeros_like(r)

        if by_ref:
            fn(i, n, t_r, p_r, n_r, c_r, o_r, a_r, s_r)
        else:
            outs = fn(i, n, [r[...] for r in t_r], [r[...] for r in p_r], [r[...] for r in n_r],
                      [r[...] for r in c_r], a_r, s_r)
            for r, v in zip(o_r, outs):
                r[...] = v.astype(r.dtype)
        at_exit()

    res = pl.pallas_call(
        body, grid=(n,), in_specs=in_specs + [pl.BlockSpec(memory_space=pl.ANY)] * n_extra + ci_specs,
        out_specs=out_specs + co_specs, out_shape=out_shape + co_shapes, scratch_shapes=list(scratch) + csems,
        input_output_aliases=aliases, compiler_params=_cparams(("arbitrary",)), name=name,
    )(*tiled, *prev, *nxt, *consts, *extra_in, *([] if comm is None else comm.ins))
    return res if comm is None else (res[:not_ + na], res[not_ + na:])


def _colsum(v):
    return jnp.sum(v, axis=0, keepdims=True)


def _rowmean(v):
    return jnp.mean(v, axis=-1, keepdims=True)


def _pre_fwd(x, gains, mod, *, T, s, name):
    def fn(i, n, t, p, nx, c, acc, scr):
        xv, (g, m) = t[0], c
        g_pre, shift, scale = g[2 * s:2 * s + 1], m[3 * s:3 * s + 1], m[3 * s + 1:3 * s + 2]
        r = lax.rsqrt(_rowmean(xv * xv) + RMS_EPS)
        return [((xv * r) * g_pre) * (1.0 + scale) + shift]

    return _rowwise(fn, T=T, tm=_tile(T, 512, HALO), name=name, tiled=[x], consts=[gains, mod],
                    out_tiled=[(x.shape[1], BF16)])[0]


def _post_fwd(x, f, gains, mod, *, T, s, res_w, name):
    def fn(i, n, t, p, nx, c, acc, scr):
        (xv, fv), (g, m) = t, c
        g_post, gate = g[2 * s + 1:2 * s + 2], m[3 * s + 2:3 * s + 3]
        y = (fv * lax.rsqrt(_rowmean(fv * fv) + RMS_EPS)) * g_post
        return [xv + (res_w * (1.0 + gate)) * y]

    return _rowwise(fn, T=T, tm=_tile(T, 512, HALO), name=name, tiled=[x, f], consts=[gains, mod],
                    out_tiled=[(x.shape[1], F32)])[0]


def _post_fwd_loss(x, f, target, gains, mod, *, T, s, res_w, name):
    D = x.shape[1]

    def fn(i, n, t, p, nx, c, acc, scr):
        (xv, fv, tv), (g, m) = t, c
        g_post, gate = g[2 * s + 1:2 * s + 2], m[3 * s + 2:3 * s + 3]
        y = (fv * lax.rsqrt(_rowmean(fv * fv) + RMS_EPS)) * g_post
        err = (xv + (res_w * (1.0 + gate)) * y) - tv
        acc[0][...] += _colsum(err * err)
        return [err * (1.0 / D)]

    dout, sq = _rowwise(fn, T=T, tm=_tile(T, 512, HALO), name=name, tiled=[x, f, target], consts=[gains, mod],
                        out_tiled=[(D, F32)], out_acc=[(1, D)])
    return dout, sq


def _post_bwd(dout, f, gains, mod, *, T, s, res_w, name):
    D = f.shape[1]

    def fn(i, n, t, p, nx, c, acc, scr):
        (dv, fv), (g, m) = t, c
        g_post, gate = g[2 * s + 1:2 * s + 2], m[3 * s + 2:3 * s + 3]
        r2 = lax.rsqrt(_rowmean(fv * fv) + RMS_EPS)
        fh = fv * r2
        dy = dv * (res_w * (1.0 + gate))
        acc[0][...] += _colsum(dv * (res_w * (fh * g_post)))
        acc[1][...] += _colsum(dy * fh)
        gy = dy * g_post
        return [r2 * (gy - fh * _rowmean(gy * fh))]

    return _rowwise(fn, T=T, tm=_tile(T, 512, HALO), name=name, tiled=[dout, f], consts=[gains, mod],
                    out_tiled=[(D, BF16)], out_acc=[(1, D), (1, D)])


def _pre_bwd(dh, x, dout, gains, mod, *, T, s, name, comm=None):
    D = x.shape[1]

    def fn(i, n, t, p, nx, c, acc, scr):
        (dhv, xv, dv), (g, m) = t, c
        g_pre, scale = g[2 * s:2 * s + 1], m[3 * s + 1:3 * s + 2]
        r = lax.rsqrt(_rowmean(xv * xv) + RMS_EPS)
        nv = xv * r
        acc[0][...] += _colsum(dhv)
        acc[1][...] += _colsum(dhv * (nv * g_pre))
        acc[2][...] += _colsum(dhv * ((1.0 + scale) * nv))
        gn = dhv * (g_pre * (1.0 + scale))
        return [r * (gn - nv * _rowmean(gn * nv)) + dv]

    return _rowwise(fn, T=T, tm=_tile(T, 512, HALO), name=name, tiled=[dh, x, dout], consts=[gains, mod],
                    out_tiled=[(D, F32)], out_acc=[(1, D), (1, D), (1, D)], comm=comm)


SUBLANES = 8
CONV_CHUNK = 64


def _glu(cvg, C):
    return cvg[:, :C] * _sigmoid(cvg[:, C:])


def _fill_rotations(ext, rot, rows):
    for r in range(SUBLANES):
        rot[r] = ext[pl.ds(r, rows), :]


def _conv_taps(rot, w, r0, rows, off):
    acc = None
    for k in range(CONV_KERNEL):
        a, r = divmod(off(k), SUBLANES)
        term = w[k:k + 1] * rot[r, pl.ds(pl.multiple_of(r0 + a * SUBLANES, SUBLANES), rows), :]
        acc = term if acc is None else acc + term
    return acc


def _causal_off(k):
    return HALO - (CONV_KERNEL - 1) + k


def _conv_norm(rot, cw, cb, r0, rows):
    yc = _conv_taps(rot, cw, r0, rows, _causal_off) + cb
    mu = _rowmean(yc)
    d = yc - mu
    rstd = lax.rsqrt(_rowmean(d * d) + LN_EPS)
    return d * rstd, rstd


def _stage_glu(i, t, p, ext, rot, tm, C):
    ext[pl.ds(0, HALO), :] = jnp.where(i == 0, 0.0, _glu(p[0][...], C))
    ext[pl.ds(HALO, tm), :] = _glu(t[0][...], C)
    ext[pl.ds(HALO + tm, SUBLANES), :] = jnp.zeros((SUBLANES, C), F32)
    _fill_rotations(ext, rot, tm + HALO)


def _conv_scratch(tm, C):
    return [pltpu.VMEM((HALO + tm + SUBLANES, C), F32), pltpu.VMEM((SUBLANES, HALO + tm, C), F32)]


def _conv_fwd(cvg, cw, cvec, *, T, C, name, into=None):
    tm = _tile(T, 512, HALO)
    ch = min(CONV_CHUNK, tm)

    def fn(i, n, t, p, nx, c, o, acc, scr):
        ext, rot = scr
        _stage_glu(i, t, p, ext, rot, tm, C)
        w, vec = c[0][...], c[1][...]

        def chunk(ci, carry):
            r0 = pl.multiple_of(ci * ch, ch)
            yh, _ = _conv_norm(rot, w, vec[0:1], r0, ch)
            zz = yh * vec[1:2] + vec[2:3]
            o[0][pl.ds(r0, ch), :] = (zz * _sigmoid(zz)).astype(BF16)
            return carry

        lax.fori_loop(0, tm // ch, chunk, 0)

    return _rowwise(fn, T=T, tm=tm, name=name, tiled=[cvg], prev=[cvg], consts=[cw, cvec],
                    out_tiled=[(C, BF16)], scratch=_conv_scratch(tm, C), by_ref=True, into=into)[0]


def _conv_bwd1(cvg, duc, cw, cvec, *, T, C, name, comm=None):
    tm = _tile(T, 512, HALO)
    ch = min(CONV_CHUNK, tm)

    def fn(i, n, t, p, nx, c, o, acc, scr):
        ext, rot, w8 = scr

        @pl.when(i == 0)
        def _():
            w8[...] = jnp.zeros_like(w8)

        _stage_glu(i, t, p, ext, rot, tm, C)
        w, vec = c[0][...], c[1][...]
        ln_g = vec[1:2]

        def chunk(ci, carry):
            r0 = pl.multiple_of(ci * ch, ch)
            yh, rstd = _conv_norm(rot, w, vec[0:1], r0, ch)
            zz = yh * ln_g + vec[2:3]
            s = _sigmoid(zz)
            dz = t[1][pl.ds(r0, ch), :] * (s * (1.0 + zz * (1.0 - s)))
            dyh = dz * ln_g
            dyc = rstd * (dyh - _rowmean(dyh) - yh * _rowmean(dyh * yh))
            o[0][pl.ds(r0, ch), :] = dyc
            acc[0][0:1, :] += _colsum(dyc)
            acc[0][1:2, :] += _colsum(dz * yh)
            acc[0][2:3, :] += _colsum(dz)
            for k in range(CONV_KERNEL):
                a, r = divmod(_causal_off(k), SUBLANES)
                prod = dyc * rot[r, pl.ds(pl.multiple_of(r0 + a * SUBLANES, SUBLANES), ch), :]
                part = prod[0:SUBLANES]
                for g in range(1, ch // SUBLANES):
                    part = part + prod[g * SUBLANES:(g + 1) * SUBLANES]
                w8[pl.ds(k * SUBLANES, SUBLANES), :] += part
            return carry

        lax.fori_loop(0, tm // ch, chunk, 0)

        @pl.when(i == n - 1)
        def _():
            for k in range(CONV_KERNEL):
                acc[1][k:k + 1, :] = _colsum(w8[pl.ds(k * SUBLANES, SUBLANES), :])

    return _rowwise(fn, T=T, tm=tm, name=name, tiled=[cvg, duc], prev=[cvg], consts=[cw, cvec],
                    out_tiled=[(C, F32)], out_acc=[(8, C), (HALO, C)],
                    scratch=_conv_scratch(tm, C) + [pltpu.VMEM((HALO * SUBLANES, C), F32)], by_ref=True, comm=comm)


def _conv_bwd2(dyc, cvg, cw, *, T, C, name):
    tm = _tile(T, 512, HALO)
    ch = min(CONV_CHUNK, tm)

    def fn(i, n, t, p, nx, c, o, acc, scr):
        ext, rot = scr
        ext[pl.ds(0, tm), :] = t[0][...]
        ext[pl.ds(tm, HALO), :] = jnp.where(i == n - 1, 0.0, nx[0][...])
        _fill_rotations(ext, rot, tm + HALO - SUBLANES)
        w = c[0][...]

        def chunk(ci, carry):
            r0 = pl.multiple_of(ci * ch, ch)
            dug = _conv_taps(rot, w, r0, ch, lambda k: (CONV_KERNEL - 1) - k)
            cv = t[1][pl.ds(r0, ch), pl.ds(0, C)]
            s = _sigmoid(t[1][pl.ds(r0, ch), pl.ds(C, C)])
            o[0][pl.ds(r0, ch), :] = (dug * s).astype(BF16)
            o[1][pl.ds(r0, ch), :] = (dug * cv * (s * (1.0 - s))).astype(BF16)
            return carry

        lax.fori_loop(0, tm // ch, chunk, 0)

    return _rowwise(fn, T=T, tm=tm, name=name, tiled=[dyc, cvg], nxt=[dyc], consts=[cw],
                    out_tiled=[(C, BF16), (C, BF16)],
                    scratch=[pltpu.VMEM((tm + HALO, C), F32), pltpu.VMEM((SUBLANES, tm + HALO - SUBLANES, C), F32)],
                    by_ref=True)


def _split(v):
    hi = v.astype(BF16)
    return hi, (v - hi.astype(F32)).astype(BF16)


def _dot2(v, m):
    hi, lo = _split(v)
    return jnp.dot(hi, m, preferred_element_type=F32) + jnp.dot(lo, m, preferred_element_type=F32)


def _log_gap(z):
    return -(jnp.maximum(z, 0.0) + jnp.log(1.0 + jnp.exp(-jnp.abs(z))))


def _head_masks():
    lane = lax.broadcasted_iota(jnp.int32, (1, LANES), 1)
    return lane < HEAD_DIM, lane >= HEAD_DIM


LOG_WEIGHT_FLOOR = -110.0
ATTN_BLOCK = 256


def _key_norm_bound(k_ref, masks, T):
    ch = _tile(T, 512)

    def chunk(r, m):
        kk = k_ref[pl.ds(pl.multiple_of(r * ch, ch), ch), :].astype(F32)
        k2 = kk * kk
        return tuple(jnp.maximum(m[h], jnp.max(jnp.sum(jnp.where(masks[h], k2, 0.0), -1, keepdims=True),
                                               axis=0, keepdims=True)) for h in (0, 1))

    m0, m1 = lax.fori_loop(0, T // ch, chunk, (jnp.zeros((1, 1), F32), jnp.zeros((1, 1), F32)))
    row = lax.broadcasted_iota(jnp.int32, (8, LANES), 0)
    return jnp.where(row == 0, jnp.sqrt(m0), jnp.sqrt(m1))


def _score_bound(qh, kn):
    qf = qh.astype(F32)
    return jnp.sqrt(jnp.sum(qf * qf, -1, keepdims=True)) * (kn * 1.01) + 0.01


def _some_weight_left(carries, bounds):
    m = jnp.maximum(jnp.max(carries[0] + bounds[0]), jnp.max(carries[1] + bounds[1]))
    return m > LOG_WEIGHT_FLOOR


def _attn_fwd(qkv, g_attn, *, T, AW, a_cols, name, comm=None):
    P = AW // LANES
    tq = _tile(T, 2 * ATTN_BLOCK)
    tb = tq // 2
    nq = T // tq
    scale = HEAD_DIM ** -0.5
    ci_specs, co_specs, co_shapes, csems = _comm_specs(comm)
    nci, nco = len(ci_specs), len(co_specs)

    def body(q_ref, k_ref, v_ref, g_ref, *rest):
        o_ref, a_ref, kn_ref = rest[nci], rest[nci + 1], rest[nci + 2 + nco]
        at_entry, at_exit = _grid2_hooks(comm, P, nq, (rest[:nci], rest[nci + 2:nci + 2 + nco], rest[nci + 3 + nco:]))
        at_entry()
        i = pl.program_id(1)
        lo_mask, hi_mask = masks = _head_masks()

        @pl.when(i == 0)
        def _():
            kn_ref[...] = _key_norm_bound(k_ref, masks, T)

        rows = lax.broadcasted_iota(jnp.int32, (tb, tb), 0)
        cols = lax.broadcasted_iota(jnp.int32, (tb, tb), 1)
        strict = cols < rows
        everywhere = cols >= 0
        tri = jnp.where(rows >= cols, 1.0, 0.0).astype(BF16)
        qhs, zbs = [], []
        for part in (0, 1):
            q = q_ref[pl.ds(part * tb, tb), :]
            qhs.append([jnp.where(m, q, jnp.zeros_like(q)) * jnp.asarray(scale, BF16) for m in masks])
            zbs.append([_score_bound(qhs[part][h], kn_ref[h:h + 1, 0:1]) for h in (0, 1)])

        def block(kb, part, carry, mask=None):
            st = pl.multiple_of(kb * tb, tb)
            kj = k_ref[pl.ds(st, tb), :]
            vj = v_ref[pl.ds(st, tb), :]
            new = []
            for h in (0, 1):
                acc, c = carry[h]
                z = lax.dot_general(qhs[part][h], kj, _DIMS['nt'], preferred_element_type=F32)
                l = _log_gap(z)
                if mask is not None:
                    l = jnp.where(mask, l, 0.0)
                cum = _dot2(l, tri)
                w = jnp.exp(z + cum + c)
                if mask is not None:
                    w = jnp.where(mask, w, 0.0)
                new.append((acc + _dot2(w, vj), c + cum[:, 0:1]))
            return tuple(new)

        zero = (jnp.zeros((tb, LANES), F32), jnp.zeros((tb, 1), F32))
        carries = []
        for part in (0, 1):
            kb0 = 2 * i + part
            cr = block(kb0, part, (zero, zero), strict)
            cr = block(jnp.maximum(kb0 - 1, 0), part, cr, jnp.logical_and(i > 0, everywhere) if part == 0 else None)
            carries.append(cr)

        def live(st):
            jj, ca, cb = st
            return jnp.logical_and(jj < 2 * i, jnp.logical_or(
                _some_weight_left([ca[0][1], ca[1][1]], zbs[0]), _some_weight_left([cb[0][1], cb[1][1]], zbs[1])))

        def more(st):
            jj, ca, cb = st
            ka = 2 * i - 2 - jj
            ca = block(jnp.maximum(ka, 0), 0, ca, jnp.logical_and(ka >= 0, everywhere))
            cb = block(ka + 1, 1, cb)
            return jj + 1, ca, cb

        _, ca, cb = lax.while_loop(live, more, (jnp.int32(0), carries[0], carries[1]))
        o = jnp.concatenate([jnp.where(lo_mask, c2[0][0], c2[1][0]) for c2 in (ca, cb)], axis=0)
        o2 = o * o
        r0 = lax.rsqrt(jnp.sum(jnp.where(lo_mask, o2, 0.0), -1, keepdims=True) * (1.0 / HEAD_DIM) + RMS_EPS)
        r1 = lax.rsqrt(jnp.sum(jnp.where(hi_mask, o2, 0.0), -1, keepdims=True) * (1.0 / HEAD_DIM) + RMS_EPS)
        o_ref[...] = o
        a_ref[...] = ((o * jnp.where(lo_mask, r0, r1)) * g_ref[...]).astype(BF16)
        at_exit()

    res = pl.pallas_call(
        body, grid=(P, nq),
        in_specs=[pl.BlockSpec((tq, LANES), lambda p, i: (i, p)),
                  pl.BlockSpec((T, LANES), lambda p, i: (0, P + p)),
                  pl.BlockSpec((T, LANES), lambda p, i: (0, 2 * P + p)),
                  pl.BlockSpec((1, LANES), lambda p, i: (0, p))] + ci_specs,
        out_specs=[pl.BlockSpec((tq, LANES), lambda p, i: (i, p)),
                   pl.BlockSpec((tq, LANES), lambda p, i: (i, p))] + co_specs,
        out_shape=[jax.ShapeDtypeStruct((T, AW), F32), jax.ShapeDtypeStruct((T, a_cols), BF16)] + co_shapes,
        scratch_shapes=[pltpu.VMEM((8, LANES), F32)] + csems,
        compiler_params=_cparams(("parallel", "arbitrary") if comm is None else ("arbitrary", "arbitrary")),
        name=name)(qkv, qkv, qkv, g_attn, *([] if comm is None else comm.ins))
    return (res[0], res[1]) if comm is None else (res[0], res[1], res[2:])


def _attn_bwd(qkv, o, da, g_attn, *, T, AW, name):
    P = AW // LANES
    tq = _tile(T, 2 * ATTN_BLOCK)
    tb = tq // 2
    nq, nb = T // tq, T // tb
    scale = HEAD_DIM ** -0.5

    def body(q_ref, k_ref, v_ref, o_ref, da_ref, g_ref, dq_ref, dk_out, dv_out, dg_ref, kn_ref, dk_ref, dv_ref):
        i = pl.program_id(1)
        lo_mask, hi_mask = masks = _head_masks()

        @pl.when(i == 0)
        def _():
            dk_ref[...] = jnp.zeros_like(dk_ref)
            dv_ref[...] = jnp.zeros_like(dv_ref)
            dg_ref[...] = jnp.zeros_like(dg_ref)
            kn_ref[...] = _key_norm_bound(k_ref, masks, T)

        rows = lax.broadcasted_iota(jnp.int32, (tb, tb), 0)
        cols = lax.broadcasted_iota(jnp.int32, (tb, tb), 1)
        strict = cols < rows
        everywhere = cols >= 0
        tri = jnp.where(rows >= cols, 1.0, 0.0).astype(BF16)
        tri_s = jnp.where(rows > cols, 1.0, 0.0).astype(BF16)
        o_all = o_ref[...]
        da = da_ref[...]
        g = g_ref[...]
        o2 = o_all * o_all
        r0 = lax.rsqrt(jnp.sum(jnp.where(lo_mask, o2, 0.0), -1, keepdims=True) * (1.0 / HEAD_DIM) + RMS_EPS)
        r1 = lax.rsqrt(jnp.sum(jnp.where(hi_mask, o2, 0.0), -1, keepdims=True) * (1.0 / HEAD_DIM) + RMS_EPS)
        r = jnp.where(lo_mask, r0, r1)
        oh = o_all * r
        gy = da * g
        gyo = gy * oh
        m0 = jnp.sum(jnp.where(lo_mask, gyo, 0.0), -1, keepdims=True) * (1.0 / HEAD_DIM)
        m1 = jnp.sum(jnp.where(hi_mask, gyo, 0.0), -1, keepdims=True) * (1.0 / HEAD_DIM)
        do_all = r * (gy - oh * jnp.where(lo_mask, m0, m1))
        dg_ref[...] += _colsum(da * oh)

        qhs, zbs, do_bs, deltas, q_ts, do_ts = [], [], [], [], [], []
        for part in (0, 1):
            q = q_ref[pl.ds(part * tb, tb), :]
            o = o_all[part * tb:(part + 1) * tb]
            do = do_all[part * tb:(part + 1) * tb]
            qhs.append([jnp.where(m, q, jnp.zeros_like(q)) * jnp.asarray(scale, BF16) for m in masks])
            zbs.append([_score_bound(qhs[part][h], kn_ref[h:h + 1, 0:1]) for h in (0, 1)])
            do_bs.append([jnp.where(m, do, 0.0).astype(BF16) for m in masks])
            deltas.append([jnp.sum(d.astype(F32) * o, -1, keepdims=True) for d in do_bs[part]])
            q_ts.append([qh.astype(F32).T.astype(BF16) for qh in qhs[part]])
            do_ts.append([d.astype(F32).T.astype(BF16) for d in do_bs[part]])

        def block(kb, part, carry, mask=None):
            masked = mask is not None
            st = pl.multiple_of(kb * tb, tb)
            kj = k_ref[pl.ds(st, tb), :]
            vj = v_ref[pl.ds(st, tb), :]
            new = []
            dk = dv = None
            for h in (0, 1):
                dq, c, gsum = carry[h]
                z = lax.dot_general(qhs[part][h], kj, _DIMS['nt'], preferred_element_type=F32)
                l = _log_gap(z)
                sig = jnp.exp(z + l)
                if masked:
                    l = jnp.where(mask, l, 0.0)
                cum = _dot2(l, tri)
                w = jnp.exp(z + cum + c)
                if masked:
                    w = jnp.where(mask, w, 0.0)
                dp = lax.dot_general(do_bs[part][h], vj, _DIMS['nt'], preferred_element_type=F32)
                pw = w * dp
                after = _dot2(pw, tri_s)
                dz = pw - sig * (deltas[part][h] - gsum - after)
                if masked:
                    dz = jnp.where(mask, dz, 0.0)
                dz_b = dz.astype(BF16)
                dk_h = jnp.dot(q_ts[part][h], dz_b, preferred_element_type=F32)
                dv_h = jnp.dot(do_ts[part][h], w.astype(BF16), preferred_element_type=F32)
                dk = dk_h if dk is None else dk + dk_h
                dv = dv_h if dv is None else dv + dv_h
                dq = dq + jnp.dot(dz_b, kj, preferred_element_type=F32)
                new.append((dq, c + cum[:, 0:1], gsum + (after[:, 0:1] + pw[:, 0:1])))
            dk_ref[kb] += dk
            dv_ref[kb] += dv
            return tuple(new)

        zero1 = jnp.zeros((tb, 1), F32)
        zero = (jnp.zeros((tb, LANES), F32), zero1, zero1)
        carries = []
        for part in (0, 1):
            kb0 = 2 * i + part
            cr = block(kb0, part, (zero, zero), strict)
            cr = block(jnp.maximum(kb0 - 1, 0), part, cr, jnp.logical_and(i > 0, everywhere) if part == 0 else None)
            carries.append(cr)

        def live(st):
            jj, ca, cb = st
            return jnp.logical_and(jj < 2 * i, jnp.logical_or(
                _some_weight_left([ca[0][1], ca[1][1]], zbs[0]), _some_weight_left([cb[0][1], cb[1][1]], zbs[1])))

        def more(st):
            jj, ca, cb = st
            ka = 2 * i - 2 - jj
            ca = block(jnp.maximum(ka, 0), 0, ca, jnp.logical_and(ka >= 0, everywhere))
            cb = block(ka + 1, 1, cb)
            return jj + 1, ca, cb

        _, ca, cb = lax.while_loop(live, more, (jnp.int32(0), carries[0], carries[1]))
        dq_ref[...] = (jnp.concatenate([jnp.where(lo_mask, c2[0][0], c2[1][0]) for c2 in (ca, cb)], axis=0)
                       * scale).astype(BF16)

        @pl.when(i == nq - 1)
        def _():
            def turn(j, carry_):
                st = pl.multiple_of(j * tb, tb)
                dk_out[pl.ds(st, tb), :] = dk_ref[j].T.astype(BF16)
                dv_out[pl.ds(st, tb), :] = dv_ref[j].T.astype(BF16)
                return carry_

            lax.fori_loop(0, nb, turn, 0)

    return pl.pallas_call(
        body, grid=(P, nq),
        in_specs=[pl.BlockSpec((tq, LANES), lambda p, i: (i, p)),
                  pl.BlockSpec((T, LANES), lambda p, i: (0, P + p)),
                  pl.BlockSpec((T, LANES), lambda p, i: (0, 2 * P + p)),
                  pl.BlockSpec((tq, LANES), lambda p, i: (i, p)),
                  pl.BlockSpec((tq, LANES), lambda p, i: (i, p)),
                  pl.BlockSpec((1, LANES), lambda p, i: (0, p))],
        out_specs=[pl.BlockSpec((tq, LANES), lambda p, i: (i, p)),
                   pl.BlockSpec((T, LANES), lambda p, i: (0, p)),
                   pl.BlockSpec((T, LANES), lambda p, i: (0, p)),
                   pl.BlockSpec((1, LANES), lambda p, i: (0, p))],
        out_shape=[jax.ShapeDtypeStruct((T, AW), BF16)] * 3 + [jax.ShapeDtypeStruct((1, AW), F32)],
        scratch_shapes=[pltpu.VMEM((8, LANES), F32), pltpu.VMEM((nb, LANES, tb), F32),
                        pltpu.VMEM((nb, LANES, tb), F32)],
        compiler_params=_cparams(("parallel", "arbitrary")), name=name)(qkv, qkv, qkv, o, da, g_attn)


def _ada_fwd(c_all, w_ada, b_ada, *, name):
    def body(c_ref, w_ref, b_ref, o_ref):
        cv = c_ref[...]
        sc = cv * _sigmoid(cv)
        o_ref[...] = jnp.dot(sc, w_ref[...], preferred_element_type=F32,
                             precision=lax.Precision.HIGHEST) + b_ref[...]

    return pl.pallas_call(body, out_shape=jax.ShapeDtypeStruct((c_all.shape[0], w_ada.shape[1]), F32),
                          compiler_params=_cparams(), name=name)(c_all, w_ada, b_ada)


def _ada_bwd(c_all_t, dmod, *, name):
    def body(c_ref, d_ref, o_ref):
        cv = c_ref[...]
        sc = cv * _sigmoid(cv)
        o_ref[...] = jnp.dot(sc, d_ref[...], preferred_element_type=F32, precision=lax.Precision.HIGHEST)

    return pl.pallas_call(body, out_shape=jax.ShapeDtypeStruct((c_all_t.shape[0], dmod.shape[1]), F32),
                          compiler_params=_cparams(), name=name)(c_all_t, dmod)


def _adamw(w, g, m, v, *, name):
    R, C = w.shape
    tr = _tile(R, max(8, (1 << 18) // C), 8)

    def body(w_ref, g_ref, m_ref, v_ref, d_ref, nm_ref, nv_ref):
        gv = g_ref[...]
        m2 = ADAM_B1 * m_ref[...] + (1.0 - ADAM_B1) * gv
        v2 = ADAM_B2 * v_ref[...] + (1.0 - ADAM_B2) * jnp.square(gv)
        m_hat = m2 / (1.0 - ADAM_B1 ** ADAM_STEP)
        v_hat = v2 / (1.0 - ADAM_B2 ** ADAM_STEP)
        d_ref[...] = -ADAM_LR * (m_hat / (jnp.sqrt(v_hat) + ADAM_EPS) + ADAM_WD * w_ref[...])
        nm_ref[...] = m2
        nv_ref[...] = v2

    spec = pl.BlockSpec((tr, C), lambda i: (i, 0))
    return pl.pallas_call(
        body, grid=(R // tr,), in_specs=[spec] * 4, out_specs=[spec] * 3,
        out_shape=[jax.ShapeDtypeStruct((R, C), F32)] * 3,
        compiler_params=_cparams(("parallel",)), name=name)(w, g, m, v)


def _sum_devices(a, *, name):
    def body(a_ref, o_ref):
        s = a_ref[0]
        for d in range(1, a_ref.shape[0]):
            s = s + a_ref[d]
        o_ref[...] = s

    return pl.pallas_call(body, out_shape=jax.ShapeDtypeStruct(a.shape[1:], F32),
                          compiler_params=_cparams(), name=name)(a)


def _place():
    return lax.axis_index("x"), lax.axis_index("y"), lax.axis_index("c")


def _flip(v, bit):
    return 1 - v if bit else v


def _allgather8(blk, *, name):
    R, C = blk.shape

    def body(x_ref, out_ref, send_sems, recv_sems):
        x, y, c = _place()
        me = 4 * x + 2 * y + c
        out_ref[me] = x_ref[...]
        copies = []
        for k in range(1, 8):
            peer = (_flip(x, (k >> 2) & 1), _flip(y, (k >> 1) & 1), _flip(c, k & 1))
            cp = pltpu.make_async_remote_copy(
                src_ref=x_ref, dst_ref=out_ref.at[me], send_sem=send_sems.at[k - 1],
                recv_sem=recv_sems.at[k - 1], device_id=peer, device_id_type=MESH)
            cp.start()
            copies.append(cp)
        for cp in copies:
            cp.wait()

    return pl.pallas_call(
        body, out_shape=jax.ShapeDtypeStruct((8, R, C), F32),
        in_specs=[pl.BlockSpec(memory_space=pltpu.VMEM)], out_specs=pl.BlockSpec(memory_space=pltpu.VMEM),
        scratch_shapes=[pltpu.SemaphoreType.DMA((7,)), pltpu.SemaphoreType.DMA((7,))],
        compiler_params=_cparams(), name=name)(blk)


def _aligned(v, m):
    return v if isinstance(v, int) else pl.multiple_of(v, m)


def _rows_half(ref, half):
    n = ref.shape[0] // 2
    return ref.at[pl.ds(_aligned(half * n, 16), n)]


def _region(ref, kind, slot, half):
    if kind == 'col':
        n, cs = ref.shape[0] // 2, ref.shape[1] // 4
        return ref.at[pl.ds(_aligned(half * n, 16), n), pl.ds(_aligned(slot * cs, LANES), cs)]
    rs = ref.shape[0] // 4
    return ref.at[pl.ds(_aligned(slot * rs + half * (rs // 2), 16), rs // 2)]


def _other_chips(x, y):
    return [(1 - x, y), (x, 1 - y), (1 - x, 1 - y)]


class _Comm:
    def __init__(self, ins, outs, sems, start, finish):
        self.ins, self.outs, self.sems, self.start, self.finish = list(ins), list(outs), list(sems), start, finish


def _comm_specs(comm):
    if comm is None:
        return [], [], [], []
    anyspec = pl.BlockSpec(memory_space=pl.ANY)
    return [anyspec] * len(comm.ins), [anyspec] * len(comm.outs), list(comm.outs), list(comm.sems)


def _comm_hooks(comm, first, last, refs):
    if comm is None:
        return (lambda: None), (lambda: None)

    def at_entry():
        pl.when(first)(lambda: comm.start(*refs))

    def at_exit():
        pl.when(last)(lambda: comm.finish(*refs))

    return at_entry, at_exit


def _comm_alone(comm, *, name):
    ni, no = len(comm.ins), len(comm.outs)

    def body(*refs):
        parts = (refs[:ni], refs[ni:ni + no], refs[ni + no:])
        comm.start(*parts)
        comm.finish(*parts)

    i_specs, o_specs, o_shapes, sems = _comm_specs(comm)
    return pl.pallas_call(body, out_shape=o_shapes, in_specs=i_specs, out_specs=o_specs, scratch_shapes=sems,
                          compiler_params=_cparams(), name=name)(*comm.ins)


def _gather_comm(shards, kinds):
    nw = len(shards)
    full_shapes = []
    for s, kind in zip(shards, kinds):
        full_shapes.append((s.shape[0], 4 * s.shape[1]) if kind == 'col' else (4 * s.shape[0], s.shape[1]))

    def copies(sh, full, sems):
        lsem, ssem, rsem, fssem, frsem = sems
        x, y, c = _place()
        me_slot = 2 * x + y
        chips = _other_chips(x, y)
        local, ici, landed, fwd, passed = [], [], [], [], []
        for w in range(nw):
            for h in (0, 1):
                local.append(pltpu.make_async_copy(_rows_half(sh[w], h), _region(full[w], kinds[w], me_slot, h),
                                                   lsem.at[w, h]))
            for r, (px, py) in enumerate(chips):
                ici.append(pltpu.make_async_remote_copy(
                    src_ref=_rows_half(sh[w], c), dst_ref=_region(full[w], kinds[w], me_slot, c),
                    send_sem=ssem.at[w, r], recv_sem=rsem.at[w, r], device_id=(px, py, c), device_id_type=MESH))
                mine = _region(full[w], kinds[w], 2 * px + py, c)
                landed.append(pltpu.make_async_remote_copy(
                    src_ref=mine, dst_ref=mine, send_sem=ssem.at[w, r], recv_sem=rsem.at[w, r],
                    device_id=(px, py, c), device_id_type=MESH))
                fwd.append(pltpu.make_async_remote_copy(
                    src_ref=mine, dst_ref=mine, send_sem=fssem.at[w, r], recv_sem=frsem.at[w, r],
                    device_id=(x, y, 1 - c), device_id_type=MESH))
                theirs = _region(full[w], kinds[w], 2 * px + py, 1 - c)
                passed.append(pltpu.make_async_remote_copy(
                    src_ref=theirs, dst_ref=theirs, send_sem=fssem.at[w, r], recv_sem=frsem.at[w, r],
                    device_id=(x, y, 1 - c), device_id_type=MESH))
        return local, ici, landed, fwd, passed

    def start(sh, full, sems):
        local, ici, _, _, _ = copies(sh, full, sems)
        for cp in local + ici:
            cp.start()

    def finish(sh, full, sems):
        local, ici, landed, fwd, passed = copies(sh, full, sems)
        for got, cp in zip(landed, fwd):
            got.wait_recv()
            cp.start()
        for got in passed:
            got.wait_recv()
        for cp in ici + fwd:
            cp.wait_send()
        for cp in local:
            cp.wait()

    return _Comm(shards, [jax.ShapeDtypeStruct(s, BF16) for s in full_shapes],
                 [pltpu.SemaphoreType.DMA((nw, 2))] + [pltpu.SemaphoreType.DMA((nw, 3))] * 4, start, finish)


def _exchange_comm(grads, kinds):
    nw = len(grads)

    def copies(g, r1, sems):
        ssem, rsem = sems
        x, y, c = _place()
        out, back = [], []
        for w in range(nw):
            for slot in range(4):
                out.append(pltpu.make_async_remote_copy(
                    src_ref=_region(g[w], kinds[w], slot, 1 - c), dst_ref=_region(r1[w], kinds[w], slot, 1 - c),
                    send_sem=ssem.at[w, slot], recv_sem=rsem.at[w, slot], device_id=(x, y, 1 - c),
                    device_id_type=MESH))
                mine = _region(r1[w], kinds[w], slot, c)
                back.append(pltpu.make_async_remote_copy(
                    src_ref=mine, dst_ref=mine, send_sem=ssem.at[w, slot], recv_sem=rsem.at[w, slot],
                    device_id=(x, y, 1 - c), device_id_type=MESH))
        return out, back

    def start(g, r1, sems):
        for cp in copies(g, r1, sems)[0]:
            cp.start()

    def finish(g, r1, sems):
        out, back = copies(g, r1, sems)
        for got in back:
            got.wait_recv()
        for cp in out:
            cp.wait_send()

    return _Comm(grads, [jax.ShapeDtypeStruct(g.shape, F32) for g in grads],
                 [pltpu.SemaphoreType.DMA((nw, 4))] * 2, start, finish)


def _add_core_halves(g, r1, place, kind, *, name):
    if kind == 'col':
        n, cs = g.shape[0] // 2, g.shape[1] // 4
        tr = _tile(n, 256, 16)
        nt = n // tr
        ispec = pl.BlockSpec((tr, cs), lambda s, t, pr: (pr[0] * nt + t, s))
    else:
        rs, cs = g.shape[0] // 4, g.shape[1]
        n = rs // 2
        tr, nt = n, 1
        ispec = pl.BlockSpec((tr, cs), lambda s, t, pr: (s * 2 + pr[0], 0))

    def body(pr, a_ref, b_ref, o_ref):
        o_ref[...] = (a_ref[...] + b_ref[...]).astype(BF16)

    return pl.pallas_call(
        body,
        grid_spec=pltpu.PrefetchScalarGridSpec(
            num_scalar_prefetch=1, grid=(4, nt), in_specs=[ispec, ispec],
            out_specs=pl.BlockSpec((None, tr, cs), lambda s, t, pr: (s, t, 0))),
        out_shape=jax.ShapeDtypeStruct((4, n, cs), BF16),
        compiler_params=_cparams(("parallel", "parallel")), name=name)(place, g, r1)


def _scatter_comm(hs):
    nw = len(hs)

    def copies(h, r2, sems):
        ssem, rsem = sems
        x, y, c = _place()
        return [pltpu.make_async_remote_copy(
            src_ref=h[w].at[2 * px + py], dst_ref=r2[w].at[r], send_sem=ssem.at[w, r],
            recv_sem=rsem.at[w, r], device_id=(px, py, c), device_id_type=MESH)
            for w in range(nw) for r, (px, py) in enumerate(_other_chips(x, y))]

    def start(h, r2, sems):
        for cp in copies(h, r2, sems):
            cp.start()

    def finish(h, r2, sems):
        for cp in copies(h, r2, sems):
            cp.wait()

    return _Comm(hs, [jax.ShapeDtypeStruct((3,) + a.shape[1:], a.dtype) for a in hs],
                 [pltpu.SemaphoreType.DMA((nw, 3))] * 2, start, finish)


def _sum_owner(hs, r2, place, *, name):
    _, n, cs = hs.shape
    tr = _tile(n, 256, 16)
    nt = n // tr

    def body(pr, h_ref, r_ref, o_ref):
        o_ref[...] = ((h_ref[...].astype(F32) + r_ref[0].astype(F32)) + r_ref[1].astype(F32)) + r_ref[2].astype(F32)

    return pl.pallas_call(
        body,
        grid_spec=pltpu.PrefetchScalarGridSpec(
            num_scalar_prefetch=1, grid=(nt,),
            in_specs=[pl.BlockSpec((None, tr, cs), lambda t, pr: (pr[1], t, 0)),
                      pl.BlockSpec((3, tr, cs), lambda t, pr: (0, t, 0))],
            out_specs=pl.BlockSpec((None, tr, cs), lambda t, pr: (pr[0], t, 0))),
        out_shape=jax.ShapeDtypeStruct((2, n, cs), F32),
        compiler_params=_cparams(("parallel",)), name=name)(place, hs, r2)


def _share_with_sibling(fins, *, name):
    nw = len(fins)

    def body(*refs):
        fin, out = refs[:nw], refs[nw:2 * nw]
        ssem, rsem = refs[2 * nw:]
        x, y, c = _place()
        copies = []
        for w in range(nw):
            cp = pltpu.make_async_remote_copy(
                src_ref=fin[w].at[c], dst_ref=out[w].at[c], send_sem=ssem.at[w], recv_sem=rsem.at[w],
                device_id=(x, y, 1 - c), device_id_type=MESH)
            cp.start()
            copies.append(cp)
        for w in range(nw):
            theirs = out[w].at[1 - c]
            pltpu.make_async_remote_copy(
                src_ref=theirs, dst_ref=theirs, send_sem=ssem.at[w], recv_sem=rsem.at[w],
                device_id=(x, y, 1 - c), device_id_type=MESH).wait_recv()
        for cp in copies:
            cp.wait_send()

    anyspec = pl.BlockSpec(memory_space=pl.ANY)
    return pl.pallas_call(
        body, out_shape=[jax.ShapeDtypeStruct(a.shape, F32) for a in fins],
        in_specs=[anyspec] * nw, out_specs=[anyspec] * nw,
        input_output_aliases={w: w for w in range(nw)},
        scratch_shapes=[pltpu.SemaphoreType.DMA((nw,))] * 2,
        compiler_params=_cparams(), name=name)(*fins)


class _Plan:
    def __init__(self, wfull):
        self.w = dict(wfull)
        self.grads = {}

    def comm(self, site):
        return None

    def done(self, site, results):
        pass

    def ready(self, group, names, arrays):
        self.grads.update(zip(names, arrays))


def _riding(plan, site, call):
    comm = plan.comm(site)
    res = call(comm)
    if comm is None:
        return res
    *main, extra = res
    plan.done(site, extra)
    return main[0] if len(main) == 1 else tuple(main)


def _merge_comms(comms):
    if len(comms) == 1:
        return comms[0]

    def parts(refs, field):
        out, pos = [], 0
        for c in comms:
            n = len(getattr(c, field))
            out.append(refs[pos:pos + n])
            pos += n
        return out

    def run(which):
        def fn(ins, outs, sems):
            for c, i, o, s in zip(comms, parts(ins, 'ins'), parts(outs, 'outs'), parts(sems, 'sems')):
                getattr(c, which)(i, o, s)
        return fn

    return _Comm(sum((c.ins for c in comms), []), sum((c.outs for c in comms), []),
                 sum((c.sems for c in comms), []), run('start'), run('finish'))


class _DistPlan(_Plan):
    RIDES = {
        'ffn_in_ff1': [('gather', ['ff1_w_out', 'w_in_mix', 'w_out_mix'])],
        'attn_fwd': [('gather', ['ff2_w_in', 'ff2_w_out'])],
        'pre_bwd_ff2': [('exchange', ['ff2_w_in', 'ff2_w_out'])],
        'conv_bwd1': [('scatter', ['ff2_w_in', 'ff2_w_out'])],
        'pre_bwd_mix': [('exchange', ['w_in_mix', 'w_out_mix'])],
        'ffn_dact_ff1': [('scatter', ['w_in_mix', 'w_out_mix'])],
        'dw_out_ff1': [('exchange', ['ff1_w_in'])],
        'dh_ff1': [('scatter', ['ff1_w_in']), ('exchange', ['ff1_w_out'])],
    }
    AFTER = [('scatter', ['ff1_w_out'])]

    def __init__(self, shards, place):
        self.shards, self.place, self.kind = shards, place, dict(BIG)
        self.w, self.grads, self.hs, self.fin = {}, {}, {}, {}

    def _make(self, kind, names):
        if kind == 'gather':
            return _gather_comm([self.shards[n] for n in names], [self.kind[n] for n in names])
        if kind == 'exchange':
            return _exchange_comm([self.grads[n] for n in names], [self.kind[n] for n in names])
        return _scatter_comm([self.hs[n] for n in names])

    def _take(self, kind, names, results):
        for n, r in zip(names, results):
            if kind == 'gather':
                self.w[n] = r
            elif kind == 'exchange':
                self.hs[n] = _add_core_halves(self.grads[n], r, self.place, self.kind[n], name=f"grad_core_add_{n}")
            else:
                self.fin[n] = _sum_owner(self.hs[n], r, self.place, name=f"grad_owner_sum_{n}")

    def gather_now(self, names, *, name):
        self._take('gather', names, _comm_alone(self._make('gather', names), name=name))

    def comm(self, site):
        rides = self.RIDES.get(site)
        return None if rides is None else _merge_comms([self._make(k, names) for k, names in rides])

    def done(self, site, results):
        pos = 0
        for kind, names in self.RIDES[site]:
            self._take(kind, names, results[pos:pos + len(names)])
            pos += len(names)

    def finish(self):
        for kind, names in self.AFTER:
            self._take(kind, names, _comm_alone(self._make(kind, names), name=f"grad_{kind}_{names[0]}"))
        names = list(self.shards)
        return dict(zip(names, _share_with_sibling([self.fin[n] for n in names], name="grad_share")))


def _local_step(x, target, mod, gains, plan, g_attn, conv_w, cvec):
    T, D = x.shape
    F = plan.w['ff1_w_in'].shape[1] // 2
    AW = D // 2
    C = D - AW
    NQKV = 3 * AW
    MIX = NQKV + 2 * C
    tM = _tile(T, 1024)
    tkT = _tile(T, 1024)

    def ffn_fwd(xin, s, tag):
        h = _pre_fwd(xin, gains, mod, T=T, s=s, name=f"pre_fwd_{tag}")
        w_in = plan.w[f"{tag}_w_in"]
        jac, act = _riding(plan, f"ffn_in_{tag}",
                           lambda cm: _ffn_in(h, w_in, T=T, D=D, F=F, name=f"ffn_in_{tag}", comm=cm))
        f = _matmul(act, plan.w[f"{tag}_w_out"], mode='nn', M=T, N=D, K=F, tm=tM, tn=_tile(D, 1024), tk=F,
                    out_dtype=F32, name=f"ffn_out_{tag}")
        return h, jac, act, f

    def ffn_bwd(dout, xin, saved, s, res_w, tag):
        h, jac, act, f = saved
        w_in, w_out = plan.w[f"{tag}_w_in"], plan.w[f"{tag}_w_out"]
        df, dgate, dgpost = _post_bwd(dout, f, gains, mod, T=T, s=s, res_w=res_w, name=f"post_bwd_{tag}")
        dgu = _riding(plan, f"ffn_dact_{tag}",
                      lambda cm: _ffn_dact(df, w_out, jac, T=T, D=D, F=F, name=f"ffn_dact_{tag}", comm=cm))
        tnf = _tile(F, 2816)
        nf = F // tnf
        dw_in = _matmul(h, dgu, mode='tn', M=D, N=2 * F, K=T, tm=_tile(D, 1024), tn=tnf, tk=tkT, out_dtype=F32,
                        b_spec=pl.BlockSpec((None, tkT, tnf), lambda i, j, k: (j // nf, k, j % nf)),
                        name=f"dw_in_{tag}")
        plan.ready(tag, [f"{tag}_w_in"], [dw_in])
        dw_out = _riding(plan, f"dw_out_{tag}", lambda cm: _matmul(
            act, df, mode='tn', M=F, N=D, K=T, tm=_tile(F, 1408), tn=_tile(D, 1024), tk=tkT, out_dtype=F32,
            name=f"dw_out_{tag}", comm=cm))
        plan.ready(tag, [f"{tag}_w_out"], [dw_out])
        dh = _riding(plan, f"dh_{tag}", lambda cm: _matmul(
            dgu, w_in, mode='nt', M=T, N=D, K=2 * F, tm=tM, tn=_tile(D, 1024), tk=F, out_dtype=F32,
            a_spec=pl.BlockSpec((None, tM, F), lambda i, j, k: (k, i, 0)), name=f"dh_{tag}", comm=cm))
        dx, dshift, dscale, dgpre = _riding(plan, f"pre_bwd_{tag}", lambda cm: _pre_bwd(
            dh, xin, dout, gains, mod, T=T, s=s, name=f"pre_bwd_{tag}", comm=cm))
        return dx, (dshift, dscale, dgate), dgpre, dgpost

    s1 = ffn_fwd(x, 0, "ff1")
    x1 = _post_fwd(x, s1[3], gains, mod, T=T, s=0, res_w=0.5, name="post_fwd_ff1")

    h2 = _pre_fwd(x1, gains, mod, T=T, s=1, name="pre_fwd_mix")
    w_in_mix, w_out_mix = plan.w['w_in_mix'], plan.w['w_out_mix']
    tnq = _tile(AW, 512)
    qkv = _matmul(h2, w_in_mix, mode='nn', M=T, N=NQKV, K=D, tm=tM, tn=tnq, tk=D, out_dtype=BF16, name="proj_qkv")
    tnc = _tile(C, 512)
    off = NQKV // tnc
    cvg = _matmul(h2, w_in_mix, mode='nn', M=T, N=2 * C, K=D, tm=tM, tn=tnc, tk=D, out_dtype=F32,
                  b_spec=pl.BlockSpec((D, tnc), lambda i, j, k: (0, off + j)), name="proj_conv")
    o_attn, a_attn = _riding(plan, "attn_fwd", lambda cm: _attn_fwd(
        qkv, g_attn, T=T, AW=AW, a_cols=D, name="attn_fwd", comm=cm))
    mixcat = _conv_fwd(cvg, conv_w, cvec, T=T, C=C, name="conv_fwd", into=(a_attn, AW // C))
    f_mix = _matmul(mixcat, w_out_mix, mode='nn', M=T, N=D, K=D, tm=tM, tn=_tile(D, 1024), tk=D, out_dtype=F32,
                    name="mix_out")
    x2 = _post_fwd(x1, f_mix, gains, mod, T=T, s=1, res_w=1.0, name="post_fwd_mix")

    s3 = ffn_fwd(x2, 2, "ff2")
    dout, sq = _post_fwd_loss(x2, s3[3], target, gains, mod, T=T, s=2, res_w=0.5, name="post_fwd_loss")

    dx2, dmod2, dgpre2, dgpost2 = ffn_bwd(dout, x2, s3, 2, 0.5, "ff2")

    df_mix, dgate_m, dgpost_m = _post_bwd(dx2, f_mix, gains, mod, T=T, s=1, res_w=1.0, name="post_bwd_mix")
    dmixcat = _matmul(df_mix, w_out_mix, mode='nt', M=T, N=D, K=D, tm=tM, tn=_tile(D, 1024), tk=D, out_dtype=F32,
                      name="d_mixcat")
    dw_out_mix = _matmul(mixcat, df_mix, mode='tn', M=D, N=D, K=T, tm=_tile(D, 1024), tn=_tile(D, 1024),
                         tk=tkT, out_dtype=F32, name="dw_out_mix")
    dq, dk, dv, dg_attn = _attn_bwd(qkv, o_attn, dmixcat, g_attn, T=T, AW=AW, name="attn_bwd")
    dyc, csum, dconv_w = _riding(plan, "conv_bwd1", lambda cm: _conv_bwd1(
        cvg, (dmixcat, C, AW // C), conv_w, cvec, T=T, C=C, name="conv_bwd1", comm=cm))
    dcv, dcg = _conv_bwd2(dyc, cvg, conv_w, T=T, C=C, name="conv_bwd2")
    dproj = jnp.concatenate([dq, dk, dv, dcv, dcg], axis=1)
    dh2 = _matmul(dproj, w_in_mix, mode='nt', M=T, N=D, K=MIX, tm=tM, tn=_tile(D, 1024), tk=MIX, out_dtype=F32,
                  name="dh_mix")
    dw_in_mix = _matmul(h2, dproj, mode='tn', M=D, N=MIX, K=T, tm=_tile(D, 1024), tn=_tile(MIX, 1280),
                        tk=tkT, out_dtype=F32, name="dw_in_mix")
    plan.ready("mix", ['w_in_mix', 'w_out_mix'], [dw_in_mix, dw_out_mix])
    dx1, dshift_m, dscale_m, dgpre_m = _riding(plan, "pre_bwd_mix", lambda cm: _pre_bwd(
        dh2, x1, dx2, gains, mod, T=T, s=1, name="pre_bwd_mix", comm=cm))

    dx0, dmod1, dgpre1, dgpost1 = ffn_bwd(dx1, x, s1, 0, 0.5, "ff1")

    dgains = [dgpre1, dgpost1, dgpre_m, dgpost_m, dgpre2, dgpost2]
    dmod = list(dmod1) + [dshift_m, dscale_m, dgate_m] + list(dmod2)
    return sq, dx0, dgains, dmod, dg_attn, csum, dconv_w


def _pack_rows(pieces, width):
    rows = jnp.concatenate([p.reshape(-1) for p in pieces]).reshape(-1, width)
    pad = (-rows.shape[0]) % 8
    return jnp.pad(rows, ((0, pad), (0, 0)))


def kernel(x, c, w_ada, b_ada, g_pre_ff1, g_post_ff1, ff1_w_in, ff1_w_out, g_pre_mix, g_post_mix, w_in_mix, g_attn_out, conv_w, conv_b, conv_ln_g, conv_ln_b, w_out_mix, g_pre_ff2, g_post_ff2, ff2_w_in, ff2_w_out, loss_target, m_w_ada, m_b_ada, m_g_pre_ff1, m_g_post_ff1, m_ff1_w_in, m_ff1_w_out, m_g_pre_mix, m_g_post_mix, m_w_in_mix, m_g_attn_out, m_conv_w, m_conv_b, m_conv_ln_g, m_conv_ln_b, m_w_out_mix, m_g_pre_ff2, m_g_post_ff2, m_ff2_w_in, m_ff2_w_out, v_w_ada, v_b_ada, v_g_pre_ff1, v_g_post_ff1, v_ff1_w_in, v_ff1_w_out, v_g_pre_mix, v_g_post_mix, v_w_in_mix, v_g_attn_out, v_conv_w, v_conv_b, v_conv_ln_g, v_conv_ln_b, v_w_out_mix, v_g_pre_ff2, v_g_post_ff2, v_ff2_w_in, v_ff2_w_out):
    W = dict(w_ada=w_ada, b_ada=b_ada, g_pre_ff1=g_pre_ff1, g_post_ff1=g_post_ff1, ff1_w_in=ff1_w_in,
             ff1_w_out=ff1_w_out, g_pre_mix=g_pre_mix, g_post_mix=g_post_mix, w_in_mix=w_in_mix,
             g_attn_out=g_attn_out, conv_w=conv_w, conv_b=conv_b, conv_ln_g=conv_ln_g, conv_ln_b=conv_ln_b,
             w_out_mix=w_out_mix, g_pre_ff2=g_pre_ff2, g_post_ff2=g_post_ff2, ff2_w_in=ff2_w_in,
             ff2_w_out=ff2_w_out)
    Mo = dict(w_ada=m_w_ada, b_ada=m_b_ada, g_pre_ff1=m_g_pre_ff1, g_post_ff1=m_g_post_ff1, ff1_w_in=m_ff1_w_in,
              ff1_w_out=m_ff1_w_out, g_pre_mix=m_g_pre_mix, g_post_mix=m_g_post_mix, w_in_mix=m_w_in_mix,
              g_attn_out=m_g_attn_out, conv_w=m_conv_w, conv_b=m_conv_b, conv_ln_g=m_conv_ln_g,
              conv_ln_b=m_conv_ln_b, w_out_mix=m_w_out_mix, g_pre_ff2=m_g_pre_ff2, g_post_ff2=m_g_post_ff2,
              ff2_w_in=m_ff2_w_in, ff2_w_out=m_ff2_w_out)
    Vo = dict(w_ada=v_w_ada, b_ada=v_b_ada, g_pre_ff1=v_g_pre_ff1, g_post_ff1=v_g_post_ff1, ff1_w_in=v_ff1_w_in,
              ff1_w_out=v_ff1_w_out, g_pre_mix=v_g_pre_mix, g_post_mix=v_g_post_mix, w_in_mix=v_w_in_mix,
              g_attn_out=v_g_attn_out, conv_w=v_conv_w, conv_b=v_conv_b, conv_ln_g=v_conv_ln_g,
              conv_ln_b=v_conv_ln_b, w_out_mix=v_w_out_mix, g_pre_ff2=v_g_pre_ff2, g_post_ff2=v_g_post_ff2,
              ff2_w_in=v_ff2_w_in, ff2_w_out=v_ff2_w_out)

    T, D = x.shape[1], x.shape[2]
    AW = D // 2
    C = D - AW
    xi, yi, ci = _place()
    me = 4 * xi + 2 * yi + ci
    chip = 2 * xi + yi
    place = jnp.stack([ci, chip]).astype(jnp.int32)

    c_all = _allgather8(jnp.tile(c, (8, 1)), name="gather_c")[:, 0, :]
    ncol = w_ada.shape[1]
    b_cols = lax.dynamic_index_in_dim(b_ada.reshape(4, ncol), chip, keepdims=True).reshape(1, ncol)
    modp = _ada_fwd(c_all, w_ada, b_cols, name="ada_fwd")
    mod_g = _allgather8(modp, name="gather_mod")
    mod_all = jnp.transpose(mod_g[0::2], (1, 0, 2)).reshape(8, 4 * ncol)
    mod = lax.dynamic_index_in_dim(mod_all, me, keepdims=False).reshape(9, D)

    names = [n for n, _ in BIG]
    plan = _DistPlan({n: W[n].astype(BF16) for n in names}, place)
    plan.gather_now(['ff1_w_in'], name="gather_ff1_w_in")
    cs = conv_w.shape[1]
    cw_all = _allgather8(jnp.pad(conv_w, ((0, HALO - CONV_KERNEL), (0, (-cs) % LANES))), name="gather_conv_w")
    conv_w_full = jnp.transpose(cw_all[0::2, :, :cs], (1, 0, 2)).reshape(HALO, 4 * cs)

    gains = _pack_rows([g_pre_ff1, g_post_ff1, g_pre_mix, g_post_mix, g_pre_ff2, g_post_ff2], D)
    cvec = _pack_rows([conv_b, conv_ln_g, conv_ln_b], C)
    g_attn = g_attn_out.reshape(1, AW)

    sq, dx, dgains, dmod, dg_attn, csum, dconv_w = _local_step(
        x[0], loss_target[0], mod, gains, plan, g_attn, conv_w_full, cvec)

    loss_row = jnp.zeros((1, D), F32).at[0, 0].set(jnp.sum(sq) * (0.5 / D))
    small = _pack_rows(dgains + dmod + [dg_attn, csum[0:3], dconv_w, loss_row], D)
    small_all = _allgather8(small, name="gather_small")
    tot = _sum_devices(small_all, name="sum_small")
    n_g, n_m = 6, 9
    r0 = n_g + n_m
    flat = tot.reshape(-1)
    p = r0 * D
    g_attn_grad = flat[p:p + AW]
    p += AW
    gconv_b, gln_g, gln_b = flat[p:p + C], flat[p + C:p + 2 * C], flat[p + 2 * C:p + 3 * C]
    p += 3 * C
    gconv_w_full = flat[p:p + HALO * C].reshape(HALO, C)[:CONV_KERNEL]
    p += HALO * C
    loss = flat[p]
    gconv_w = lax.dynamic_slice_in_dim(gconv_w_full, chip * cs, cs, axis=1)
    grad_small = {'g_pre_ff1': tot[0], 'g_post_ff1': tot[1], 'g_pre_mix': tot[2], 'g_post_mix': tot[3],
                  'g_pre_ff2': tot[4], 'g_post_ff2': tot[5], 'b_ada': tot[n_g:r0].reshape(-1),
                  'g_attn_out': g_attn_grad.reshape(g_attn_out.shape), 'conv_w': gconv_w, 'conv_b': gconv_b,
                  'conv_ln_g': gln_g, 'conv_ln_b': gln_b}

    dmod_all = small_all[:, n_g:r0, :].reshape(8, 9 * D)
    dmod_cols = lax.dynamic_slice_in_dim(dmod_all, chip * ncol, ncol, axis=1)
    grad_w_ada = _ada_bwd(jnp.transpose(c_all), dmod_cols, name="ada_bwd")

    grads = dict(grad_small)
    grads['w_ada'] = grad_w_ada
    for n, a in plan.finish().items():
        grads[n] = a.reshape(W[n].shape)

    delta, new_m, new_v = {}, {}, {}
    for n in ['w_ada'] + names:
        delta[n], new_m[n], new_v[n] = _adamw(W[n], grads[n], Mo[n], Vo[n], name=f"adamw_{n}")
    smalls = [n for n in WEIGHTS if n not in delta]
    sizes = [W[n].size for n in smalls]
    tot_sz = sum(sizes)
    padn = (-tot_sz) % (8 * LANES)

    def pack(d):
        return jnp.pad(jnp.concatenate([d[n].reshape(-1) for n in smalls]), (0, padn)).reshape(-1, LANES)

    d_s, m_s, v_s = _adamw(pack(W), pack(grads), pack(Mo), pack(Vo), name="adamw_small")
    pos = 0
    for n, sz in zip(smalls, sizes):
        for dst, src in ((delta, d_s), (new_m, m_s), (new_v, v_s)):
            dst[n] = src.reshape(-1)[pos:pos + sz].reshape(W[n].shape)
        pos += sz

    return (loss, dx[None], *[grads[n] for n in WEIGHTS], *[delta[n] for n in WEIGHTS],
            *[new_m[n] for n in WEIGHTS], *[new_v[n] for n in WEIGHTS])
```

```python
import functools

import jax
import jax.numpy as jnp
from jax import lax
from jax.experimental import pallas as pl
from jax.experimental.pallas import tpu as pltpu

F32 = jnp.float32
BF16 = jnp.bfloat16
MESH = pl.DeviceIdType.MESH

HEAD_DIM = 64
CONV_KERNEL = 31
RMS_EPS = 1e-6
LN_EPS = 1e-5
ADAM_LR = 0.001
ADAM_B1 = 0.9
ADAM_B2 = 0.999
ADAM_EPS = 1e-08
ADAM_WD = 0.01
ADAM_STEP = 10

LANES = 128
HALO = 32
VMEM_LIMIT = 52 * 1024 * 1024

WEIGHTS = ['w_ada', 'b_ada', 'g_pre_ff1', 'g_post_ff1', 'ff1_w_in', 'ff1_w_out', 'g_pre_mix',
           'g_post_mix', 'w_in_mix', 'g_attn_out', 'conv_w', 'conv_b', 'conv_ln_g', 'conv_ln_b',
           'w_out_mix', 'g_pre_ff2', 'g_post_ff2', 'ff2_w_in', 'ff2_w_out']
BIG = [('ff1_w_in', 'col'), ('ff1_w_out', 'row'), ('w_in_mix', 'col'), ('w_out_mix', 'row'),
       ('ff2_w_in', 'col'), ('ff2_w_out', 'row')]


def _tile(dim, pref, mult=LANES):
    if dim <= pref:
        return dim
    best = None
    for t in range(mult, pref + 1, mult):
        if dim % t == 0:
            best = t
    assert best is not None, (dim, pref, mult)
    return best


def _cparams(sem=None):
    kw = dict(vmem_limit_bytes=VMEM_LIMIT)
    if sem is not None:
        kw['dimension_semantics'] = sem
    return pltpu.CompilerParams(**kw)


def _sigmoid(x):
    return 1.0 / (1.0 + jnp.exp(-x))


_DIMS = {'nn': (((1,), (0,)), ((), ())), 'nt': (((1,), (1,)), ((), ())), 'tn': (((0,), (0,)), ((), ()))}


def _matmul(a, b, *, mode, M, N, K, tm, tn, tk, out_dtype, name, a_spec=None, b_spec=None, comm=None):
    nm, nn, nk = M // tm, N // tn, K // tk
    assert nm * tm == M and nn * tn == N and nk * tk == K, (name, M, N, K, tm, tn, tk)
    if a_spec is None:
        a_spec = (pl.BlockSpec((tk, tm), lambda i, j, k: (k, i)) if mode == 'tn'
                  else pl.BlockSpec((tm, tk), lambda i, j, k: (i, k)))
    if b_spec is None:
        b_spec = (pl.BlockSpec((tn, tk), lambda i, j, k: (j, k)) if mode == 'nt'
                  else pl.BlockSpec((tk, tn), lambda i, j, k: (k, j)))
    dims = _DIMS[mode]
    assert nk == 1 or out_dtype == F32, name
    ci_specs, co_specs, co_shapes, csems = _comm_specs(comm)
    nci, nco = len(ci_specs), len(co_specs)

    def body(a_ref, b_ref, *rest):
        o_ref = rest[nci]
        i, j, k = pl.program_id(0), pl.program_id(1), pl.program_id(2)
        first = jnp.logical_and(jnp.logical_and(i == 0, j == 0), k == 0)
        last = jnp.logical_and(jnp.logical_and(i == nm - 1, j == nn - 1), k == nk - 1)
        at_entry, at_exit = _comm_hooks(comm, first, last, (rest[:nci], rest[nci + 1:nci + 1 + nco], rest[nci + 1 + nco:]))
        at_entry()

        def prod():
            return lax.dot_general(a_ref[...], b_ref[...], dims, preferred_element_type=F32)

        if nk == 1:
            o_ref[...] = prod().astype(o_ref.dtype)
        else:
            @pl.when(k == 0)
            def _():
                o_ref[...] = prod()

            @pl.when(k > 0)
            def _():
                o_ref[...] += prod()
        at_exit()

    sem = ("parallel", "parallel", "arbitrary") if comm is None else ("arbitrary",) * 3
    res = pl.pallas_call(
        body, grid=(nm, nn, nk), in_specs=[a_spec, b_spec] + ci_specs,
        out_specs=[pl.BlockSpec((tm, tn), lambda i, j, k: (i, j))] + co_specs,
        out_shape=[jax.ShapeDtypeStruct((M, N), out_dtype)] + co_shapes, scratch_shapes=csems,
        compiler_params=_cparams(sem), name=name)(a, b, *([] if comm is None else comm.ins))
    return res[0] if comm is None else (res[0], res[1:])


def _grid2_hooks(comm, n0, n1, refs):
    j, i = pl.program_id(0), pl.program_id(1)
    return _comm_hooks(comm, jnp.logical_and(j == 0, i == 0), jnp.logical_and(j == n0 - 1, i == n1 - 1), refs)


def _ffn_in(h, w_in, *, T, D, F, name, comm=None):
    tm, tn = _tile(T, 256), _tile(F, 2816)
    nf, nt = F // tn, T // tm
    ci_specs, co_specs, co_shapes, csems = _comm_specs(comm)
    nci, nco = len(ci_specs), len(co_specs)

    def body(h_ref, wg_ref, wu_ref, *rest):
        jac_ref, a_ref = rest[nci], rest[nci + 1]
        at_entry, at_exit = _grid2_hooks(comm, nf, nt, (rest[:nci], rest[nci + 2:nci + 2 + nco], rest[nci + 2 + nco:]))
        at_entry()
        hh = h_ref[...]
        g = jnp.dot(hh, wg_ref[...], preferred_element_type=F32)
        u = jnp.dot(hh, wu_ref[...], preferred_element_type=F32)
        s = _sigmoid(g)
        sg = g * s
        jac_ref[0] = (u * (s * (1.0 + g * (1.0 - s)))).astype(BF16)
        jac_ref[1] = sg.astype(BF16)
        a_ref[...] = (sg * u).astype(BF16)
        at_exit()

    res = pl.pallas_call(
        body, grid=(nf, nt),
        in_specs=[pl.BlockSpec((tm, D), lambda j, i: (i, 0)),
                  pl.BlockSpec((D, tn), lambda j, i: (0, j)),
                  pl.BlockSpec((D, tn), lambda j, i: (0, nf + j))] + ci_specs,
        out_specs=[pl.BlockSpec((2, tm, tn), lambda j, i: (0, i, j)),
                   pl.BlockSpec((tm, tn), lambda j, i: (i, j))] + co_specs,
        out_shape=[jax.ShapeDtypeStruct((2, T, F), BF16), jax.ShapeDtypeStruct((T, F), BF16)] + co_shapes,
        scratch_shapes=csems,
        compiler_params=_cparams(("parallel", "parallel") if comm is None else ("arbitrary", "arbitrary")),
        name=name)(h, w_in, w_in, *([] if comm is None else comm.ins))
    return (res[0], res[1]) if comm is None else (res[0], res[1], res[2:])


def _ffn_dact(df, w_out, jac, *, T, D, F, name, comm=None):
    tm, tn = _tile(T, 256), _tile(F, 2816)
    nf, nt = F // tn, T // tm
    ci_specs, co_specs, co_shapes, csems = _comm_specs(comm)
    nci, nco = len(ci_specs), len(co_specs)

    def body(df_ref, w_ref, jac_ref, *rest):
        o_ref = rest[nci]
        at_entry, at_exit = _grid2_hooks(comm, nf, nt, (rest[:nci], rest[nci + 1:nci + 1 + nco], rest[nci + 1 + nco:]))
        at_entry()
        da = lax.dot_general(df_ref[...], w_ref[...], _DIMS['nt'], preferred_element_type=F32)
        o_ref[0] = (da * jac_ref[0].astype(F32)).astype(BF16)
        o_ref[1] = (da * jac_ref[1].astype(F32)).astype(BF16)
        at_exit()

    res = pl.pallas_call(
        body, grid=(nf, nt),
        in_specs=[pl.BlockSpec((tm, D), lambda j, i: (i, 0)),
                  pl.BlockSpec((tn, D), lambda j, i: (j, 0)),
                  pl.BlockSpec((2, tm, tn), lambda j, i: (0, i, j))] + ci_specs,
        out_specs=[pl.BlockSpec((2, tm, tn), lambda j, i: (0, i, j))] + co_specs,
        out_shape=[jax.ShapeDtypeStruct((2, T, F), BF16)] + co_shapes, scratch_shapes=csems,
        compiler_params=_cparams(("parallel", "parallel") if comm is None else ("arbitrary", "arbitrary")),
        name=name)(df, w_out, jac, *([] if comm is None else comm.ins))
    return res[0] if comm is None else (res[0], res[1:])


def _rowwise(fn, *, T, tm, name, tiled=(), prev=(), nxt=(), consts=(), out_tiled=(), out_acc=(), scratch=(),
             by_ref=False, comm=None, into=None):
    n = T // tm
    assert n * tm == T and tm % HALO == 0
    hb = tm // HALO
    cols = [a if isinstance(a, tuple) else (a, a.shape[1], 0) for a in tiled]
    tiled = [a for a, _, _ in cols]
    in_specs = [pl.BlockSpec((tm, w), functools.partial(lambda cb, i: (i, cb), cb)) for _, w, cb in cols]
    in_specs += [pl.BlockSpec((HALO, a.shape[1]), lambda i: (jnp.maximum(i * hb - 1, 0), 0)) for a in prev]
    in_specs += [pl.BlockSpec((HALO, a.shape[1]), lambda i: (jnp.minimum((i + 1) * hb, T // HALO - 1), 0))
                 for a in nxt]
    in_specs += [pl.BlockSpec(a.shape, lambda i: (0, 0)) for a in consts]
    out_shape = [jax.ShapeDtypeStruct((T, c), dt) for c, dt in out_tiled]
    out_shape += [jax.ShapeDtypeStruct(s, F32) for s in out_acc]
    out_specs = [pl.BlockSpec((tm, c), lambda i: (i, 0)) for c, _ in out_tiled]
    out_specs += [pl.BlockSpec(s, lambda i: (0, 0)) for s in out_acc]
    nt, npv, nnx, nc, not_, na = len(tiled), len(prev), len(nxt), len(consts), len(out_tiled), len(out_acc)
    ci_specs, co_specs, co_shapes, csems = _comm_specs(comm)
    extra_in, aliases = [], {}
    if into is not None:
        arr, cb = into
        width = out_tiled[0][0]
        out_shape[0] = jax.ShapeDtypeStruct(arr.shape, arr.dtype)
        out_specs[0] = pl.BlockSpec((tm, width), lambda i: (i, cb))
        extra_in = [arr]
        aliases = {nt + npv + nnx + nc: 0}
    n_extra = len(extra_in)

    def body(*refs):
        pos = 0
        groups = []
        for cnt in (nt, npv, nnx, nc, n_extra, len(ci_specs), not_, na, len(co_specs), len(scratch), len(csems)):
            groups.append(refs[pos:pos + cnt])
            pos += cnt
        t_r, p_r, n_r, c_r, _, ci_r, o_r, a_r, co_r, s_r, cs_r = groups
        i = pl.program_id(0)
        at_entry, at_exit = _comm_hooks(comm, i == 0, i == n - 1, (ci_r, co_r, cs_r))
        at_entry()

        @pl.when(i == 0)
        def _():
            for r in a_r:
                r[...] = jnp.zeros_like(r)

        if by_ref:
            fn(i, n, t_r, p_r, n_r, c_r, o_r, a_r, s_r)
        else:
            outs = fn(i, n, [r[...] for r in t_r], [r[...] for r in p_r], [r[...] for r in n_r],
                      [r[...] for r in c_r], a_r, s_r)
            for r, v in zip(o_r, outs):
                r[...] = v.astype(r.dtype)
        at_exit()

    res = pl.pallas_call(
        body, grid=(n,), in_specs=in_specs + [pl.BlockSpec(memory_space=pl.ANY)] * n_extra + ci_specs,
        out_specs=out_specs + co_specs, out_shape=out_shape + co_shapes, scratch_shapes=list(scratch) + csems,
        input_output_aliases=aliases, compiler_params=_cparams(("arbitrary",)), name=name,
    )(*tiled, *prev, *nxt, *consts, *extra_in, *([] if comm is None else comm.ins))
    return res if comm is None else (res[:not_ + na], res[not_ + na:])


def _colsum(v):
    return jnp.sum(v, axis=0, keepdims=True)


def _rowmean(v):
    return jnp.mean(v, axis=-1, keepdims=True)


def _pre_math(xv, g, m, s):
    g_pre, shift, scale = g[2 * s:2 * s + 1], m[3 * s:3 * s + 1], m[3 * s + 1:3 * s + 2]
    r = lax.rsqrt(_rowmean(xv * xv) + RMS_EPS)
    return ((xv * r) * g_pre) * (1.0 + scale) + shift


def _post_math(xv, fv, g, m, s, res_w):
    g_post, gate = g[2 * s + 1:2 * s + 2], m[3 * s + 2:3 * s + 3]
    y = (fv * lax.rsqrt(_rowmean(fv * fv) + RMS_EPS)) * g_post
    return xv + (res_w * (1.0 + gate)) * y


def _pre_fwd(x, gains, mod, *, T, s, name, comm=None):
    def fn(i, n, t, p, nx, c, acc, scr):
        return [_pre_math(t[0], c[0], c[1], s)]

    return _rowwise(fn, T=T, tm=_tile(T, 512, HALO), name=name, tiled=[x], consts=[gains, mod],
                    out_tiled=[(x.shape[1], BF16)], comm=comm)


def _post_pre_fwd(x, f, gains, mod, *, T, s, res_w, name):
    def fn(i, n, t, p, nx, c, acc, scr):
        out = _post_math(t[0], t[1], c[0], c[1], s, res_w)
        return [out, _pre_math(out, c[0], c[1], s + 1)]

    return _rowwise(fn, T=T, tm=_tile(T, 512, HALO), name=name, tiled=[x, f], consts=[gains, mod],
                    out_tiled=[(x.shape[1], F32), (x.shape[1], BF16)])


def _post_fwd_loss(x, f, target, gains, mod, *, T, s, res_w, name):
    D = x.shape[1]

    def fn(i, n, t, p, nx, c, acc, scr):
        (xv, fv, tv), (g, m) = t, c
        g_post, gate = g[2 * s + 1:2 * s + 2], m[3 * s + 2:3 * s + 3]
        y = (fv * lax.rsqrt(_rowmean(fv * fv) + RMS_EPS)) * g_post
        err = (xv + (res_w * (1.0 + gate)) * y) - tv
        acc[0][...] += _colsum(err * err)
        return [err * (1.0 / D)]

    dout, sq = _rowwise(fn, T=T, tm=_tile(T, 512, HALO), name=name, tiled=[x, f, target], consts=[gains, mod],
                        out_tiled=[(D, F32)], out_acc=[(1, D)])
    return dout, sq


def _post_bwd_math(dv, fv, g, m, s, res_w, acc):
    g_post, gate = g[2 * s + 1:2 * s + 2], m[3 * s + 2:3 * s + 3]
    r2 = lax.rsqrt(_rowmean(fv * fv) + RMS_EPS)
    fh = fv * r2
    dy = dv * (res_w * (1.0 + gate))
    acc[0][...] += _colsum(dv * (res_w * (fh * g_post)))
    acc[1][...] += _colsum(dy * fh)
    gy = dy * g_post
    return r2 * (gy - fh * _rowmean(gy * fh))


def _pre_bwd_math(dhv, xv, dv, g, m, s, acc):
    g_pre, scale = g[2 * s:2 * s + 1], m[3 * s + 1:3 * s + 2]
    r = lax.rsqrt(_rowmean(xv * xv) + RMS_EPS)
    nv = xv * r
    acc[0][...] += _colsum(dhv)
    acc[1][...] += _colsum(dhv * (nv * g_pre))
    acc[2][...] += _colsum(dhv * ((1.0 + scale) * nv))
    gn = dhv * (g_pre * (1.0 + scale))
    return r * (gn - nv * _rowmean(gn * nv)) + dv


def _post_bwd(dout, f, gains, mod, *, T, s, res_w, name):
    D = f.shape[1]

    def fn(i, n, t, p, nx, c, acc, scr):
        return [_post_bwd_math(t[0], t[1], c[0], c[1], s, res_w, acc)]

    return _rowwise(fn, T=T, tm=_tile(T, 512, HALO), name=name, tiled=[dout, f], consts=[gains, mod],
                    out_tiled=[(D, BF16)], out_acc=[(1, D), (1, D)])


def _pre_bwd(dh, x, dout, gains, mod, *, T, s, name, comm=None):
    D = x.shape[1]

    def fn(i, n, t, p, nx, c, acc, scr):
        return [_pre_bwd_math(t[0], t[1], t[2], c[0], c[1], s, acc)]

    return _rowwise(fn, T=T, tm=_tile(T, 512, HALO), name=name, tiled=[dh, x, dout], consts=[gains, mod],
                    out_tiled=[(D, F32)], out_acc=[(1, D), (1, D), (1, D)], comm=comm)


def _pre_post_bwd(dh, x, dout, f_prev, gains, mod, *, T, s, res_w_prev, name, comm=None):
    D = x.shape[1]

    def fn(i, n, t, p, nx, c, acc, scr):
        dx = _pre_bwd_math(t[0], t[1], t[2], c[0], c[1], s, acc[0:3])
        return [dx, _post_bwd_math(dx, t[3], c[0], c[1], s - 1, res_w_prev, acc[3:5])]

    return _rowwise(fn, T=T, tm=_tile(T, 512, HALO), name=name, tiled=[dh, x, dout, f_prev], consts=[gains, mod],
                    out_tiled=[(D, F32), (D, BF16)], out_acc=[(1, D)] * 5, comm=comm)


SUBLANES = 8
CONV_CHUNK = 64


def _glu(cvg, C):
    return cvg[:, :C] * _sigmoid(cvg[:, C:])


def _fill_rotations(ext, rot, rows):
    for r in range(SUBLANES):
        rot[r] = ext[pl.ds(r, rows), :]


def _conv_taps(rot, w, r0, rows, off):
    acc = None
    for k in range(CONV_KERNEL):
        a, r = divmod(off(k), SUBLANES)
        term = w[k:k + 1] * rot[r, pl.ds(pl.multiple_of(r0 + a * SUBLANES, SUBLANES), rows), :]
        acc = term if acc is None else acc + term
    return acc


def _causal_off(k):
    return HALO - (CONV_KERNEL - 1) + k


def _conv_norm(rot, cw, cb, r0, rows):
    yc = _conv_taps(rot, cw, r0, rows, _causal_off) + cb
    mu = _rowmean(yc)
    d = yc - mu
    rstd = lax.rsqrt(_rowmean(d * d) + LN_EPS)
    return d * rstd, rstd


def _stage_glu(i, t, p, ext, rot, tm, C):
    ext[pl.ds(0, HALO), :] = jnp.where(i == 0, 0.0, _glu(p[0][...], C))
    ext[pl.ds(HALO, tm), :] = _glu(t[0][...], C)
    ext[pl.ds(HALO + tm, SUBLANES), :] = jnp.zeros((SUBLANES, C), F32)
    _fill_rotations(ext, rot, tm + HALO)


def _conv_scratch(tm, C):
    return [pltpu.VMEM((HALO + tm + SUBLANES, C), F32), pltpu.VMEM((SUBLANES, HALO + tm, C), F32)]


def _conv_fwd(cvg, cw, cvec, *, T, C, name, into=None):
    tm = _tile(T, 512, HALO)
    ch = min(CONV_CHUNK, tm)

    def fn(i, n, t, p, nx, c, o, acc, scr):
        ext, rot = scr
        _stage_glu(i, t, p, ext, rot, tm, C)
        w, vec = c[0][...], c[1][...]

        def chunk(ci, carry):
            r0 = pl.multiple_of(ci * ch, ch)
            yh, _ = _conv_norm(rot, w, vec[0:1], r0, ch)
            zz = yh * vec[1:2] + vec[2:3]
            o[0][pl.ds(r0, ch), :] = (zz * _sigmoid(zz)).astype(BF16)
            return carry

        lax.fori_loop(0, tm // ch, chunk, 0)

    return _rowwise(fn, T=T, tm=tm, name=name, tiled=[cvg], prev=[cvg], consts=[cw, cvec],
                    out_tiled=[(C, BF16)], scratch=_conv_scratch(tm, C), by_ref=True, into=into)[0]


def _conv_bwd1(cvg, duc, cw, cvec, *, T, C, name, comm=None):
    tm = _tile(T, 512, HALO)
    ch = min(CONV_CHUNK, tm)

    def fn(i, n, t, p, nx, c, o, acc, scr):
        ext, rot, w8 = scr

        @pl.when(i == 0)
        def _():
            w8[...] = jnp.zeros_like(w8)

        _stage_glu(i, t, p, ext, rot, tm, C)
        w, vec = c[0][...], c[1][...]
        ln_g = vec[1:2]

        def chunk(ci, carry):
            r0 = pl.multiple_of(ci * ch, ch)
            yh, rstd = _conv_norm(rot, w, vec[0:1], r0, ch)
            zz = yh * ln_g + vec[2:3]
            s = _sigmoid(zz)
            dz = t[1][pl.ds(r0, ch), :] * (s * (1.0 + zz * (1.0 - s)))
            dyh = dz * ln_g
            dyc = rstd * (dyh - _rowmean(dyh) - yh * _rowmean(dyh * yh))
            o[0][pl.ds(r0, ch), :] = dyc
            acc[0][0:1, :] += _colsum(dyc)
            acc[0][1:2, :] += _colsum(dz * yh)
            acc[0][2:3, :] += _colsum(dz)
            for k in range(CONV_KERNEL):
                a, r = divmod(_causal_off(k), SUBLANES)
                prod = dyc * rot[r, pl.ds(pl.multiple_of(r0 + a * SUBLANES, SUBLANES), ch), :]
                part = prod[0:SUBLANES]
                for g in range(1, ch // SUBLANES):
                    part = part + prod[g * SUBLANES:(g + 1) * SUBLANES]
                w8[pl.ds(k * SUBLANES, SUBLANES), :] += part
            return carry

        lax.fori_loop(0, tm // ch, chunk, 0)

        @pl.when(i == n - 1)
        def _():
            for k in range(CONV_KERNEL):
                acc[1][k:k + 1, :] = _colsum(w8[pl.ds(k * SUBLANES, SUBLANES), :])

    return _rowwise(fn, T=T, tm=tm, name=name, tiled=[cvg, duc], prev=[cvg], consts=[cw, cvec],
                    out_tiled=[(C, F32)], out_acc=[(8, C), (HALO, C)],
                    scratch=_conv_scratch(tm, C) + [pltpu.VMEM((HALO * SUBLANES, C), F32)], by_ref=True, comm=comm)


def _conv_bwd2(dyc, cvg, cw, *, T, C, name):
    tm = _tile(T, 512, HALO)
    ch = min(CONV_CHUNK, tm)

    def fn(i, n, t, p, nx, c, o, acc, scr):
        ext, rot = scr
        ext[pl.ds(0, tm), :] = t[0][...]
        ext[pl.ds(tm, HALO), :] = jnp.where(i == n - 1, 0.0, nx[0][...])
        _fill_rotations(ext, rot, tm + HALO - SUBLANES)
        w = c[0][...]

        def chunk(ci, carry):
            r0 = pl.multiple_of(ci * ch, ch)
            dug = _conv_taps(rot, w, r0, ch, lambda k: (CONV_KERNEL - 1) - k)
            cv = t[1][pl.ds(r0, ch), pl.ds(0, C)]
            s = _sigmoid(t[1][pl.ds(r0, ch), pl.ds(C, C)])
            o[0][pl.ds(r0, ch), :] = (dug * s).astype(BF16)
            o[1][pl.ds(r0, ch), :] = (dug * cv * (s * (1.0 - s))).astype(BF16)
            return carry

        lax.fori_loop(0, tm // ch, chunk, 0)

    return _rowwise(fn, T=T, tm=tm, name=name, tiled=[dyc, cvg], nxt=[dyc], consts=[cw],
                    out_tiled=[(C, BF16), (C, BF16)],
                    scratch=[pltpu.VMEM((tm + HALO, C), F32), pltpu.VMEM((SUBLANES, tm + HALO - SUBLANES, C), F32)],
                    by_ref=True)


def _split(v):
    hi = v.astype(BF16)
    return hi, (v - hi.astype(F32)).astype(BF16)


def _dot2(v, m):
    hi, lo = _split(v)
    return jnp.dot(hi, m, preferred_element_type=F32) + jnp.dot(lo, m, preferred_element_type=F32)


def _log_gap(z):
    return -(jnp.maximum(z, 0.0) + jnp.log(1.0 + jnp.exp(-jnp.abs(z))))


def _head_masks():
    lane = lax.broadcasted_iota(jnp.int32, (1, LANES), 1)
    return lane < HEAD_DIM, lane >= HEAD_DIM


LOG_WEIGHT_FLOOR = -110.0
ATTN_BLOCK = 256


def _key_norm_bound(k_ref, masks, T):
    ch = _tile(T, 512)

    def chunk(r, m):
        kk = k_ref[pl.ds(pl.multiple_of(r * ch, ch), ch), :].astype(F32)
        k2 = kk * kk
        return tuple(jnp.maximum(m[h], jnp.max(jnp.sum(jnp.where(masks[h], k2, 0.0), -1, keepdims=True),
                                               axis=0, keepdims=True)) for h in (0, 1))

    m0, m1 = lax.fori_loop(0, T // ch, chunk, (jnp.zeros((1, 1), F32), jnp.zeros((1, 1), F32)))
    row = lax.broadcasted_iota(jnp.int32, (8, LANES), 0)
    return jnp.where(row == 0, jnp.sqrt(m0), jnp.sqrt(m1))


def _score_bound(qh, kn):
    qf = qh.astype(F32)
    return jnp.sqrt(jnp.sum(qf * qf, -1, keepdims=True)) * (kn * 1.01) + 0.01


def _some_weight_left(carries, bounds):
    m = jnp.maximum(jnp.max(carries[0] + bounds[0]), jnp.max(carries[1] + bounds[1]))
    return m > LOG_WEIGHT_FLOOR


def _attn_fwd(qkv, g_attn, *, T, AW, a_cols, name, comm=None):
    P = AW // LANES
    tq = _tile(T, 2 * ATTN_BLOCK)
    tb = tq // 2
    nq = T // tq
    scale = HEAD_DIM ** -0.5
    ci_specs, co_specs, co_shapes, csems = _comm_specs(comm)
    nci, nco = len(ci_specs), len(co_specs)

    def body(q_ref, k_ref, v_ref, g_ref, *rest):
        o_ref, a_ref, kn_ref = rest[nci], rest[nci + 1], rest[nci + 2 + nco]
        at_entry, at_exit = _grid2_hooks(comm, P, nq, (rest[:nci], rest[nci + 2:nci + 2 + nco], rest[nci + 3 + nco:]))
        at_entry()
        i = pl.program_id(1)
        lo_mask, hi_mask = masks = _head_masks()

        @pl.when(i == 0)
        def _():
            kn_ref[...] = _key_norm_bound(k_ref, masks, T)

        rows = lax.broadcasted_iota(jnp.int32, (tb, tb), 0)
        cols = lax.broadcasted_iota(jnp.int32, (tb, tb), 1)
        strict = cols < rows
        everywhere = cols >= 0
        tri = jnp.where(rows >= cols, 1.0, 0.0).astype(BF16)
        qhs, zbs = [], []
        for part in (0, 1):
            q = q_ref[pl.ds(part * tb, tb), :]
            qhs.append([jnp.where(m, q, jnp.zeros_like(q)) * jnp.asarray(scale, BF16) for m in masks])
            zbs.append([_score_bound(qhs[part][h], kn_ref[h:h + 1, 0:1]) for h in (0, 1)])

        def block(kb, part, carry, mask=None):
            st = pl.multiple_of(kb * tb, tb)
            kj = k_ref[pl.ds(st, tb), :]
            vj = v_ref[pl.ds(st, tb), :]
            new = []
            for h in (0, 1):
                acc, c = carry[h]
                z = lax.dot_general(qhs[part][h], kj, _DIMS['nt'], preferred_element_type=F32)
                l = _log_gap(z)
                if mask is not None:
                    l = jnp.where(mask, l, 0.0)
                cum = _dot2(l, tri)
                w = jnp.exp(z + cum + c)
                if mask is not None:
                    w = jnp.where(mask, w, 0.0)
                new.append((acc + _dot2(w, vj), c + cum[:, 0:1]))
            return tuple(new)

        zero = (jnp.zeros((tb, LANES), F32), jnp.zeros((tb, 1), F32))
        carries = []
        for part in (0, 1):
            kb0 = 2 * i + part
            cr = block(kb0, part, (zero, zero), strict)
            cr = block(jnp.maximum(kb0 - 1, 0), part, cr, jnp.logical_and(i > 0, everywhere) if part == 0 else None)
            carries.append(cr)

        def live(st):
            jj, ca, cb = st
            return jnp.logical_and(jj < 2 * i, jnp.logical_or(
                _some_weight_left([ca[0][1], ca[1][1]], zbs[0]), _some_weight_left([cb[0][1], cb[1][1]], zbs[1])))

        def more(st):
            jj, ca, cb = st
            ka = 2 * i - 2 - jj
            ca = block(jnp.maximum(ka, 0), 0, ca, jnp.logical_and(ka >= 0, everywhere))
            cb = block(ka + 1, 1, cb)
            return jj + 1, ca, cb

        _, ca, cb = lax.while_loop(live, more, (jnp.int32(0), carries[0], carries[1]))
        o = jnp.concatenate([jnp.where(lo_mask, c2[0][0], c2[1][0]) for c2 in (ca, cb)], axis=0)
        o2 = o * o
        r0 = lax.rsqrt(jnp.sum(jnp.where(lo_mask, o2, 0.0), -1, keepdims=True) * (1.0 / HEAD_DIM) + RMS_EPS)
        r1 = lax.rsqrt(jnp.sum(jnp.where(hi_mask, o2, 0.0), -1, keepdims=True) * (1.0 / HEAD_DIM) + RMS_EPS)
        o_ref[...] = o
        a_ref[...] = ((o * jnp.where(lo_mask, r0, r1)) * g_ref[...]).astype(BF16)
        at_exit()

    res = pl.pallas_call(
        body, grid=(P, nq),
        in_specs=[pl.BlockSpec((tq, LANES), lambda p, i: (i, p)),
                  pl.BlockSpec((T, LANES), lambda p, i: (0, P + p)),
                  pl.BlockSpec((T, LANES), lambda p, i: (0, 2 * P + p)),
                  pl.BlockSpec((1, LANES), lambda p, i: (0, p))] + ci_specs,
        out_specs=[pl.BlockSpec((tq, LANES), lambda p, i: (i, p)),
                   pl.BlockSpec((tq, LANES), lambda p, i: (i, p))] + co_specs,
        out_shape=[jax.ShapeDtypeStruct((T, AW), F32), jax.ShapeDtypeStruct((T, a_cols), BF16)] + co_shapes,
        scratch_shapes=[pltpu.VMEM((8, LANES), F32)] + csems,
        compiler_params=_cparams(("parallel", "arbitrary") if comm is None else ("arbitrary", "arbitrary")),
        name=name)(qkv, qkv, qkv, g_attn, *([] if comm is None else comm.ins))
    return (res[0], res[1]) if comm is None else (res[0], res[1], res[2:])


def _attn_bwd(qkv, o, da, g_attn, *, T, AW, name):
    P = AW // LANES
    tq = _tile(T, 2 * ATTN_BLOCK)
    tb = tq // 2
    nq, nb = T // tq, T // tb
    scale = HEAD_DIM ** -0.5

    def body(q_ref, k_ref, v_ref, o_ref, da_ref, g_ref, dq_ref, dk_out, dv_out, dg_ref, kn_ref, dk_ref, dv_ref):
        i = pl.program_id(1)
        lo_mask, hi_mask = masks = _head_masks()

        @pl.when(i == 0)
        def _():
            dk_ref[...] = jnp.zeros_like(dk_ref)
            dv_ref[...] = jnp.zeros_like(dv_ref)
            dg_ref[...] = jnp.zeros_like(dg_ref)
            kn_ref[...] = _key_norm_bound(k_ref, masks, T)

        rows = lax.broadcasted_iota(jnp.int32, (tb, tb), 0)
        cols = lax.broadcasted_iota(jnp.int32, (tb, tb), 1)
        strict = cols < rows
        everywhere = cols >= 0
        tri = jnp.where(rows >= cols, 1.0, 0.0).astype(BF16)
        tri_s = jnp.where(rows > cols, 1.0, 0.0).astype(BF16)
        o_all = o_ref[...]
        da = da_ref[...]
        g = g_ref[...]
        o2 = o_all * o_all
        r0 = lax.rsqrt(jnp.sum(jnp.where(lo_mask, o2, 0.0), -1, keepdims=True) * (1.0 / HEAD_DIM) + RMS_EPS)
        r1 = lax.rsqrt(jnp.sum(jnp.where(hi_mask, o2, 0.0), -1, keepdims=True) * (1.0 / HEAD_DIM) + RMS_EPS)
        r = jnp.where(lo_mask, r0, r1)
        oh = o_all * r
        gy = da * g
        gyo = gy * oh
        m0 = jnp.sum(jnp.where(lo_mask, gyo, 0.0), -1, keepdims=True) * (1.0 / HEAD_DIM)
        m1 = jnp.sum(jnp.where(hi_mask, gyo, 0.0), -1, keepdims=True) * (1.0 / HEAD_DIM)
        do_all = r * (gy - oh * jnp.where(lo_mask, m0, m1))
        dg_ref[...] += _colsum(da * oh)

        qhs, zbs, do_bs, deltas, q_ts, do_ts = [], [], [], [], [], []
        for part in (0, 1):
            q = q_ref[pl.ds(part * tb, tb), :]
            o = o_all[part * tb:(part + 1) * tb]
            do = do_all[part * tb:(part + 1) * tb]
            qhs.append([jnp.where(m, q, jnp.zeros_like(q)) * jnp.asarray(scale, BF16) for m in masks])
            zbs.append([_score_bound(qhs[part][h], kn_ref[h:h + 1, 0:1]) for h in (0, 1)])
            do_bs.append([jnp.where(m, do, 0.0).astype(BF16) for m in masks])
            deltas.append([jnp.sum(d.astype(F32) * o, -1, keepdims=True) for d in do_bs[part]])
            q_ts.append([qh.astype(F32).T.astype(BF16) for qh in qhs[part]])
            do_ts.append([d.astype(F32).T.astype(BF16) for d in do_bs[part]])

        def block(kb, part, carry, mask=None):
            masked = mask is not None
            st = pl.multiple_of(kb * tb, tb)
            kj = k_ref[pl.ds(st, tb), :]
            vj = v_ref[pl.ds(st, tb), :]
            new = []
            dk = dv = None
            for h in (0, 1):
                dq, c, gsum = carry[h]
                z = lax.dot_general(qhs[part][h], kj, _DIMS['nt'], preferred_element_type=F32)
                l = _log_gap(z)
                sig = jnp.exp(z + l)
                if masked:
                    l = jnp.where(mask, l, 0.0)
                cum = _dot2(l, tri)
                w = jnp.exp(z + cum + c)
                if masked:
                    w = jnp.where(mask, w, 0.0)
                dp = lax.dot_general(do_bs[part][h], vj, _DIMS['nt'], preferred_element_type=F32)
                pw = w * dp
                after = _dot2(pw, tri_s)
                dz = pw - sig * (deltas[part][h] - gsum - after)
                if masked:
                    dz = jnp.where(mask, dz, 0.0)
                dz_b = dz.astype(BF16)
                dk_h = jnp.dot(q_ts[part][h], dz_b, preferred_element_type=F32)
                dv_h = jnp.dot(do_ts[part][h], w.astype(BF16), preferred_element_type=F32)
                dk = dk_h if dk is None else dk + dk_h
                dv = dv_h if dv is None else dv + dv_h
                dq = dq + jnp.dot(dz_b, kj, preferred_element_type=F32)
                new.append((dq, c + cum[:, 0:1], gsum + (after[:, 0:1] + pw[:, 0:1])))
            dk_ref[kb] += dk
            dv_ref[kb] += dv
            return tuple(new)

        zero1 = jnp.zeros((tb, 1), F32)
        zero = (jnp.zeros((tb, LANES), F32), zero1, zero1)
        carries = []
        for part in (0, 1):
            kb0 = 2 * i + part
            cr = block(kb0, part, (zero, zero), strict)
            cr = block(jnp.maximum(kb0 - 1, 0), part, cr, jnp.logical_and(i > 0, everywhere) if part == 0 else None)
            carries.append(cr)

        def live(st):
            jj, ca, cb = st
            return jnp.logical_and(jj < 2 * i, jnp.logical_or(
                _some_weight_left([ca[0][1], ca[1][1]], zbs[0]), _some_weight_left([cb[0][1], cb[1][1]], zbs[1])))

        def more(st):
            jj, ca, cb = st
            ka = 2 * i - 2 - jj
            ca = block(jnp.maximum(ka, 0), 0, ca, jnp.logical_and(ka >= 0, everywhere))
            cb = block(ka + 1, 1, cb)
            return jj + 1, ca, cb

        _, ca, cb = lax.while_loop(live, more, (jnp.int32(0), carries[0], carries[1]))
        dq_ref[...] = (jnp.concatenate([jnp.where(lo_mask, c2[0][0], c2[1][0]) for c2 in (ca, cb)], axis=0)
                       * scale).astype(BF16)

        @pl.when(i == nq - 1)
        def _():
            def turn(j, carry_):
                st = pl.multiple_of(j * tb, tb)
                dk_out[pl.ds(st, tb), :] = dk_ref[j].T.astype(BF16)
                dv_out[pl.ds(st, tb), :] = dv_ref[j].T.astype(BF16)
                return carry_

            lax.fori_loop(0, nb, turn, 0)

    return pl.pallas_call(
        body, grid=(P, nq),
        in_specs=[pl.BlockSpec((tq, LANES), lambda p, i: (i, p)),
                  pl.BlockSpec((T, LANES), lambda p, i: (0, P + p)),
                  pl.BlockSpec((T, LANES), lambda p, i: (0, 2 * P + p)),
                  pl.BlockSpec((tq, LANES), lambda p, i: (i, p)),
                  pl.BlockSpec((tq, LANES), lambda p, i: (i, p)),
                  pl.BlockSpec((1, LANES), lambda p, i: (0, p))],
        out_specs=[pl.BlockSpec((tq, LANES), lambda p, i: (i, p)),
                   pl.BlockSpec((T, LANES), lambda p, i: (0, p)),
                   pl.BlockSpec((T, LANES), lambda p, i: (0, p)),
                   pl.BlockSpec((1, LANES), lambda p, i: (0, p))],
        out_shape=[jax.ShapeDtypeStruct((T, AW), BF16)] * 3 + [jax.ShapeDtypeStruct((1, AW), F32)],
        scratch_shapes=[pltpu.VMEM((8, LANES), F32), pltpu.VMEM((nb, LANES, tb), F32),
                        pltpu.VMEM((nb, LANES, tb), F32)],
        compiler_params=_cparams(("parallel", "arbitrary")), name=name)(qkv, qkv, qkv, o, da, g_attn)


def _ada_fwd(c_all, w_ada, b_ada, *, name):
    def body(c_ref, w_ref, b_ref, o_ref):
        cv = c_ref[...]
        sc = cv * _sigmoid(cv)
        o_ref[...] = jnp.dot(sc, w_ref[...], preferred_element_type=F32,
                             precision=lax.Precision.HIGHEST) + b_ref[...]

    return pl.pallas_call(body, out_shape=jax.ShapeDtypeStruct((c_all.shape[0], w_ada.shape[1]), F32),
                          compiler_params=_cparams(), name=name)(c_all, w_ada, b_ada)


def _ada_bwd(c_all_t, dmod, *, name):
    def body(c_ref, d_ref, o_ref):
        cv = c_ref[...]
        sc = cv * _sigmoid(cv)
        o_ref[...] = jnp.dot(sc, d_ref[...], preferred_element_type=F32, precision=lax.Precision.HIGHEST)

    return pl.pallas_call(body, out_shape=jax.ShapeDtypeStruct((c_all_t.shape[0], dmod.shape[1]), F32),
                          compiler_params=_cparams(), name=name)(c_all_t, dmod)


def _adamw(w, g, m, v, *, name):
    R, C = w.shape
    tr = _tile(R, max(8, (1 << 18) // C), 8)

    def body(w_ref, g_ref, m_ref, v_ref, d_ref, nm_ref, nv_ref):
        gv = g_ref[...]
        m2 = ADAM_B1 * m_ref[...] + (1.0 - ADAM_B1) * gv
        v2 = ADAM_B2 * v_ref[...] + (1.0 - ADAM_B2) * jnp.square(gv)
        m_hat = m2 / (1.0 - ADAM_B1 ** ADAM_STEP)
        v_hat = v2 / (1.0 - ADAM_B2 ** ADAM_STEP)
        d_ref[...] = -ADAM_LR * (m_hat / (jnp.sqrt(v_hat) + ADAM_EPS) + ADAM_WD * w_ref[...])
        nm_ref[...] = m2
        nv_ref[...] = v2

    spec = pl.BlockSpec((tr, C), lambda i: (i, 0))
    return pl.pallas_call(
        body, grid=(R // tr,), in_specs=[spec] * 4, out_specs=[spec] * 3,
        out_shape=[jax.ShapeDtypeStruct((R, C), F32)] * 3,
        compiler_params=_cparams(("parallel",)), name=name)(w, g, m, v)


def _sum_devices(a, *, name):
    def body(a_ref, o_ref):
        s = a_ref[0]
        for d in range(1, a_ref.shape[0]):
            s = s + a_ref[d]
        o_ref[...] = s

    return pl.pallas_call(body, out_shape=jax.ShapeDtypeStruct(a.shape[1:], F32),
                          compiler_params=_cparams(), name=name)(a)


def _place():
    return lax.axis_index("x"), lax.axis_index("y"), lax.axis_index("c")


def _flip(v, bit):
    return 1 - v if bit else v


def _allgather8(blk, *, name):
    R, C = blk.shape

    def body(x_ref, out_ref, send_sems, recv_sems):
        x, y, c = _place()
        me = 4 * x + 2 * y + c
        out_ref[me] = x_ref[...]
        copies = []
        for k in range(1, 8):
            peer = (_flip(x, (k >> 2) & 1), _flip(y, (k >> 1) & 1), _flip(c, k & 1))
            cp = pltpu.make_async_remote_copy(
                src_ref=x_ref, dst_ref=out_ref.at[me], send_sem=send_sems.at[k - 1],
                recv_sem=recv_sems.at[k - 1], device_id=peer, device_id_type=MESH)
            cp.start()
            copies.append(cp)
        for cp in copies:
            cp.wait()

    return pl.pallas_call(
        body, out_shape=jax.ShapeDtypeStruct((8, R, C), F32),
        in_specs=[pl.BlockSpec(memory_space=pltpu.VMEM)], out_specs=pl.BlockSpec(memory_space=pltpu.VMEM),
        scratch_shapes=[pltpu.SemaphoreType.DMA((7,)), pltpu.SemaphoreType.DMA((7,))],
        compiler_params=_cparams(), name=name)(blk)


def _aligned(v, m):
    return v if isinstance(v, int) else pl.multiple_of(v, m)


def _rows_half(ref, half):
    n = ref.shape[0] // 2
    return ref.at[pl.ds(_aligned(half * n, 16), n)]


def _region(ref, kind, slot, half):
    if kind == 'col':
        n, cs = ref.shape[0] // 2, ref.shape[1] // 4
        return ref.at[pl.ds(_aligned(half * n, 16), n), pl.ds(_aligned(slot * cs, LANES), cs)]
    rs = ref.shape[0] // 4
    return ref.at[pl.ds(_aligned(slot * rs + half * (rs // 2), 16), rs // 2)]


def _other_chips(x, y):
    return [(1 - x, y), (x, 1 - y), (1 - x, 1 - y)]


class _Comm:
    def __init__(self, ins, outs, sems, start, finish):
        self.ins, self.outs, self.sems, self.start, self.finish = list(ins), list(outs), list(sems), start, finish


def _comm_specs(comm):
    if comm is None:
        return [], [], [], []
    anyspec = pl.BlockSpec(memory_space=pl.ANY)
    return [anyspec] * len(comm.ins), [anyspec] * len(comm.outs), list(comm.outs), list(comm.sems)


def _comm_hooks(comm, first, last, refs):
    if comm is None:
        return (lambda: None), (lambda: None)

    def at_entry():
        pl.when(first)(lambda: comm.start(*refs))

    def at_exit():
        pl.when(last)(lambda: comm.finish(*refs))

    return at_entry, at_exit


def _comm_alone(comm, *, name):
    ni, no = len(comm.ins), len(comm.outs)

    def body(*refs):
        parts = (refs[:ni], refs[ni:ni + no], refs[ni + no:])
        comm.start(*parts)
        comm.finish(*parts)

    i_specs, o_specs, o_shapes, sems = _comm_specs(comm)
    return pl.pallas_call(body, out_shape=o_shapes, in_specs=i_specs, out_specs=o_specs, scratch_shapes=sems,
                          compiler_params=_cparams(), name=name)(*comm.ins)


def _gather_comm(shards, kinds):
    nw = len(shards)
    full_shapes = []
    for s, kind in zip(shards, kinds):
        full_shapes.append((s.shape[0], 4 * s.shape[1]) if kind == 'col' else (4 * s.shape[0], s.shape[1]))

    def copies(sh, full, sems):
        lsem, ssem, rsem, fssem, frsem = sems
        x, y, c = _place()
        me_slot = 2 * x + y
        chips = _other_chips(x, y)
        local, ici, landed, fwd, passed = [], [], [], [], []
        for w in range(nw):
            for h in (0, 1):
                local.append(pltpu.make_async_copy(_rows_half(sh[w], h), _region(full[w], kinds[w], me_slot, h),
                                                   lsem.at[w, h]))
            for r, (px, py) in enumerate(chips):
                ici.append(pltpu.make_async_remote_copy(
                    src_ref=_rows_half(sh[w], c), dst_ref=_region(full[w], kinds[w], me_slot, c),
                    send_sem=ssem.at[w, r], recv_sem=rsem.at[w, r], device_id=(px, py, c), device_id_type=MESH))
                mine = _region(full[w], kinds[w], 2 * px + py, c)
                landed.append(pltpu.make_async_remote_copy(
                    src_ref=mine, dst_ref=mine, send_sem=ssem.at[w, r], recv_sem=rsem.at[w, r],
                    device_id=(px, py, c), device_id_type=MESH))
                fwd.append(pltpu.make_async_remote_copy(
                    src_ref=mine, dst_ref=mine, send_sem=fssem.at[w, r], recv_sem=frsem.at[w, r],
                    device_id=(x, y, 1 - c), device_id_type=MESH))
                theirs = _region(full[w], kinds[w], 2 * px + py, 1 - c)
                passed.append(pltpu.make_async_remote_copy(
                    src_ref=theirs, dst_ref=theirs, send_sem=fssem.at[w, r], recv_sem=frsem.at[w, r],
                    device_id=(x, y, 1 - c), device_id_type=MESH))
        return local, ici, landed, fwd, passed

    def start(sh, full, sems):
        local, ici, _, _, _ = copies(sh, full, sems)
        for cp in local + ici:
            cp.start()

    def finish(sh, full, sems):
        local, ici, landed, fwd, passed = copies(sh, full, sems)
        for got, cp in zip(landed, fwd):
            got.wait_recv()
            cp.start()
        for got in passed:
            got.wait_recv()
        for cp in ici + fwd:
            cp.wait_send()
        for cp in local:
            cp.wait()

    return _Comm(shards, [jax.ShapeDtypeStruct(s, BF16) for s in full_shapes],
                 [pltpu.SemaphoreType.DMA((nw, 2))] + [pltpu.SemaphoreType.DMA((nw, 3))] * 4, start, finish)


def _exchange_comm(grads, kinds):
    nw = len(grads)

    def copies(g, r1, sems):
        ssem, rsem = sems
        x, y, c = _place()
        out, back = [], []
        for w in range(nw):
            for slot in range(4):
                out.append(pltpu.make_async_remote_copy(
                    src_ref=_region(g[w], kinds[w], slot, 1 - c), dst_ref=_region(r1[w], kinds[w], slot, 1 - c),
                    send_sem=ssem.at[w, slot], recv_sem=rsem.at[w, slot], device_id=(x, y, 1 - c),
                    device_id_type=MESH))
                mine = _region(r1[w], kinds[w], slot, c)
                back.append(pltpu.make_async_remote_copy(
                    src_ref=mine, dst_ref=mine, send_sem=ssem.at[w, slot], recv_sem=rsem.at[w, slot],
                    device_id=(x, y, 1 - c), device_id_type=MESH))
        return out, back

    def start(g, r1, sems):
        for cp in copies(g, r1, sems)[0]:
            cp.start()

    def finish(g, r1, sems):
        out, back = copies(g, r1, sems)
        for got in back:
            got.wait_recv()
        for cp in out:
            cp.wait_send()

    return _Comm(grads, [jax.ShapeDtypeStruct(g.shape, F32) for g in grads],
                 [pltpu.SemaphoreType.DMA((nw, 4))] * 2, start, finish)


def _add_core_halves(g, r1, place, kind, *, name):
    if kind == 'col':
        n, cs = g.shape[0] // 2, g.shape[1] // 4
        tr = _tile(n, 256, 16)
        nt = n // tr
        ispec = pl.BlockSpec((tr, cs), lambda s, t, pr: (pr[0] * nt + t, s))
    else:
        rs, cs = g.shape[0] // 4, g.shape[1]
        n = rs // 2
        tr, nt = n, 1
        ispec = pl.BlockSpec((tr, cs), lambda s, t, pr: (s * 2 + pr[0], 0))

    def body(pr, a_ref, b_ref, o_ref):
        o_ref[...] = (a_ref[...] + b_ref[...]).astype(BF16)

    return pl.pallas_call(
        body,
        grid_spec=pltpu.PrefetchScalarGridSpec(
            num_scalar_prefetch=1, grid=(4, nt), in_specs=[ispec, ispec],
            out_specs=pl.BlockSpec((None, tr, cs), lambda s, t, pr: (s, t, 0))),
        out_shape=jax.ShapeDtypeStruct((4, n, cs), BF16),
        compiler_params=_cparams(("parallel", "parallel")), name=name)(place, g, r1)


def _scatter_comm(hs):
    nw = len(hs)

    def copies(h, r2, sems):
        ssem, rsem = sems
        x, y, c = _place()
        return [pltpu.make_async_remote_copy(
            src_ref=h[w].at[2 * px + py], dst_ref=r2[w].at[r], send_sem=ssem.at[w, r],
            recv_sem=rsem.at[w, r], device_id=(px, py, c), device_id_type=MESH)
            for w in range(nw) for r, (px, py) in enumerate(_other_chips(x, y))]

    def start(h, r2, sems):
        for cp in copies(h, r2, sems):
            cp.start()

    def finish(h, r2, sems):
        for cp in copies(h, r2, sems):
            cp.wait()

    return _Comm(hs, [jax.ShapeDtypeStruct((3,) + a.shape[1:], a.dtype) for a in hs],
                 [pltpu.SemaphoreType.DMA((nw, 3))] * 2, start, finish)


def _sum_owner(hs, r2, place, *, name):
    _, n, cs = hs.shape
    tr = _tile(n, 256, 16)
    nt = n // tr

    def body(pr, h_ref, r_ref, o_ref):
        o_ref[...] = ((h_ref[...].astype(F32) + r_ref[0].astype(F32)) + r_ref[1].astype(F32)) + r_ref[2].astype(F32)

    return pl.pallas_call(
        body,
        grid_spec=pltpu.PrefetchScalarGridSpec(
            num_scalar_prefetch=1, grid=(nt,),
            in_specs=[pl.BlockSpec((None, tr, cs), lambda t, pr: (pr[1], t, 0)),
                      pl.BlockSpec((3, tr, cs), lambda t, pr: (0, t, 0))],
            out_specs=pl.BlockSpec((None, tr, cs), lambda t, pr: (pr[0], t, 0))),
        out_shape=jax.ShapeDtypeStruct((2, n, cs), F32),
        compiler_params=_cparams(("parallel",)), name=name)(place, hs, r2)


def _share_with_sibling(fins, *, name):
    nw = len(fins)

    def body(*refs):
        fin, out = refs[:nw], refs[nw:2 * nw]
        ssem, rsem = refs[2 * nw:]
        x, y, c = _place()
        copies = []
        for w in range(nw):
            cp = pltpu.make_async_remote_copy(
                src_ref=fin[w].at[c], dst_ref=out[w].at[c], send_sem=ssem.at[w], recv_sem=rsem.at[w],
                device_id=(x, y, 1 - c), device_id_type=MESH)
            cp.start()
            copies.append(cp)
        for w in range(nw):
            theirs = out[w].at[1 - c]
            pltpu.make_async_remote_copy(
                src_ref=theirs, dst_ref=theirs, send_sem=ssem.at[w], recv_sem=rsem.at[w],
                device_id=(x, y, 1 - c), device_id_type=MESH).wait_recv()
        for cp in copies:
            cp.wait_send()

    anyspec = pl.BlockSpec(memory_space=pl.ANY)
    return pl.pallas_call(
        body, out_shape=[jax.ShapeDtypeStruct(a.shape, F32) for a in fins],
        in_specs=[anyspec] * nw, out_specs=[anyspec] * nw,
        input_output_aliases={w: w for w in range(nw)},
        scratch_shapes=[pltpu.SemaphoreType.DMA((nw,))] * 2,
        compiler_params=_cparams(), name=name)(*fins)


class _Plan:
    def __init__(self, wfull):
        self.w = dict(wfull)
        self.grads = {}

    def ffn_width(self):
        return self.w['ff1_w_out'].shape[0]

    def comm(self, site):
        return None

    def done(self, site, results):
        pass

    def ready(self, group, names, arrays):
        self.grads.update(zip(names, arrays))


def _riding(plan, site, call):
    comm = plan.comm(site)
    res = call(comm)
    if comm is None:
        return res
    *main, extra = res
    plan.done(site, extra)
    return main[0] if len(main) == 1 else tuple(main)


def _merge_comms(comms):
    if len(comms) == 1:
        return comms[0]

    def parts(refs, field):
        out, pos = [], 0
        for c in comms:
            n = len(getattr(c, field))
            out.append(refs[pos:pos + n])
            pos += n
        return out

    def run(which):
        def fn(ins, outs, sems):
            for c, i, o, s in zip(comms, parts(ins, 'ins'), parts(outs, 'outs'), parts(sems, 'sems')):
                getattr(c, which)(i, o, s)
        return fn

    return _Comm(sum((c.ins for c in comms), []), sum((c.outs for c in comms), []),
                 sum((c.sems for c in comms), []), run('start'), run('finish'))


class _DistPlan(_Plan):
    RIDES = {
        'pre_fwd_ff1': [('gather', ['ff1_w_in'])],
        'ffn_in_ff1': [('gather', ['ff1_w_out', 'w_in_mix', 'w_out_mix'])],
        'attn_fwd': [('gather', ['ff2_w_in', 'ff2_w_out'])],
        'pre_bwd_ff2': [('exchange', ['ff2_w_in', 'ff2_w_out'])],
        'conv_bwd1': [('scatter', ['ff2_w_in', 'ff2_w_out'])],
        'pre_bwd_mix': [('exchange', ['w_in_mix', 'w_out_mix'])],
        'ffn_dact_ff1': [('scatter', ['w_in_mix', 'w_out_mix'])],
        'dw_out_ff1': [('exchange', ['ff1_w_in'])],
        'dh_ff1': [('scatter', ['ff1_w_in']), ('exchange', ['ff1_w_out'])],
    }
    AFTER = [('scatter', ['ff1_w_out'])]

    def __init__(self, shards, place):
        self.shards, self.place, self.kind = shards, place, dict(BIG)
        self.w, self.grads, self.hs, self.fin = {}, {}, {}, {}

    def _make(self, kind, names):
        if kind == 'gather':
            return _gather_comm([self.shards[n] for n in names], [self.kind[n] for n in names])
        if kind == 'exchange':
            return _exchange_comm([self.grads[n] for n in names], [self.kind[n] for n in names])
        return _scatter_comm([self.hs[n] for n in names])

    def _take(self, kind, names, results):
        for n, r in zip(names, results):
            if kind == 'gather':
                self.w[n] = r
            elif kind == 'exchange':
                self.hs[n] = _add_core_halves(self.grads[n], r, self.place, self.kind[n], name=f"grad_core_add_{n}")
            else:
                self.fin[n] = _sum_owner(self.hs[n], r, self.place, name=f"grad_owner_sum_{n}")

    def ffn_width(self):
        return 4 * self.shards['ff1_w_out'].shape[0]

    def comm(self, site):
        rides = self.RIDES.get(site)
        return None if rides is None else _merge_comms([self._make(k, names) for k, names in rides])

    def done(self, site, results):
        pos = 0
        for kind, names in self.RIDES[site]:
            self._take(kind, names, results[pos:pos + len(names)])
            pos += len(names)

    def finish(self):
        for kind, names in self.AFTER:
            self._take(kind, names, _comm_alone(self._make(kind, names), name=f"grad_{kind}_{names[0]}"))
        names = list(self.shards)
        return dict(zip(names, _share_with_sibling([self.fin[n] for n in names], name="grad_share")))


def _local_step(x, target, mod, gains, plan, g_attn, conv_w, cvec):
    T, D = x.shape
    F = plan.ffn_width()
    AW = D // 2
    C = D - AW
    NQKV = 3 * AW
    MIX = NQKV + 2 * C
    tM = _tile(T, 1024)
    tkT = _tile(T, 1024)

    def ffn_fwd(h, tag):
        w_in = plan.w[f"{tag}_w_in"]
        jac, act = _riding(plan, f"ffn_in_{tag}",
                           lambda cm: _ffn_in(h, w_in, T=T, D=D, F=F, name=f"ffn_in_{tag}", comm=cm))
        f = _matmul(act, plan.w[f"{tag}_w_out"], mode='nn', M=T, N=D, K=F, tm=tM, tn=_tile(D, 1024), tk=F,
                    out_dtype=F32, name=f"ffn_out_{tag}")
        return h, jac, act, f

    def ffn_bwd(df, dout, xin, saved, s, tag, prev=None):
        h, jac, act, f = saved
        w_in, w_out = plan.w[f"{tag}_w_in"], plan.w[f"{tag}_w_out"]
        dgu = _riding(plan, f"ffn_dact_{tag}",
                      lambda cm: _ffn_dact(df, w_out, jac, T=T, D=D, F=F, name=f"ffn_dact_{tag}", comm=cm))
        tnf = _tile(F, 2816)
        nf = F // tnf
        dw_in = _matmul(h, dgu, mode='tn', M=D, N=2 * F, K=T, tm=_tile(D, 1024), tn=tnf, tk=tkT, out_dtype=F32,
                        b_spec=pl.BlockSpec((None, tkT, tnf), lambda i, j, k: (j // nf, k, j % nf)),
                        name=f"dw_in_{tag}")
        plan.ready(tag, [f"{tag}_w_in"], [dw_in])
        dw_out = _riding(plan, f"dw_out_{tag}", lambda cm: _matmul(
            act, df, mode='tn', M=F, N=D, K=T, tm=_tile(F, 1408), tn=_tile(D, 1024), tk=tkT, out_dtype=F32,
            name=f"dw_out_{tag}", comm=cm))
        plan.ready(tag, [f"{tag}_w_out"], [dw_out])
        dh = _riding(plan, f"dh_{tag}", lambda cm: _matmul(
            dgu, w_in, mode='nt', M=T, N=D, K=2 * F, tm=tM, tn=_tile(D, 1024), tk=F, out_dtype=F32,
            a_spec=pl.BlockSpec((None, tM, F), lambda i, j, k: (k, i, 0)), name=f"dh_{tag}", comm=cm))
        if prev is None:
            return _riding(plan, f"pre_bwd_{tag}", lambda cm: _pre_bwd(
                dh, xin, dout, gains, mod, T=T, s=s, name=f"pre_bwd_{tag}", comm=cm))
        return _riding(plan, f"pre_bwd_{tag}", lambda cm: _pre_post_bwd(
            dh, xin, dout, prev[0], gains, mod, T=T, s=s, res_w_prev=prev[1], name=f"pre_bwd_{tag}", comm=cm))

    h1 = _riding(plan, "pre_fwd_ff1", lambda cm: _pre_fwd(x, gains, mod, T=T, s=0, name="pre_fwd_ff1", comm=cm))[0]
    s1 = ffn_fwd(h1, "ff1")
    x1, h2 = _post_pre_fwd(x, s1[3], gains, mod, T=T, s=0, res_w=0.5, name="post_fwd_ff1")
    w_in_mix, w_out_mix = plan.w['w_in_mix'], plan.w['w_out_mix']
    tnq = _tile(AW, 512)
    qkv = _matmul(h2, w_in_mix, mode='nn', M=T, N=NQKV, K=D, tm=tM, tn=tnq, tk=D, out_dtype=BF16, name="proj_qkv")
    tnc = _tile(C, 512)
    off = NQKV // tnc
    cvg = _matmul(h2, w_in_mix, mode='nn', M=T, N=2 * C, K=D, tm=tM, tn=tnc, tk=D, out_dtype=F32,
                  b_spec=pl.BlockSpec((D, tnc), lambda i, j, k: (0, off + j)), name="proj_conv")
    o_attn, a_attn = _riding(plan, "attn_fwd", lambda cm: _attn_fwd(
        qkv, g_attn, T=T, AW=AW, a_cols=D, name="attn_fwd", comm=cm))
    mixcat = _conv_fwd(cvg, conv_w, cvec, T=T, C=C, name="conv_fwd", into=(a_attn, AW // C))
    f_mix = _matmul(mixcat, w_out_mix, mode='nn', M=T, N=D, K=D, tm=tM, tn=_tile(D, 1024), tk=D, out_dtype=F32,
                    name="mix_out")
    x2, h3 = _post_pre_fwd(x1, f_mix, gains, mod, T=T, s=1, res_w=1.0, name="post_fwd_mix")

    s3 = ffn_fwd(h3, "ff2")
    dout, sq = _post_fwd_loss(x2, s3[3], target, gains, mod, T=T, s=2, res_w=0.5, name="post_fwd_loss")

    df2, dgate2, dgpost2 = _post_bwd(dout, s3[3], gains, mod, T=T, s=2, res_w=0.5, name="post_bwd_ff2")
    dx2, df_mix, dshift2, dscale2, dgpre2, dgate_m, dgpost_m = ffn_bwd(
        df2, dout, x2, s3, 2, "ff2", prev=(f_mix, 1.0))
    dmixcat = _matmul(df_mix, w_out_mix, mode='nt', M=T, N=D, K=D, tm=tM, tn=_tile(D, 1024), tk=D, out_dtype=F32,
                      name="d_mixcat")
    dw_out_mix = _matmul(mixcat, df_mix, mode='tn', M=D, N=D, K=T, tm=_tile(D, 1024), tn=_tile(D, 1024),
                         tk=tkT, out_dtype=F32, name="dw_out_mix")
    dq, dk, dv, dg_attn = _attn_bwd(qkv, o_attn, dmixcat, g_attn, T=T, AW=AW, name="attn_bwd")
    dyc, csum, dconv_w = _riding(plan, "conv_bwd1", lambda cm: _conv_bwd1(
        cvg, (dmixcat, C, AW // C), conv_w, cvec, T=T, C=C, name="conv_bwd1", comm=cm))
    dcv, dcg = _conv_bwd2(dyc, cvg, conv_w, T=T, C=C, name="conv_bwd2")
    dproj = jnp.concatenate([dq, dk, dv, dcv, dcg], axis=1)
    dh2 = _matmul(dproj, w_in_mix, mode='nt', M=T, N=D, K=MIX, tm=tM, tn=_tile(D, 1024), tk=MIX, out_dtype=F32,
                  name="dh_mix")
    dw_in_mix = _matmul(h2, dproj, mode='tn', M=D, N=MIX, K=T, tm=_tile(D, 1024), tn=_tile(MIX, 1280),
                        tk=tkT, out_dtype=F32, name="dw_in_mix")
    plan.ready("mix", ['w_in_mix', 'w_out_mix'], [dw_in_mix, dw_out_mix])
    dx1, df1, dshift_m, dscale_m, dgpre_m, dgate1, dgpost1 = _riding(plan, "pre_bwd_mix", lambda cm: _pre_post_bwd(
        dh2, x1, dx2, s1[3], gains, mod, T=T, s=1, res_w_prev=0.5, name="pre_bwd_mix", comm=cm))

    dx0, dshift1, dscale1, dgpre1 = ffn_bwd(df1, dx1, x, s1, 0, "ff1")

    dgains = [dgpre1, dgpost1, dgpre_m, dgpost_m, dgpre2, dgpost2]
    dmod = [dshift1, dscale1, dgate1, dshift_m, dscale_m, dgate_m, dshift2, dscale2, dgate2]
    return sq, dx0, dgains, dmod, dg_attn, csum, dconv_w


def _pack_rows(pieces, width):
    rows = jnp.concatenate([p.reshape(-1) for p in pieces]).reshape(-1, width)
    pad = (-rows.shape[0]) % 8
    return jnp.pad(rows, ((0, pad), (0, 0)))


def kernel(x, c, w_ada, b_ada, g_pre_ff1, g_post_ff1, ff1_w_in, ff1_w_out, g_pre_mix, g_post_mix, w_in_mix, g_attn_out, conv_w, conv_b, conv_ln_g, conv_ln_b, w_out_mix, g_pre_ff2, g_post_ff2, ff2_w_in, ff2_w_out, loss_target, m_w_ada, m_b_ada, m_g_pre_ff1, m_g_post_ff1, m_ff1_w_in, m_ff1_w_out, m_g_pre_mix, m_g_post_mix, m_w_in_mix, m_g_attn_out, m_conv_w, m_conv_b, m_conv_ln_g, m_conv_ln_b, m_w_out_mix, m_g_pre_ff2, m_g_post_ff2, m_ff2_w_in, m_ff2_w_out, v_w_ada, v_b_ada, v_g_pre_ff1, v_g_post_ff1, v_ff1_w_in, v_ff1_w_out, v_g_pre_mix, v_g_post_mix, v_w_in_mix, v_g_attn_out, v_conv_w, v_conv_b, v_conv_ln_g, v_conv_ln_b, v_w_out_mix, v_g_pre_ff2, v_g_post_ff2, v_ff2_w_in, v_ff2_w_out):
    W = dict(w_ada=w_ada, b_ada=b_ada, g_pre_ff1=g_pre_ff1, g_post_ff1=g_post_ff1, ff1_w_in=ff1_w_in,
             ff1_w_out=ff1_w_out, g_pre_mix=g_pre_mix, g_post_mix=g_post_mix, w_in_mix=w_in_mix,
             g_attn_out=g_attn_out, conv_w=conv_w, conv_b=conv_b, conv_ln_g=conv_ln_g, conv_ln_b=conv_ln_b,
             w_out_mix=w_out_mix, g_pre_ff2=g_pre_ff2, g_post_ff2=g_post_ff2, ff2_w_in=ff2_w_in,
             ff2_w_out=ff2_w_out)
    Mo = dict(w_ada=m_w_ada, b_ada=m_b_ada, g_pre_ff1=m_g_pre_ff1, g_post_ff1=m_g_post_ff1, ff1_w_in=m_ff1_w_in,
              ff1_w_out=m_ff1_w_out, g_pre_mix=m_g_pre_mix, g_post_mix=m_g_post_mix, w_in_mix=m_w_in_mix,
              g_attn_out=m_g_attn_out, conv_w=m_conv_w, conv_b=m_conv_b, conv_ln_g=m_conv_ln_g,
              conv_ln_b=m_conv_ln_b, w_out_mix=m_w_out_mix, g_pre_ff2=m_g_pre_ff2, g_post_ff2=m_g_post_ff2,
              ff2_w_in=m_ff2_w_in, ff2_w_out=m_ff2_w_out)
    Vo = dict(w_ada=v_w_ada, b_ada=v_b_ada, g_pre_ff1=v_g_pre_ff1, g_post_ff1=v_g_post_ff1, ff1_w_in=v_ff1_w_in,
              ff1_w_out=v_ff1_w_out, g_pre_mix=v_g_pre_mix, g_post_mix=v_g_post_mix, w_in_mix=v_w_in_mix,
              g_attn_out=v_g_attn_out, conv_w=v_conv_w, conv_b=v_conv_b, conv_ln_g=v_conv_ln_g,
              conv_ln_b=v_conv_ln_b, w_out_mix=v_w_out_mix, g_pre_ff2=v_g_pre_ff2, g_post_ff2=v_g_post_ff2,
              ff2_w_in=v_ff2_w_in, ff2_w_out=v_ff2_w_out)

    T, D = x.shape[1], x.shape[2]
    AW = D // 2
    C = D - AW
    xi, yi, ci = _place()
    me = 4 * xi + 2 * yi + ci
    chip = 2 * xi + yi
    place = jnp.stack([ci, chip]).astype(jnp.int32)

    c_all = _allgather8(jnp.tile(c, (8, 1)), name="gather_c")[:, 0, :]
    ncol = w_ada.shape[1]
    b_cols = lax.dynamic_index_in_dim(b_ada.reshape(4, ncol), chip, keepdims=True).reshape(1, ncol)
    modp = _ada_fwd(c_all, w_ada, b_cols, name="ada_fwd")
    mod_g = _allgather8(modp, name="gather_mod")
    mod_all = jnp.transpose(mod_g[0::2], (1, 0, 2)).reshape(8, 4 * ncol)
    mod = lax.dynamic_index_in_dim(mod_all, me, keepdims=False).reshape(9, D)

    names = [n for n, _ in BIG]
    plan = _DistPlan({n: W[n].astype(BF16) for n in names}, place)
    cs = conv_w.shape[1]
    cw_all = _allgather8(jnp.pad(conv_w, ((0, HALO - CONV_KERNEL), (0, (-cs) % LANES))), name="gather_conv_w")
    conv_w_full = jnp.transpose(cw_all[0::2, :, :cs], (1, 0, 2)).reshape(HALO, 4 * cs)

    gains = _pack_rows([g_pre_ff1, g_post_ff1, g_pre_mix, g_post_mix, g_pre_ff2, g_post_ff2], D)
    cvec = _pack_rows([conv_b, conv_ln_g, conv_ln_b], C)
    g_attn = g_attn_out.reshape(1, AW)

    sq, dx, dgains, dmod, dg_attn, csum, dconv_w = _local_step(
        x[0], loss_target[0], mod, gains, plan, g_attn, conv_w_full, cvec)

    loss_row = jnp.zeros((1, D), F32).at[0, 0].set(jnp.sum(sq) * (0.5 / D))
    small = _pack_rows(dgains + dmod + [dg_attn, csum[0:3], dconv_w, loss_row], D)
    small_all = _allgather8(small, name="gather_small")
    tot = _sum_devices(small_all, name="sum_small")
    n_g, n_m = 6, 9
    r0 = n_g + n_m
    flat = tot.reshape(-1)
    p = r0 * D
    g_attn_grad = flat[p:p + AW]
    p += AW
    gconv_b, gln_g, gln_b = flat[p:p + C], flat[p + C:p + 2 * C], flat[p + 2 * C:p + 3 * C]
    p += 3 * C
    gconv_w_full = flat[p:p + HALO * C].reshape(HALO, C)[:CONV_KERNEL]
    p += HALO * C
    loss = flat[p]
    gconv_w = lax.dynamic_slice_in_dim(gconv_w_full, chip * cs, cs, axis=1)
    grad_small = {'g_pre_ff1': tot[0], 'g_post_ff1': tot[1], 'g_pre_mix': tot[2], 'g_post_mix': tot[3],
                  'g_pre_ff2': tot[4], 'g_post_ff2': tot[5], 'b_ada': tot[n_g:r0].reshape(-1),
                  'g_attn_out': g_attn_grad.reshape(g_attn_out.shape), 'conv_w': gconv_w, 'conv_b': gconv_b,
                  'conv_ln_g': gln_g, 'conv_ln_b': gln_b}

    dmod_all = small_all[:, n_g:r0, :].reshape(8, 9 * D)
    dmod_cols = lax.dynamic_slice_in_dim(dmod_all, chip * ncol, ncol, axis=1)
    grad_w_ada = _ada_bwd(jnp.transpose(c_all), dmod_cols, name="ada_bwd")

    grads = dict(grad_small)
    grads['w_ada'] = grad_w_ada
    for n, a in plan.finish().items():
        grads[n] = a.reshape(W[n].shape)

    delta, new_m, new_v = {}, {}, {}
    for n in ['w_ada'] + names:
        delta[n], new_m[n], new_v[n] = _adamw(W[n], grads[n], Mo[n], Vo[n], name=f"adamw_{n}")
    smalls = [n for n in WEIGHTS if n not in delta]
    sizes = [W[n].size for n in smalls]
    tot_sz = sum(sizes)
    padn = (-tot_sz) % (8 * LANES)

    def pack(d):
        return jnp.pad(jnp.concatenate([d[n].reshape(-1) for n in smalls]), (0, padn)).reshape(-1, LANES)

    d_s, m_s, v_s = _adamw(pack(W), pack(grads), pack(Mo), pack(Vo), name="adamw_small")
    pos = 0
    for n, sz in zip(smalls, sizes):
        for dst, src in ((delta, d_s), (new_m, m_s), (new_v, v_s)):
            dst[n] = src.reshape(-1)[pos:pos + sz].reshape(W[n].shape)
        pos += sz

    return (loss, dx[None], *[grads[n] for n in WEIGHTS], *[delta[n] for n in WEIGHTS],
            *[new_m[n] for n in WEIGHTS], *[new_v[n] for n in WEIGHTS])
```

```python
import functools

import jax
import jax.numpy as jnp
from jax import lax
from jax.experimental import pallas as pl
from jax.experimental.pallas import tpu as pltpu

F32 = jnp.float32
BF16 = jnp.bfloat16
MESH = pl.DeviceIdType.MESH

HEAD_DIM = 64
CONV_KERNEL = 31
RMS_EPS = 1e-6
LN_EPS = 1e-5
ADAM_LR = 0.001
ADAM_B1 = 0.9
ADAM_B2 = 0.999
ADAM_EPS = 1e-08
ADAM_WD = 0.01
ADAM_STEP = 10

LANES = 128
HALO = 32
VMEM_LIMIT = 52 * 1024 * 1024

WEIGHTS = ['w_ada', 'b_ada', 'g_pre_ff1', 'g_post_ff1', 'ff1_w_in', 'ff1_w_out', 'g_pre_mix',
           'g_post_mix', 'w_in_mix', 'g_attn_out', 'conv_w', 'conv_b', 'conv_ln_g', 'conv_ln_b',
           'w_out_mix', 'g_pre_ff2', 'g_post_ff2', 'ff2_w_in', 'ff2_w_out']
BIG = [('ff1_w_in', 'col'), ('ff1_w_out', 'row'), ('w_in_mix', 'col'), ('w_out_mix', 'row'),
       ('ff2_w_in', 'col'), ('ff2_w_out', 'row')]


def _tile(dim, pref, mult=LANES):
    if dim <= pref:
        return dim
    best = None
    for t in range(mult, pref + 1, mult):
        if dim % t == 0:
            best = t
    assert best is not None, (dim, pref, mult)
    return best


def _cparams(sem=None):
    kw = dict(vmem_limit_bytes=VMEM_LIMIT)
    if sem is not None:
        kw['dimension_semantics'] = sem
    return pltpu.CompilerParams(**kw)


def _sigmoid(x):
    return 1.0 / (1.0 + jnp.exp(-x))


_DIMS = {'nn': (((1,), (0,)), ((), ())), 'nt': (((1,), (1,)), ((), ())), 'tn': (((0,), (0,)), ((), ()))}


def _matmul(a, b, *, mode, M, N, K, tm, tn, tk, out_dtype, name, a_spec=None, b_spec=None, comm=None):
    nm, nn, nk = M // tm, N // tn, K // tk
    assert nm * tm == M and nn * tn == N and nk * tk == K, (name, M, N, K, tm, tn, tk)
    if a_spec is None:
        a_spec = (pl.BlockSpec((tk, tm), lambda i, j, k: (k, i)) if mode == 'tn'
                  else pl.BlockSpec((tm, tk), lambda i, j, k: (i, k)))
    if b_spec is None:
        b_spec = (pl.BlockSpec((tn, tk), lambda i, j, k: (j, k)) if mode == 'nt'
                  else pl.BlockSpec((tk, tn), lambda i, j, k: (k, j)))
    dims = _DIMS[mode]
    assert nk == 1 or out_dtype == F32, name
    ci_specs, co_specs, co_shapes, csems = _comm_specs(comm)
    nci, nco = len(ci_specs), len(co_specs)

    def body(a_ref, b_ref, *rest):
        o_ref = rest[nci]
        i, j, k = pl.program_id(0), pl.program_id(1), pl.program_id(2)
        first = jnp.logical_and(jnp.logical_and(i == 0, j == 0), k == 0)
        last = jnp.logical_and(jnp.logical_and(i == nm - 1, j == nn - 1), k == nk - 1)
        at_entry, at_exit = _comm_hooks(comm, first, last, (rest[:nci], rest[nci + 1:nci + 1 + nco], rest[nci + 1 + nco:]))
        at_entry()

        def prod():
            return lax.dot_general(a_ref[...], b_ref[...], dims, preferred_element_type=F32)

        if nk == 1:
            o_ref[...] = prod().astype(o_ref.dtype)
        else:
            @pl.when(k == 0)
            def _():
                o_ref[...] = prod()

            @pl.when(k > 0)
            def _():
                o_ref[...] += prod()
        at_exit()

    sem = ("parallel", "parallel", "arbitrary") if comm is None else ("arbitrary",) * 3
    res = pl.pallas_call(
        body, grid=(nm, nn, nk), in_specs=[a_spec, b_spec] + ci_specs,
        out_specs=[pl.BlockSpec((tm, tn), lambda i, j, k: (i, j))] + co_specs,
        out_shape=[jax.ShapeDtypeStruct((M, N), out_dtype)] + co_shapes, scratch_shapes=csems,
        compiler_params=_cparams(sem), name=name)(a, b, *([] if comm is None else comm.ins))
    return res[0] if comm is None else (res[0], res[1:])


def _grid2_hooks(comm, n0, n1, refs):
    j, i = pl.program_id(0), pl.program_id(1)
    return _comm_hooks(comm, jnp.logical_and(j == 0, i == 0), jnp.logical_and(j == n0 - 1, i == n1 - 1), refs)


def _ffn_in(h, w_in, *, T, D, F, name, comm=None):
    tm, tn = _tile(T, 256), _tile(F, 2816)
    nf, nt = F // tn, T // tm
    ci_specs, co_specs, co_shapes, csems = _comm_specs(comm)
    nci, nco = len(ci_specs), len(co_specs)

    def body(h_ref, wg_ref, wu_ref, *rest):
        jac_ref, a_ref = rest[nci], rest[nci + 1]
        at_entry, at_exit = _grid2_hooks(comm, nf, nt, (rest[:nci], rest[nci + 2:nci + 2 + nco], rest[nci + 2 + nco:]))
        at_entry()
        hh = h_ref[...]
        g = jnp.dot(hh, wg_ref[...], preferred_element_type=F32)
        u = jnp.dot(hh, wu_ref[...], preferred_element_type=F32)
        s = _sigmoid(g)
        sg = g * s
        jac_ref[0] = (u * (s * (1.0 + g * (1.0 - s)))).astype(BF16)
        jac_ref[1] = sg.astype(BF16)
        a_ref[...] = (sg * u).astype(BF16)
        at_exit()

    res = pl.pallas_call(
        body, grid=(nf, nt),
        in_specs=[pl.BlockSpec((tm, D), lambda j, i: (i, 0)),
                  pl.BlockSpec((D, tn), lambda j, i: (0, j)),
                  pl.BlockSpec((D, tn), lambda j, i: (0, nf + j))] + ci_specs,
        out_specs=[pl.BlockSpec((2, tm, tn), lambda j, i: (0, i, j)),
                   pl.BlockSpec((tm, tn), lambda j, i: (i, j))] + co_specs,
        out_shape=[jax.ShapeDtypeStruct((2, T, F), BF16), jax.ShapeDtypeStruct((T, F), BF16)] + co_shapes,
        scratch_shapes=csems,
        compiler_params=_cparams(("parallel", "parallel") if comm is None else ("arbitrary", "arbitrary")),
        name=name)(h, w_in, w_in, *([] if comm is None else comm.ins))
    return (res[0], res[1]) if comm is None else (res[0], res[1], res[2:])


def _ffn_dact(df, w_out, jac, *, T, D, F, name, comm=None):
    tm, tn = _tile(T, 256), _tile(F, 2816)
    nf, nt = F // tn, T // tm
    ci_specs, co_specs, co_shapes, csems = _comm_specs(comm)
    nci, nco = len(ci_specs), len(co_specs)

    def body(df_ref, w_ref, jac_ref, *rest):
        o_ref = rest[nci]
        at_entry, at_exit = _grid2_hooks(comm, nf, nt, (rest[:nci], rest[nci + 1:nci + 1 + nco], rest[nci + 1 + nco:]))
        at_entry()
        da = lax.dot_general(df_ref[...], w_ref[...], _DIMS['nt'], preferred_element_type=F32)
        o_ref[0] = (da * jac_ref[0].astype(F32)).astype(BF16)
        o_ref[1] = (da * jac_ref[1].astype(F32)).astype(BF16)
        at_exit()

    res = pl.pallas_call(
        body, grid=(nf, nt),
        in_specs=[pl.BlockSpec((tm, D), lambda j, i: (i, 0)),
                  pl.BlockSpec((tn, D), lambda j, i: (j, 0)),
                  pl.BlockSpec((2, tm, tn), lambda j, i: (0, i, j))] + ci_specs,
        out_specs=[pl.BlockSpec((2, tm, tn), lambda j, i: (0, i, j))] + co_specs,
        out_shape=[jax.ShapeDtypeStruct((2, T, F), BF16)] + co_shapes, scratch_shapes=csems,
        compiler_params=_cparams(("parallel", "parallel") if comm is None else ("arbitrary", "arbitrary")),
        name=name)(df, w_out, jac, *([] if comm is None else comm.ins))
    return res[0] if comm is None else (res[0], res[1:])


def _rowwise(fn, *, T, tm, name, tiled=(), prev=(), nxt=(), consts=(), out_tiled=(), out_acc=(), scratch=(),
             by_ref=False, comm=None, into=None):
    n = T // tm
    assert n * tm == T and tm % HALO == 0
    hb = tm // HALO
    cols = [a if isinstance(a, tuple) else (a, a.shape[1], 0) for a in tiled]
    tiled = [a for a, _, _ in cols]
    in_specs = [pl.BlockSpec((tm, w), functools.partial(lambda cb, i: (i, cb), cb)) for _, w, cb in cols]
    in_specs += [pl.BlockSpec((HALO, a.shape[1]), lambda i: (jnp.maximum(i * hb - 1, 0), 0)) for a in prev]
    in_specs += [pl.BlockSpec((HALO, a.shape[1]), lambda i: (jnp.minimum((i + 1) * hb, T // HALO - 1), 0))
                 for a in nxt]
    in_specs += [pl.BlockSpec(a.shape, lambda i: (0, 0)) for a in consts]
    out_shape = [jax.ShapeDtypeStruct((T, c), dt) for c, dt in out_tiled]
    out_shape += [jax.ShapeDtypeStruct(s, F32) for s in out_acc]
    out_specs = [pl.BlockSpec((tm, c), lambda i: (i, 0)) for c, _ in out_tiled]
    out_specs += [pl.BlockSpec(s, lambda i: (0, 0)) for s in out_acc]
    nt, npv, nnx, nc, not_, na = len(tiled), len(prev), len(nxt), len(consts), len(out_tiled), len(out_acc)
    ci_specs, co_specs, co_shapes, csems = _comm_specs(comm)
    extra_in, aliases = [], {}
    if into is not None:
        arr, cb = into
        width = out_tiled[0][0]
        out_shape[0] = jax.ShapeDtypeStruct(arr.shape, arr.dtype)
        out_specs[0] = pl.BlockSpec((tm, width), lambda i: (i, cb))
        extra_in = [arr]
        aliases = {nt + npv + nnx + nc: 0}
    n_extra = len(extra_in)

    def body(*refs):
        pos = 0
        groups = []
        for cnt in (nt, npv, nnx, nc, n_extra, len(ci_specs), not_, na, len(co_specs), len(scratch), len(csems)):
            groups.append(refs[pos:pos + cnt])
            pos += cnt
        t_r, p_r, n_r, c_r, _, ci_r, o_r, a_r, co_r, s_r, cs_r = groups
        i = pl.program_id(0)
        at_entry, at_exit = _comm_hooks(comm, i == 0, i == n - 1, (ci_r, co_r, cs_r))
        at_entry()

        @pl.when(i == 0)
        def _():
            for r in a_r:
                r[...] = jnp.zeros_like(r)

        if by_ref:
            fn(i, n, t_r, p_r, n_r, c_r, o_r, a_r, s_r)
        else:
            outs = fn(i, n, [r[...] for r in t_r], [r[...] for r in p_r], [r[...] for r in n_r],
                      [r[...] for r in c_r], a_r, s_r)
            for r, v in zip(o_r, outs):
                r[...] = v.astype(r.dtype)
        at_exit()

    res = pl.pallas_call(
        body, grid=(n,), in_specs=in_specs + [pl.BlockSpec(memory_space=pl.ANY)] * n_extra + ci_specs,
        out_specs=out_specs + co_specs, out_shape=out_shape + co_shapes, scratch_shapes=list(scratch) + csems,
        input_output_aliases=aliases, compiler_params=_cparams(("arbitrary",)), name=name,
    )(*tiled, *prev, *nxt, *consts, *extra_in, *([] if comm is None else comm.ins))
    return res if comm is None else (res[:not_ + na], res[not_ + na:])


def _colsum(v):
    return jnp.sum(v, axis=0, keepdims=True)


def _rowmean(v):
    return jnp.mean(v, axis=-1, keepdims=True)


def _pre_math(xv, g, m, s):
    g_pre, shift, scale = g[2 * s:2 * s + 1], m[3 * s:3 * s + 1], m[3 * s + 1:3 * s + 2]
    r = lax.rsqrt(_rowmean(xv * xv) + RMS_EPS)
    return ((xv * r) * g_pre) * (1.0 + scale) + shift


def _post_math(xv, fv, g, m, s, res_w):
    g_post, gate = g[2 * s + 1:2 * s + 2], m[3 * s + 2:3 * s + 3]
    y = (fv * lax.rsqrt(_rowmean(fv * fv) + RMS_EPS)) * g_post
    return xv + (res_w * (1.0 + gate)) * y


def _pre_fwd(x, gains, mod, *, T, s, name, comm=None):
    def fn(i, n, t, p, nx, c, acc, scr):
        return [_pre_math(t[0], c[0], c[1], s)]

    return _rowwise(fn, T=T, tm=_tile(T, 512, HALO), name=name, tiled=[x], consts=[gains, mod],
                    out_tiled=[(x.shape[1], BF16)], comm=comm)


def _post_pre_fwd(x, f, gains, mod, *, T, s, res_w, name):
    def fn(i, n, t, p, nx, c, acc, scr):
        out = _post_math(t[0], t[1], c[0], c[1], s, res_w)
        return [out, _pre_math(out, c[0], c[1], s + 1)]

    return _rowwise(fn, T=T, tm=_tile(T, 512, HALO), name=name, tiled=[x, f], consts=[gains, mod],
                    out_tiled=[(x.shape[1], F32), (x.shape[1], BF16)])


def _post_fwd_loss(x, f, target, gains, mod, *, T, s, res_w, name):
    D = x.shape[1]

    def fn(i, n, t, p, nx, c, acc, scr):
        (xv, fv, tv), (g, m) = t, c
        g_post, gate = g[2 * s + 1:2 * s + 2], m[3 * s + 2:3 * s + 3]
        y = (fv * lax.rsqrt(_rowmean(fv * fv) + RMS_EPS)) * g_post
        err = (xv + (res_w * (1.0 + gate)) * y) - tv
        acc[0][...] += _colsum(err * err)
        return [err * (1.0 / D)]

    dout, sq = _rowwise(fn, T=T, tm=_tile(T, 512, HALO), name=name, tiled=[x, f, target], consts=[gains, mod],
                        out_tiled=[(D, F32)], out_acc=[(1, D)])
    return dout, sq


def _post_bwd_math(dv, fv, g, m, s, res_w, acc):
    g_post, gate = g[2 * s + 1:2 * s + 2], m[3 * s + 2:3 * s + 3]
    r2 = lax.rsqrt(_rowmean(fv * fv) + RMS_EPS)
    fh = fv * r2
    dy = dv * (res_w * (1.0 + gate))
    acc[0][...] += _colsum(dv * (res_w * (fh * g_post)))
    acc[1][...] += _colsum(dy * fh)
    gy = dy * g_post
    return r2 * (gy - fh * _rowmean(gy * fh))


def _pre_bwd_math(dhv, xv, dv, g, m, s, acc):
    g_pre, scale = g[2 * s:2 * s + 1], m[3 * s + 1:3 * s + 2]
    r = lax.rsqrt(_rowmean(xv * xv) + RMS_EPS)
    nv = xv * r
    acc[0][...] += _colsum(dhv)
    acc[1][...] += _colsum(dhv * (nv * g_pre))
    acc[2][...] += _colsum(dhv * ((1.0 + scale) * nv))
    gn = dhv * (g_pre * (1.0 + scale))
    return r * (gn - nv * _rowmean(gn * nv)) + dv


def _post_bwd(dout, f, gains, mod, *, T, s, res_w, name):
    D = f.shape[1]

    def fn(i, n, t, p, nx, c, acc, scr):
        return [_post_bwd_math(t[0], t[1], c[0], c[1], s, res_w, acc)]

    return _rowwise(fn, T=T, tm=_tile(T, 512, HALO), name=name, tiled=[dout, f], consts=[gains, mod],
                    out_tiled=[(D, BF16)], out_acc=[(1, D), (1, D)])


def _pre_bwd(dh, x, dout, gains, mod, *, T, s, name, comm=None):
    D = x.shape[1]

    def fn(i, n, t, p, nx, c, acc, scr):
        return [_pre_bwd_math(t[0], t[1], t[2], c[0], c[1], s, acc)]

    return _rowwise(fn, T=T, tm=_tile(T, 512, HALO), name=name, tiled=[dh, x, dout], consts=[gains, mod],
                    out_tiled=[(D, F32)], out_acc=[(1, D), (1, D), (1, D)], comm=comm)


def _pre_post_bwd(dh, x, dout, f_prev, gains, mod, *, T, s, res_w_prev, name, comm=None):
    D = x.shape[1]

    def fn(i, n, t, p, nx, c, acc, scr):
        dx = _pre_bwd_math(t[0], t[1], t[2], c[0], c[1], s, acc[0:3])
        return [dx, _post_bwd_math(dx, t[3], c[0], c[1], s - 1, res_w_prev, acc[3:5])]

    return _rowwise(fn, T=T, tm=_tile(T, 512, HALO), name=name, tiled=[dh, x, dout, f_prev], consts=[gains, mod],
                    out_tiled=[(D, F32), (D, BF16)], out_acc=[(1, D)] * 5, comm=comm)


SUBLANES = 8
CONV_CHUNK = 64


def _glu(cvg, C):
    return cvg[:, :C] * _sigmoid(cvg[:, C:])


def _fill_rotations(ext, rot, rows):
    for r in range(SUBLANES):
        rot[r] = ext[pl.ds(r, rows), :]


def _conv_taps(rot, w, r0, rows, off):
    acc = None
    for k in range(CONV_KERNEL):
        a, r = divmod(off(k), SUBLANES)
        term = w[k:k + 1] * rot[r, pl.ds(pl.multiple_of(r0 + a * SUBLANES, SUBLANES), rows), :]
        acc = term if acc is None else acc + term
    return acc


def _causal_off(k):
    return HALO - (CONV_KERNEL - 1) + k


def _conv_norm(rot, cw, cb, r0, rows):
    yc = _conv_taps(rot, cw, r0, rows, _causal_off) + cb
    mu = _rowmean(yc)
    d = yc - mu
    rstd = lax.rsqrt(_rowmean(d * d) + LN_EPS)
    return d * rstd, rstd


def _stage_glu(i, t, p, ext, rot, tm, C):
    ext[pl.ds(0, HALO), :] = jnp.where(i == 0, 0.0, _glu(p[0][...], C))
    ext[pl.ds(HALO, tm), :] = _glu(t[0][...], C)
    ext[pl.ds(HALO + tm, SUBLANES), :] = jnp.zeros((SUBLANES, C), F32)
    _fill_rotations(ext, rot, tm + HALO)


def _conv_scratch(tm, C):
    return [pltpu.VMEM((HALO + tm + SUBLANES, C), F32), pltpu.VMEM((SUBLANES, HALO + tm, C), F32)]


def _conv_fwd(cvg, cw, cvec, *, T, C, name, into=None):
    tm = _tile(T, 512, HALO)
    ch = min(CONV_CHUNK, tm)

    def fn(i, n, t, p, nx, c, o, acc, scr):
        ext, rot = scr
        _stage_glu(i, t, p, ext, rot, tm, C)
        w, vec = c[0][...], c[1][...]

        def chunk(ci, carry):
            r0 = pl.multiple_of(ci * ch, ch)
            yh, _ = _conv_norm(rot, w, vec[0:1], r0, ch)
            zz = yh * vec[1:2] + vec[2:3]
            o[0][pl.ds(r0, ch), :] = (zz * _sigmoid(zz)).astype(BF16)
            return carry

        lax.fori_loop(0, tm // ch, chunk, 0)

    return _rowwise(fn, T=T, tm=tm, name=name, tiled=[cvg], prev=[cvg], consts=[cw, cvec],
                    out_tiled=[(C, BF16)], scratch=_conv_scratch(tm, C), by_ref=True, into=into)[0]


def _conv_bwd1(cvg, duc, cw, cvec, *, T, C, name, comm=None):
    tm = _tile(T, 512, HALO)
    ch = min(CONV_CHUNK, tm)

    def fn(i, n, t, p, nx, c, o, acc, scr):
        ext, rot, w8 = scr

        @pl.when(i == 0)
        def _():
            w8[...] = jnp.zeros_like(w8)

        _stage_glu(i, t, p, ext, rot, tm, C)
        w, vec = c[0][...], c[1][...]
        ln_g = vec[1:2]

        def chunk(ci, carry):
            r0 = pl.multiple_of(ci * ch, ch)
            yh, rstd = _conv_norm(rot, w, vec[0:1], r0, ch)
            zz = yh * ln_g + vec[2:3]
            s = _sigmoid(zz)
            dz = t[1][pl.ds(r0, ch), :] * (s * (1.0 + zz * (1.0 - s)))
            dyh = dz * ln_g
            dyc = rstd * (dyh - _rowmean(dyh) - yh * _rowmean(dyh * yh))
            o[0][pl.ds(r0, ch), :] = dyc
            acc[0][0:1, :] += _colsum(dyc)
            acc[0][1:2, :] += _colsum(dz * yh)
            acc[0][2:3, :] += _colsum(dz)
            for k in range(CONV_KERNEL):
                a, r = divmod(_causal_off(k), SUBLANES)
                prod = dyc * rot[r, pl.ds(pl.multiple_of(r0 + a * SUBLANES, SUBLANES), ch), :]
                part = prod[0:SUBLANES]
                for g in range(1, ch // SUBLANES):
                    part = part + prod[g * SUBLANES:(g + 1) * SUBLANES]
                w8[pl.ds(k * SUBLANES, SUBLANES), :] += part
            return carry

        lax.fori_loop(0, tm // ch, chunk, 0)

        @pl.when(i == n - 1)
        def _():
            for k in range(CONV_KERNEL):
                acc[1][k:k + 1, :] = _colsum(w8[pl.ds(k * SUBLANES, SUBLANES), :])

    return _rowwise(fn, T=T, tm=tm, name=name, tiled=[cvg, duc], prev=[cvg], consts=[cw, cvec],
                    out_tiled=[(C, F32)], out_acc=[(8, C), (HALO, C)],
                    scratch=_conv_scratch(tm, C) + [pltpu.VMEM((HALO * SUBLANES, C), F32)], by_ref=True, comm=comm)


def _conv_bwd2(dyc, cvg, cw, *, T, C, name):
    tm = _tile(T, 512, HALO)
    ch = min(CONV_CHUNK, tm)

    def fn(i, n, t, p, nx, c, o, acc, scr):
        ext, rot = scr
        ext[pl.ds(0, tm), :] = t[0][...]
        ext[pl.ds(tm, HALO), :] = jnp.where(i == n - 1, 0.0, nx[0][...])
        _fill_rotations(ext, rot, tm + HALO - SUBLANES)
        w = c[0][...]

        def chunk(ci, carry):
            r0 = pl.multiple_of(ci * ch, ch)
            dug = _conv_taps(rot, w, r0, ch, lambda k: (CONV_KERNEL - 1) - k)
            cv = t[1][pl.ds(r0, ch), pl.ds(0, C)]
            s = _sigmoid(t[1][pl.ds(r0, ch), pl.ds(C, C)])
            o[0][pl.ds(r0, ch), :] = (dug * s).astype(BF16)
            o[1][pl.ds(r0, ch), :] = (dug * cv * (s * (1.0 - s))).astype(BF16)
            return carry

        lax.fori_loop(0, tm // ch, chunk, 0)

    return _rowwise(fn, T=T, tm=tm, name=name, tiled=[dyc, cvg], nxt=[dyc], consts=[cw],
                    out_tiled=[(C, BF16), (C, BF16)],
                    scratch=[pltpu.VMEM((tm + HALO, C), F32), pltpu.VMEM((SUBLANES, tm + HALO - SUBLANES, C), F32)],
                    by_ref=True)


def _split(v):
    hi = v.astype(BF16)
    return hi, (v - hi.astype(F32)).astype(BF16)


def _dot2(v, m):
    hi, lo = _split(v)
    return jnp.dot(hi, m, preferred_element_type=F32) + jnp.dot(lo, m, preferred_element_type=F32)


def _log_gap(z):
    return -(jnp.maximum(z, 0.0) + jnp.log(1.0 + jnp.exp(-jnp.abs(z))))


def _head_masks():
    lane = lax.broadcasted_iota(jnp.int32, (1, LANES), 1)
    return lane < HEAD_DIM, lane >= HEAD_DIM


LOG_WEIGHT_FLOOR = -110.0
ATTN_BLOCK = 256


def _key_norm_bound(k_ref, masks, T):
    ch = _tile(T, 512)

    def chunk(r, m):
        kk = k_ref[pl.ds(pl.multiple_of(r * ch, ch), ch), :].astype(F32)
        k2 = kk * kk
        return tuple(jnp.maximum(m[h], jnp.max(jnp.sum(jnp.where(masks[h], k2, 0.0), -1, keepdims=True),
                                               axis=0, keepdims=True)) for h in (0, 1))

    m0, m1 = lax.fori_loop(0, T // ch, chunk, (jnp.zeros((1, 1), F32), jnp.zeros((1, 1), F32)))
    row = lax.broadcasted_iota(jnp.int32, (8, LANES), 0)
    return jnp.where(row == 0, jnp.sqrt(m0), jnp.sqrt(m1))


def _score_bound(qh, kn):
    qf = qh.astype(F32)
    return jnp.sqrt(jnp.sum(qf * qf, -1, keepdims=True)) * (kn * 1.01) + 0.01


def _some_weight_left(carries, bounds):
    m = jnp.maximum(jnp.max(carries[0] + bounds[0]), jnp.max(carries[1] + bounds[1]))
    return m > LOG_WEIGHT_FLOOR


def _attn_fwd(qkv, g_attn, *, T, AW, a_cols, name, comm=None):
    P = AW // LANES
    tq = _tile(T, 2 * ATTN_BLOCK)
    tb = tq // 2
    nq = T // tq
    scale = HEAD_DIM ** -0.5
    ci_specs, co_specs, co_shapes, csems = _comm_specs(comm)
    nci, nco = len(ci_specs), len(co_specs)

    def body(q_ref, k_ref, v_ref, g_ref, *rest):
        o_ref, a_ref, kn_ref = rest[nci], rest[nci + 1], rest[nci + 2 + nco]
        at_entry, at_exit = _grid2_hooks(comm, P, nq, (rest[:nci], rest[nci + 2:nci + 2 + nco], rest[nci + 3 + nco:]))
        at_entry()
        i = pl.program_id(1)
        lo_mask, hi_mask = masks = _head_masks()

        @pl.when(i == 0)
        def _():
            kn_ref[...] = _key_norm_bound(k_ref, masks, T)

        rows = lax.broadcasted_iota(jnp.int32, (tb, tb), 0)
        cols = lax.broadcasted_iota(jnp.int32, (tb, tb), 1)
        strict = cols < rows
        everywhere = cols >= 0
        tri = jnp.where(rows >= cols, 1.0, 0.0).astype(BF16)
        qhs, zbs = [], []
        for part in (0, 1):
            q = q_ref[pl.ds(part * tb, tb), :]
            qhs.append([jnp.where(m, q, jnp.zeros_like(q)) * jnp.asarray(scale, BF16) for m in masks])
            zbs.append([_score_bound(qhs[part][h], kn_ref[h:h + 1, 0:1]) for h in (0, 1)])

        def block(kb, part, carry, mask=None):
            st = pl.multiple_of(kb * tb, tb)
            kj = k_ref[pl.ds(st, tb), :]
            vj = v_ref[pl.ds(st, tb), :]
            new = []
            for h in (0, 1):
                acc, c = carry[h]
                z = lax.dot_general(qhs[part][h], kj, _DIMS['nt'], preferred_element_type=F32)
                l = _log_gap(z)
                if mask is not None:
                    l = jnp.where(mask, l, 0.0)
                cum = _dot2(l, tri)
                w = jnp.exp(z + cum + c)
                if mask is not None:
                    w = jnp.where(mask, w, 0.0)
                new.append((acc + _dot2(w, vj), c + cum[:, 0:1]))
            return tuple(new)

        zero = (jnp.zeros((tb, LANES), F32), jnp.zeros((tb, 1), F32))
        carries = []
        for part in (0, 1):
            kb0 = 2 * i + part
            cr = block(kb0, part, (zero, zero), strict)
            cr = block(jnp.maximum(kb0 - 1, 0), part, cr, jnp.logical_and(i > 0, everywhere) if part == 0 else None)
            carries.append(cr)

        def live(st):
            jj, ca, cb = st
            return jnp.logical_and(jj < 2 * i, jnp.logical_or(
                _some_weight_left([ca[0][1], ca[1][1]], zbs[0]), _some_weight_left([cb[0][1], cb[1][1]], zbs[1])))

        def more(st):
            jj, ca, cb = st
            ka = 2 * i - 2 - jj
            ca = block(jnp.maximum(ka, 0), 0, ca, jnp.logical_and(ka >= 0, everywhere))
            cb = block(ka + 1, 1, cb)
            return jj + 1, ca, cb

        _, ca, cb = lax.while_loop(live, more, (jnp.int32(0), carries[0], carries[1]))
        o = jnp.concatenate([jnp.where(lo_mask, c2[0][0], c2[1][0]) for c2 in (ca, cb)], axis=0)
        o2 = o * o
        r0 = lax.rsqrt(jnp.sum(jnp.where(lo_mask, o2, 0.0), -1, keepdims=True) * (1.0 / HEAD_DIM) + RMS_EPS)
        r1 = lax.rsqrt(jnp.sum(jnp.where(hi_mask, o2, 0.0), -1, keepdims=True) * (1.0 / HEAD_DIM) + RMS_EPS)
        o_ref[...] = o
        a_ref[...] = ((o * jnp.where(lo_mask, r0, r1)) * g_ref[...]).astype(BF16)
        at_exit()

    res = pl.pallas_call(
        body, grid=(P, nq),
        in_specs=[pl.BlockSpec((tq, LANES), lambda p, i: (i, p)),
                  pl.BlockSpec((T, LANES), lambda p, i: (0, P + p)),
                  pl.BlockSpec((T, LANES), lambda p, i: (0, 2 * P + p)),
                  pl.BlockSpec((1, LANES), lambda p, i: (0, p))] + ci_specs,
        out_specs=[pl.BlockSpec((tq, LANES), lambda p, i: (i, p)),
                   pl.BlockSpec((tq, LANES), lambda p, i: (i, p))] + co_specs,
        out_shape=[jax.ShapeDtypeStruct((T, AW), F32), jax.ShapeDtypeStruct((T, a_cols), BF16)] + co_shapes,
        scratch_shapes=[pltpu.VMEM((8, LANES), F32)] + csems,
        compiler_params=_cparams(("parallel", "arbitrary") if comm is None else ("arbitrary", "arbitrary")),
        name=name)(qkv, qkv, qkv, g_attn, *([] if comm is None else comm.ins))
    return (res[0], res[1]) if comm is None else (res[0], res[1], res[2:])


def _attn_bwd(qkv, o, da, g_attn, *, T, AW, name):
    P = AW // LANES
    tq = _tile(T, 2 * ATTN_BLOCK)
    tb = tq // 2
    nq, nb = T // tq, T // tb
    scale = HEAD_DIM ** -0.5

    def body(q_ref, k_ref, v_ref, o_ref, da_ref, g_ref, dq_ref, dk_out, dv_out, dg_ref, kn_ref, dk_ref, dv_ref):
        i = pl.program_id(1)
        lo_mask, hi_mask = masks = _head_masks()

        @pl.when(i == 0)
        def _():
            dk_ref[...] = jnp.zeros_like(dk_ref)
            dv_ref[...] = jnp.zeros_like(dv_ref)
            dg_ref[...] = jnp.zeros_like(dg_ref)
            kn_ref[...] = _key_norm_bound(k_ref, masks, T)

        rows = lax.broadcasted_iota(jnp.int32, (tb, tb), 0)
        cols = lax.broadcasted_iota(jnp.int32, (tb, tb), 1)
        strict = cols < rows
        everywhere = cols >= 0
        tri = jnp.where(rows >= cols, 1.0, 0.0).astype(BF16)
        tri_s = jnp.where(rows > cols, 1.0, 0.0).astype(BF16)
        o_all = o_ref[...]
        da = da_ref[...]
        g = g_ref[...]
        o2 = o_all * o_all
        r0 = lax.rsqrt(jnp.sum(jnp.where(lo_mask, o2, 0.0), -1, keepdims=True) * (1.0 / HEAD_DIM) + RMS_EPS)
        r1 = lax.rsqrt(jnp.sum(jnp.where(hi_mask, o2, 0.0), -1, keepdims=True) * (1.0 / HEAD_DIM) + RMS_EPS)
        r = jnp.where(lo_mask, r0, r1)
        oh = o_all * r
        gy = da * g
        gyo = gy * oh
        m0 = jnp.sum(jnp.where(lo_mask, gyo, 0.0), -1, keepdims=True) * (1.0 / HEAD_DIM)
        m1 = jnp.sum(jnp.where(hi_mask, gyo, 0.0), -1, keepdims=True) * (1.0 / HEAD_DIM)
        do_all = r * (gy - oh * jnp.where(lo_mask, m0, m1))
        dg_ref[...] += _colsum(da * oh)

        qhs, zbs, do_bs, deltas, q_ts, do_ts = [], [], [], [], [], []
        for part in (0, 1):
            q = q_ref[pl.ds(part * tb, tb), :]
            o = o_all[part * tb:(part + 1) * tb]
            do = do_all[part * tb:(part + 1) * tb]
            qhs.append([jnp.where(m, q, jnp.zeros_like(q)) * jnp.asarray(scale, BF16) for m in masks])
            zbs.append([_score_bound(qhs[part][h], kn_ref[h:h + 1, 0:1]) for h in (0, 1)])
            do_bs.append([jnp.where(m, do, 0.0).astype(BF16) for m in masks])
            deltas.append([jnp.sum(d.astype(F32) * o, -1, keepdims=True) for d in do_bs[part]])
            q_ts.append([qh.astype(F32).T.astype(BF16) for qh in qhs[part]])
            do_ts.append([d.astype(F32).T.astype(BF16) for d in do_bs[part]])

        def block(kb, part, carry, mask=None):
            masked = mask is not None
            st = pl.multiple_of(kb * tb, tb)
            kj = k_ref[pl.ds(st, tb), :]
            vj = v_ref[pl.ds(st, tb), :]
            new = []
            dk = dv = None
            for h in (0, 1):
                dq, c, gsum = carry[h]
                z = lax.dot_general(qhs[part][h], kj, _DIMS['nt'], preferred_element_type=F32)
                l = _log_gap(z)
                sig = jnp.exp(z + l)
                if masked:
                    l = jnp.where(mask, l, 0.0)
                cum = _dot2(l, tri)
                w = jnp.exp(z + cum + c)
                if masked:
                    w = jnp.where(mask, w, 0.0)
                dp = lax.dot_general(do_bs[part][h], vj, _DIMS['nt'], preferred_element_type=F32)
                pw = w * dp
                after = _dot2(pw, tri_s)
                dz = pw - sig * (deltas[part][h] - gsum - after)
                if masked:
                    dz = jnp.where(mask, dz, 0.0)
                dz_b = dz.astype(BF16)
                dk_h = jnp.dot(q_ts[part][h], dz_b, preferred_element_type=F32)
                dv_h = jnp.dot(do_ts[part][h], w.astype(BF16), preferred_element_type=F32)
                dk = dk_h if dk is None else dk + dk_h
                dv = dv_h if dv is None else dv + dv_h
                dq = dq + jnp.dot(dz_b, kj, preferred_element_type=F32)
                new.append((dq, c + cum[:, 0:1], gsum + (after[:, 0:1] + pw[:, 0:1])))
            dk_ref[kb] += dk
            dv_ref[kb] += dv
            return tuple(new)

        zero1 = jnp.zeros((tb, 1), F32)
        zero = (jnp.zeros((tb, LANES), F32), zero1, zero1)
        carries = []
        for part in (0, 1):
            kb0 = 2 * i + part
            cr = block(kb0, part, (zero, zero), strict)
            cr = block(jnp.maximum(kb0 - 1, 0), part, cr, jnp.logical_and(i > 0, everywhere) if part == 0 else None)
            carries.append(cr)

        def live(st):
            jj, ca, cb = st
            return jnp.logical_and(jj < 2 * i, jnp.logical_or(
                _some_weight_left([ca[0][1], ca[1][1]], zbs[0]), _some_weight_left([cb[0][1], cb[1][1]], zbs[1])))

        def more(st):
            jj, ca, cb = st
            ka = 2 * i - 2 - jj
            ca = block(jnp.maximum(ka, 0), 0, ca, jnp.logical_and(ka >= 0, everywhere))
            cb = block(ka + 1, 1, cb)
            return jj + 1, ca, cb

        _, ca, cb = lax.while_loop(live, more, (jnp.int32(0), carries[0], carries[1]))
        dq_ref[...] = (jnp.concatenate([jnp.where(lo_mask, c2[0][0], c2[1][0]) for c2 in (ca, cb)], axis=0)
                       * scale).astype(BF16)

        @pl.when(i == nq - 1)
        def _():
            def turn(j, carry_):
                st = pl.multiple_of(j * tb, tb)
                dk_out[pl.ds(st, tb), :] = dk_ref[j].T.astype(BF16)
                dv_out[pl.ds(st, tb), :] = dv_ref[j].T.astype(BF16)
                return carry_

            lax.fori_loop(0, nb, turn, 0)

    return pl.pallas_call(
        body, grid=(P, nq),
        in_specs=[pl.BlockSpec((tq, LANES), lambda p, i: (i, p)),
                  pl.BlockSpec((T, LANES), lambda p, i: (0, P + p)),
                  pl.BlockSpec((T, LANES), lambda p, i: (0, 2 * P + p)),
                  pl.BlockSpec((tq, LANES), lambda p, i: (i, p)),
                  pl.BlockSpec((tq, LANES), lambda p, i: (i, p)),
                  pl.BlockSpec((1, LANES), lambda p, i: (0, p))],
        out_specs=[pl.BlockSpec((tq, LANES), lambda p, i: (i, p)),
                   pl.BlockSpec((T, LANES), lambda p, i: (0, p)),
                   pl.BlockSpec((T, LANES), lambda p, i: (0, p)),
                   pl.BlockSpec((1, LANES), lambda p, i: (0, p))],
        out_shape=[jax.ShapeDtypeStruct((T, AW), BF16)] * 3 + [jax.ShapeDtypeStruct((1, AW), F32)],
        scratch_shapes=[pltpu.VMEM((8, LANES), F32), pltpu.VMEM((nb, LANES, tb), F32),
                        pltpu.VMEM((nb, LANES, tb), F32)],
        compiler_params=_cparams(("parallel", "arbitrary")), name=name)(qkv, qkv, qkv, o, da, g_attn)


def _ada_fwd(c_all, w_ada, b_ada, *, name):
    def body(c_ref, w_ref, b_ref, o_ref):
        cv = c_ref[...]
        sc = cv * _sigmoid(cv)
        o_ref[...] = jnp.dot(sc, w_ref[...], preferred_element_type=F32,
                             precision=lax.Precision.HIGHEST) + b_ref[...]

    return pl.pallas_call(body, out_shape=jax.ShapeDtypeStruct((c_all.shape[0], w_ada.shape[1]), F32),
                          compiler_params=_cparams(), name=name)(c_all, w_ada, b_ada)


def _ada_bwd(c_all_t, dmod, *, name):
    def body(c_ref, d_ref, o_ref):
        cv = c_ref[...]
        sc = cv * _sigmoid(cv)
        o_ref[...] = jnp.dot(sc, d_ref[...], preferred_element_type=F32, precision=lax.Precision.HIGHEST)

    return pl.pallas_call(body, out_shape=jax.ShapeDtypeStruct((c_all_t.shape[0], dmod.shape[1]), F32),
                          compiler_params=_cparams(), name=name)(c_all_t, dmod)


def _adamw_update(w_ref, g_ref, m_ref, v_ref, d_ref, nm_ref, nv_ref):
    gv = g_ref[...]
    m2 = ADAM_B1 * m_ref[...] + (1.0 - ADAM_B1) * gv
    v2 = ADAM_B2 * v_ref[...] + (1.0 - ADAM_B2) * jnp.square(gv)
    m_hat = m2 / (1.0 - ADAM_B1 ** ADAM_STEP)
    v_hat = v2 / (1.0 - ADAM_B2 ** ADAM_STEP)
    d_ref[...] = -ADAM_LR * (m_hat / (jnp.sqrt(v_hat) + ADAM_EPS) + ADAM_WD * w_ref[...])
    nm_ref[...] = m2
    nv_ref[...] = v2


def _adamw_many(wgmv, *, name):
    n = len(wgmv[0])

    def body(*refs):
        ins, outs = refs[:4 * n], refs[4 * n:]
        for k in range(n):
            _adamw_update(ins[k], ins[n + k], ins[2 * n + k], ins[3 * n + k], *outs[3 * k:3 * k + 3])

    return pl.pallas_call(
        body, out_shape=[jax.ShapeDtypeStruct(w.shape, F32) for w in wgmv[0] for _ in range(3)],
        compiler_params=_cparams(), name=name)(*wgmv[0], *wgmv[1], *wgmv[2], *wgmv[3])


def _adamw(w, g, m, v, *, name):
    R, C = w.shape
    tr = _tile(R, max(8, (1 << 18) // C), 8)
    body = functools.partial(_adamw_update)

    spec = pl.BlockSpec((tr, C), lambda i: (i, 0))
    return pl.pallas_call(
        body, grid=(R // tr,), in_specs=[spec] * 4, out_specs=[spec] * 3,
        out_shape=[jax.ShapeDtypeStruct((R, C), F32)] * 3,
        compiler_params=_cparams(("parallel",)), name=name)(w, g, m, v)


def _sum_devices(a, *, name):
    def body(a_ref, o_ref):
        s = a_ref[0]
        for d in range(1, a_ref.shape[0]):
            s = s + a_ref[d]
        o_ref[...] = s

    return pl.pallas_call(body, out_shape=jax.ShapeDtypeStruct(a.shape[1:], F32),
                          compiler_params=_cparams(), name=name)(a)


def _place():
    return lax.axis_index("x"), lax.axis_index("y"), lax.axis_index("c")


def _flip(v, bit):
    return 1 - v if bit else v


def _allgather8(blk, *, name):
    R, C = blk.shape

    def body(x_ref, out_ref, send_sems, recv_sems):
        x, y, c = _place()
        me = 4 * x + 2 * y + c
        out_ref[me] = x_ref[...]
        copies = []
        for k in range(1, 8):
            peer = (_flip(x, (k >> 2) & 1), _flip(y, (k >> 1) & 1), _flip(c, k & 1))
            cp = pltpu.make_async_remote_copy(
                src_ref=x_ref, dst_ref=out_ref.at[me], send_sem=send_sems.at[k - 1],
                recv_sem=recv_sems.at[k - 1], device_id=peer, device_id_type=MESH)
            cp.start()
            copies.append(cp)
        for cp in copies:
            cp.wait()

    return pl.pallas_call(
        body, out_shape=jax.ShapeDtypeStruct((8, R, C), F32),
        in_specs=[pl.BlockSpec(memory_space=pltpu.VMEM)], out_specs=pl.BlockSpec(memory_space=pltpu.VMEM),
        scratch_shapes=[pltpu.SemaphoreType.DMA((7,)), pltpu.SemaphoreType.DMA((7,))],
        compiler_params=_cparams(), name=name)(blk)


def _aligned(v, m):
    return v if isinstance(v, int) else pl.multiple_of(v, m)


def _rows_half(ref, half):
    n = ref.shape[0] // 2
    return ref.at[pl.ds(_aligned(half * n, 16), n)]


def _region(ref, kind, slot, half):
    if kind == 'col':
        n, cs = ref.shape[0] // 2, ref.shape[1] // 4
        return ref.at[pl.ds(_aligned(half * n, 16), n), pl.ds(_aligned(slot * cs, LANES), cs)]
    rs = ref.shape[0] // 4
    return ref.at[pl.ds(_aligned(slot * rs + half * (rs // 2), 16), rs // 2)]


def _other_chips(x, y):
    return [(1 - x, y), (x, 1 - y), (1 - x, 1 - y)]


class _Comm:
    def __init__(self, ins, outs, sems, start, finish):
        self.ins, self.outs, self.sems, self.start, self.finish = list(ins), list(outs), list(sems), start, finish


def _comm_specs(comm):
    if comm is None:
        return [], [], [], []
    anyspec = pl.BlockSpec(memory_space=pl.ANY)
    return [anyspec] * len(comm.ins), [anyspec] * len(comm.outs), list(comm.outs), list(comm.sems)


def _comm_hooks(comm, first, last, refs):
    if comm is None:
        return (lambda: None), (lambda: None)

    def at_entry():
        pl.when(first)(lambda: comm.start(*refs))

    def at_exit():
        pl.when(last)(lambda: comm.finish(*refs))

    return at_entry, at_exit


def _comm_alone(comm, *, name):
    ni, no = len(comm.ins), len(comm.outs)

    def body(*refs):
        parts = (refs[:ni], refs[ni:ni + no], refs[ni + no:])
        comm.start(*parts)
        comm.finish(*parts)

    i_specs, o_specs, o_shapes, sems = _comm_specs(comm)
    return pl.pallas_call(body, out_shape=o_shapes, in_specs=i_specs, out_specs=o_specs, scratch_shapes=sems,
                          compiler_params=_cparams(), name=name)(*comm.ins)


def _gather_comm(shards, kinds):
    nw = len(shards)
    full_shapes = []
    for s, kind in zip(shards, kinds):
        full_shapes.append((s.shape[0], 4 * s.shape[1]) if kind == 'col' else (4 * s.shape[0], s.shape[1]))

    def copies(sh, full, sems):
        lsem, ssem, rsem, fssem, frsem = sems
        x, y, c = _place()
        me_slot = 2 * x + y
        chips = _other_chips(x, y)
        local, ici, landed, fwd, passed = [], [], [], [], []
        for w in range(nw):
            for h in (0, 1):
                local.append(pltpu.make_async_copy(_rows_half(sh[w], h), _region(full[w], kinds[w], me_slot, h),
                                                   lsem.at[w, h]))
            for r, (px, py) in enumerate(chips):
                ici.append(pltpu.make_async_remote_copy(
                    src_ref=_rows_half(sh[w], c), dst_ref=_region(full[w], kinds[w], me_slot, c),
                    send_sem=ssem.at[w, r], recv_sem=rsem.at[w, r], device_id=(px, py, c), device_id_type=MESH))
                mine = _region(full[w], kinds[w], 2 * px + py, c)
                landed.append(pltpu.make_async_remote_copy(
                    src_ref=mine, dst_ref=mine, send_sem=ssem.at[w, r], recv_sem=rsem.at[w, r],
                    device_id=(px, py, c), device_id_type=MESH))
                fwd.append(pltpu.make_async_remote_copy(
                    src_ref=mine, dst_ref=mine, send_sem=fssem.at[w, r], recv_sem=frsem.at[w, r],
                    device_id=(x, y, 1 - c), device_id_type=MESH))
                theirs = _region(full[w], kinds[w], 2 * px + py, 1 - c)
                passed.append(pltpu.make_async_remote_copy(
                    src_ref=theirs, dst_ref=theirs, send_sem=fssem.at[w, r], recv_sem=frsem.at[w, r],
                    device_id=(x, y, 1 - c), device_id_type=MESH))
        return local, ici, landed, fwd, passed

    def start(sh, full, sems):
        local, ici, _, _, _ = copies(sh, full, sems)
        for cp in local + ici:
            cp.start()

    def finish(sh, full, sems):
        local, ici, landed, fwd, passed = copies(sh, full, sems)
        for got, cp in zip(landed, fwd):
            got.wait_recv()
            cp.start()
        for got in passed:
            got.wait_recv()
        for cp in ici + fwd:
            cp.wait_send()
        for cp in local:
            cp.wait()

    return _Comm(shards, [jax.ShapeDtypeStruct(s, BF16) for s in full_shapes],
                 [pltpu.SemaphoreType.DMA((nw, 2))] + [pltpu.SemaphoreType.DMA((nw, 3))] * 4, start, finish)


def _exchange_comm(grads, kinds):
    nw = len(grads)

    def copies(g, r1, sems):
        ssem, rsem = sems
        x, y, c = _place()
        out, back = [], []
        for w in range(nw):
            for slot in range(4):
                out.append(pltpu.make_async_remote_copy(
                    src_ref=_region(g[w], kinds[w], slot, 1 - c), dst_ref=_region(r1[w], kinds[w], slot, 1 - c),
                    send_sem=ssem.at[w, slot], recv_sem=rsem.at[w, slot], device_id=(x, y, 1 - c),
                    device_id_type=MESH))
                mine = _region(r1[w], kinds[w], slot, c)
                back.append(pltpu.make_async_remote_copy(
                    src_ref=mine, dst_ref=mine, send_sem=ssem.at[w, slot], recv_sem=rsem.at[w, slot],
                    device_id=(x, y, 1 - c), device_id_type=MESH))
        return out, back

    def start(g, r1, sems):
        for cp in copies(g, r1, sems)[0]:
            cp.start()

    def finish(g, r1, sems):
        out, back = copies(g, r1, sems)
        for got in back:
            got.wait_recv()
        for cp in out:
            cp.wait_send()

    return _Comm(grads, [jax.ShapeDtypeStruct(g.shape, F32) for g in grads],
                 [pltpu.SemaphoreType.DMA((nw, 4))] * 2, start, finish)


def _add_core_halves(g, r1, place, kind, *, name):
    if kind == 'col':
        n, cs = g.shape[0] // 2, g.shape[1] // 4
        tr = _tile(n, 256, 16)
        nt = n // tr
        ispec = pl.BlockSpec((tr, cs), lambda s, t, pr: (pr[0] * nt + t, s))
    else:
        rs, cs = g.shape[0] // 4, g.shape[1]
        n = rs // 2
        tr, nt = n, 1
        ispec = pl.BlockSpec((tr, cs), lambda s, t, pr: (s * 2 + pr[0], 0))

    def body(pr, a_ref, b_ref, o_ref):
        o_ref[...] = (a_ref[...] + b_ref[...]).astype(BF16)

    return pl.pallas_call(
        body,
        grid_spec=pltpu.PrefetchScalarGridSpec(
            num_scalar_prefetch=1, grid=(4, nt), in_specs=[ispec, ispec],
            out_specs=pl.BlockSpec((None, tr, cs), lambda s, t, pr: (s, t, 0))),
        out_shape=jax.ShapeDtypeStruct((4, n, cs), BF16),
        compiler_params=_cparams(("parallel", "parallel")), name=name)(place, g, r1)


def _scatter_comm(hs):
    nw = len(hs)

    def copies(h, r2, sems):
        ssem, rsem = sems
        x, y, c = _place()
        return [pltpu.make_async_remote_copy(
            src_ref=h[w].at[2 * px + py], dst_ref=r2[w].at[r], send_sem=ssem.at[w, r],
            recv_sem=rsem.at[w, r], device_id=(px, py, c), device_id_type=MESH)
            for w in range(nw) for r, (px, py) in enumerate(_other_chips(x, y))]

    def start(h, r2, sems):
        for cp in copies(h, r2, sems):
            cp.start()

    def finish(h, r2, sems):
        for cp in copies(h, r2, sems):
            cp.wait()

    return _Comm(hs, [jax.ShapeDtypeStruct((3,) + a.shape[1:], a.dtype) for a in hs],
                 [pltpu.SemaphoreType.DMA((nw, 3))] * 2, start, finish)


def _sum_owner(hs, r2, place, *, name):
    _, n, cs = hs.shape
    tr = _tile(n, 256, 16)
    nt = n // tr

    def body(pr, h_ref, r_ref, o_ref):
        o_ref[...] = ((h_ref[...].astype(F32) + r_ref[0].astype(F32)) + r_ref[1].astype(F32)) + r_ref[2].astype(F32)

    return pl.pallas_call(
        body,
        grid_spec=pltpu.PrefetchScalarGridSpec(
            num_scalar_prefetch=1, grid=(nt,),
            in_specs=[pl.BlockSpec((None, tr, cs), lambda t, pr: (pr[1], t, 0)),
                      pl.BlockSpec((3, tr, cs), lambda t, pr: (0, t, 0))],
            out_specs=pl.BlockSpec((None, tr, cs), lambda t, pr: (pr[0], t, 0))),
        out_shape=jax.ShapeDtypeStruct((2, n, cs), F32),
        compiler_params=_cparams(("parallel",)), name=name)(place, hs, r2)


def _share_with_sibling(fins, *, name):
    nw = len(fins)

    def body(*refs):
        fin, out = refs[:nw], refs[nw:2 * nw]
        ssem, rsem = refs[2 * nw:]
        x, y, c = _place()
        copies = []
        for w in range(nw):
            cp = pltpu.make_async_remote_copy(
                src_ref=fin[w].at[c], dst_ref=out[w].at[c], send_sem=ssem.at[w], recv_sem=rsem.at[w],
                device_id=(x, y, 1 - c), device_id_type=MESH)
            cp.start()
            copies.append(cp)
        for w in range(nw):
            theirs = out[w].at[1 - c]
            pltpu.make_async_remote_copy(
                src_ref=theirs, dst_ref=theirs, send_sem=ssem.at[w], recv_sem=rsem.at[w],
                device_id=(x, y, 1 - c), device_id_type=MESH).wait_recv()
        for cp in copies:
            cp.wait_send()

    anyspec = pl.BlockSpec(memory_space=pl.ANY)
    return pl.pallas_call(
        body, out_shape=[jax.ShapeDtypeStruct(a.shape, F32) for a in fins],
        in_specs=[anyspec] * nw, out_specs=[anyspec] * nw,
        input_output_aliases={w: w for w in range(nw)},
        scratch_shapes=[pltpu.SemaphoreType.DMA((nw,))] * 2,
        compiler_params=_cparams(), name=name)(*fins)


class _Plan:
    def __init__(self, wfull):
        self.w = dict(wfull)
        self.grads = {}

    def ffn_width(self):
        return self.w['ff1_w_out'].shape[0]

    def comm(self, site):
        return None

    def done(self, site, results):
        pass

    def ready(self, group, names, arrays):
        self.grads.update(zip(names, arrays))


def _riding(plan, site, call):
    comm = plan.comm(site)
    res = call(comm)
    if comm is None:
        return res
    *main, extra = res
    plan.done(site, extra)
    return main[0] if len(main) == 1 else tuple(main)


def _merge_comms(comms):
    if len(comms) == 1:
        return comms[0]

    def parts(refs, field):
        out, pos = [], 0
        for c in comms:
            n = len(getattr(c, field))
            out.append(refs[pos:pos + n])
            pos += n
        return out

    def run(which):
        def fn(ins, outs, sems):
            for c, i, o, s in zip(comms, parts(ins, 'ins'), parts(outs, 'outs'), parts(sems, 'sems')):
                getattr(c, which)(i, o, s)
        return fn

    return _Comm(sum((c.ins for c in comms), []), sum((c.outs for c in comms), []),
                 sum((c.sems for c in comms), []), run('start'), run('finish'))


class _DistPlan(_Plan):
    RIDES = {
        'pre_fwd_ff1': [('gather', ['ff1_w_in'])],
        'ffn_in_ff1': [('gather', ['ff1_w_out', 'w_in_mix', 'w_out_mix'])],
        'attn_fwd': [('gather', ['ff2_w_in', 'ff2_w_out'])],
        'pre_bwd_ff2': [('exchange', ['ff2_w_in', 'ff2_w_out'])],
        'conv_bwd1': [('scatter', ['ff2_w_in', 'ff2_w_out'])],
        'pre_bwd_mix': [('exchange', ['w_in_mix', 'w_out_mix'])],
        'ffn_dact_ff1': [('scatter', ['w_in_mix', 'w_out_mix'])],
        'dw_out_ff1': [('exchange', ['ff1_w_in'])],
        'dh_ff1': [('scatter', ['ff1_w_in']), ('exchange', ['ff1_w_out'])],
    }
    AFTER = [('scatter', ['ff1_w_out'])]

    def __init__(self, shards, place):
        self.shards, self.place, self.kind = shards, place, dict(BIG)
        self.w, self.grads, self.hs, self.fin = {}, {}, {}, {}

    def _make(self, kind, names):
        if kind == 'gather':
            return _gather_comm([self.shards[n] for n in names], [self.kind[n] for n in names])
        if kind == 'exchange':
            return _exchange_comm([self.grads[n] for n in names], [self.kind[n] for n in names])
        return _scatter_comm([self.hs[n] for n in names])

    def _take(self, kind, names, results):
        for n, r in zip(names, results):
            if kind == 'gather':
                self.w[n] = r
            elif kind == 'exchange':
                self.hs[n] = _add_core_halves(self.grads[n], r, self.place, self.kind[n], name=f"grad_core_add_{n}")
            else:
                self.fin[n] = _sum_owner(self.hs[n], r, self.place, name=f"grad_owner_sum_{n}")

    def ffn_width(self):
        return 4 * self.shards['ff1_w_out'].shape[0]

    def comm(self, site):
        rides = self.RIDES.get(site)
        return None if rides is None else _merge_comms([self._make(k, names) for k, names in rides])

    def done(self, site, results):
        pos = 0
        for kind, names in self.RIDES[site]:
            self._take(kind, names, results[pos:pos + len(names)])
            pos += len(names)

    def finish(self):
        for kind, names in self.AFTER:
            self._take(kind, names, _comm_alone(self._make(kind, names), name=f"grad_{kind}_{names[0]}"))
        names = list(self.shards)
        return dict(zip(names, _share_with_sibling([self.fin[n] for n in names], name="grad_share")))


def _local_step(x, target, mod, gains, plan, g_attn, conv_w, cvec):
    T, D = x.shape
    F = plan.ffn_width()
    AW = D // 2
    C = D - AW
    NQKV = 3 * AW
    MIX = NQKV + 2 * C
    tM = _tile(T, 1024)
    tkT = _tile(T, 1024)

    def ffn_fwd(h, tag):
        w_in = plan.w[f"{tag}_w_in"]
        jac, act = _riding(plan, f"ffn_in_{tag}",
                           lambda cm: _ffn_in(h, w_in, T=T, D=D, F=F, name=f"ffn_in_{tag}", comm=cm))
        f = _matmul(act, plan.w[f"{tag}_w_out"], mode='nn', M=T, N=D, K=F, tm=tM, tn=_tile(D, 1024), tk=F,
                    out_dtype=F32, name=f"ffn_out_{tag}")
        return h, jac, act, f

    def ffn_bwd(df, dout, xin, saved, s, tag, prev=None):
        h, jac, act, f = saved
        w_in, w_out = plan.w[f"{tag}_w_in"], plan.w[f"{tag}_w_out"]
        dgu = _riding(plan, f"ffn_dact_{tag}",
                      lambda cm: _ffn_dact(df, w_out, jac, T=T, D=D, F=F, name=f"ffn_dact_{tag}", comm=cm))
        tnf = _tile(F, 2816)
        nf = F // tnf
        dw_in = _matmul(h, dgu, mode='tn', M=D, N=2 * F, K=T, tm=_tile(D, 1024), tn=tnf, tk=tkT, out_dtype=F32,
                        b_spec=pl.BlockSpec((None, tkT, tnf), lambda i, j, k: (j // nf, k, j % nf)),
                        name=f"dw_in_{tag}")
        plan.ready(tag, [f"{tag}_w_in"], [dw_in])
        dw_out = _riding(plan, f"dw_out_{tag}", lambda cm: _matmul(
            act, df, mode='tn', M=F, N=D, K=T, tm=_tile(F, 1408), tn=_tile(D, 1024), tk=tkT, out_dtype=F32,
            name=f"dw_out_{tag}", comm=cm))
        plan.ready(tag, [f"{tag}_w_out"], [dw_out])
        dh = _riding(plan, f"dh_{tag}", lambda cm: _matmul(
            dgu, w_in, mode='nt', M=T, N=D, K=2 * F, tm=tM, tn=_tile(D, 1024), tk=F, out_dtype=F32,
            a_spec=pl.BlockSpec((None, tM, F), lambda i, j, k: (k, i, 0)), name=f"dh_{tag}", comm=cm))
        if prev is None:
            return _riding(plan, f"pre_bwd_{tag}", lambda cm: _pre_bwd(
                dh, xin, dout, gains, mod, T=T, s=s, name=f"pre_bwd_{tag}", comm=cm))
        return _riding(plan, f"pre_bwd_{tag}", lambda cm: _pre_post_bwd(
            dh, xin, dout, prev[0], gains, mod, T=T, s=s, res_w_prev=prev[1], name=f"pre_bwd_{tag}", comm=cm))

    h1 = _riding(plan, "pre_fwd_ff1", lambda cm: _pre_fwd(x, gains, mod, T=T, s=0, name="pre_fwd_ff1", comm=cm))[0]
    s1 = ffn_fwd(h1, "ff1")
    x1, h2 = _post_pre_fwd(x, s1[3], gains, mod, T=T, s=0, res_w=0.5, name="post_fwd_ff1")
    w_in_mix, w_out_mix = plan.w['w_in_mix'], plan.w['w_out_mix']
    tnq = _tile(AW, 512)
    qkv = _matmul(h2, w_in_mix, mode='nn', M=T, N=NQKV, K=D, tm=tM, tn=tnq, tk=D, out_dtype=BF16, name="proj_qkv")
    tnc = _tile(C, 512)
    off = NQKV // tnc
    cvg = _matmul(h2, w_in_mix, mode='nn', M=T, N=2 * C, K=D, tm=tM, tn=tnc, tk=D, out_dtype=F32,
                  b_spec=pl.BlockSpec((D, tnc), lambda i, j, k: (0, off + j)), name="proj_conv")
    o_attn, a_attn = _riding(plan, "attn_fwd", lambda cm: _attn_fwd(
        qkv, g_attn, T=T, AW=AW, a_cols=D, name="attn_fwd", comm=cm))
    mixcat = _conv_fwd(cvg, conv_w, cvec, T=T, C=C, name="conv_fwd", into=(a_attn, AW // C))
    f_mix = _matmul(mixcat, w_out_mix, mode='nn', M=T, N=D, K=D, tm=tM, tn=_tile(D, 1024), tk=D, out_dtype=F32,
                    name="mix_out")
    x2, h3 = _post_pre_fwd(x1, f_mix, gains, mod, T=T, s=1, res_w=1.0, name="post_fwd_mix")

    s3 = ffn_fwd(h3, "ff2")
    dout, sq = _post_fwd_loss(x2, s3[3], target, gains, mod, T=T, s=2, res_w=0.5, name="post_fwd_loss")

    df2, dgate2, dgpost2 = _post_bwd(dout, s3[3], gains, mod, T=T, s=2, res_w=0.5, name="post_bwd_ff2")
    dx2, df_mix, dshift2, dscale2, dgpre2, dgate_m, dgpost_m = ffn_bwd(
        df2, dout, x2, s3, 2, "ff2", prev=(f_mix, 1.0))
    dmixcat = _matmul(df_mix, w_out_mix, mode='nt', M=T, N=D, K=D, tm=tM, tn=_tile(D, 1024), tk=D, out_dtype=F32,
                      name="d_mixcat")
    dw_out_mix = _matmul(mixcat, df_mix, mode='tn', M=D, N=D, K=T, tm=_tile(D, 1024), tn=_tile(D, 1024),
                         tk=tkT, out_dtype=F32, name="dw_out_mix")
    dq, dk, dv, dg_attn = _attn_bwd(qkv, o_attn, dmixcat, g_attn, T=T, AW=AW, name="attn_bwd")
    dyc, csum, dconv_w = _riding(plan, "conv_bwd1", lambda cm: _conv_bwd1(
        cvg, (dmixcat, C, AW // C), conv_w, cvec, T=T, C=C, name="conv_bwd1", comm=cm))
    dcv, dcg = _conv_bwd2(dyc, cvg, conv_w, T=T, C=C, name="conv_bwd2")
    dproj = jnp.concatenate([dq, dk, dv, dcv, dcg], axis=1)
    dh2 = _matmul(dproj, w_in_mix, mode='nt', M=T, N=D, K=MIX, tm=tM, tn=_tile(D, 1024), tk=MIX, out_dtype=F32,
                  name="dh_mix")
    dw_in_mix = _matmul(h2, dproj, mode='tn', M=D, N=MIX, K=T, tm=_tile(D, 1024), tn=_tile(MIX, 1280),
                        tk=tkT, out_dtype=F32, name="dw_in_mix")
    plan.ready("mix", ['w_in_mix', 'w_out_mix'], [dw_in_mix, dw_out_mix])
    dx1, df1, dshift_m, dscale_m, dgpre_m, dgate1, dgpost1 = _riding(plan, "pre_bwd_mix", lambda cm: _pre_post_bwd(
        dh2, x1, dx2, s1[3], gains, mod, T=T, s=1, res_w_prev=0.5, name="pre_bwd_mix", comm=cm))

    dx0, dshift1, dscale1, dgpre1 = ffn_bwd(df1, dx1, x, s1, 0, "ff1")

    dgains = [dgpre1, dgpost1, dgpre_m, dgpost_m, dgpre2, dgpost2]
    dmod = [dshift1, dscale1, dgate1, dshift_m, dscale_m, dgate_m, dshift2, dscale2, dgate2]
    return sq, dx0, dgains, dmod, dg_attn, csum, dconv_w


def _pack_rows(pieces, width):
    rows = jnp.concatenate([p.reshape(-1) for p in pieces]).reshape(-1, width)
    pad = (-rows.shape[0]) % 8
    return jnp.pad(rows, ((0, pad), (0, 0)))


def kernel(x, c, w_ada, b_ada, g_pre_ff1, g_post_ff1, ff1_w_in, ff1_w_out, g_pre_mix, g_post_mix, w_in_mix, g_attn_out, conv_w, conv_b, conv_ln_g, conv_ln_b, w_out_mix, g_pre_ff2, g_post_ff2, ff2_w_in, ff2_w_out, loss_target, m_w_ada, m_b_ada, m_g_pre_ff1, m_g_post_ff1, m_ff1_w_in, m_ff1_w_out, m_g_pre_mix, m_g_post_mix, m_w_in_mix, m_g_attn_out, m_conv_w, m_conv_b, m_conv_ln_g, m_conv_ln_b, m_w_out_mix, m_g_pre_ff2, m_g_post_ff2, m_ff2_w_in, m_ff2_w_out, v_w_ada, v_b_ada, v_g_pre_ff1, v_g_post_ff1, v_ff1_w_in, v_ff1_w_out, v_g_pre_mix, v_g_post_mix, v_w_in_mix, v_g_attn_out, v_conv_w, v_conv_b, v_conv_ln_g, v_conv_ln_b, v_w_out_mix, v_g_pre_ff2, v_g_post_ff2, v_ff2_w_in, v_ff2_w_out):
    W = dict(w_ada=w_ada, b_ada=b_ada, g_pre_ff1=g_pre_ff1, g_post_ff1=g_post_ff1, ff1_w_in=ff1_w_in,
             ff1_w_out=ff1_w_out, g_pre_mix=g_pre_mix, g_post_mix=g_post_mix, w_in_mix=w_in_mix,
             g_attn_out=g_attn_out, conv_w=conv_w, conv_b=conv_b, conv_ln_g=conv_ln_g, conv_ln_b=conv_ln_b,
             w_out_mix=w_out_mix, g_pre_ff2=g_pre_ff2, g_post_ff2=g_post_ff2, ff2_w_in=ff2_w_in,
             ff2_w_out=ff2_w_out)
    Mo = dict(w_ada=m_w_ada, b_ada=m_b_ada, g_pre_ff1=m_g_pre_ff1, g_post_ff1=m_g_post_ff1, ff1_w_in=m_ff1_w_in,
              ff1_w_out=m_ff1_w_out, g_pre_mix=m_g_pre_mix, g_post_mix=m_g_post_mix, w_in_mix=m_w_in_mix,
              g_attn_out=m_g_attn_out, conv_w=m_conv_w, conv_b=m_conv_b, conv_ln_g=m_conv_ln_g,
              conv_ln_b=m_conv_ln_b, w_out_mix=m_w_out_mix, g_pre_ff2=m_g_pre_ff2, g_post_ff2=m_g_post_ff2,
              ff2_w_in=m_ff2_w_in, ff2_w_out=m_ff2_w_out)
    Vo = dict(w_ada=v_w_ada, b_ada=v_b_ada, g_pre_ff1=v_g_pre_ff1, g_post_ff1=v_g_post_ff1, ff1_w_in=v_ff1_w_in,
              ff1_w_out=v_ff1_w_out, g_pre_mix=v_g_pre_mix, g_post_mix=v_g_post_mix, w_in_mix=v_w_in_mix,
              g_attn_out=v_g_attn_out, conv_w=v_conv_w, conv_b=v_conv_b, conv_ln_g=v_conv_ln_g,
              conv_ln_b=v_conv_ln_b, w_out_mix=v_w_out_mix, g_pre_ff2=v_g_pre_ff2, g_post_ff2=v_g_post_ff2,
              ff2_w_in=v_ff2_w_in, ff2_w_out=v_ff2_w_out)

    T, D = x.shape[1], x.shape[2]
    AW = D // 2
    C = D - AW
    xi, yi, ci = _place()
    me = 4 * xi + 2 * yi + ci
    chip = 2 * xi + yi
    place = jnp.stack([ci, chip]).astype(jnp.int32)

    c_all = _allgather8(jnp.tile(c, (8, 1)), name="gather_c")[:, 0, :]
    ncol = w_ada.shape[1]
    b_cols = lax.dynamic_index_in_dim(b_ada.reshape(4, ncol), chip, keepdims=True).reshape(1, ncol)
    modp = _ada_fwd(c_all, w_ada, b_cols, name="ada_fwd")
    mod_g = _allgather8(modp, name="gather_mod")
    mod_all = jnp.transpose(mod_g[0::2], (1, 0, 2)).reshape(8, 4 * ncol)
    mod = lax.dynamic_index_in_dim(mod_all, me, keepdims=False).reshape(9, D)

    names = [n for n, _ in BIG]
    plan = _DistPlan({n: W[n].astype(BF16) for n in names}, place)
    cs = conv_w.shape[1]
    cw_all = _allgather8(jnp.pad(conv_w, ((0, HALO - CONV_KERNEL), (0, (-cs) % LANES))), name="gather_conv_w")
    conv_w_full = jnp.transpose(cw_all[0::2, :, :cs], (1, 0, 2)).reshape(HALO, 4 * cs)

    gains = _pack_rows([g_pre_ff1, g_post_ff1, g_pre_mix, g_post_mix, g_pre_ff2, g_post_ff2], D)
    cvec = _pack_rows([conv_b, conv_ln_g, conv_ln_b], C)
    g_attn = g_attn_out.reshape(1, AW)

    sq, dx, dgains, dmod, dg_attn, csum, dconv_w = _local_step(
        x[0], loss_target[0], mod, gains, plan, g_attn, conv_w_full, cvec)

    loss_row = jnp.zeros((1, D), F32).at[0, 0].set(jnp.sum(sq) * (0.5 / D))
    small = _pack_rows(dgains + dmod + [dg_attn, csum[0:3], dconv_w, loss_row], D)
    small_all = _allgather8(small, name="gather_small")
    tot = _sum_devices(small_all, name="sum_small")
    n_g, n_m = 6, 9
    r0 = n_g + n_m
    flat = tot.reshape(-1)
    p = r0 * D
    g_attn_grad = flat[p:p + AW]
    p += AW
    gconv_b, gln_g, gln_b = flat[p:p + C], flat[p + C:p + 2 * C], flat[p + 2 * C:p + 3 * C]
    p += 3 * C
    gconv_w_full = flat[p:p + HALO * C].reshape(HALO, C)[:CONV_KERNEL]
    p += HALO * C
    loss = flat[p]
    gconv_w = lax.dynamic_slice_in_dim(gconv_w_full, chip * cs, cs, axis=1)
    grad_small = {'g_pre_ff1': tot[0], 'g_post_ff1': tot[1], 'g_pre_mix': tot[2], 'g_post_mix': tot[3],
                  'g_pre_ff2': tot[4], 'g_post_ff2': tot[5], 'b_ada': tot[n_g:r0].reshape(-1),
                  'g_attn_out': g_attn_grad.reshape(g_attn_out.shape), 'conv_w': gconv_w, 'conv_b': gconv_b,
                  'conv_ln_g': gln_g, 'conv_ln_b': gln_b}

    dmod_all = small_all[:, n_g:r0, :].reshape(8, 9 * D)
    dmod_cols = lax.dynamic_slice_in_dim(dmod_all, chip * ncol, ncol, axis=1)
    grad_w_ada = _ada_bwd(jnp.transpose(c_all), dmod_cols, name="ada_bwd")

    grads = dict(grad_small)
    grads['w_ada'] = grad_w_ada
    for n, a in plan.finish().items():
        grads[n] = a.reshape(W[n].shape)

    delta, new_m, new_v = {}, {}, {}
    for n in ['w_ada'] + names:
        delta[n], new_m[n], new_v[n] = _adamw(W[n], grads[n], Mo[n], Vo[n], name=f"adamw_{n}")
    smalls = [n for n in WEIGHTS if n not in delta]

    def as2d(a):
        return a if a.ndim == 2 else a.reshape(-1, LANES)

    outs = _adamw_many([[as2d(d[n]) for n in smalls] for d in (W, grads, Mo, Vo)], name="adamw_small")
    for k, n in enumerate(smalls):
        delta[n], new_m[n], new_v[n] = (o.reshape(W[n].shape) for o in outs[3 * k:3 * k + 3])

    return (loss, dx[None], *[grads[n] for n in WEIGHTS], *[delta[n] for n in WEIGHTS],
            *[new_m[n] for n in WEIGHTS], *[new_v[n] for n in WEIGHTS])
```

```python
import functools

import jax
import jax.numpy as jnp
from jax import lax
from jax.experimental import pallas as pl
from jax.experimental.pallas import tpu as pltpu

F32 = jnp.float32
BF16 = jnp.bfloat16
MESH = pl.DeviceIdType.MESH

HEAD_DIM = 64
CONV_KERNEL = 31
RMS_EPS = 1e-6
LN_EPS = 1e-5
ADAM_LR = 0.001
ADAM_B1 = 0.9
ADAM_B2 = 0.999
ADAM_EPS = 1e-08
ADAM_WD = 0.01
ADAM_STEP = 10

LANES = 128
HALO = 32
VMEM_LIMIT = 52 * 1024 * 1024

WEIGHTS = ['w_ada', 'b_ada', 'g_pre_ff1', 'g_post_ff1', 'ff1_w_in', 'ff1_w_out', 'g_pre_mix',
           'g_post_mix', 'w_in_mix', 'g_attn_out', 'conv_w', 'conv_b', 'conv_ln_g', 'conv_ln_b',
           'w_out_mix', 'g_pre_ff2', 'g_post_ff2', 'ff2_w_in', 'ff2_w_out']
BIG = [('ff1_w_in', 'col'), ('ff1_w_out', 'row'), ('w_in_mix', 'col'), ('w_out_mix', 'row'),
       ('ff2_w_in', 'col'), ('ff2_w_out', 'row')]


def _tile(dim, pref, mult=LANES):
    if dim <= pref:
        return dim
    best = None
    for t in range(mult, pref + 1, mult):
        if dim % t == 0:
            best = t
    assert best is not None, (dim, pref, mult)
    return best


def _cparams(sem=None):
    kw = dict(vmem_limit_bytes=VMEM_LIMIT)
    if sem is not None:
        kw['dimension_semantics'] = sem
    return pltpu.CompilerParams(**kw)


def _sigmoid(x):
    return 1.0 / (1.0 + jnp.exp(-x))


_DIMS = {'nn': (((1,), (0,)), ((), ())), 'nt': (((1,), (1,)), ((), ())), 'tn': (((0,), (0,)), ((), ()))}


def _matmul(a, b, *, mode, M, N, K, tm, tn, tk, out_dtype, name, a_spec=None, b_spec=None, comm=None):
    nm, nn, nk = M // tm, N // tn, K // tk
    assert nm * tm == M and nn * tn == N and nk * tk == K, (name, M, N, K, tm, tn, tk)
    a_list = list(a) if isinstance(a, (list, tuple)) else [a]
    b_list = list(b) if isinstance(b, (list, tuple)) else [b]
    if len(a_list) > 1:
        assert mode == 'nt' and tk == K and a_spec is None, name
        a_specs = [pl.BlockSpec((tm, p.shape[1]), lambda i, j, k: (i, 0)) for p in a_list]
    elif a_spec is None:
        a_specs = [pl.BlockSpec((tk, tm), lambda i, j, k: (k, i)) if mode == 'tn'
                   else pl.BlockSpec((tm, tk), lambda i, j, k: (i, k))]
    else:
        a_specs = [a_spec]
    if len(b_list) > 1:
        assert mode == 'tn' and tn == N and b_spec is None, name
        b_specs = [pl.BlockSpec((tk, p.shape[1]), lambda i, j, k: (k, 0)) for p in b_list]
    elif b_spec is None:
        b_specs = [pl.BlockSpec((tn, tk), lambda i, j, k: (j, k)) if mode == 'nt'
                   else pl.BlockSpec((tk, tn), lambda i, j, k: (k, j))]
    else:
        b_specs = [b_spec]
    na, nbb = len(a_list), len(b_list)
    dims = _DIMS[mode]
    assert nk == 1 or out_dtype == F32, name
    ci_specs, co_specs, co_shapes, csems = _comm_specs(comm)
    nci, nco = len(ci_specs), len(co_specs)

    def side_by_side(refs):
        return refs[0][...] if len(refs) == 1 else jnp.concatenate([r[...] for r in refs], axis=1)

    def body(*refs):
        a_refs, b_refs, rest = refs[:na], refs[na:na + nbb], refs[na + nbb:]
        o_ref = rest[nci]
        i, j, k = pl.program_id(0), pl.program_id(1), pl.program_id(2)
        first = jnp.logical_and(jnp.logical_and(i == 0, j == 0), k == 0)
        last = jnp.logical_and(jnp.logical_and(i == nm - 1, j == nn - 1), k == nk - 1)
        at_entry, at_exit = _comm_hooks(comm, first, last, (rest[:nci], rest[nci + 1:nci + 1 + nco], rest[nci + 1 + nco:]))
        at_entry()

        def prod():
            return lax.dot_general(side_by_side(a_refs), side_by_side(b_refs), dims, preferred_element_type=F32)

        if nk == 1:
            o_ref[...] = prod().astype(o_ref.dtype)
        else:
            @pl.when(k == 0)
            def _():
                o_ref[...] = prod()

            @pl.when(k > 0)
            def _():
                o_ref[...] += prod()
        at_exit()

    sem = ("parallel", "parallel", "arbitrary") if comm is None else ("arbitrary",) * 3
    res = pl.pallas_call(
        body, grid=(nm, nn, nk), in_specs=a_specs + b_specs + ci_specs,
        out_specs=[pl.BlockSpec((tm, tn), lambda i, j, k: (i, j))] + co_specs,
        out_shape=[jax.ShapeDtypeStruct((M, N), out_dtype)] + co_shapes, scratch_shapes=csems,
        compiler_params=_cparams(sem), name=name)(*a_list, *b_list, *([] if comm is None else comm.ins))
    return res[0] if comm is None else (res[0], res[1:])


def _grid2_hooks(comm, n0, n1, refs):
    j, i = pl.program_id(0), pl.program_id(1)
    return _comm_hooks(comm, jnp.logical_and(j == 0, i == 0), jnp.logical_and(j == n0 - 1, i == n1 - 1), refs)


def _ffn_in(h, w_in, *, T, D, F, name, comm=None):
    tm, tn = _tile(T, 256), _tile(F, 2816)
    nf, nt = F // tn, T // tm
    ci_specs, co_specs, co_shapes, csems = _comm_specs(comm)
    nci, nco = len(ci_specs), len(co_specs)

    def body(h_ref, wg_ref, wu_ref, *rest):
        jac_ref, a_ref = rest[nci], rest[nci + 1]
        at_entry, at_exit = _grid2_hooks(comm, nf, nt, (rest[:nci], rest[nci + 2:nci + 2 + nco], rest[nci + 2 + nco:]))
        at_entry()
        hh = h_ref[...]
        g = jnp.dot(hh, wg_ref[...], preferred_element_type=F32)
        u = jnp.dot(hh, wu_ref[...], preferred_element_type=F32)
        s = _sigmoid(g)
        sg = g * s
        jac_ref[0] = (u * (s * (1.0 + g * (1.0 - s)))).astype(BF16)
        jac_ref[1] = sg.astype(BF16)
        a_ref[...] = (sg * u).astype(BF16)
        at_exit()

    res = pl.pallas_call(
        body, grid=(nf, nt),
        in_specs=[pl.BlockSpec((tm, D), lambda j, i: (i, 0)),
                  pl.BlockSpec((D, tn), lambda j, i: (0, j)),
                  pl.BlockSpec((D, tn), lambda j, i: (0, nf + j))] + ci_specs,
        out_specs=[pl.BlockSpec((2, tm, tn), lambda j, i: (0, i, j)),
                   pl.BlockSpec((tm, tn), lambda j, i: (i, j))] + co_specs,
        out_shape=[jax.ShapeDtypeStruct((2, T, F), BF16), jax.ShapeDtypeStruct((T, F), BF16)] + co_shapes,
        scratch_shapes=csems,
        compiler_params=_cparams(("parallel", "parallel") if comm is None else ("arbitrary", "arbitrary")),
        name=name)(h, w_in, w_in, *([] if comm is None else comm.ins))
    return (res[0], res[1]) if comm is None else (res[0], res[1], res[2:])


def _ffn_dact(df, w_out, jac, *, T, D, F, name, comm=None):
    tm, tn = _tile(T, 256), _tile(F, 2816)
    nf, nt = F // tn, T // tm
    ci_specs, co_specs, co_shapes, csems = _comm_specs(comm)
    nci, nco = len(ci_specs), len(co_specs)

    def body(df_ref, w_ref, jac_ref, *rest):
        o_ref = rest[nci]
        at_entry, at_exit = _grid2_hooks(comm, nf, nt, (rest[:nci], rest[nci + 1:nci + 1 + nco], rest[nci + 1 + nco:]))
        at_entry()
        da = lax.dot_general(df_ref[...], w_ref[...], _DIMS['nt'], preferred_element_type=F32)
        o_ref[0] = (da * jac_ref[0].astype(F32)).astype(BF16)
        o_ref[1] = (da * jac_ref[1].astype(F32)).astype(BF16)
        at_exit()

    res = pl.pallas_call(
        body, grid=(nf, nt),
        in_specs=[pl.BlockSpec((tm, D), lambda j, i: (i, 0)),
                  pl.BlockSpec((tn, D), lambda j, i: (j, 0)),
                  pl.BlockSpec((2, tm, tn), lambda j, i: (0, i, j))] + ci_specs,
        out_specs=[pl.BlockSpec((2, tm, tn), lambda j, i: (0, i, j))] + co_specs,
        out_shape=[jax.ShapeDtypeStruct((2, T, F), BF16)] + co_shapes, scratch_shapes=csems,
        compiler_params=_cparams(("parallel", "parallel") if comm is None else ("arbitrary", "arbitrary")),
        name=name)(df, w_out, jac, *([] if comm is None else comm.ins))
    return res[0] if comm is None else (res[0], res[1:])


def _rowwise(fn, *, T, tm, name, tiled=(), prev=(), nxt=(), consts=(), out_tiled=(), out_acc=(), scratch=(),
             by_ref=False, comm=None, into=None):
    n = T // tm
    assert n * tm == T and tm % HALO == 0
    hb = tm // HALO
    cols = [a if isinstance(a, tuple) else (a, a.shape[1], 0) for a in tiled]
    tiled = [a for a, _, _ in cols]
    in_specs = [pl.BlockSpec((tm, w), functools.partial(lambda cb, i: (i, cb), cb)) for _, w, cb in cols]
    in_specs += [pl.BlockSpec((HALO, a.shape[1]), lambda i: (jnp.maximum(i * hb - 1, 0), 0)) for a in prev]
    in_specs += [pl.BlockSpec((HALO, a.shape[1]), lambda i: (jnp.minimum((i + 1) * hb, T // HALO - 1), 0))
                 for a in nxt]
    in_specs += [pl.BlockSpec(a.shape, lambda i: (0, 0)) for a in consts]
    out_shape = [jax.ShapeDtypeStruct((T, c), dt) for c, dt in out_tiled]
    out_shape += [jax.ShapeDtypeStruct(s, F32) for s in out_acc]
    out_specs = [pl.BlockSpec((tm, c), lambda i: (i, 0)) for c, _ in out_tiled]
    out_specs += [pl.BlockSpec(s, lambda i: (0, 0)) for s in out_acc]
    nt, npv, nnx, nc, not_, na = len(tiled), len(prev), len(nxt), len(consts), len(out_tiled), len(out_acc)
    ci_specs, co_specs, co_shapes, csems = _comm_specs(comm)
    extra_in, aliases = [], {}
    if into is not None:
        arr, cb = into
        width = out_tiled[0][0]
        out_shape[0] = jax.ShapeDtypeStruct(arr.shape, arr.dtype)
        out_specs[0] = pl.BlockSpec((tm, width), lambda i: (i, cb))
        extra_in = [arr]
        aliases = {nt + npv + nnx + nc: 0}
    n_extra = len(extra_in)

    def body(*refs):
        pos = 0
        groups = []
        for cnt in (nt, npv, nnx, nc, n_extra, len(ci_specs), not_, na, len(co_specs), len(scratch), len(csems)):
            groups.append(refs[pos:pos + cnt])
            pos += cnt
        t_r, p_r, n_r, c_r, _, ci_r, o_r, a_r, co_r, s_r, cs_r = groups
        i = pl.program_id(0)
        at_entry, at_exit = _comm_hooks(comm, i == 0, i == n - 1, (ci_r, co_r, cs_r))
        at_entry()

        @pl.when(i == 0)
        def _():
            for r in a_r:
                r[...] = jnp.zeros_like(r)

        if by_ref:
            fn(i, n, t_r, p_r, n_r, c_r, o_r, a_r, s_r)
        else:
            outs = fn(i, n, [r[...] for r in t_r], [r[...] for r in p_r], [r[...] for r in n_r],
                      [r[...] for r in c_r], a_r, s_r)
            for r, v in zip(o_r, outs):
                r[...] = v.astype(r.dtype)
        at_exit()

    res = pl.pallas_call(
        body, grid=(n,), in_specs=in_specs + [pl.BlockSpec(memory_space=pl.ANY)] * n_extra + ci_specs,
        out_specs=out_specs + co_specs, out_shape=out_shape + co_shapes, scratch_shapes=list(scratch) + csems,
        input_output_aliases=aliases, compiler_params=_cparams(("arbitrary",)), name=name,
    )(*tiled, *prev, *nxt, *consts, *extra_in, *([] if comm is None else comm.ins))
    return res if comm is None else (res[:not_ + na], res[not_ + na:])


def _colsum(v):
    return jnp.sum(v, axis=0, keepdims=True)


def _rowmean(v):
    return jnp.mean(v, axis=-1, keepdims=True)


def _pre_math(xv, g, m, s):
    g_pre, shift, scale = g[2 * s:2 * s + 1], m[3 * s:3 * s + 1], m[3 * s + 1:3 * s + 2]
    r = lax.rsqrt(_rowmean(xv * xv) + RMS_EPS)
    return ((xv * r) * g_pre) * (1.0 + scale) + shift


def _post_math(xv, fv, g, m, s, res_w):
    g_post, gate = g[2 * s + 1:2 * s + 2], m[3 * s + 2:3 * s + 3]
    y = (fv * lax.rsqrt(_rowmean(fv * fv) + RMS_EPS)) * g_post
    return xv + (res_w * (1.0 + gate)) * y


def _pre_fwd(x, gains, mod, *, T, s, name, comm=None):
    def fn(i, n, t, p, nx, c, acc, scr):
        return [_pre_math(t[0], c[0], c[1], s)]

    return _rowwise(fn, T=T, tm=_tile(T, 512, HALO), name=name, tiled=[x], consts=[gains, mod],
                    out_tiled=[(x.shape[1], BF16)], comm=comm)


def _post_pre_fwd(x, f, gains, mod, *, T, s, res_w, name):
    def fn(i, n, t, p, nx, c, acc, scr):
        out = _post_math(t[0], t[1], c[0], c[1], s, res_w)
        return [out, _pre_math(out, c[0], c[1], s + 1)]

    return _rowwise(fn, T=T, tm=_tile(T, 512, HALO), name=name, tiled=[x, f], consts=[gains, mod],
                    out_tiled=[(x.shape[1], F32), (x.shape[1], BF16)])


def _post_fwd_loss(x, f, target, gains, mod, *, T, s, res_w, name):
    D = x.shape[1]

    def fn(i, n, t, p, nx, c, acc, scr):
        (xv, fv, tv), (g, m) = t, c
        g_post, gate = g[2 * s + 1:2 * s + 2], m[3 * s + 2:3 * s + 3]
        y = (fv * lax.rsqrt(_rowmean(fv * fv) + RMS_EPS)) * g_post
        err = (xv + (res_w * (1.0 + gate)) * y) - tv
        acc[0][...] += _colsum(err * err)
        return [err * (1.0 / D)]

    dout, sq = _rowwise(fn, T=T, tm=_tile(T, 512, HALO), name=name, tiled=[x, f, target], consts=[gains, mod],
                        out_tiled=[(D, F32)], out_acc=[(1, D)])
    return dout, sq


def _post_bwd_math(dv, fv, g, m, s, res_w, acc):
    g_post, gate = g[2 * s + 1:2 * s + 2], m[3 * s + 2:3 * s + 3]
    r2 = lax.rsqrt(_rowmean(fv * fv) + RMS_EPS)
    fh = fv * r2
    dy = dv * (res_w * (1.0 + gate))
    acc[0][...] += _colsum(dv * (res_w * (fh * g_post)))
    acc[1][...] += _colsum(dy * fh)
    gy = dy * g_post
    return r2 * (gy - fh * _rowmean(gy * fh))


def _pre_bwd_math(dhv, xv, dv, g, m, s, acc):
    g_pre, scale = g[2 * s:2 * s + 1], m[3 * s + 1:3 * s + 2]
    r = lax.rsqrt(_rowmean(xv * xv) + RMS_EPS)
    nv = xv * r
    acc[0][...] += _colsum(dhv)
    acc[1][...] += _colsum(dhv * (nv * g_pre))
    acc[2][...] += _colsum(dhv * ((1.0 + scale) * nv))
    gn = dhv * (g_pre * (1.0 + scale))
    return r * (gn - nv * _rowmean(gn * nv)) + dv


def _post_bwd(dout, f, gains, mod, *, T, s, res_w, name):
    D = f.shape[1]

    def fn(i, n, t, p, nx, c, acc, scr):
        return [_post_bwd_math(t[0], t[1], c[0], c[1], s, res_w, acc)]

    return _rowwise(fn, T=T, tm=_tile(T, 512, HALO), name=name, tiled=[dout, f], consts=[gains, mod],
                    out_tiled=[(D, BF16)], out_acc=[(1, D), (1, D)])


def _pre_bwd(dh, x, dout, gains, mod, *, T, s, name, comm=None):
    D = x.shape[1]

    def fn(i, n, t, p, nx, c, acc, scr):
        return [_pre_bwd_math(t[0], t[1], t[2], c[0], c[1], s, acc)]

    return _rowwise(fn, T=T, tm=_tile(T, 512, HALO), name=name, tiled=[dh, x, dout], consts=[gains, mod],
                    out_tiled=[(D, F32)], out_acc=[(1, D), (1, D), (1, D)], comm=comm)


def _pre_post_bwd(dh, x, dout, f_prev, gains, mod, *, T, s, res_w_prev, name, comm=None):
    D = x.shape[1]

    def fn(i, n, t, p, nx, c, acc, scr):
        dx = _pre_bwd_math(t[0], t[1], t[2], c[0], c[1], s, acc[0:3])
        return [dx, _post_bwd_math(dx, t[3], c[0], c[1], s - 1, res_w_prev, acc[3:5])]

    return _rowwise(fn, T=T, tm=_tile(T, 512, HALO), name=name, tiled=[dh, x, dout, f_prev], consts=[gains, mod],
                    out_tiled=[(D, F32), (D, BF16)], out_acc=[(1, D)] * 5, comm=comm)


SUBLANES = 8
CONV_CHUNK = 64


def _glu(cvg, C):
    return cvg[:, :C] * _sigmoid(cvg[:, C:])


def _fill_rotations(ext, rot, rows):
    for r in range(SUBLANES):
        rot[r] = ext[pl.ds(r, rows), :]


def _conv_taps(rot, w, r0, rows, off):
    acc = None
    for k in range(CONV_KERNEL):
        a, r = divmod(off(k), SUBLANES)
        term = w[k:k + 1] * rot[r, pl.ds(pl.multiple_of(r0 + a * SUBLANES, SUBLANES), rows), :]
        acc = term if acc is None else acc + term
    return acc


def _causal_off(k):
    return HALO - (CONV_KERNEL - 1) + k


def _conv_norm(rot, cw, cb, r0, rows):
    yc = _conv_taps(rot, cw, r0, rows, _causal_off) + cb
    mu = _rowmean(yc)
    d = yc - mu
    rstd = lax.rsqrt(_rowmean(d * d) + LN_EPS)
    return d * rstd, rstd


def _stage_glu(i, t, p, ext, rot, tm, C):
    ext[pl.ds(0, HALO), :] = jnp.where(i == 0, 0.0, _glu(p[0][...], C))
    ext[pl.ds(HALO, tm), :] = _glu(t[0][...], C)
    ext[pl.ds(HALO + tm, SUBLANES), :] = jnp.zeros((SUBLANES, C), F32)
    _fill_rotations(ext, rot, tm + HALO)


def _conv_scratch(tm, C):
    return [pltpu.VMEM((HALO + tm + SUBLANES, C), F32), pltpu.VMEM((SUBLANES, HALO + tm, C), F32)]


def _conv_fwd(cvg, cw, cvec, *, T, C, name, into=None):
    tm = _tile(T, 512, HALO)
    ch = min(CONV_CHUNK, tm)

    def fn(i, n, t, p, nx, c, o, acc, scr):
        ext, rot = scr
        _stage_glu(i, t, p, ext, rot, tm, C)
        w, vec = c[0][...], c[1][...]

        def chunk(ci, carry):
            r0 = pl.multiple_of(ci * ch, ch)
            yh, _ = _conv_norm(rot, w, vec[0:1], r0, ch)
            zz = yh * vec[1:2] + vec[2:3]
            o[0][pl.ds(r0, ch), :] = (zz * _sigmoid(zz)).astype(BF16)
            return carry

        lax.fori_loop(0, tm // ch, chunk, 0)

    return _rowwise(fn, T=T, tm=tm, name=name, tiled=[cvg], prev=[cvg], consts=[cw, cvec],
                    out_tiled=[(C, BF16)], scratch=_conv_scratch(tm, C), by_ref=True, into=into)[0]


def _conv_bwd1(cvg, duc, cw, cvec, *, T, C, name, comm=None):
    tm = _tile(T, 512, HALO)
    ch = min(CONV_CHUNK, tm)

    def fn(i, n, t, p, nx, c, o, acc, scr):
        ext, rot, w8 = scr

        @pl.when(i == 0)
        def _():
            w8[...] = jnp.zeros_like(w8)

        _stage_glu(i, t, p, ext, rot, tm, C)
        w, vec = c[0][...], c[1][...]
        ln_g = vec[1:2]

        def chunk(ci, carry):
            r0 = pl.multiple_of(ci * ch, ch)
            yh, rstd = _conv_norm(rot, w, vec[0:1], r0, ch)
            zz = yh * ln_g + vec[2:3]
            s = _sigmoid(zz)
            dz = t[1][pl.ds(r0, ch), :] * (s * (1.0 + zz * (1.0 - s)))
            dyh = dz * ln_g
            dyc = rstd * (dyh - _rowmean(dyh) - yh * _rowmean(dyh * yh))
            o[0][pl.ds(r0, ch), :] = dyc
            acc[0][0:1, :] += _colsum(dyc)
            acc[0][1:2, :] += _colsum(dz * yh)
            acc[0][2:3, :] += _colsum(dz)
            for k in range(CONV_KERNEL):
                a, r = divmod(_causal_off(k), SUBLANES)
                prod = dyc * rot[r, pl.ds(pl.multiple_of(r0 + a * SUBLANES, SUBLANES), ch), :]
                part = prod[0:SUBLANES]
                for g in range(1, ch // SUBLANES):
                    part = part + prod[g * SUBLANES:(g + 1) * SUBLANES]
                w8[pl.ds(k * SUBLANES, SUBLANES), :] += part
            return carry

        lax.fori_loop(0, tm // ch, chunk, 0)

        @pl.when(i == n - 1)
        def _():
            for k in range(CONV_KERNEL):
                acc[1][k:k + 1, :] = _colsum(w8[pl.ds(k * SUBLANES, SUBLANES), :])

    return _rowwise(fn, T=T, tm=tm, name=name, tiled=[cvg, duc], prev=[cvg], consts=[cw, cvec],
                    out_tiled=[(C, F32)], out_acc=[(8, C), (HALO, C)],
                    scratch=_conv_scratch(tm, C) + [pltpu.VMEM((HALO * SUBLANES, C), F32)], by_ref=True, comm=comm)


def _conv_bwd2(dyc, cvg, cw, *, T, C, name):
    tm = _tile(T, 512, HALO)
    ch = min(CONV_CHUNK, tm)

    def fn(i, n, t, p, nx, c, o, acc, scr):
        ext, rot = scr
        ext[pl.ds(0, tm), :] = t[0][...]
        ext[pl.ds(tm, HALO), :] = jnp.where(i == n - 1, 0.0, nx[0][...])
        _fill_rotations(ext, rot, tm + HALO - SUBLANES)
        w = c[0][...]

        def chunk(ci, carry):
            r0 = pl.multiple_of(ci * ch, ch)
            dug = _conv_taps(rot, w, r0, ch, lambda k: (CONV_KERNEL - 1) - k)
            cv = t[1][pl.ds(r0, ch), pl.ds(0, C)]
            s = _sigmoid(t[1][pl.ds(r0, ch), pl.ds(C, C)])
            o[0][pl.ds(r0, ch), :] = (dug * s).astype(BF16)
            o[1][pl.ds(r0, ch), :] = (dug * cv * (s * (1.0 - s))).astype(BF16)
            return carry

        lax.fori_loop(0, tm // ch, chunk, 0)

    return _rowwise(fn, T=T, tm=tm, name=name, tiled=[dyc, cvg], nxt=[dyc], consts=[cw],
                    out_tiled=[(C, BF16), (C, BF16)],
                    scratch=[pltpu.VMEM((tm + HALO, C), F32), pltpu.VMEM((SUBLANES, tm + HALO - SUBLANES, C), F32)],
                    by_ref=True)


def _split(v):
    hi = v.astype(BF16)
    return hi, (v - hi.astype(F32)).astype(BF16)


def _dot2(v, m):
    hi, lo = _split(v)
    return jnp.dot(hi, m, preferred_element_type=F32) + jnp.dot(lo, m, preferred_element_type=F32)


def _log_gap(z):
    return -(jnp.maximum(z, 0.0) + jnp.log(1.0 + jnp.exp(-jnp.abs(z))))


def _head_masks():
    lane = lax.broadcasted_iota(jnp.int32, (1, LANES), 1)
    return lane < HEAD_DIM, lane >= HEAD_DIM


LOG_WEIGHT_FLOOR = -110.0
ATTN_BLOCK = 256


def _key_norm_bound(k_ref, masks, T):
    ch = _tile(T, 512)

    def chunk(r, m):
        kk = k_ref[pl.ds(pl.multiple_of(r * ch, ch), ch), :].astype(F32)
        k2 = kk * kk
        return tuple(jnp.maximum(m[h], jnp.max(jnp.sum(jnp.where(masks[h], k2, 0.0), -1, keepdims=True),
                                               axis=0, keepdims=True)) for h in (0, 1))

    m0, m1 = lax.fori_loop(0, T // ch, chunk, (jnp.zeros((1, 1), F32), jnp.zeros((1, 1), F32)))
    row = lax.broadcasted_iota(jnp.int32, (8, LANES), 0)
    return jnp.where(row == 0, jnp.sqrt(m0), jnp.sqrt(m1))


def _score_bound(qh, kn):
    qf = qh.astype(F32)
    return jnp.sqrt(jnp.sum(qf * qf, -1, keepdims=True)) * (kn * 1.01) + 0.01


def _some_weight_left(carries, bounds):
    m = jnp.maximum(jnp.max(carries[0] + bounds[0]), jnp.max(carries[1] + bounds[1]))
    return m > LOG_WEIGHT_FLOOR


def _attn_fwd(qkv, g_attn, *, T, AW, a_cols, name, comm=None):
    P = AW // LANES
    tq = _tile(T, 2 * ATTN_BLOCK)
    tb = tq // 2
    nq = T // tq
    scale = HEAD_DIM ** -0.5
    ci_specs, co_specs, co_shapes, csems = _comm_specs(comm)
    nci, nco = len(ci_specs), len(co_specs)

    def body(q_ref, k_ref, v_ref, g_ref, *rest):
        o_ref, a_ref, kn_ref = rest[nci], rest[nci + 1], rest[nci + 2 + nco]
        at_entry, at_exit = _grid2_hooks(comm, P, nq, (rest[:nci], rest[nci + 2:nci + 2 + nco], rest[nci + 3 + nco:]))
        at_entry()
        i = pl.program_id(1)
        lo_mask, hi_mask = masks = _head_masks()

        @pl.when(i == 0)
        def _():
            kn_ref[...] = _key_norm_bound(k_ref, masks, T)

        rows = lax.broadcasted_iota(jnp.int32, (tb, tb), 0)
        cols = lax.broadcasted_iota(jnp.int32, (tb, tb), 1)
        strict = cols < rows
        everywhere = cols >= 0
        tri = jnp.where(rows >= cols, 1.0, 0.0).astype(BF16)
        qhs, zbs = [], []
        for part in (0, 1):
            q = q_ref[pl.ds(part * tb, tb), :]
            qhs.append([jnp.where(m, q, jnp.zeros_like(q)) * jnp.asarray(scale, BF16) for m in masks])
            zbs.append([_score_bound(qhs[part][h], kn_ref[h:h + 1, 0:1]) for h in (0, 1)])

        def block(kb, part, carry, mask=None):
            st = pl.multiple_of(kb * tb, tb)
            kj = k_ref[pl.ds(st, tb), :]
            vj = v_ref[pl.ds(st, tb), :]
            new = []
            for h in (0, 1):
                acc, c = carry[h]
                z = lax.dot_general(qhs[part][h], kj, _DIMS['nt'], preferred_element_type=F32)
                l = _log_gap(z)
                if mask is not None:
                    l = jnp.where(mask, l, 0.0)
                cum = _dot2(l, tri)
                w = jnp.exp(z + cum + c)
                if mask is not None:
                    w = jnp.where(mask, w, 0.0)
                new.append((acc + _dot2(w, vj), c + cum[:, 0:1]))
            return tuple(new)

        zero = (jnp.zeros((tb, LANES), F32), jnp.zeros((tb, 1), F32))
        carries = []
        for part in (0, 1):
            kb0 = 2 * i + part
            cr = block(kb0, part, (zero, zero), strict)
            cr = block(jnp.maximum(kb0 - 1, 0), part, cr, jnp.logical_and(i > 0, everywhere) if part == 0 else None)
            carries.append(cr)

        def live(st):
            jj, ca, cb = st
            return jnp.logical_and(jj < 2 * i, jnp.logical_or(
                _some_weight_left([ca[0][1], ca[1][1]], zbs[0]), _some_weight_left([cb[0][1], cb[1][1]], zbs[1])))

        def more(st):
            jj, ca, cb = st
            ka = 2 * i - 2 - jj
            ca = block(jnp.maximum(ka, 0), 0, ca, jnp.logical_and(ka >= 0, everywhere))
            cb = block(ka + 1, 1, cb)
            return jj + 1, ca, cb

        _, ca, cb = lax.while_loop(live, more, (jnp.int32(0), carries[0], carries[1]))
        o = jnp.concatenate([jnp.where(lo_mask, c2[0][0], c2[1][0]) for c2 in (ca, cb)], axis=0)
        o2 = o * o
        r0 = lax.rsqrt(jnp.sum(jnp.where(lo_mask, o2, 0.0), -1, keepdims=True) * (1.0 / HEAD_DIM) + RMS_EPS)
        r1 = lax.rsqrt(jnp.sum(jnp.where(hi_mask, o2, 0.0), -1, keepdims=True) * (1.0 / HEAD_DIM) + RMS_EPS)
        o_ref[...] = o
        a_ref[...] = ((o * jnp.where(lo_mask, r0, r1)) * g_ref[...]).astype(BF16)
        at_exit()

    res = pl.pallas_call(
        body, grid=(P, nq),
        in_specs=[pl.BlockSpec((tq, LANES), lambda p, i: (i, p)),
                  pl.BlockSpec((T, LANES), lambda p, i: (0, P + p)),
                  pl.BlockSpec((T, LANES), lambda p, i: (0, 2 * P + p)),
                  pl.BlockSpec((1, LANES), lambda p, i: (0, p))] + ci_specs,
        out_specs=[pl.BlockSpec((tq, LANES), lambda p, i: (i, p)),
                   pl.BlockSpec((tq, LANES), lambda p, i: (i, p))] + co_specs,
        out_shape=[jax.ShapeDtypeStruct((T, AW), F32), jax.ShapeDtypeStruct((T, a_cols), BF16)] + co_shapes,
        scratch_shapes=[pltpu.VMEM((8, LANES), F32)] + csems,
        compiler_params=_cparams(("parallel", "arbitrary") if comm is None else ("arbitrary", "arbitrary")),
        name=name)(qkv, qkv, qkv, g_attn, *([] if comm is None else comm.ins))
    return (res[0], res[1]) if comm is None else (res[0], res[1], res[2:])


def _attn_bwd(qkv, o, da, g_attn, *, T, AW, name):
    P = AW // LANES
    tq = _tile(T, 2 * ATTN_BLOCK)
    tb = tq // 2
    nq, nb = T // tq, T // tb
    scale = HEAD_DIM ** -0.5

    def body(q_ref, k_ref, v_ref, o_ref, da_ref, g_ref, dq_ref, dk_out, dv_out, dg_ref, kn_ref, dk_ref, dv_ref):
        i = pl.program_id(1)
        lo_mask, hi_mask = masks = _head_masks()

        @pl.when(i == 0)
        def _():
            dk_ref[...] = jnp.zeros_like(dk_ref)
            dv_ref[...] = jnp.zeros_like(dv_ref)
            dg_ref[...] = jnp.zeros_like(dg_ref)
            kn_ref[...] = _key_norm_bound(k_ref, masks, T)

        rows = lax.broadcasted_iota(jnp.int32, (tb, tb), 0)
        cols = lax.broadcasted_iota(jnp.int32, (tb, tb), 1)
        strict = cols < rows
        everywhere = cols >= 0
        tri = jnp.where(rows >= cols, 1.0, 0.0).astype(BF16)
        tri_s = jnp.where(rows > cols, 1.0, 0.0).astype(BF16)
        o_all = o_ref[...]
        da = da_ref[...]
        g = g_ref[...]
        o2 = o_all * o_all
        r0 = lax.rsqrt(jnp.sum(jnp.where(lo_mask, o2, 0.0), -1, keepdims=True) * (1.0 / HEAD_DIM) + RMS_EPS)
        r1 = lax.rsqrt(jnp.sum(jnp.where(hi_mask, o2, 0.0), -1, keepdims=True) * (1.0 / HEAD_DIM) + RMS_EPS)
        r = jnp.where(lo_mask, r0, r1)
        oh = o_all * r
        gy = da * g
        gyo = gy * oh
        m0 = jnp.sum(jnp.where(lo_mask, gyo, 0.0), -1, keepdims=True) * (1.0 / HEAD_DIM)
        m1 = jnp.sum(jnp.where(hi_mask, gyo, 0.0), -1, keepdims=True) * (1.0 / HEAD_DIM)
        do_all = r * (gy - oh * jnp.where(lo_mask, m0, m1))
        dg_ref[...] += _colsum(da * oh)

        qhs, zbs, do_bs, deltas, q_ts, do_ts = [], [], [], [], [], []
        for part in (0, 1):
            q = q_ref[pl.ds(part * tb, tb), :]
            o = o_all[part * tb:(part + 1) * tb]
            do = do_all[part * tb:(part + 1) * tb]
            qhs.append([jnp.where(m, q, jnp.zeros_like(q)) * jnp.asarray(scale, BF16) for m in masks])
            zbs.append([_score_bound(qhs[part][h], kn_ref[h:h + 1, 0:1]) for h in (0, 1)])
            do_bs.append([jnp.where(m, do, 0.0).astype(BF16) for m in masks])
            deltas.append([jnp.sum(d.astype(F32) * o, -1, keepdims=True) for d in do_bs[part]])
            q_ts.append([qh.astype(F32).T.astype(BF16) for qh in qhs[part]])
            do_ts.append([d.astype(F32).T.astype(BF16) for d in do_bs[part]])

        def block(kb, part, carry, mask=None):
            masked = mask is not None
            st = pl.multiple_of(kb * tb, tb)
            kj = k_ref[pl.ds(st, tb), :]
            vj = v_ref[pl.ds(st, tb), :]
            new = []
            dk = dv = None
            for h in (0, 1):
                dq, c, gsum = carry[h]
                z = lax.dot_general(qhs[part][h], kj, _DIMS['nt'], preferred_element_type=F32)
                l = _log_gap(z)
                sig = jnp.exp(z + l)
                if masked:
                    l = jnp.where(mask, l, 0.0)
                cum = _dot2(l, tri)
                w = jnp.exp(z + cum + c)
                if masked:
                    w = jnp.where(mask, w, 0.0)
                dp = lax.dot_general(do_bs[part][h], vj, _DIMS['nt'], preferred_element_type=F32)
                pw = w * dp
                after = _dot2(pw, tri_s)
                dz = pw - sig * (deltas[part][h] - gsum - after)
                if masked:
                    dz = jnp.where(mask, dz, 0.0)
                dz_b = dz.astype(BF16)
                dk_h = jnp.dot(q_ts[part][h], dz_b, preferred_element_type=F32)
                dv_h = jnp.dot(do_ts[part][h], w.astype(BF16), preferred_element_type=F32)
                dk = dk_h if dk is None else dk + dk_h
                dv = dv_h if dv is None else dv + dv_h
                dq = dq + jnp.dot(dz_b, kj, preferred_element_type=F32)
                new.append((dq, c + cum[:, 0:1], gsum + (after[:, 0:1] + pw[:, 0:1])))
            dk_ref[kb] += dk
            dv_ref[kb] += dv
            return tuple(new)

        zero1 = jnp.zeros((tb, 1), F32)
        zero = (jnp.zeros((tb, LANES), F32), zero1, zero1)
        carries = []
        for part in (0, 1):
            kb0 = 2 * i + part
            cr = block(kb0, part, (zero, zero), strict)
            cr = block(jnp.maximum(kb0 - 1, 0), part, cr, jnp.logical_and(i > 0, everywhere) if part == 0 else None)
            carries.append(cr)

        def live(st):
            jj, ca, cb = st
            return jnp.logical_and(jj < 2 * i, jnp.logical_or(
                _some_weight_left([ca[0][1], ca[1][1]], zbs[0]), _some_weight_left([cb[0][1], cb[1][1]], zbs[1])))

        def more(st):
            jj, ca, cb = st
            ka = 2 * i - 2 - jj
            ca = block(jnp.maximum(ka, 0), 0, ca, jnp.logical_and(ka >= 0, everywhere))
            cb = block(ka + 1, 1, cb)
            return jj + 1, ca, cb

        _, ca, cb = lax.while_loop(live, more, (jnp.int32(0), carries[0], carries[1]))
        dq_ref[...] = (jnp.concatenate([jnp.where(lo_mask, c2[0][0], c2[1][0]) for c2 in (ca, cb)], axis=0)
                       * scale).astype(BF16)

        @pl.when(i == nq - 1)
        def _():
            def turn(j, carry_):
                st = pl.multiple_of(j * tb, tb)
                dk_out[pl.ds(st, tb), :] = dk_ref[j].T.astype(BF16)
                dv_out[pl.ds(st, tb), :] = dv_ref[j].T.astype(BF16)
                return carry_

            lax.fori_loop(0, nb, turn, 0)

    return pl.pallas_call(
        body, grid=(P, nq),
        in_specs=[pl.BlockSpec((tq, LANES), lambda p, i: (i, p)),
                  pl.BlockSpec((T, LANES), lambda p, i: (0, P + p)),
                  pl.BlockSpec((T, LANES), lambda p, i: (0, 2 * P + p)),
                  pl.BlockSpec((tq, LANES), lambda p, i: (i, p)),
                  pl.BlockSpec((tq, LANES), lambda p, i: (i, p)),
                  pl.BlockSpec((1, LANES), lambda p, i: (0, p))],
        out_specs=[pl.BlockSpec((tq, LANES), lambda p, i: (i, p)),
                   pl.BlockSpec((T, LANES), lambda p, i: (0, p)),
                   pl.BlockSpec((T, LANES), lambda p, i: (0, p)),
                   pl.BlockSpec((1, LANES), lambda p, i: (0, p))],
        out_shape=[jax.ShapeDtypeStruct((T, AW), BF16)] * 3 + [jax.ShapeDtypeStruct((1, AW), F32)],
        scratch_shapes=[pltpu.VMEM((8, LANES), F32), pltpu.VMEM((nb, LANES, tb), F32),
                        pltpu.VMEM((nb, LANES, tb), F32)],
        compiler_params=_cparams(("parallel", "arbitrary")), name=name)(qkv, qkv, qkv, o, da, g_attn)


def _ada_fwd(c_all, w_ada, b_ada, *, name):
    def body(c_ref, w_ref, b_ref, o_ref):
        cv = c_ref[...]
        sc = cv * _sigmoid(cv)
        o_ref[...] = jnp.dot(sc, w_ref[...], preferred_element_type=F32,
                             precision=lax.Precision.HIGHEST) + b_ref[...]

    return pl.pallas_call(body, out_shape=jax.ShapeDtypeStruct((c_all.shape[0], w_ada.shape[1]), F32),
                          compiler_params=_cparams(), name=name)(c_all, w_ada, b_ada)


def _ada_bwd(c_all_t, dmod, *, name):
    def body(c_ref, d_ref, o_ref):
        cv = c_ref[...]
        sc = cv * _sigmoid(cv)
        o_ref[...] = jnp.dot(sc, d_ref[...], preferred_element_type=F32, precision=lax.Precision.HIGHEST)

    return pl.pallas_call(body, out_shape=jax.ShapeDtypeStruct((c_all_t.shape[0], dmod.shape[1]), F32),
                          compiler_params=_cparams(), name=name)(c_all_t, dmod)


def _adamw_update(w_ref, g_ref, m_ref, v_ref, d_ref, nm_ref, nv_ref):
    gv = g_ref[...]
    m2 = ADAM_B1 * m_ref[...] + (1.0 - ADAM_B1) * gv
    v2 = ADAM_B2 * v_ref[...] + (1.0 - ADAM_B2) * jnp.square(gv)
    m_hat = m2 / (1.0 - ADAM_B1 ** ADAM_STEP)
    v_hat = v2 / (1.0 - ADAM_B2 ** ADAM_STEP)
    d_ref[...] = -ADAM_LR * (m_hat / (jnp.sqrt(v_hat) + ADAM_EPS) + ADAM_WD * w_ref[...])
    nm_ref[...] = m2
    nv_ref[...] = v2


def _adamw_many(wgmv, *, name):
    n = len(wgmv[0])

    def body(*refs):
        ins, outs = refs[:4 * n], refs[4 * n:]
        for k in range(n):
            _adamw_update(ins[k], ins[n + k], ins[2 * n + k], ins[3 * n + k], *outs[3 * k:3 * k + 3])

    return pl.pallas_call(
        body, out_shape=[jax.ShapeDtypeStruct(w.shape, F32) for w in wgmv[0] for _ in range(3)],
        compiler_params=_cparams(), name=name)(*wgmv[0], *wgmv[1], *wgmv[2], *wgmv[3])


def _adamw(w, g, m, v, *, name):
    R, C = w.shape
    tr = _tile(R, max(8, (1 << 18) // C), 8)
    body = functools.partial(_adamw_update)

    spec = pl.BlockSpec((tr, C), lambda i: (i, 0))
    return pl.pallas_call(
        body, grid=(R // tr,), in_specs=[spec] * 4, out_specs=[spec] * 3,
        out_shape=[jax.ShapeDtypeStruct((R, C), F32)] * 3,
        compiler_params=_cparams(("parallel",)), name=name)(w, g, m, v)


def _sum_devices(a, *, name):
    def body(a_ref, o_ref):
        s = a_ref[0]
        for d in range(1, a_ref.shape[0]):
            s = s + a_ref[d]
        o_ref[...] = s

    return pl.pallas_call(body, out_shape=jax.ShapeDtypeStruct(a.shape[1:], F32),
                          compiler_params=_cparams(), name=name)(a)


def _place():
    return lax.axis_index("x"), lax.axis_index("y"), lax.axis_index("c")


def _flip(v, bit):
    return 1 - v if bit else v


def _allgather8(blk, *, name):
    R, C = blk.shape

    def body(x_ref, out_ref, send_sems, recv_sems):
        x, y, c = _place()
        me = 4 * x + 2 * y + c
        out_ref[me] = x_ref[...]
        copies = []
        for k in range(1, 8):
            peer = (_flip(x, (k >> 2) & 1), _flip(y, (k >> 1) & 1), _flip(c, k & 1))
            cp = pltpu.make_async_remote_copy(
                src_ref=x_ref, dst_ref=out_ref.at[me], send_sem=send_sems.at[k - 1],
                recv_sem=recv_sems.at[k - 1], device_id=peer, device_id_type=MESH)
            cp.start()
            copies.append(cp)
        for cp in copies:
            cp.wait()

    return pl.pallas_call(
        body, out_shape=jax.ShapeDtypeStruct((8, R, C), F32),
        in_specs=[pl.BlockSpec(memory_space=pltpu.VMEM)], out_specs=pl.BlockSpec(memory_space=pltpu.VMEM),
        scratch_shapes=[pltpu.SemaphoreType.DMA((7,)), pltpu.SemaphoreType.DMA((7,))],
        compiler_params=_cparams(), name=name)(blk)


def _aligned(v, m):
    return v if isinstance(v, int) else pl.multiple_of(v, m)


def _rows_half(ref, half):
    n = ref.shape[0] // 2
    return ref.at[pl.ds(_aligned(half * n, 16), n)]


def _region(ref, kind, slot, half):
    if kind == 'col':
        n, cs = ref.shape[0] // 2, ref.shape[1] // 4
        return ref.at[pl.ds(_aligned(half * n, 16), n), pl.ds(_aligned(slot * cs, LANES), cs)]
    rs = ref.shape[0] // 4
    return ref.at[pl.ds(_aligned(slot * rs + half * (rs // 2), 16), rs // 2)]


def _other_chips(x, y):
    return [(1 - x, y), (x, 1 - y), (1 - x, 1 - y)]


class _Comm:
    def __init__(self, ins, outs, sems, start, finish):
        self.ins, self.outs, self.sems, self.start, self.finish = list(ins), list(outs), list(sems), start, finish


def _comm_specs(comm):
    if comm is None:
        return [], [], [], []
    anyspec = pl.BlockSpec(memory_space=pl.ANY)
    return [anyspec] * len(comm.ins), [anyspec] * len(comm.outs), list(comm.outs), list(comm.sems)


def _comm_hooks(comm, first, last, refs):
    if comm is None:
        return (lambda: None), (lambda: None)

    def at_entry():
        pl.when(first)(lambda: comm.start(*refs))

    def at_exit():
        pl.when(last)(lambda: comm.finish(*refs))

    return at_entry, at_exit


def _comm_alone(comm, *, name):
    ni, no = len(comm.ins), len(comm.outs)

    def body(*refs):
        parts = (refs[:ni], refs[ni:ni + no], refs[ni + no:])
        comm.start(*parts)
        comm.finish(*parts)

    i_specs, o_specs, o_shapes, sems = _comm_specs(comm)
    return pl.pallas_call(body, out_shape=o_shapes, in_specs=i_specs, out_specs=o_specs, scratch_shapes=sems,
                          compiler_params=_cparams(), name=name)(*comm.ins)


def _gather_comm(shards, kinds):
    nw = len(shards)
    full_shapes = []
    for s, kind in zip(shards, kinds):
        full_shapes.append((s.shape[0], 4 * s.shape[1]) if kind == 'col' else (4 * s.shape[0], s.shape[1]))

    def copies(sh, full, sems):
        lsem, ssem, rsem, fssem, frsem = sems
        x, y, c = _place()
        me_slot = 2 * x + y
        chips = _other_chips(x, y)
        local, ici, landed, fwd, passed = [], [], [], [], []
        for w in range(nw):
            for h in (0, 1):
                local.append(pltpu.make_async_copy(_rows_half(sh[w], h), _region(full[w], kinds[w], me_slot, h),
                                                   lsem.at[w, h]))
            for r, (px, py) in enumerate(chips):
                ici.append(pltpu.make_async_remote_copy(
                    src_ref=_rows_half(sh[w], c), dst_ref=_region(full[w], kinds[w], me_slot, c),
                    send_sem=ssem.at[w, r], recv_sem=rsem.at[w, r], device_id=(px, py, c), device_id_type=MESH))
                mine = _region(full[w], kinds[w], 2 * px + py, c)
                landed.append(pltpu.make_async_remote_copy(
                    src_ref=mine, dst_ref=mine, send_sem=ssem.at[w, r], recv_sem=rsem.at[w, r],
                    device_id=(px, py, c), device_id_type=MESH))
                fwd.append(pltpu.make_async_remote_copy(
                    src_ref=mine, dst_ref=mine, send_sem=fssem.at[w, r], recv_sem=frsem.at[w, r],
                    device_id=(x, y, 1 - c), device_id_type=MESH))
                theirs = _region(full[w], kinds[w], 2 * px + py, 1 - c)
                passed.append(pltpu.make_async_remote_copy(
                    src_ref=theirs, dst_ref=theirs, send_sem=fssem.at[w, r], recv_sem=frsem.at[w, r],
                    device_id=(x, y, 1 - c), device_id_type=MESH))
        return local, ici, landed, fwd, passed

    def start(sh, full, sems):
        local, ici, _, _, _ = copies(sh, full, sems)
        for cp in local + ici:
            cp.start()

    def finish(sh, full, sems):
        local, ici, landed, fwd, passed = copies(sh, full, sems)
        for got, cp in zip(landed, fwd):
            got.wait_recv()
            cp.start()
        for got in passed:
            got.wait_recv()
        for cp in ici + fwd:
            cp.wait_send()
        for cp in local:
            cp.wait()

    return _Comm(shards, [jax.ShapeDtypeStruct(s, BF16) for s in full_shapes],
                 [pltpu.SemaphoreType.DMA((nw, 2))] + [pltpu.SemaphoreType.DMA((nw, 3))] * 4, start, finish)


def _exchange_comm(grads, kinds):
    nw = len(grads)

    def copies(g, r1, sems):
        ssem, rsem = sems
        x, y, c = _place()
        out, back = [], []
        for w in range(nw):
            for slot in range(4):
                out.append(pltpu.make_async_remote_copy(
                    src_ref=_region(g[w], kinds[w], slot, 1 - c), dst_ref=_region(r1[w], kinds[w], slot, 1 - c),
                    send_sem=ssem.at[w, slot], recv_sem=rsem.at[w, slot], device_id=(x, y, 1 - c),
                    device_id_type=MESH))
                mine = _region(r1[w], kinds[w], slot, c)
                back.append(pltpu.make_async_remote_copy(
                    src_ref=mine, dst_ref=mine, send_sem=ssem.at[w, slot], recv_sem=rsem.at[w, slot],
                    device_id=(x, y, 1 - c), device_id_type=MESH))
        return out, back

    def start(g, r1, sems):
        for cp in copies(g, r1, sems)[0]:
            cp.start()

    def finish(g, r1, sems):
        out, back = copies(g, r1, sems)
        for got in back:
            got.wait_recv()
        for cp in out:
            cp.wait_send()

    return _Comm(grads, [jax.ShapeDtypeStruct(g.shape, F32) for g in grads],
                 [pltpu.SemaphoreType.DMA((nw, 4))] * 2, start, finish)


def _add_core_halves(g, r1, place, kind, *, name):
    if kind == 'col':
        n, cs = g.shape[0] // 2, g.shape[1] // 4
        tr = _tile(n, 256, 16)
        nt = n // tr
        ispec = pl.BlockSpec((tr, cs), lambda s, t, pr: (pr[0] * nt + t, s))
    else:
        rs, cs = g.shape[0] // 4, g.shape[1]
        n = rs // 2
        tr, nt = n, 1
        ispec = pl.BlockSpec((tr, cs), lambda s, t, pr: (s * 2 + pr[0], 0))

    def body(pr, a_ref, b_ref, o_ref):
        o_ref[...] = (a_ref[...] + b_ref[...]).astype(BF16)

    return pl.pallas_call(
        body,
        grid_spec=pltpu.PrefetchScalarGridSpec(
            num_scalar_prefetch=1, grid=(4, nt), in_specs=[ispec, ispec],
            out_specs=pl.BlockSpec((None, tr, cs), lambda s, t, pr: (s, t, 0))),
        out_shape=jax.ShapeDtypeStruct((4, n, cs), BF16),
        compiler_params=_cparams(("parallel", "parallel")), name=name)(place, g, r1)


def _scatter_comm(hs):
    nw = len(hs)

    def copies(h, r2, sems):
        ssem, rsem = sems
        x, y, c = _place()
        return [pltpu.make_async_remote_copy(
            src_ref=h[w].at[2 * px + py], dst_ref=r2[w].at[r], send_sem=ssem.at[w, r],
            recv_sem=rsem.at[w, r], device_id=(px, py, c), device_id_type=MESH)
            for w in range(nw) for r, (px, py) in enumerate(_other_chips(x, y))]

    def start(h, r2, sems):
        for cp in copies(h, r2, sems):
            cp.start()

    def finish(h, r2, sems):
        for cp in copies(h, r2, sems):
            cp.wait()

    return _Comm(hs, [jax.ShapeDtypeStruct((3,) + a.shape[1:], a.dtype) for a in hs],
                 [pltpu.SemaphoreType.DMA((nw, 3))] * 2, start, finish)


def _sum_owner(hs, r2, place, *, name):
    _, n, cs = hs.shape
    tr = _tile(n, 256, 16)
    nt = n // tr

    def body(pr, h_ref, r_ref, o_ref):
        o_ref[...] = ((h_ref[...].astype(F32) + r_ref[0].astype(F32)) + r_ref[1].astype(F32)) + r_ref[2].astype(F32)

    return pl.pallas_call(
        body,
        grid_spec=pltpu.PrefetchScalarGridSpec(
            num_scalar_prefetch=1, grid=(nt,),
            in_specs=[pl.BlockSpec((None, tr, cs), lambda t, pr: (pr[1], t, 0)),
                      pl.BlockSpec((3, tr, cs), lambda t, pr: (0, t, 0))],
            out_specs=pl.BlockSpec((None, tr, cs), lambda t, pr: (pr[0], t, 0))),
        out_shape=jax.ShapeDtypeStruct((2, n, cs), F32),
        compiler_params=_cparams(("parallel",)), name=name)(place, hs, r2)


def _share_with_sibling(fins, *, name):
    nw = len(fins)

    def body(*refs):
        fin, out = refs[:nw], refs[nw:2 * nw]
        ssem, rsem = refs[2 * nw:]
        x, y, c = _place()
        copies = []
        for w in range(nw):
            cp = pltpu.make_async_remote_copy(
                src_ref=fin[w].at[c], dst_ref=out[w].at[c], send_sem=ssem.at[w], recv_sem=rsem.at[w],
                device_id=(x, y, 1 - c), device_id_type=MESH)
            cp.start()
            copies.append(cp)
        for w in range(nw):
            theirs = out[w].at[1 - c]
            pltpu.make_async_remote_copy(
                src_ref=theirs, dst_ref=theirs, send_sem=ssem.at[w], recv_sem=rsem.at[w],
                device_id=(x, y, 1 - c), device_id_type=MESH).wait_recv()
        for cp in copies:
            cp.wait_send()

    anyspec = pl.BlockSpec(memory_space=pl.ANY)
    return pl.pallas_call(
        body, out_shape=[jax.ShapeDtypeStruct(a.shape, F32) for a in fins],
        in_specs=[anyspec] * nw, out_specs=[anyspec] * nw,
        input_output_aliases={w: w for w in range(nw)},
        scratch_shapes=[pltpu.SemaphoreType.DMA((nw,))] * 2,
        compiler_params=_cparams(), name=name)(*fins)


class _Plan:
    def __init__(self, wfull):
        self.w = dict(wfull)
        self.grads = {}

    def ffn_width(self):
        return self.w['ff1_w_out'].shape[0]

    def comm(self, site):
        return None

    def done(self, site, results):
        pass

    def ready(self, group, names, arrays):
        self.grads.update(zip(names, arrays))


def _riding(plan, site, call):
    comm = plan.comm(site)
    res = call(comm)
    if comm is None:
        return res
    *main, extra = res
    plan.done(site, extra)
    return main[0] if len(main) == 1 else tuple(main)


def _merge_comms(comms):
    if len(comms) == 1:
        return comms[0]

    def parts(refs, field):
        out, pos = [], 0
        for c in comms:
            n = len(getattr(c, field))
            out.append(refs[pos:pos + n])
            pos += n
        return out

    def run(which):
        def fn(ins, outs, sems):
            for c, i, o, s in zip(comms, parts(ins, 'ins'), parts(outs, 'outs'), parts(sems, 'sems')):
                getattr(c, which)(i, o, s)
        return fn

    return _Comm(sum((c.ins for c in comms), []), sum((c.outs for c in comms), []),
                 sum((c.sems for c in comms), []), run('start'), run('finish'))


class _DistPlan(_Plan):
    RIDES = {
        'pre_fwd_ff1': [('gather', ['ff1_w_in'])],
        'ffn_in_ff1': [('gather', ['ff1_w_out', 'w_in_mix', 'w_out_mix'])],
        'attn_fwd': [('gather', ['ff2_w_in', 'ff2_w_out'])],
        'pre_bwd_ff2': [('exchange', ['ff2_w_in', 'ff2_w_out'])],
        'conv_bwd1': [('scatter', ['ff2_w_in', 'ff2_w_out'])],
        'pre_bwd_mix': [('exchange', ['w_in_mix', 'w_out_mix'])],
        'ffn_dact_ff1': [('scatter', ['w_in_mix', 'w_out_mix'])],
        'dw_out_ff1': [('exchange', ['ff1_w_in'])],
        'dh_ff1': [('scatter', ['ff1_w_in']), ('exchange', ['ff1_w_out'])],
    }
    AFTER = [('scatter', ['ff1_w_out'])]

    def __init__(self, shards, place):
        self.shards, self.place, self.kind = shards, place, dict(BIG)
        self.w, self.grads, self.hs, self.fin = {}, {}, {}, {}

    def _make(self, kind, names):
        if kind == 'gather':
            return _gather_comm([self.shards[n] for n in names], [self.kind[n] for n in names])
        if kind == 'exchange':
            return _exchange_comm([self.grads[n] for n in names], [self.kind[n] for n in names])
        return _scatter_comm([self.hs[n] for n in names])

    def _take(self, kind, names, results):
        for n, r in zip(names, results):
            if kind == 'gather':
                self.w[n] = r
            elif kind == 'exchange':
                self.hs[n] = _add_core_halves(self.grads[n], r, self.place, self.kind[n], name=f"grad_core_add_{n}")
            else:
                self.fin[n] = _sum_owner(self.hs[n], r, self.place, name=f"grad_owner_sum_{n}")

    def ffn_width(self):
        return 4 * self.shards['ff1_w_out'].shape[0]

    def comm(self, site):
        rides = self.RIDES.get(site)
        return None if rides is None else _merge_comms([self._make(k, names) for k, names in rides])

    def done(self, site, results):
        pos = 0
        for kind, names in self.RIDES[site]:
            self._take(kind, names, results[pos:pos + len(names)])
            pos += len(names)

    def finish(self):
        for kind, names in self.AFTER:
            self._take(kind, names, _comm_alone(self._make(kind, names), name=f"grad_{kind}_{names[0]}"))
        names = list(self.shards)
        return dict(zip(names, _share_with_sibling([self.fin[n] for n in names], name="grad_share")))


def _local_step(x, target, mod, gains, plan, g_attn, conv_w, cvec):
    T, D = x.shape
    F = plan.ffn_width()
    AW = D // 2
    C = D - AW
    NQKV = 3 * AW
    MIX = NQKV + 2 * C
    tM = _tile(T, 1024)
    tkT = _tile(T, 1024)

    def ffn_fwd(h, tag):
        w_in = plan.w[f"{tag}_w_in"]
        jac, act = _riding(plan, f"ffn_in_{tag}",
                           lambda cm: _ffn_in(h, w_in, T=T, D=D, F=F, name=f"ffn_in_{tag}", comm=cm))
        f = _matmul(act, plan.w[f"{tag}_w_out"], mode='nn', M=T, N=D, K=F, tm=tM, tn=_tile(D, 1024), tk=F,
                    out_dtype=F32, name=f"ffn_out_{tag}")
        return h, jac, act, f

    def ffn_bwd(df, dout, xin, saved, s, tag, prev=None):
        h, jac, act, f = saved
        w_in, w_out = plan.w[f"{tag}_w_in"], plan.w[f"{tag}_w_out"]
        dgu = _riding(plan, f"ffn_dact_{tag}",
                      lambda cm: _ffn_dact(df, w_out, jac, T=T, D=D, F=F, name=f"ffn_dact_{tag}", comm=cm))
        tnf = _tile(F, 2816)
        nf = F // tnf
        dw_in = _matmul(h, dgu, mode='tn', M=D, N=2 * F, K=T, tm=_tile(D, 1024), tn=tnf, tk=tkT, out_dtype=F32,
                        b_spec=pl.BlockSpec((None, tkT, tnf), lambda i, j, k: (j // nf, k, j % nf)),
                        name=f"dw_in_{tag}")
        plan.ready(tag, [f"{tag}_w_in"], [dw_in])
        dw_out = _riding(plan, f"dw_out_{tag}", lambda cm: _matmul(
            act, df, mode='tn', M=F, N=D, K=T, tm=_tile(F, 1408), tn=_tile(D, 1024), tk=tkT, out_dtype=F32,
            name=f"dw_out_{tag}", comm=cm))
        plan.ready(tag, [f"{tag}_w_out"], [dw_out])
        dh = _riding(plan, f"dh_{tag}", lambda cm: _matmul(
            dgu, w_in, mode='nt', M=T, N=D, K=2 * F, tm=tM, tn=_tile(D, 1024), tk=F, out_dtype=F32,
            a_spec=pl.BlockSpec((None, tM, F), lambda i, j, k: (k, i, 0)), name=f"dh_{tag}", comm=cm))
        if prev is None:
            return _riding(plan, f"pre_bwd_{tag}", lambda cm: _pre_bwd(
                dh, xin, dout, gains, mod, T=T, s=s, name=f"pre_bwd_{tag}", comm=cm))
        return _riding(plan, f"pre_bwd_{tag}", lambda cm: _pre_post_bwd(
            dh, xin, dout, prev[0], gains, mod, T=T, s=s, res_w_prev=prev[1], name=f"pre_bwd_{tag}", comm=cm))

    h1 = _riding(plan, "pre_fwd_ff1", lambda cm: _pre_fwd(x, gains, mod, T=T, s=0, name="pre_fwd_ff1", comm=cm))[0]
    s1 = ffn_fwd(h1, "ff1")
    x1, h2 = _post_pre_fwd(x, s1[3], gains, mod, T=T, s=0, res_w=0.5, name="post_fwd_ff1")
    w_in_mix, w_out_mix = plan.w['w_in_mix'], plan.w['w_out_mix']
    tnq = _tile(AW, 512)
    qkv = _matmul(h2, w_in_mix, mode='nn', M=T, N=NQKV, K=D, tm=tM, tn=tnq, tk=D, out_dtype=BF16, name="proj_qkv")
    tnc = _tile(C, 512)
    off = NQKV // tnc
    cvg = _matmul(h2, w_in_mix, mode='nn', M=T, N=2 * C, K=D, tm=tM, tn=tnc, tk=D, out_dtype=F32,
                  b_spec=pl.BlockSpec((D, tnc), lambda i, j, k: (0, off + j)), name="proj_conv")
    o_attn, a_attn = _riding(plan, "attn_fwd", lambda cm: _attn_fwd(
        qkv, g_attn, T=T, AW=AW, a_cols=D, name="attn_fwd", comm=cm))
    mixcat = _conv_fwd(cvg, conv_w, cvec, T=T, C=C, name="conv_fwd", into=(a_attn, AW // C))
    f_mix = _matmul(mixcat, w_out_mix, mode='nn', M=T, N=D, K=D, tm=tM, tn=_tile(D, 1024), tk=D, out_dtype=F32,
                    name="mix_out")
    x2, h3 = _post_pre_fwd(x1, f_mix, gains, mod, T=T, s=1, res_w=1.0, name="post_fwd_mix")

    s3 = ffn_fwd(h3, "ff2")
    dout, sq = _post_fwd_loss(x2, s3[3], target, gains, mod, T=T, s=2, res_w=0.5, name="post_fwd_loss")

    df2, dgate2, dgpost2 = _post_bwd(dout, s3[3], gains, mod, T=T, s=2, res_w=0.5, name="post_bwd_ff2")
    dx2, df_mix, dshift2, dscale2, dgpre2, dgate_m, dgpost_m = ffn_bwd(
        df2, dout, x2, s3, 2, "ff2", prev=(f_mix, 1.0))
    dmixcat = _matmul(df_mix, w_out_mix, mode='nt', M=T, N=D, K=D, tm=tM, tn=_tile(D, 1024), tk=D, out_dtype=F32,
                      name="d_mixcat")
    dw_out_mix = _matmul(mixcat, df_mix, mode='tn', M=D, N=D, K=T, tm=_tile(D, 1024), tn=_tile(D, 1024),
                         tk=tkT, out_dtype=F32, name="dw_out_mix")
    dq, dk, dv, dg_attn = _attn_bwd(qkv, o_attn, dmixcat, g_attn, T=T, AW=AW, name="attn_bwd")
    dyc, csum, dconv_w = _riding(plan, "conv_bwd1", lambda cm: _conv_bwd1(
        cvg, (dmixcat, C, AW // C), conv_w, cvec, T=T, C=C, name="conv_bwd1", comm=cm))
    dcv, dcg = _conv_bwd2(dyc, cvg, conv_w, T=T, C=C, name="conv_bwd2")
    dproj = [dq, dk, dv, dcv, dcg]
    dh2 = _matmul(dproj, w_in_mix, mode='nt', M=T, N=D, K=MIX, tm=tM, tn=_tile(D, 1024), tk=MIX, out_dtype=F32,
                  name="dh_mix")
    dw_in_mix = _matmul(h2, dproj, mode='tn', M=D, N=MIX, K=T, tm=_tile(D, 1024), tn=MIX,
                        tk=tkT, out_dtype=F32, name="dw_in_mix")
    plan.ready("mix", ['w_in_mix', 'w_out_mix'], [dw_in_mix, dw_out_mix])
    dx1, df1, dshift_m, dscale_m, dgpre_m, dgate1, dgpost1 = _riding(plan, "pre_bwd_mix", lambda cm: _pre_post_bwd(
        dh2, x1, dx2, s1[3], gains, mod, T=T, s=1, res_w_prev=0.5, name="pre_bwd_mix", comm=cm))

    dx0, dshift1, dscale1, dgpre1 = ffn_bwd(df1, dx1, x, s1, 0, "ff1")

    dgains = [dgpre1, dgpost1, dgpre_m, dgpost_m, dgpre2, dgpost2]
    dmod = [dshift1, dscale1, dgate1, dshift_m, dscale_m, dgate_m, dshift2, dscale2, dgate2]
    return sq, dx0, dgains, dmod, dg_attn, csum, dconv_w


def _pack_rows(pieces, width):
    rows = jnp.concatenate([p.reshape(-1) for p in pieces]).reshape(-1, width)
    pad = (-rows.shape[0]) % 8
    return jnp.pad(rows, ((0, pad), (0, 0)))


def kernel(x, c, w_ada, b_ada, g_pre_ff1, g_post_ff1, ff1_w_in, ff1_w_out, g_pre_mix, g_post_mix, w_in_mix, g_attn_out, conv_w, conv_b, conv_ln_g, conv_ln_b, w_out_mix, g_pre_ff2, g_post_ff2, ff2_w_in, ff2_w_out, loss_target, m_w_ada, m_b_ada, m_g_pre_ff1, m_g_post_ff1, m_ff1_w_in, m_ff1_w_out, m_g_pre_mix, m_g_post_mix, m_w_in_mix, m_g_attn_out, m_conv_w, m_conv_b, m_conv_ln_g, m_conv_ln_b, m_w_out_mix, m_g_pre_ff2, m_g_post_ff2, m_ff2_w_in, m_ff2_w_out, v_w_ada, v_b_ada, v_g_pre_ff1, v_g_post_ff1, v_ff1_w_in, v_ff1_w_out, v_g_pre_mix, v_g_post_mix, v_w_in_mix, v_g_attn_out, v_conv_w, v_conv_b, v_conv_ln_g, v_conv_ln_b, v_w_out_mix, v_g_pre_ff2, v_g_post_ff2, v_ff2_w_in, v_ff2_w_out):
    W = dict(w_ada=w_ada, b_ada=b_ada, g_pre_ff1=g_pre_ff1, g_post_ff1=g_post_ff1, ff1_w_in=ff1_w_in,
             ff1_w_out=ff1_w_out, g_pre_mix=g_pre_mix, g_post_mix=g_post_mix, w_in_mix=w_in_mix,
             g_attn_out=g_attn_out, conv_w=conv_w, conv_b=conv_b, conv_ln_g=conv_ln_g, conv_ln_b=conv_ln_b,
             w_out_mix=w_out_mix, g_pre_ff2=g_pre_ff2, g_post_ff2=g_post_ff2, ff2_w_in=ff2_w_in,
             ff2_w_out=ff2_w_out)
    Mo = dict(w_ada=m_w_ada, b_ada=m_b_ada, g_pre_ff1=m_g_pre_ff1, g_post_ff1=m_g_post_ff1, ff1_w_in=m_ff1_w_in,
              ff1_w_out=m_ff1_w_out, g_pre_mix=m_g_pre_mix, g_post_mix=m_g_post_mix, w_in_mix=m_w_in_mix,
              g_attn_out=m_g_attn_out, conv_w=m_conv_w, conv_b=m_conv_b, conv_ln_g=m_conv_ln_g,
              conv_ln_b=m_conv_ln_b, w_out_mix=m_w_out_mix, g_pre_ff2=m_g_pre_ff2, g_post_ff2=m_g_post_ff2,
              ff2_w_in=m_ff2_w_in, ff2_w_out=m_ff2_w_out)
    Vo = dict(w_ada=v_w_ada, b_ada=v_b_ada, g_pre_ff1=v_g_pre_ff1, g_post_ff1=v_g_post_ff1, ff1_w_in=v_ff1_w_in,
              ff1_w_out=v_ff1_w_out, g_pre_mix=v_g_pre_mix, g_post_mix=v_g_post_mix, w_in_mix=v_w_in_mix,
              g_attn_out=v_g_attn_out, conv_w=v_conv_w, conv_b=v_conv_b, conv_ln_g=v_conv_ln_g,
              conv_ln_b=v_conv_ln_b, w_out_mix=v_w_out_mix, g_pre_ff2=v_g_pre_ff2, g_post_ff2=v_g_post_ff2,
              ff2_w_in=v_ff2_w_in, ff2_w_out=v_ff2_w_out)

    T, D = x.shape[1], x.shape[2]
    AW = D // 2
    C = D - AW
    xi, yi, ci = _place()
    me = 4 * xi + 2 * yi + ci
    chip = 2 * xi + yi
    place = jnp.stack([ci, chip]).astype(jnp.int32)

    c_all = _allgather8(jnp.tile(c, (8, 1)), name="gather_c")[:, 0, :]
    ncol = w_ada.shape[1]
    b_cols = lax.dynamic_index_in_dim(b_ada.reshape(4, ncol), chip, keepdims=True).reshape(1, ncol)
    modp = _ada_fwd(c_all, w_ada, b_cols, name="ada_fwd")
    mod_g = _allgather8(modp, name="gather_mod")
    mod_all = jnp.transpose(mod_g[0::2], (1, 0, 2)).reshape(8, 4 * ncol)
    mod = lax.dynamic_index_in_dim(mod_all, me, keepdims=False).reshape(9, D)

    names = [n for n, _ in BIG]
    plan = _DistPlan({n: W[n].astype(BF16) for n in names}, place)
    cs = conv_w.shape[1]
    cw_all = _allgather8(jnp.pad(conv_w, ((0, HALO - CONV_KERNEL), (0, (-cs) % LANES))), name="gather_conv_w")
    conv_w_full = jnp.transpose(cw_all[0::2, :, :cs], (1, 0, 2)).reshape(HALO, 4 * cs)

    gains = _pack_rows([g_pre_ff1, g_post_ff1, g_pre_mix, g_post_mix, g_pre_ff2, g_post_ff2], D)
    cvec = _pack_rows([conv_b, conv_ln_g, conv_ln_b], C)
    g_attn = g_attn_out.reshape(1, AW)

    sq, dx, dgains, dmod, dg_attn, csum, dconv_w = _local_step(
        x[0], loss_target[0], mod, gains, plan, g_attn, conv_w_full, cvec)

    loss_row = jnp.zeros((1, D), F32).at[0, 0].set(jnp.sum(sq) * (0.5 / D))
    small = _pack_rows(dgains + dmod + [dg_attn, csum[0:3], dconv_w, loss_row], D)
    small_all = _allgather8(small, name="gather_small")
    tot = _sum_devices(small_all, name="sum_small")
    n_g, n_m = 6, 9
    r0 = n_g + n_m
    flat = tot.reshape(-1)
    p = r0 * D
    g_attn_grad = flat[p:p + AW]
    p += AW
    gconv_b, gln_g, gln_b = flat[p:p + C], flat[p + C:p + 2 * C], flat[p + 2 * C:p + 3 * C]
    p += 3 * C
    gconv_w_full = flat[p:p + HALO * C].reshape(HALO, C)[:CONV_KERNEL]
    p += HALO * C
    loss = flat[p]
    gconv_w = lax.dynamic_slice_in_dim(gconv_w_full, chip * cs, cs, axis=1)
    grad_small = {'g_pre_ff1': tot[0], 'g_post_ff1': tot[1], 'g_pre_mix': tot[2], 'g_post_mix': tot[3],
                  'g_pre_ff2': tot[4], 'g_post_ff2': tot[5], 'b_ada': tot[n_g:r0].reshape(-1),
                  'g_attn_out': g_attn_grad.reshape(g_attn_out.shape), 'conv_w': gconv_w, 'conv_b': gconv_b,
                  'conv_ln_g': gln_g, 'conv_ln_b': gln_b}

    dmod_all = small_all[:, n_g:r0, :].reshape(8, 9 * D)
    dmod_cols = lax.dynamic_slice_in_dim(dmod_all, chip * ncol, ncol, axis=1)
    grad_w_ada = _ada_bwd(jnp.transpose(c_all), dmod_cols, name="ada_bwd")

    grads = dict(grad_small)
    grads['w_ada'] = grad_w_ada
    for n, a in plan.finish().items():
        grads[n] = a.reshape(W[n].shape)

    delta, new_m, new_v = {}, {}, {}
    for n in ['w_ada'] + names:
        delta[n], new_m[n], new_v[n] = _adamw(W[n], grads[n], Mo[n], Vo[n], name=f"adamw_{n}")
    smalls = [n for n in WEIGHTS if n not in delta]

    def as2d(a):
        return a if a.ndim == 2 else a.reshape(-1, LANES)

    outs = _adamw_many([[as2d(d[n]) for n in smalls] for d in (W, grads, Mo, Vo)], name="adamw_small")
    for k, n in enumerate(smalls):
        delta[n], new_m[n], new_v[n] = (o.reshape(W[n].shape) for o in outs[3 * k:3 * k + 3])

    return (loss, dx[None], *[grads[n] for n in WEIGHTS], *[delta[n] for n in WEIGHTS],
            *[new_m[n] for n in WEIGHTS], *[new_v[n] for n in WEIGHTS])
```

```python
import functools

import jax
import jax.numpy as jnp
from jax import lax
from jax.experimental import pallas as pl
from jax.experimental.pallas import tpu as pltpu

F32 = jnp.float32
BF16 = jnp.bfloat16
MESH = pl.DeviceIdType.MESH

HEAD_DIM = 64
CONV_KERNEL = 31
RMS_EPS = 1e-6
LN_EPS = 1e-5
ADAM_LR = 0.001
ADAM_B1 = 0.9
ADAM_B2 = 0.999
ADAM_EPS = 1e-08
ADAM_WD = 0.01
ADAM_STEP = 10

LANES = 128
HALO = 32
VMEM_LIMIT = 52 * 1024 * 1024

WEIGHTS = ['w_ada', 'b_ada', 'g_pre_ff1', 'g_post_ff1', 'ff1_w_in', 'ff1_w_out', 'g_pre_mix',
           'g_post_mix', 'w_in_mix', 'g_attn_out', 'conv_w', 'conv_b', 'conv_ln_g', 'conv_ln_b',
           'w_out_mix', 'g_pre_ff2', 'g_post_ff2', 'ff2_w_in', 'ff2_w_out']
BIG = [('ff1_w_in', 'col'), ('ff1_w_out', 'row'), ('w_in_mix', 'col'), ('w_out_mix', 'row'),
       ('ff2_w_in', 'col'), ('ff2_w_out', 'row')]


def _tile(dim, pref, mult=LANES):
    if dim <= pref:
        return dim
    best = None
    for t in range(mult, pref + 1, mult):
        if dim % t == 0:
            best = t
    assert best is not None, (dim, pref, mult)
    return best


def _cparams(sem=None):
    kw = dict(vmem_limit_bytes=VMEM_LIMIT)
    if sem is not None:
        kw['dimension_semantics'] = sem
    return pltpu.CompilerParams(**kw)


def _sigmoid(x):
    return 1.0 / (1.0 + jnp.exp(-x))


_DIMS = {'nn': (((1,), (0,)), ((), ())), 'nt': (((1,), (1,)), ((), ())), 'tn': (((0,), (0,)), ((), ()))}


def _matmul(a, b, *, mode, M, N, K, tm, tn, tk, out_dtype, name, a_spec=None, b_spec=None, comm=None):
    nm, nn, nk = M // tm, N // tn, K // tk
    assert nm * tm == M and nn * tn == N and nk * tk == K, (name, M, N, K, tm, tn, tk)
    a_list = list(a) if isinstance(a, (list, tuple)) else [a]
    b_list = list(b) if isinstance(b, (list, tuple)) else [b]
    if len(a_list) > 1:
        assert mode == 'nt' and tk == K and a_spec is None, name
        a_specs = [pl.BlockSpec((tm, p.shape[1]), lambda i, j, k: (i, 0)) for p in a_list]
    elif a_spec is None:
        a_specs = [pl.BlockSpec((tk, tm), lambda i, j, k: (k, i)) if mode == 'tn'
                   else pl.BlockSpec((tm, tk), lambda i, j, k: (i, k))]
    else:
        a_specs = [a_spec]
    if len(b_list) > 1:
        assert mode == 'tn' and tn == N and b_spec is None, name
        b_specs = [pl.BlockSpec((tk, p.shape[1]), lambda i, j, k: (k, 0)) for p in b_list]
    elif b_spec is None:
        b_specs = [pl.BlockSpec((tn, tk), lambda i, j, k: (j, k)) if mode == 'nt'
                   else pl.BlockSpec((tk, tn), lambda i, j, k: (k, j))]
    else:
        b_specs = [b_spec]
    na, nbb = len(a_list), len(b_list)
    dims = _DIMS[mode]
    assert nk == 1 or out_dtype == F32, name
    ci_specs, co_specs, co_shapes, csems = _comm_specs(comm)
    nci, nco = len(ci_specs), len(co_specs)

    def side_by_side(refs):
        return refs[0][...] if len(refs) == 1 else jnp.concatenate([r[...] for r in refs], axis=1)

    def body(*refs):
        a_refs, b_refs, rest = refs[:na], refs[na:na + nbb], refs[na + nbb:]
        o_ref = rest[nci]
        i, j, k = pl.program_id(0), pl.program_id(1), pl.program_id(2)
        first = jnp.logical_and(jnp.logical_and(i == 0, j == 0), k == 0)
        last = jnp.logical_and(jnp.logical_and(i == nm - 1, j == nn - 1), k == nk - 1)
        at_entry, at_exit = _comm_hooks(comm, first, last, (rest[:nci], rest[nci + 1:nci + 1 + nco], rest[nci + 1 + nco:]))
        at_entry()

        def prod():
            return lax.dot_general(side_by_side(a_refs), side_by_side(b_refs), dims, preferred_element_type=F32)

        if nk == 1:
            o_ref[...] = prod().astype(o_ref.dtype)
        else:
            @pl.when(k == 0)
            def _():
                o_ref[...] = prod()

            @pl.when(k > 0)
            def _():
                o_ref[...] += prod()
        at_exit()

    sem = ("parallel", "parallel", "arbitrary") if comm is None else ("arbitrary",) * 3
    res = pl.pallas_call(
        body, grid=(nm, nn, nk), in_specs=a_specs + b_specs + ci_specs,
        out_specs=[pl.BlockSpec((tm, tn), lambda i, j, k: (i, j))] + co_specs,
        out_shape=[jax.ShapeDtypeStruct((M, N), out_dtype)] + co_shapes, scratch_shapes=csems,
        compiler_params=_cparams(sem), name=name)(*a_list, *b_list, *([] if comm is None else comm.ins))
    return res[0] if comm is None else (res[0], res[1:])


def _grid2_hooks(comm, n0, n1, refs):
    j, i = pl.program_id(0), pl.program_id(1)
    return _comm_hooks(comm, jnp.logical_and(j == 0, i == 0), jnp.logical_and(j == n0 - 1, i == n1 - 1), refs)


def _ffn_in(h, w_in, *, T, D, F, name, comm=None):
    tm, tn = _tile(T, 256), _tile(F, 2816)
    nf, nt = F // tn, T // tm
    ci_specs, co_specs, co_shapes, csems = _comm_specs(comm)
    nci, nco = len(ci_specs), len(co_specs)

    def body(h_ref, wg_ref, wu_ref, *rest):
        jac_ref, a_ref = rest[nci], rest[nci + 1]
        at_entry, at_exit = _grid2_hooks(comm, nf, nt, (rest[:nci], rest[nci + 2:nci + 2 + nco], rest[nci + 2 + nco:]))
        at_entry()
        hh = h_ref[...]
        g = jnp.dot(hh, wg_ref[...], preferred_element_type=F32)
        u = jnp.dot(hh, wu_ref[...], preferred_element_type=F32)
        s = _sigmoid(g)
        sg = g * s
        jac_ref[0] = (u * (s * (1.0 + g * (1.0 - s)))).astype(BF16)
        jac_ref[1] = sg.astype(BF16)
        a_ref[...] = (sg * u).astype(BF16)
        at_exit()

    res = pl.pallas_call(
        body, grid=(nf, nt),
        in_specs=[pl.BlockSpec((tm, D), lambda j, i: (i, 0)),
                  pl.BlockSpec((D, tn), lambda j, i: (0, j)),
                  pl.BlockSpec((D, tn), lambda j, i: (0, nf + j))] + ci_specs,
        out_specs=[pl.BlockSpec((2, tm, tn), lambda j, i: (0, i, j)),
                   pl.BlockSpec((tm, tn), lambda j, i: (i, j))] + co_specs,
        out_shape=[jax.ShapeDtypeStruct((2, T, F), BF16), jax.ShapeDtypeStruct((T, F), BF16)] + co_shapes,
        scratch_shapes=csems,
        compiler_params=_cparams(("parallel", "parallel") if comm is None else ("arbitrary", "arbitrary")),
        name=name)(h, w_in, w_in, *([] if comm is None else comm.ins))
    return (res[0], res[1]) if comm is None else (res[0], res[1], res[2:])


def _ffn_dact(df, w_out, jac, *, T, D, F, name, comm=None):
    tm, tn = _tile(T, 256), _tile(F, 2816)
    nf, nt = F // tn, T // tm
    ci_specs, co_specs, co_shapes, csems = _comm_specs(comm)
    nci, nco = len(ci_specs), len(co_specs)

    def body(df_ref, w_ref, jac_ref, *rest):
        o_ref = rest[nci]
        at_entry, at_exit = _grid2_hooks(comm, nf, nt, (rest[:nci], rest[nci + 1:nci + 1 + nco], rest[nci + 1 + nco:]))
        at_entry()
        da = lax.dot_general(df_ref[...], w_ref[...], _DIMS['nt'], preferred_element_type=F32)
        o_ref[0] = (da * jac_ref[0].astype(F32)).astype(BF16)
        o_ref[1] = (da * jac_ref[1].astype(F32)).astype(BF16)
        at_exit()

    res = pl.pallas_call(
        body, grid=(nf, nt),
        in_specs=[pl.BlockSpec((tm, D), lambda j, i: (i, 0)),
                  pl.BlockSpec((tn, D), lambda j, i: (j, 0)),
                  pl.BlockSpec((2, tm, tn), lambda j, i: (0, i, j))] + ci_specs,
        out_specs=[pl.BlockSpec((2, tm, tn), lambda j, i: (0, i, j))] + co_specs,
        out_shape=[jax.ShapeDtypeStruct((2, T, F), BF16)] + co_shapes, scratch_shapes=csems,
        compiler_params=_cparams(("parallel", "parallel") if comm is None else ("arbitrary", "arbitrary")),
        name=name)(df, w_out, jac, *([] if comm is None else comm.ins))
    return res[0] if comm is None else (res[0], res[1:])


def _rowwise(fn, *, T, tm, name, tiled=(), prev=(), nxt=(), consts=(), out_tiled=(), out_acc=(), scratch=(),
             by_ref=False, comm=None, into=None):
    n = T // tm
    assert n * tm == T and tm % HALO == 0
    hb = tm // HALO
    cols = [a if isinstance(a, tuple) else (a, a.shape[1], 0) for a in tiled]
    tiled = [a for a, _, _ in cols]
    in_specs = [pl.BlockSpec((tm, w), functools.partial(lambda cb, i: (i, cb), cb)) for _, w, cb in cols]
    in_specs += [pl.BlockSpec((HALO, a.shape[1]), lambda i: (jnp.maximum(i * hb - 1, 0), 0)) for a in prev]
    in_specs += [pl.BlockSpec((HALO, a.shape[1]), lambda i: (jnp.minimum((i + 1) * hb, T // HALO - 1), 0))
                 for a in nxt]
    in_specs += [pl.BlockSpec(a.shape, lambda i: (0, 0)) for a in consts]
    out_shape = [jax.ShapeDtypeStruct((T, c), dt) for c, dt in out_tiled]
    out_shape += [jax.ShapeDtypeStruct(s, F32) for s in out_acc]
    out_specs = [pl.BlockSpec((tm, c), lambda i: (i, 0)) for c, _ in out_tiled]
    out_specs += [pl.BlockSpec(s, lambda i: (0, 0)) for s in out_acc]
    nt, npv, nnx, nc, not_, na = len(tiled), len(prev), len(nxt), len(consts), len(out_tiled), len(out_acc)
    ci_specs, co_specs, co_shapes, csems = _comm_specs(comm)
    extra_in, aliases = [], {}
    if into is not None:
        arr, cb = into
        width = out_tiled[0][0]
        out_shape[0] = jax.ShapeDtypeStruct(arr.shape, arr.dtype)
        out_specs[0] = pl.BlockSpec((tm, width), lambda i: (i, cb))
        extra_in = [arr]
        aliases = {nt + npv + nnx + nc: 0}
    n_extra = len(extra_in)

    def body(*refs):
        pos = 0
        groups = []
        for cnt in (nt, npv, nnx, nc, n_extra, len(ci_specs), not_, na, len(co_specs), len(scratch), len(csems)):
            groups.append(refs[pos:pos + cnt])
            pos += cnt
        t_r, p_r, n_r, c_r, _, ci_r, o_r, a_r, co_r, s_r, cs_r = groups
        i = pl.program_id(0)
        at_entry, at_exit = _comm_hooks(comm, i == 0, i == n - 1, (ci_r, co_r, cs_r))
        at_entry()

        @pl.when(i == 0)
        def _():
            for r in a_r:
                r[...] = jnp.zeros_like(r)

        if by_ref:
            fn(i, n, t_r, p_r, n_r, c_r, o_r, a_r, s_r)
        else:
            outs = fn(i, n, [r[...] for r in t_r], [r[...] for r in p_r], [r[...] for r in n_r],
                      [r[...] for r in c_r], a_r, s_r)
            for r, v in zip(o_r, outs):
                r[...] = v.astype(r.dtype)
        at_exit()

    res = pl.pallas_call(
        body, grid=(n,), in_specs=in_specs + [pl.BlockSpec(memory_space=pl.ANY)] * n_extra + ci_specs,
        out_specs=out_specs + co_specs, out_shape=out_shape + co_shapes, scratch_shapes=list(scratch) + csems,
        input_output_aliases=aliases, compiler_params=_cparams(("arbitrary",)), name=name,
    )(*tiled, *prev, *nxt, *consts, *extra_in, *([] if comm is None else comm.ins))
    return res if comm is None else (res[:not_ + na], res[not_ + na:])


def _colsum(v):
    return jnp.sum(v, axis=0, keepdims=True)


def _rowmean(v):
    return jnp.mean(v, axis=-1, keepdims=True)


def _pre_math(xv, g, m, s):
    g_pre, shift, scale = g[2 * s:2 * s + 1], m[3 * s:3 * s + 1], m[3 * s + 1:3 * s + 2]
    r = lax.rsqrt(_rowmean(xv * xv) + RMS_EPS)
    return ((xv * r) * g_pre) * (1.0 + scale) + shift


def _post_math(xv, fv, g, m, s, res_w):
    g_post, gate = g[2 * s + 1:2 * s + 2], m[3 * s + 2:3 * s + 3]
    y = (fv * lax.rsqrt(_rowmean(fv * fv) + RMS_EPS)) * g_post
    return xv + (res_w * (1.0 + gate)) * y


def _pre_fwd(x, gains, mod, *, T, s, name, comm=None):
    def fn(i, n, t, p, nx, c, acc, scr):
        return [_pre_math(t[0], c[0], c[1], s)]

    return _rowwise(fn, T=T, tm=_tile(T, 512, HALO), name=name, tiled=[x], consts=[gains, mod],
                    out_tiled=[(x.shape[1], BF16)], comm=comm)


def _post_pre_fwd(x, f, gains, mod, *, T, s, res_w, name):
    def fn(i, n, t, p, nx, c, acc, scr):
        out = _post_math(t[0], t[1], c[0], c[1], s, res_w)
        return [out, _pre_math(out, c[0], c[1], s + 1)]

    return _rowwise(fn, T=T, tm=_tile(T, 512, HALO), name=name, tiled=[x, f], consts=[gains, mod],
                    out_tiled=[(x.shape[1], F32), (x.shape[1], BF16)])


def _loss_post_bwd(x, f, target, gains, mod, *, T, s, res_w, name):
    D = x.shape[1]

    def fn(i, n, t, p, nx, c, acc, scr):
        (xv, fv, tv), (g, m) = t, c
        err = _post_math(xv, fv, g, m, s, res_w) - tv
        acc[0][...] += _colsum(err * err)
        dout = err * (1.0 / D)
        return [dout, _post_bwd_math(dout, fv, g, m, s, res_w, acc[1:3])]

    return _rowwise(fn, T=T, tm=_tile(T, 512, HALO), name=name, tiled=[x, f, target], consts=[gains, mod],
                    out_tiled=[(D, F32), (D, BF16)], out_acc=[(1, D)] * 3)


def _post_bwd_math(dv, fv, g, m, s, res_w, acc):
    g_post, gate = g[2 * s + 1:2 * s + 2], m[3 * s + 2:3 * s + 3]
    r2 = lax.rsqrt(_rowmean(fv * fv) + RMS_EPS)
    fh = fv * r2
    dy = dv * (res_w * (1.0 + gate))
    acc[0][...] += _colsum(dv * (res_w * (fh * g_post)))
    acc[1][...] += _colsum(dy * fh)
    gy = dy * g_post
    return r2 * (gy - fh * _rowmean(gy * fh))


def _pre_bwd_math(dhv, xv, dv, g, m, s, acc):
    g_pre, scale = g[2 * s:2 * s + 1], m[3 * s + 1:3 * s + 2]
    r = lax.rsqrt(_rowmean(xv * xv) + RMS_EPS)
    nv = xv * r
    acc[0][...] += _colsum(dhv)
    acc[1][...] += _colsum(dhv * (nv * g_pre))
    acc[2][...] += _colsum(dhv * ((1.0 + scale) * nv))
    gn = dhv * (g_pre * (1.0 + scale))
    return r * (gn - nv * _rowmean(gn * nv)) + dv


def _pre_bwd(dh, x, dout, gains, mod, *, T, s, name, comm=None):
    D = x.shape[1]

    def fn(i, n, t, p, nx, c, acc, scr):
        return [_pre_bwd_math(t[0], t[1], t[2], c[0], c[1], s, acc)]

    return _rowwise(fn, T=T, tm=_tile(T, 512, HALO), name=name, tiled=[dh, x, dout], consts=[gains, mod],
                    out_tiled=[(D, F32)], out_acc=[(1, D), (1, D), (1, D)], comm=comm)


def _pre_post_bwd(dh, x, dout, f_prev, gains, mod, *, T, s, res_w_prev, name, comm=None):
    D = x.shape[1]

    def fn(i, n, t, p, nx, c, acc, scr):
        dx = _pre_bwd_math(t[0], t[1], t[2], c[0], c[1], s, acc[0:3])
        return [dx, _post_bwd_math(dx, t[3], c[0], c[1], s - 1, res_w_prev, acc[3:5])]

    return _rowwise(fn, T=T, tm=_tile(T, 512, HALO), name=name, tiled=[dh, x, dout, f_prev], consts=[gains, mod],
                    out_tiled=[(D, F32), (D, BF16)], out_acc=[(1, D)] * 5, comm=comm)


SUBLANES = 8
CONV_CHUNK = 64


def _glu(cvg, C):
    return cvg[:, :C] * _sigmoid(cvg[:, C:])


def _fill_rotations(ext, rot, rows):
    for r in range(SUBLANES):
        rot[r] = ext[pl.ds(r, rows), :]


def _conv_taps(rot, w, r0, rows, off):
    acc = None
    for k in range(CONV_KERNEL):
        a, r = divmod(off(k), SUBLANES)
        term = w[k:k + 1] * rot[r, pl.ds(pl.multiple_of(r0 + a * SUBLANES, SUBLANES), rows), :]
        acc = term if acc is None else acc + term
    return acc


def _causal_off(k):
    return HALO - (CONV_KERNEL - 1) + k


def _conv_norm(rot, cw, cb, r0, rows):
    yc = _conv_taps(rot, cw, r0, rows, _causal_off) + cb
    mu = _rowmean(yc)
    d = yc - mu
    rstd = lax.rsqrt(_rowmean(d * d) + LN_EPS)
    return d * rstd, rstd


def _stage_glu(i, t, p, ext, rot, tm, C):
    ext[pl.ds(0, HALO), :] = jnp.where(i == 0, 0.0, _glu(p[0][...], C))
    ext[pl.ds(HALO, tm), :] = _glu(t[0][...], C)
    ext[pl.ds(HALO + tm, SUBLANES), :] = jnp.zeros((SUBLANES, C), F32)
    _fill_rotations(ext, rot, tm + HALO)


def _conv_scratch(tm, C):
    return [pltpu.VMEM((HALO + tm + SUBLANES, C), F32), pltpu.VMEM((SUBLANES, HALO + tm, C), F32)]


def _conv_fwd(cvg, cw, cvec, *, T, C, name, into=None):
    tm = _tile(T, 512, HALO)
    ch = min(CONV_CHUNK, tm)

    def fn(i, n, t, p, nx, c, o, acc, scr):
        ext, rot = scr
        _stage_glu(i, t, p, ext, rot, tm, C)
        w, vec = c[0][...], c[1][...]

        def chunk(ci, carry):
            r0 = pl.multiple_of(ci * ch, ch)
            yh, _ = _conv_norm(rot, w, vec[0:1], r0, ch)
            zz = yh * vec[1:2] + vec[2:3]
            o[0][pl.ds(r0, ch), :] = (zz * _sigmoid(zz)).astype(BF16)
            return carry

        lax.fori_loop(0, tm // ch, chunk, 0)

    return _rowwise(fn, T=T, tm=tm, name=name, tiled=[cvg], prev=[cvg], consts=[cw, cvec],
                    out_tiled=[(C, BF16)], scratch=_conv_scratch(tm, C), by_ref=True, into=into)[0]


def _conv_bwd1(cvg, duc, cw, cvec, *, T, C, name, comm=None):
    tm = _tile(T, 512, HALO)
    ch = min(CONV_CHUNK, tm)

    def fn(i, n, t, p, nx, c, o, acc, scr):
        ext, rot, w8 = scr

        @pl.when(i == 0)
        def _():
            w8[...] = jnp.zeros_like(w8)

        _stage_glu(i, t, p, ext, rot, tm, C)
        w, vec = c[0][...], c[1][...]
        ln_g = vec[1:2]

        def chunk(ci, carry):
            r0 = pl.multiple_of(ci * ch, ch)
            yh, rstd = _conv_norm(rot, w, vec[0:1], r0, ch)
            zz = yh * ln_g + vec[2:3]
            s = _sigmoid(zz)
            dz = t[1][pl.ds(r0, ch), :] * (s * (1.0 + zz * (1.0 - s)))
            dyh = dz * ln_g
            dyc = rstd * (dyh - _rowmean(dyh) - yh * _rowmean(dyh * yh))
            o[0][pl.ds(r0, ch), :] = dyc
            acc[0][0:1, :] += _colsum(dyc)
            acc[0][1:2, :] += _colsum(dz * yh)
            acc[0][2:3, :] += _colsum(dz)
            for k in range(CONV_KERNEL):
                a, r = divmod(_causal_off(k), SUBLANES)
                prod = dyc * rot[r, pl.ds(pl.multiple_of(r0 + a * SUBLANES, SUBLANES), ch), :]
                part = prod[0:SUBLANES]
                for g in range(1, ch // SUBLANES):
                    part = part + prod[g * SUBLANES:(g + 1) * SUBLANES]
                w8[pl.ds(k * SUBLANES, SUBLANES), :] += part
            return carry

        lax.fori_loop(0, tm // ch, chunk, 0)

        @pl.when(i == n - 1)
        def _():
            for k in range(CONV_KERNEL):
                acc[1][k:k + 1, :] = _colsum(w8[pl.ds(k * SUBLANES, SUBLANES), :])

    return _rowwise(fn, T=T, tm=tm, name=name, tiled=[cvg, duc], prev=[cvg], consts=[cw, cvec],
                    out_tiled=[(C, F32)], out_acc=[(8, C), (HALO, C)],
                    scratch=_conv_scratch(tm, C) + [pltpu.VMEM((HALO * SUBLANES, C), F32)], by_ref=True, comm=comm)


def _conv_bwd2(dyc, cvg, cw, *, T, C, name):
    tm = _tile(T, 512, HALO)
    ch = min(CONV_CHUNK, tm)

    def fn(i, n, t, p, nx, c, o, acc, scr):
        ext, rot = scr
        ext[pl.ds(0, tm), :] = t[0][...]
        ext[pl.ds(tm, HALO), :] = jnp.where(i == n - 1, 0.0, nx[0][...])
        _fill_rotations(ext, rot, tm + HALO - SUBLANES)
        w = c[0][...]

        def chunk(ci, carry):
            r0 = pl.multiple_of(ci * ch, ch)
            dug = _conv_taps(rot, w, r0, ch, lambda k: (CONV_KERNEL - 1) - k)
            cv = t[1][pl.ds(r0, ch), pl.ds(0, C)]
            s = _sigmoid(t[1][pl.ds(r0, ch), pl.ds(C, C)])
            o[0][pl.ds(r0, ch), :] = (dug * s).astype(BF16)
            o[1][pl.ds(r0, ch), :] = (dug * cv * (s * (1.0 - s))).astype(BF16)
            return carry

        lax.fori_loop(0, tm // ch, chunk, 0)

    return _rowwise(fn, T=T, tm=tm, name=name, tiled=[dyc, cvg], nxt=[dyc], consts=[cw],
                    out_tiled=[(C, BF16), (C, BF16)],
                    scratch=[pltpu.VMEM((tm + HALO, C), F32), pltpu.VMEM((SUBLANES, tm + HALO - SUBLANES, C), F32)],
                    by_ref=True)


def _split(v):
    hi = v.astype(BF16)
    return hi, (v - hi.astype(F32)).astype(BF16)


def _dot2(v, m):
    hi, lo = _split(v)
    return jnp.dot(hi, m, preferred_element_type=F32) + jnp.dot(lo, m, preferred_element_type=F32)


def _log_gap(z):
    return -(jnp.maximum(z, 0.0) + jnp.log(1.0 + jnp.exp(-jnp.abs(z))))


def _head_masks():
    lane = lax.broadcasted_iota(jnp.int32, (1, LANES), 1)
    return lane < HEAD_DIM, lane >= HEAD_DIM


LOG_WEIGHT_FLOOR = -110.0
ATTN_BLOCK = 256


def _key_norm_bound(k_ref, masks, T):
    ch = _tile(T, 512)

    def chunk(r, m):
        kk = k_ref[pl.ds(pl.multiple_of(r * ch, ch), ch), :].astype(F32)
        k2 = kk * kk
        return tuple(jnp.maximum(m[h], jnp.max(jnp.sum(jnp.where(masks[h], k2, 0.0), -1, keepdims=True),
                                               axis=0, keepdims=True)) for h in (0, 1))

    m0, m1 = lax.fori_loop(0, T // ch, chunk, (jnp.zeros((1, 1), F32), jnp.zeros((1, 1), F32)))
    row = lax.broadcasted_iota(jnp.int32, (8, LANES), 0)
    return jnp.where(row == 0, jnp.sqrt(m0), jnp.sqrt(m1))


def _score_bound(qh, kn):
    qf = qh.astype(F32)
    return jnp.sqrt(jnp.sum(qf * qf, -1, keepdims=True)) * (kn * 1.01) + 0.01


def _some_weight_left(carries, bounds):
    m = jnp.maximum(jnp.max(carries[0] + bounds[0]), jnp.max(carries[1] + bounds[1]))
    return m > LOG_WEIGHT_FLOOR


def _attn_fwd(qkv, g_attn, *, T, AW, a_cols, name, comm=None):
    P = AW // LANES
    tq = _tile(T, 2 * ATTN_BLOCK)
    tb = tq // 2
    nq = T // tq
    scale = HEAD_DIM ** -0.5
    ci_specs, co_specs, co_shapes, csems = _comm_specs(comm)
    nci, nco = len(ci_specs), len(co_specs)

    def body(q_ref, k_ref, v_ref, g_ref, *rest):
        o_ref, a_ref, kn_ref = rest[nci], rest[nci + 1], rest[nci + 2 + nco]
        at_entry, at_exit = _grid2_hooks(comm, P, nq, (rest[:nci], rest[nci + 2:nci + 2 + nco], rest[nci + 3 + nco:]))
        at_entry()
        i = pl.program_id(1)
        lo_mask, hi_mask = masks = _head_masks()

        @pl.when(i == 0)
        def _():
            kn_ref[...] = _key_norm_bound(k_ref, masks, T)

        rows = lax.broadcasted_iota(jnp.int32, (tb, tb), 0)
        cols = lax.broadcasted_iota(jnp.int32, (tb, tb), 1)
        strict = cols < rows
        everywhere = cols >= 0
        tri = jnp.where(rows >= cols, 1.0, 0.0).astype(BF16)
        qhs, zbs = [], []
        for part in (0, 1):
            q = q_ref[pl.ds(part * tb, tb), :]
            qhs.append([jnp.where(m, q, jnp.zeros_like(q)) * jnp.asarray(scale, BF16) for m in masks])
            zbs.append([_score_bound(qhs[part][h], kn_ref[h:h + 1, 0:1]) for h in (0, 1)])

        def block(kb, part, carry, mask=None):
            st = pl.multiple_of(kb * tb, tb)
            kj = k_ref[pl.ds(st, tb), :]
            vj = v_ref[pl.ds(st, tb), :]
            new = []
            for h in (0, 1):
                acc, c = carry[h]
                z = lax.dot_general(qhs[part][h], kj, _DIMS['nt'], preferred_element_type=F32)
                l = _log_gap(z)
                if mask is not None:
                    l = jnp.where(mask, l, 0.0)
                cum = _dot2(l, tri)
                w = jnp.exp(z + cum + c)
                if mask is not None:
                    w = jnp.where(mask, w, 0.0)
                new.append((acc + _dot2(w, vj), c + cum[:, 0:1]))
            return tuple(new)

        zero = (jnp.zeros((tb, LANES), F32), jnp.zeros((tb, 1), F32))
        carries = []
        for part in (0, 1):
            kb0 = 2 * i + part
            cr = block(kb0, part, (zero, zero), strict)
            cr = block(jnp.maximum(kb0 - 1, 0), part, cr, jnp.logical_and(i > 0, everywhere) if part == 0 else None)
            carries.append(cr)

        def live(st):
            jj, ca, cb = st
            return jnp.logical_and(jj < 2 * i, jnp.logical_or(
                _some_weight_left([ca[0][1], ca[1][1]], zbs[0]), _some_weight_left([cb[0][1], cb[1][1]], zbs[1])))

        def more(st):
            jj, ca, cb = st
            ka = 2 * i - 2 - jj
            ca = block(jnp.maximum(ka, 0), 0, ca, jnp.logical_and(ka >= 0, everywhere))
            cb = block(ka + 1, 1, cb)
            return jj + 1, ca, cb

        _, ca, cb = lax.while_loop(live, more, (jnp.int32(0), carries[0], carries[1]))
        o = jnp.concatenate([jnp.where(lo_mask, c2[0][0], c2[1][0]) for c2 in (ca, cb)], axis=0)
        o2 = o * o
        r0 = lax.rsqrt(jnp.sum(jnp.where(lo_mask, o2, 0.0), -1, keepdims=True) * (1.0 / HEAD_DIM) + RMS_EPS)
        r1 = lax.rsqrt(jnp.sum(jnp.where(hi_mask, o2, 0.0), -1, keepdims=True) * (1.0 / HEAD_DIM) + RMS_EPS)
        o_ref[...] = o
        a_ref[...] = ((o * jnp.where(lo_mask, r0, r1)) * g_ref[...]).astype(BF16)
        at_exit()

    res = pl.pallas_call(
        body, grid=(P, nq),
        in_specs=[pl.BlockSpec((tq, LANES), lambda p, i: (i, p)),
                  pl.BlockSpec((T, LANES), lambda p, i: (0, P + p)),
                  pl.BlockSpec((T, LANES), lambda p, i: (0, 2 * P + p)),
                  pl.BlockSpec((1, LANES), lambda p, i: (0, p))] + ci_specs,
        out_specs=[pl.BlockSpec((tq, LANES), lambda p, i: (i, p)),
                   pl.BlockSpec((tq, LANES), lambda p, i: (i, p))] + co_specs,
        out_shape=[jax.ShapeDtypeStruct((T, AW), F32), jax.ShapeDtypeStruct((T, a_cols), BF16)] + co_shapes,
        scratch_shapes=[pltpu.VMEM((8, LANES), F32)] + csems,
        compiler_params=_cparams(("parallel", "arbitrary") if comm is None else ("arbitrary", "arbitrary")),
        name=name)(qkv, qkv, qkv, g_attn, *([] if comm is None else comm.ins))
    return (res[0], res[1]) if comm is None else (res[0], res[1], res[2:])


def _attn_bwd(qkv, o, da, g_attn, *, T, AW, name):
    P = AW // LANES
    tq = _tile(T, 2 * ATTN_BLOCK)
    tb = tq // 2
    nq, nb = T // tq, T // tb
    scale = HEAD_DIM ** -0.5

    def body(q_ref, k_ref, v_ref, o_ref, da_ref, g_ref, dq_ref, dk_out, dv_out, dg_ref, kn_ref, dk_ref, dv_ref):
        i = pl.program_id(1)
        lo_mask, hi_mask = masks = _head_masks()

        @pl.when(i == 0)
        def _():
            dk_ref[...] = jnp.zeros_like(dk_ref)
            dv_ref[...] = jnp.zeros_like(dv_ref)
            dg_ref[...] = jnp.zeros_like(dg_ref)
            kn_ref[...] = _key_norm_bound(k_ref, masks, T)

        rows = lax.broadcasted_iota(jnp.int32, (tb, tb), 0)
        cols = lax.broadcasted_iota(jnp.int32, (tb, tb), 1)
        strict = cols < rows
        everywhere = cols >= 0
        tri = jnp.where(rows >= cols, 1.0, 0.0).astype(BF16)
        tri_s = jnp.where(rows > cols, 1.0, 0.0).astype(BF16)
        o_all = o_ref[...]
        da = da_ref[...]
        g = g_ref[...]
        o2 = o_all * o_all
        r0 = lax.rsqrt(jnp.sum(jnp.where(lo_mask, o2, 0.0), -1, keepdims=True) * (1.0 / HEAD_DIM) + RMS_EPS)
        r1 = lax.rsqrt(jnp.sum(jnp.where(hi_mask, o2, 0.0), -1, keepdims=True) * (1.0 / HEAD_DIM) + RMS_EPS)
        r = jnp.where(lo_mask, r0, r1)
        oh = o_all * r
        gy = da * g
        gyo = gy * oh
        m0 = jnp.sum(jnp.where(lo_mask, gyo, 0.0), -1, keepdims=True) * (1.0 / HEAD_DIM)
        m1 = jnp.sum(jnp.where(hi_mask, gyo, 0.0), -1, keepdims=True) * (1.0 / HEAD_DIM)
        do_all = r * (gy - oh * jnp.where(lo_mask, m0, m1))
        dg_ref[...] += _colsum(da * oh)

        qhs, zbs, do_bs, deltas, q_ts, do_ts = [], [], [], [], [], []
        for part in (0, 1):
            q = q_ref[pl.ds(part * tb, tb), :]
            o = o_all[part * tb:(part + 1) * tb]
            do = do_all[part * tb:(part + 1) * tb]
            qhs.append([jnp.where(m, q, jnp.zeros_like(q)) * jnp.asarray(scale, BF16) for m in masks])
            zbs.append([_score_bound(qhs[part][h], kn_ref[h:h + 1, 0:1]) for h in (0, 1)])
            do_bs.append([jnp.where(m, do, 0.0).astype(BF16) for m in masks])
            deltas.append([jnp.sum(d.astype(F32) * o, -1, keepdims=True) for d in do_bs[part]])
            q_ts.append([qh.astype(F32).T.astype(BF16) for qh in qhs[part]])
            do_ts.append([d.astype(F32).T.astype(BF16) for d in do_bs[part]])

        def block(kb, part, carry, mask=None):
            masked = mask is not None
            st = pl.multiple_of(kb * tb, tb)
            kj = k_ref[pl.ds(st, tb), :]
            vj = v_ref[pl.ds(st, tb), :]
            new = []
            dk = dv = None
            for h in (0, 1):
                dq, c, gsum = carry[h]
                z = lax.dot_general(qhs[part][h], kj, _DIMS['nt'], preferred_element_type=F32)
                l = _log_gap(z)
                sig = jnp.exp(z + l)
                if masked:
                    l = jnp.where(mask, l, 0.0)
                cum = _dot2(l, tri)
                w = jnp.exp(z + cum + c)
                if masked:
                    w = jnp.where(mask, w, 0.0)
                dp = lax.dot_general(do_bs[part][h], vj, _DIMS['nt'], preferred_element_type=F32)
                pw = w * dp
                after = _dot2(pw, tri_s)
                dz = pw - sig * (deltas[part][h] - gsum - after)
                if masked:
                    dz = jnp.where(mask, dz, 0.0)
                dz_b = dz.astype(BF16)
                dk_h = jnp.dot(q_ts[part][h], dz_b, preferred_element_type=F32)
                dv_h = jnp.dot(do_ts[part][h], w.astype(BF16), preferred_element_type=F32)
                dk = dk_h if dk is None else dk + dk_h
                dv = dv_h if dv is None else dv + dv_h
                dq = dq + jnp.dot(dz_b, kj, preferred_element_type=F32)
                new.append((dq, c + cum[:, 0:1], gsum + (after[:, 0:1] + pw[:, 0:1])))
            dk_ref[kb] += dk
            dv_ref[kb] += dv
            return tuple(new)

        zero1 = jnp.zeros((tb, 1), F32)
        zero = (jnp.zeros((tb, LANES), F32), zero1, zero1)
        carries = []
        for part in (0, 1):
            kb0 = 2 * i + part
            cr = block(kb0, part, (zero, zero), strict)
            cr = block(jnp.maximum(kb0 - 1, 0), part, cr, jnp.logical_and(i > 0, everywhere) if part == 0 else None)
            carries.append(cr)

        def live(st):
            jj, ca, cb = st
            return jnp.logical_and(jj < 2 * i, jnp.logical_or(
                _some_weight_left([ca[0][1], ca[1][1]], zbs[0]), _some_weight_left([cb[0][1], cb[1][1]], zbs[1])))

        def more(st):
            jj, ca, cb = st
            ka = 2 * i - 2 - jj
            ca = block(jnp.maximum(ka, 0), 0, ca, jnp.logical_and(ka >= 0, everywhere))
            cb = block(ka + 1, 1, cb)
            return jj + 1, ca, cb

        _, ca, cb = lax.while_loop(live, more, (jnp.int32(0), carries[0], carries[1]))
        dq_ref[...] = (jnp.concatenate([jnp.where(lo_mask, c2[0][0], c2[1][0]) for c2 in (ca, cb)], axis=0)
                       * scale).astype(BF16)

        @pl.when(i == nq - 1)
        def _():
            def turn(j, carry_):
                st = pl.multiple_of(j * tb, tb)
                dk_out[pl.ds(st, tb), :] = dk_ref[j].T.astype(BF16)
                dv_out[pl.ds(st, tb), :] = dv_ref[j].T.astype(BF16)
                return carry_

            lax.fori_loop(0, nb, turn, 0)

    return pl.pallas_call(
        body, grid=(P, nq),
        in_specs=[pl.BlockSpec((tq, LANES), lambda p, i: (i, p)),
                  pl.BlockSpec((T, LANES), lambda p, i: (0, P + p)),
                  pl.BlockSpec((T, LANES), lambda p, i: (0, 2 * P + p)),
                  pl.BlockSpec((tq, LANES), lambda p, i: (i, p)),
                  pl.BlockSpec((tq, LANES), lambda p, i: (i, p)),
                  pl.BlockSpec((1, LANES), lambda p, i: (0, p))],
        out_specs=[pl.BlockSpec((tq, LANES), lambda p, i: (i, p)),
                   pl.BlockSpec((T, LANES), lambda p, i: (0, p)),
                   pl.BlockSpec((T, LANES), lambda p, i: (0, p)),
                   pl.BlockSpec((1, LANES), lambda p, i: (0, p))],
        out_shape=[jax.ShapeDtypeStruct((T, AW), BF16)] * 3 + [jax.ShapeDtypeStruct((1, AW), F32)],
        scratch_shapes=[pltpu.VMEM((8, LANES), F32), pltpu.VMEM((nb, LANES, tb), F32),
                        pltpu.VMEM((nb, LANES, tb), F32)],
        compiler_params=_cparams(("parallel", "arbitrary")), name=name)(qkv, qkv, qkv, o, da, g_attn)


def _ada_fwd(c_all, w_ada, b_ada, *, name):
    def body(c_ref, w_ref, b_ref, o_ref):
        cv = c_ref[...]
        sc = cv * _sigmoid(cv)
        o_ref[...] = jnp.dot(sc, w_ref[...], preferred_element_type=F32,
                             precision=lax.Precision.HIGHEST) + b_ref[...]

    return pl.pallas_call(body, out_shape=jax.ShapeDtypeStruct((c_all.shape[0], w_ada.shape[1]), F32),
                          compiler_params=_cparams(), name=name)(c_all, w_ada, b_ada)


def _ada_bwd(c_all_t, dmod, *, name):
    def body(c_ref, d_ref, o_ref):
        cv = c_ref[...]
        sc = cv * _sigmoid(cv)
        o_ref[...] = jnp.dot(sc, d_ref[...], preferred_element_type=F32, precision=lax.Precision.HIGHEST)

    return pl.pallas_call(body, out_shape=jax.ShapeDtypeStruct((c_all_t.shape[0], dmod.shape[1]), F32),
                          compiler_params=_cparams(), name=name)(c_all_t, dmod)


def _adamw_update(w_ref, g_ref, m_ref, v_ref, d_ref, nm_ref, nv_ref):
    gv = g_ref[...]
    m2 = ADAM_B1 * m_ref[...] + (1.0 - ADAM_B1) * gv
    v2 = ADAM_B2 * v_ref[...] + (1.0 - ADAM_B2) * jnp.square(gv)
    m_hat = m2 / (1.0 - ADAM_B1 ** ADAM_STEP)
    v_hat = v2 / (1.0 - ADAM_B2 ** ADAM_STEP)
    d_ref[...] = -ADAM_LR * (m_hat / (jnp.sqrt(v_hat) + ADAM_EPS) + ADAM_WD * w_ref[...])
    nm_ref[...] = m2
    nv_ref[...] = v2


def _adamw_many(wgmv, *, name):
    n = len(wgmv[0])

    def body(*refs):
        ins, outs = refs[:4 * n], refs[4 * n:]
        for k in range(n):
            _adamw_update(ins[k], ins[n + k], ins[2 * n + k], ins[3 * n + k], *outs[3 * k:3 * k + 3])

    return pl.pallas_call(
        body, out_shape=[jax.ShapeDtypeStruct(w.shape, F32) for w in wgmv[0] for _ in range(3)],
        compiler_params=_cparams(), name=name)(*wgmv[0], *wgmv[1], *wgmv[2], *wgmv[3])


def _adamw(w, g, m, v, *, name):
    R, C = w.shape
    tr = _tile(R, max(8, (1 << 18) // C), 8)
    body = functools.partial(_adamw_update)

    spec = pl.BlockSpec((tr, C), lambda i: (i, 0))
    return pl.pallas_call(
        body, grid=(R // tr,), in_specs=[spec] * 4, out_specs=[spec] * 3,
        out_shape=[jax.ShapeDtypeStruct((R, C), F32)] * 3,
        compiler_params=_cparams(("parallel",)), name=name)(w, g, m, v)


def _sum_devices(a, *, name):
    def body(a_ref, o_ref):
        s = a_ref[0]
        for d in range(1, a_ref.shape[0]):
            s = s + a_ref[d]
        o_ref[...] = s

    return pl.pallas_call(body, out_shape=jax.ShapeDtypeStruct(a.shape[1:], F32),
                          compiler_params=_cparams(), name=name)(a)


def _place():
    return lax.axis_index("x"), lax.axis_index("y"), lax.axis_index("c")


def _flip(v, bit):
    return 1 - v if bit else v


def _allgather8(blk, *, name):
    R, C = blk.shape

    def body(x_ref, out_ref, send_sems, recv_sems):
        x, y, c = _place()
        me = 4 * x + 2 * y + c
        out_ref[me] = x_ref[...]
        copies = []
        for k in range(1, 8):
            peer = (_flip(x, (k >> 2) & 1), _flip(y, (k >> 1) & 1), _flip(c, k & 1))
            cp = pltpu.make_async_remote_copy(
                src_ref=x_ref, dst_ref=out_ref.at[me], send_sem=send_sems.at[k - 1],
                recv_sem=recv_sems.at[k - 1], device_id=peer, device_id_type=MESH)
            cp.start()
            copies.append(cp)
        for cp in copies:
            cp.wait()

    return pl.pallas_call(
        body, out_shape=jax.ShapeDtypeStruct((8, R, C), F32),
        in_specs=[pl.BlockSpec(memory_space=pltpu.VMEM)], out_specs=pl.BlockSpec(memory_space=pltpu.VMEM),
        scratch_shapes=[pltpu.SemaphoreType.DMA((7,)), pltpu.SemaphoreType.DMA((7,))],
        compiler_params=_cparams(), name=name)(blk)


def _aligned(v, m):
    return v if isinstance(v, int) else pl.multiple_of(v, m)


def _rows_half(ref, half):
    n = ref.shape[0] // 2
    return ref.at[pl.ds(_aligned(half * n, 16), n)]


def _region(ref, kind, slot, half):
    if kind == 'col':
        n, cs = ref.shape[0] // 2, ref.shape[1] // 4
        return ref.at[pl.ds(_aligned(half * n, 16), n), pl.ds(_aligned(slot * cs, LANES), cs)]
    rs = ref.shape[0] // 4
    return ref.at[pl.ds(_aligned(slot * rs + half * (rs // 2), 16), rs // 2)]


def _other_chips(x, y):
    return [(1 - x, y), (x, 1 - y), (1 - x, 1 - y)]


class _Comm:
    def __init__(self, ins, outs, sems, start, finish):
        self.ins, self.outs, self.sems, self.start, self.finish = list(ins), list(outs), list(sems), start, finish


def _comm_specs(comm):
    if comm is None:
        return [], [], [], []
    anyspec = pl.BlockSpec(memory_space=pl.ANY)
    return [anyspec] * len(comm.ins), [anyspec] * len(comm.outs), list(comm.outs), list(comm.sems)


def _comm_hooks(comm, first, last, refs):
    if comm is None:
        return (lambda: None), (lambda: None)

    def at_entry():
        pl.when(first)(lambda: comm.start(*refs))

    def at_exit():
        pl.when(last)(lambda: comm.finish(*refs))

    return at_entry, at_exit


def _comm_alone(comm, *, name):
    ni, no = len(comm.ins), len(comm.outs)

    def body(*refs):
        parts = (refs[:ni], refs[ni:ni + no], refs[ni + no:])
        comm.start(*parts)
        comm.finish(*parts)

    i_specs, o_specs, o_shapes, sems = _comm_specs(comm)
    return pl.pallas_call(body, out_shape=o_shapes, in_specs=i_specs, out_specs=o_specs, scratch_shapes=sems,
                          compiler_params=_cparams(), name=name)(*comm.ins)


def _gather_comm(shards, kinds):
    nw = len(shards)
    full_shapes = []
    for s, kind in zip(shards, kinds):
        full_shapes.append((s.shape[0], 4 * s.shape[1]) if kind == 'col' else (4 * s.shape[0], s.shape[1]))

    def copies(sh, full, sems):
        lsem, ssem, rsem, fssem, frsem = sems
        x, y, c = _place()
        me_slot = 2 * x + y
        chips = _other_chips(x, y)
        local, ici, landed, fwd, passed = [], [], [], [], []
        for w in range(nw):
            for h in (0, 1):
                local.append(pltpu.make_async_copy(_rows_half(sh[w], h), _region(full[w], kinds[w], me_slot, h),
                                                   lsem.at[w, h]))
            for r, (px, py) in enumerate(chips):
                ici.append(pltpu.make_async_remote_copy(
                    src_ref=_rows_half(sh[w], c), dst_ref=_region(full[w], kinds[w], me_slot, c),
                    send_sem=ssem.at[w, r], recv_sem=rsem.at[w, r], device_id=(px, py, c), device_id_type=MESH))
                mine = _region(full[w], kinds[w], 2 * px + py, c)
                landed.append(pltpu.make_async_remote_copy(
                    src_ref=mine, dst_ref=mine, send_sem=ssem.at[w, r], recv_sem=rsem.at[w, r],
                    device_id=(px, py, c), device_id_type=MESH))
                fwd.append(pltpu.make_async_remote_copy(
                    src_ref=mine, dst_ref=mine, send_sem=fssem.at[w, r], recv_sem=frsem.at[w, r],
                    device_id=(x, y, 1 - c), device_id_type=MESH))
                theirs = _region(full[w], kinds[w], 2 * px + py, 1 - c)
                passed.append(pltpu.make_async_remote_copy(
                    src_ref=theirs, dst_ref=theirs, send_sem=fssem.at[w, r], recv_sem=frsem.at[w, r],
                    device_id=(x, y, 1 - c), device_id_type=MESH))
        return local, ici, landed, fwd, passed

    def start(sh, full, sems):
        local, ici, _, _, _ = copies(sh, full, sems)
        for cp in local + ici:
            cp.start()

    def finish(sh, full, sems):
        local, ici, landed, fwd, passed = copies(sh, full, sems)
        for got, cp in zip(landed, fwd):
            got.wait_recv()
            cp.start()
        for got in passed:
            got.wait_recv()
        for cp in ici + fwd:
            cp.wait_send()
        for cp in local:
            cp.wait()

    return _Comm(shards, [jax.ShapeDtypeStruct(s, BF16) for s in full_shapes],
                 [pltpu.SemaphoreType.DMA((nw, 2))] + [pltpu.SemaphoreType.DMA((nw, 3))] * 4, start, finish)


def _exchange_comm(grads, kinds):
    nw = len(grads)

    def copies(g, r1, sems):
        ssem, rsem = sems
        x, y, c = _place()
        out, back = [], []
        for w in range(nw):
            for slot in range(4):
                out.append(pltpu.make_async_remote_copy(
                    src_ref=_region(g[w], kinds[w], slot, 1 - c), dst_ref=_region(r1[w], kinds[w], slot, 1 - c),
                    send_sem=ssem.at[w, slot], recv_sem=rsem.at[w, slot], device_id=(x, y, 1 - c),
                    device_id_type=MESH))
                mine = _region(r1[w], kinds[w], slot, c)
                back.append(pltpu.make_async_remote_copy(
                    src_ref=mine, dst_ref=mine, send_sem=ssem.at[w, slot], recv_sem=rsem.at[w, slot],
                    device_id=(x, y, 1 - c), device_id_type=MESH))
        return out, back

    def start(g, r1, sems):
        for cp in copies(g, r1, sems)[0]:
            cp.start()

    def finish(g, r1, sems):
        out, back = copies(g, r1, sems)
        for got in back:
            got.wait_recv()
        for cp in out:
            cp.wait_send()

    return _Comm(grads, [jax.ShapeDtypeStruct(g.shape, F32) for g in grads],
                 [pltpu.SemaphoreType.DMA((nw, 4))] * 2, start, finish)


def _add_core_halves(g, r1, place, kind, *, name):
    if kind == 'col':
        n, cs = g.shape[0] // 2, g.shape[1] // 4
        tr = _tile(n, 256, 16)
        nt = n // tr
        ispec = pl.BlockSpec((tr, cs), lambda s, t, pr: (pr[0] * nt + t, s))
    else:
        rs, cs = g.shape[0] // 4, g.shape[1]
        n = rs // 2
        tr, nt = n, 1
        ispec = pl.BlockSpec((tr, cs), lambda s, t, pr: (s * 2 + pr[0], 0))

    def body(pr, a_ref, b_ref, o_ref):
        o_ref[...] = (a_ref[...] + b_ref[...]).astype(BF16)

    return pl.pallas_call(
        body,
        grid_spec=pltpu.PrefetchScalarGridSpec(
            num_scalar_prefetch=1, grid=(4, nt), in_specs=[ispec, ispec],
            out_specs=pl.BlockSpec((None, tr, cs), lambda s, t, pr: (s, t, 0))),
        out_shape=jax.ShapeDtypeStruct((4, n, cs), BF16),
        compiler_params=_cparams(("parallel", "parallel")), name=name)(place, g, r1)


def _scatter_comm(hs):
    nw = len(hs)

    def copies(h, r2, sems):
        ssem, rsem = sems
        x, y, c = _place()
        return [pltpu.make_async_remote_copy(
            src_ref=h[w].at[2 * px + py], dst_ref=r2[w].at[r], send_sem=ssem.at[w, r],
            recv_sem=rsem.at[w, r], device_id=(px, py, c), device_id_type=MESH)
            for w in range(nw) for r, (px, py) in enumerate(_other_chips(x, y))]

    def start(h, r2, sems):
        for cp in copies(h, r2, sems):
            cp.start()

    def finish(h, r2, sems):
        for cp in copies(h, r2, sems):
            cp.wait()

    return _Comm(hs, [jax.ShapeDtypeStruct((3,) + a.shape[1:], a.dtype) for a in hs],
                 [pltpu.SemaphoreType.DMA((nw, 3))] * 2, start, finish)


def _sum_owner(hs, r2, place, *, name):
    _, n, cs = hs.shape
    tr = _tile(n, 256, 16)
    nt = n // tr

    def body(pr, h_ref, r_ref, o_ref):
        o_ref[...] = ((h_ref[...].astype(F32) + r_ref[0].astype(F32)) + r_ref[1].astype(F32)) + r_ref[2].astype(F32)

    return pl.pallas_call(
        body,
        grid_spec=pltpu.PrefetchScalarGridSpec(
            num_scalar_prefetch=1, grid=(nt,),
            in_specs=[pl.BlockSpec((None, tr, cs), lambda t, pr: (pr[1], t, 0)),
                      pl.BlockSpec((3, tr, cs), lambda t, pr: (0, t, 0))],
            out_specs=pl.BlockSpec((None, tr, cs), lambda t, pr: (pr[0], t, 0))),
        out_shape=jax.ShapeDtypeStruct((2, n, cs), F32),
        compiler_params=_cparams(("parallel",)), name=name)(place, hs, r2)


def _share_with_sibling(fins, *, name):
    nw = len(fins)

    def body(*refs):
        fin, out = refs[:nw], refs[nw:2 * nw]
        ssem, rsem = refs[2 * nw:]
        x, y, c = _place()
        copies = []
        for w in range(nw):
            cp = pltpu.make_async_remote_copy(
                src_ref=fin[w].at[c], dst_ref=out[w].at[c], send_sem=ssem.at[w], recv_sem=rsem.at[w],
                device_id=(x, y, 1 - c), device_id_type=MESH)
            cp.start()
            copies.append(cp)
        for w in range(nw):
            theirs = out[w].at[1 - c]
            pltpu.make_async_remote_copy(
                src_ref=theirs, dst_ref=theirs, send_sem=ssem.at[w], recv_sem=rsem.at[w],
                device_id=(x, y, 1 - c), device_id_type=MESH).wait_recv()
        for cp in copies:
            cp.wait_send()

    anyspec = pl.BlockSpec(memory_space=pl.ANY)
    return pl.pallas_call(
        body, out_shape=[jax.ShapeDtypeStruct(a.shape, F32) for a in fins],
        in_specs=[anyspec] * nw, out_specs=[anyspec] * nw,
        input_output_aliases={w: w for w in range(nw)},
        scratch_shapes=[pltpu.SemaphoreType.DMA((nw,))] * 2,
        compiler_params=_cparams(), name=name)(*fins)


class _Plan:
    def __init__(self, wfull):
        self.w = dict(wfull)
        self.grads = {}

    def ffn_width(self):
        return self.w['ff1_w_out'].shape[0]

    def comm(self, site):
        return None

    def done(self, site, results):
        pass

    def ready(self, group, names, arrays):
        self.grads.update(zip(names, arrays))


def _riding(plan, site, call):
    comm = plan.comm(site)
    res = call(comm)
    if comm is None:
        return res
    *main, extra = res
    plan.done(site, extra)
    return main[0] if len(main) == 1 else tuple(main)


def _merge_comms(comms):
    if len(comms) == 1:
        return comms[0]

    def parts(refs, field):
        out, pos = [], 0
        for c in comms:
            n = len(getattr(c, field))
            out.append(refs[pos:pos + n])
            pos += n
        return out

    def run(which):
        def fn(ins, outs, sems):
            for c, i, o, s in zip(comms, parts(ins, 'ins'), parts(outs, 'outs'), parts(sems, 'sems')):
                getattr(c, which)(i, o, s)
        return fn

    return _Comm(sum((c.ins for c in comms), []), sum((c.outs for c in comms), []),
                 sum((c.sems for c in comms), []), run('start'), run('finish'))


class _DistPlan(_Plan):
    RIDES = {
        'pre_fwd_ff1': [('gather', ['ff1_w_in'])],
        'ffn_in_ff1': [('gather', ['ff1_w_out', 'w_in_mix', 'w_out_mix'])],
        'attn_fwd': [('gather', ['ff2_w_in', 'ff2_w_out'])],
        'pre_bwd_ff2': [('exchange', ['ff2_w_in', 'ff2_w_out'])],
        'conv_bwd1': [('scatter', ['ff2_w_in', 'ff2_w_out'])],
        'pre_bwd_mix': [('exchange', ['w_in_mix', 'w_out_mix'])],
        'ffn_dact_ff1': [('scatter', ['w_in_mix', 'w_out_mix'])],
        'dw_out_ff1': [('exchange', ['ff1_w_in'])],
        'dh_ff1': [('scatter', ['ff1_w_in']), ('exchange', ['ff1_w_out'])],
    }
    AFTER = [('scatter', ['ff1_w_out'])]

    def __init__(self, shards, place):
        self.shards, self.place, self.kind = shards, place, dict(BIG)
        self.w, self.grads, self.hs, self.fin = {}, {}, {}, {}

    def _make(self, kind, names):
        if kind == 'gather':
            return _gather_comm([self.shards[n] for n in names], [self.kind[n] for n in names])
        if kind == 'exchange':
            return _exchange_comm([self.grads[n] for n in names], [self.kind[n] for n in names])
        return _scatter_comm([self.hs[n] for n in names])

    def _take(self, kind, names, results):
        for n, r in zip(names, results):
            if kind == 'gather':
                self.w[n] = r
            elif kind == 'exchange':
                self.hs[n] = _add_core_halves(self.grads[n], r, self.place, self.kind[n], name=f"grad_core_add_{n}")
            else:
                self.fin[n] = _sum_owner(self.hs[n], r, self.place, name=f"grad_owner_sum_{n}")

    def ffn_width(self):
        return 4 * self.shards['ff1_w_out'].shape[0]

    def comm(self, site):
        rides = self.RIDES.get(site)
        return None if rides is None else _merge_comms([self._make(k, names) for k, names in rides])

    def done(self, site, results):
        pos = 0
        for kind, names in self.RIDES[site]:
            self._take(kind, names, results[pos:pos + len(names)])
            pos += len(names)

    def finish(self):
        for kind, names in self.AFTER:
            self._take(kind, names, _comm_alone(self._make(kind, names), name=f"grad_{kind}_{names[0]}"))
        names = list(self.shards)
        return dict(zip(names, _share_with_sibling([self.fin[n] for n in names], name="grad_share")))


def _local_step(x, target, mod, gains, plan, g_attn, conv_w, cvec):
    T, D = x.shape
    F = plan.ffn_width()
    AW = D // 2
    C = D - AW
    NQKV = 3 * AW
    MIX = NQKV + 2 * C
    tM = _tile(T, 1024)
    tkT = _tile(T, 1024)

    def ffn_fwd(h, tag):
        w_in = plan.w[f"{tag}_w_in"]
        jac, act = _riding(plan, f"ffn_in_{tag}",
                           lambda cm: _ffn_in(h, w_in, T=T, D=D, F=F, name=f"ffn_in_{tag}", comm=cm))
        f = _matmul(act, plan.w[f"{tag}_w_out"], mode='nn', M=T, N=D, K=F, tm=tM, tn=_tile(D, 1024), tk=F,
                    out_dtype=F32, name=f"ffn_out_{tag}")
        return h, jac, act, f

    def ffn_bwd(df, dout, xin, saved, s, tag, prev=None):
        h, jac, act, f = saved
        w_in, w_out = plan.w[f"{tag}_w_in"], plan.w[f"{tag}_w_out"]
        dgu = _riding(plan, f"ffn_dact_{tag}",
                      lambda cm: _ffn_dact(df, w_out, jac, T=T, D=D, F=F, name=f"ffn_dact_{tag}", comm=cm))
        tnf = _tile(F, 2816)
        nf = F // tnf
        dw_in = _matmul(h, dgu, mode='tn', M=D, N=2 * F, K=T, tm=_tile(D, 1024), tn=tnf, tk=tkT, out_dtype=F32,
                        b_spec=pl.BlockSpec((None, tkT, tnf), lambda i, j, k: (j // nf, k, j % nf)),
                        name=f"dw_in_{tag}")
        plan.ready(tag, [f"{tag}_w_in"], [dw_in])
        dw_out = _riding(plan, f"dw_out_{tag}", lambda cm: _matmul(
            act, df, mode='tn', M=F, N=D, K=T, tm=_tile(F, 1408), tn=_tile(D, 1024), tk=tkT, out_dtype=F32,
            name=f"dw_out_{tag}", comm=cm))
        plan.ready(tag, [f"{tag}_w_out"], [dw_out])
        dh = _riding(plan, f"dh_{tag}", lambda cm: _matmul(
            dgu, w_in, mode='nt', M=T, N=D, K=2 * F, tm=tM, tn=_tile(D, 1024), tk=F, out_dtype=F32,
            a_spec=pl.BlockSpec((None, tM, F), lambda i, j, k: (k, i, 0)), name=f"dh_{tag}", comm=cm))
        if prev is None:
            return _riding(plan, f"pre_bwd_{tag}", lambda cm: _pre_bwd(
                dh, xin, dout, gains, mod, T=T, s=s, name=f"pre_bwd_{tag}", comm=cm))
        return _riding(plan, f"pre_bwd_{tag}", lambda cm: _pre_post_bwd(
            dh, xin, dout, prev[0], gains, mod, T=T, s=s, res_w_prev=prev[1], name=f"pre_bwd_{tag}", comm=cm))

    h1 = _riding(plan, "pre_fwd_ff1", lambda cm: _pre_fwd(x, gains, mod, T=T, s=0, name="pre_fwd_ff1", comm=cm))[0]
    s1 = ffn_fwd(h1, "ff1")
    x1, h2 = _post_pre_fwd(x, s1[3], gains, mod, T=T, s=0, res_w=0.5, name="post_fwd_ff1")
    w_in_mix, w_out_mix = plan.w['w_in_mix'], plan.w['w_out_mix']
    tnq = _tile(AW, 512)
    qkv = _matmul(h2, w_in_mix, mode='nn', M=T, N=NQKV, K=D, tm=tM, tn=tnq, tk=D, out_dtype=BF16, name="proj_qkv")
    tnc = _tile(C, 512)
    off = NQKV // tnc
    cvg = _matmul(h2, w_in_mix, mode='nn', M=T, N=2 * C, K=D, tm=tM, tn=tnc, tk=D, out_dtype=F32,
                  b_spec=pl.BlockSpec((D, tnc), lambda i, j, k: (0, off + j)), name="proj_conv")
    o_attn, a_attn = _riding(plan, "attn_fwd", lambda cm: _attn_fwd(
        qkv, g_attn, T=T, AW=AW, a_cols=D, name="attn_fwd", comm=cm))
    mixcat = _conv_fwd(cvg, conv_w, cvec, T=T, C=C, name="conv_fwd", into=(a_attn, AW // C))
    f_mix = _matmul(mixcat, w_out_mix, mode='nn', M=T, N=D, K=D, tm=tM, tn=_tile(D, 1024), tk=D, out_dtype=F32,
                    name="mix_out")
    x2, h3 = _post_pre_fwd(x1, f_mix, gains, mod, T=T, s=1, res_w=1.0, name="post_fwd_mix")

    s3 = ffn_fwd(h3, "ff2")
    dout, df2, sq, dgate2, dgpost2 = _loss_post_bwd(x2, s3[3], target, gains, mod, T=T, s=2, res_w=0.5,
                                                    name="post_fwd_loss")
    dx2, df_mix, dshift2, dscale2, dgpre2, dgate_m, dgpost_m = ffn_bwd(
        df2, dout, x2, s3, 2, "ff2", prev=(f_mix, 1.0))
    dmixcat = _matmul(df_mix, w_out_mix, mode='nt', M=T, N=D, K=D, tm=tM, tn=_tile(D, 1024), tk=D, out_dtype=F32,
                      name="d_mixcat")
    dw_out_mix = _matmul(mixcat, df_mix, mode='tn', M=D, N=D, K=T, tm=_tile(D, 1024), tn=_tile(D, 1024),
                         tk=tkT, out_dtype=F32, name="dw_out_mix")
    dq, dk, dv, dg_attn = _attn_bwd(qkv, o_attn, dmixcat, g_attn, T=T, AW=AW, name="attn_bwd")
    dyc, csum, dconv_w = _riding(plan, "conv_bwd1", lambda cm: _conv_bwd1(
        cvg, (dmixcat, C, AW // C), conv_w, cvec, T=T, C=C, name="conv_bwd1", comm=cm))
    dcv, dcg = _conv_bwd2(dyc, cvg, conv_w, T=T, C=C, name="conv_bwd2")
    dproj = [dq, dk, dv, dcv, dcg]
    dh2 = _matmul(dproj, w_in_mix, mode='nt', M=T, N=D, K=MIX, tm=tM, tn=_tile(D, 1024), tk=MIX, out_dtype=F32,
                  name="dh_mix")
    dw_in_mix = _matmul(h2, dproj, mode='tn', M=D, N=MIX, K=T, tm=_tile(D, 1024), tn=MIX,
                        tk=tkT, out_dtype=F32, name="dw_in_mix")
    plan.ready("mix", ['w_in_mix', 'w_out_mix'], [dw_in_mix, dw_out_mix])
    dx1, df1, dshift_m, dscale_m, dgpre_m, dgate1, dgpost1 = _riding(plan, "pre_bwd_mix", lambda cm: _pre_post_bwd(
        dh2, x1, dx2, s1[3], gains, mod, T=T, s=1, res_w_prev=0.5, name="pre_bwd_mix", comm=cm))

    dx0, dshift1, dscale1, dgpre1 = ffn_bwd(df1, dx1, x, s1, 0, "ff1")

    dgains = [dgpre1, dgpost1, dgpre_m, dgpost_m, dgpre2, dgpost2]
    dmod = [dshift1, dscale1, dgate1, dshift_m, dscale_m, dgate_m, dshift2, dscale2, dgate2]
    return sq, dx0, dgains, dmod, dg_attn, csum, dconv_w


def _pack_rows(pieces, width):
    rows = jnp.concatenate([p.reshape(-1) for p in pieces]).reshape(-1, width)
    pad = (-rows.shape[0]) % 8
    return jnp.pad(rows, ((0, pad), (0, 0)))


def kernel(x, c, w_ada, b_ada, g_pre_ff1, g_post_ff1, ff1_w_in, ff1_w_out, g_pre_mix, g_post_mix, w_in_mix, g_attn_out, conv_w, conv_b, conv_ln_g, conv_ln_b, w_out_mix, g_pre_ff2, g_post_ff2, ff2_w_in, ff2_w_out, loss_target, m_w_ada, m_b_ada, m_g_pre_ff1, m_g_post_ff1, m_ff1_w_in, m_ff1_w_out, m_g_pre_mix, m_g_post_mix, m_w_in_mix, m_g_attn_out, m_conv_w, m_conv_b, m_conv_ln_g, m_conv_ln_b, m_w_out_mix, m_g_pre_ff2, m_g_post_ff2, m_ff2_w_in, m_ff2_w_out, v_w_ada, v_b_ada, v_g_pre_ff1, v_g_post_ff1, v_ff1_w_in, v_ff1_w_out, v_g_pre_mix, v_g_post_mix, v_w_in_mix, v_g_attn_out, v_conv_w, v_conv_b, v_conv_ln_g, v_conv_ln_b, v_w_out_mix, v_g_pre_ff2, v_g_post_ff2, v_ff2_w_in, v_ff2_w_out):
    W = dict(w_ada=w_ada, b_ada=b_ada, g_pre_ff1=g_pre_ff1, g_post_ff1=g_post_ff1, ff1_w_in=ff1_w_in,
             ff1_w_out=ff1_w_out, g_pre_mix=g_pre_mix, g_post_mix=g_post_mix, w_in_mix=w_in_mix,
             g_attn_out=g_attn_out, conv_w=conv_w, conv_b=conv_b, conv_ln_g=conv_ln_g, conv_ln_b=conv_ln_b,
             w_out_mix=w_out_mix, g_pre_ff2=g_pre_ff2, g_post_ff2=g_post_ff2, ff2_w_in=ff2_w_in,
             ff2_w_out=ff2_w_out)
    Mo = dict(w_ada=m_w_ada, b_ada=m_b_ada, g_pre_ff1=m_g_pre_ff1, g_post_ff1=m_g_post_ff1, ff1_w_in=m_ff1_w_in,
              ff1_w_out=m_ff1_w_out, g_pre_mix=m_g_pre_mix, g_post_mix=m_g_post_mix, w_in_mix=m_w_in_mix,
              g_attn_out=m_g_attn_out, conv_w=m_conv_w, conv_b=m_conv_b, conv_ln_g=m_conv_ln_g,
              conv_ln_b=m_conv_ln_b, w_out_mix=m_w_out_mix, g_pre_ff2=m_g_pre_ff2, g_post_ff2=m_g_post_ff2,
              ff2_w_in=m_ff2_w_in, ff2_w_out=m_ff2_w_out)
    Vo = dict(w_ada=v_w_ada, b_ada=v_b_ada, g_pre_ff1=v_g_pre_ff1, g_post_ff1=v_g_post_ff1, ff1_w_in=v_ff1_w_in,
              ff1_w_out=v_ff1_w_out, g_pre_mix=v_g_pre_mix, g_post_mix=v_g_post_mix, w_in_mix=v_w_in_mix,
              g_attn_out=v_g_attn_out, conv_w=v_conv_w, conv_b=v_conv_b, conv_ln_g=v_conv_ln_g,
              conv_ln_b=v_conv_ln_b, w_out_mix=v_w_out_mix, g_pre_ff2=v_g_pre_ff2, g_post_ff2=v_g_post_ff2,
              ff2_w_in=v_ff2_w_in, ff2_w_out=v_ff2_w_out)

    T, D = x.shape[1], x.shape[2]
    AW = D // 2
    C = D - AW
    xi, yi, ci = _place()
    me = 4 * xi + 2 * yi + ci
    chip = 2 * xi + yi
    place = jnp.stack([ci, chip]).astype(jnp.int32)

    c_all = _allgather8(jnp.tile(c, (8, 1)), name="gather_c")[:, 0, :]
    ncol = w_ada.shape[1]
    b_cols = lax.dynamic_index_in_dim(b_ada.reshape(4, ncol), chip, keepdims=True).reshape(1, ncol)
    modp = _ada_fwd(c_all, w_ada, b_cols, name="ada_fwd")
    mod_g = _allgather8(modp, name="gather_mod")
    mod_all = jnp.transpose(mod_g[0::2], (1, 0, 2)).reshape(8, 4 * ncol)
    mod = lax.dynamic_index_in_dim(mod_all, me, keepdims=False).reshape(9, D)

    names = [n for n, _ in BIG]
    plan = _DistPlan({n: W[n].astype(BF16) for n in names}, place)
    cs = conv_w.shape[1]
    cw_all = _allgather8(jnp.pad(conv_w, ((0, HALO - CONV_KERNEL), (0, (-cs) % LANES))), name="gather_conv_w")
    conv_w_full = jnp.transpose(cw_all[0::2, :, :cs], (1, 0, 2)).reshape(HALO, 4 * cs)

    gains = _pack_rows([g_pre_ff1, g_post_ff1, g_pre_mix, g_post_mix, g_pre_ff2, g_post_ff2], D)
    cvec = _pack_rows([conv_b, conv_ln_g, conv_ln_b], C)
    g_attn = g_attn_out.reshape(1, AW)

    sq, dx, dgains, dmod, dg_attn, csum, dconv_w = _local_step(
        x[0], loss_target[0], mod, gains, plan, g_attn, conv_w_full, cvec)

    loss_row = jnp.zeros((1, D), F32).at[0, 0].set(jnp.sum(sq) * (0.5 / D))
    small = _pack_rows(dgains + dmod + [dg_attn, csum[0:3], dconv_w, loss_row], D)
    small_all = _allgather8(small, name="gather_small")
    tot = _sum_devices(small_all, name="sum_small")
    n_g, n_m = 6, 9
    r0 = n_g + n_m
    flat = tot.reshape(-1)
    p = r0 * D
    g_attn_grad = flat[p:p + AW]
    p += AW
    gconv_b, gln_g, gln_b = flat[p:p + C], flat[p + C:p + 2 * C], flat[p + 2 * C:p + 3 * C]
    p += 3 * C
    gconv_w_full = flat[p:p + HALO * C].reshape(HALO, C)[:CONV_KERNEL]
    p += HALO * C
    loss = flat[p]
    gconv_w = lax.dynamic_slice_in_dim(gconv_w_full, chip * cs, cs, axis=1)
    grad_small = {'g_pre_ff1': tot[0], 'g_post_ff1': tot[1], 'g_pre_mix': tot[2], 'g_post_mix': tot[3],
                  'g_pre_ff2': tot[4], 'g_post_ff2': tot[5], 'b_ada': tot[n_g:r0].reshape(-1),
                  'g_attn_out': g_attn_grad.reshape(g_attn_out.shape), 'conv_w': gconv_w, 'conv_b': gconv_b,
                  'conv_ln_g': gln_g, 'conv_ln_b': gln_b}

    dmod_all = small_all[:, n_g:r0, :].reshape(8, 9 * D)
    dmod_cols = lax.dynamic_slice_in_dim(dmod_all, chip * ncol, ncol, axis=1)
    grad_w_ada = _ada_bwd(jnp.transpose(c_all), dmod_cols, name="ada_bwd")

    grads = dict(grad_small)
    grads['w_ada'] = grad_w_ada
    for n, a in plan.finish().items():
        grads[n] = a.reshape(W[n].shape)

    delta, new_m, new_v = {}, {}, {}
    for n in ['w_ada'] + names:
        delta[n], new_m[n], new_v[n] = _adamw(W[n], grads[n], Mo[n], Vo[n], name=f"adamw_{n}")
    smalls = [n for n in WEIGHTS if n not in delta]

    def as2d(a):
        return a if a.ndim == 2 else a.reshape(-1, LANES)

    outs = _adamw_many([[as2d(d[n]) for n in smalls] for d in (W, grads, Mo, Vo)], name="adamw_small")
    for k, n in enumerate(smalls):
        delta[n], new_m[n], new_v[n] = (o.reshape(W[n].shape) for o in outs[3 * k:3 * k + 3])

    return (loss, dx[None], *[grads[n] for n in WEIGHTS], *[delta[n] for n in WEIGHTS],
            *[new_m[n] for n in WEIGHTS], *[new_v[n] for n in WEIGHTS])
```

```python
import functools

import jax
import jax.numpy as jnp
from jax import lax
from jax.experimental import pallas as pl
from jax.experimental.pallas import tpu as pltpu

F32 = jnp.float32
BF16 = jnp.bfloat16
MESH = pl.DeviceIdType.MESH

HEAD_DIM = 64
CONV_KERNEL = 31
RMS_EPS = 1e-6
LN_EPS = 1e-5
ADAM_LR = 0.001
ADAM_B1 = 0.9
ADAM_B2 = 0.999
ADAM_EPS = 1e-08
ADAM_WD = 0.01
ADAM_STEP = 10

LANES = 128
HALO = 32
VMEM_LIMIT = 52 * 1024 * 1024

WEIGHTS = ['w_ada', 'b_ada', 'g_pre_ff1', 'g_post_ff1', 'ff1_w_in', 'ff1_w_out', 'g_pre_mix',
           'g_post_mix', 'w_in_mix', 'g_attn_out', 'conv_w', 'conv_b', 'conv_ln_g', 'conv_ln_b',
           'w_out_mix', 'g_pre_ff2', 'g_post_ff2', 'ff2_w_in', 'ff2_w_out']
BIG = [('ff1_w_in', 'col'), ('ff1_w_out', 'row'), ('w_in_mix', 'col'), ('w_out_mix', 'row'),
       ('ff2_w_in', 'col'), ('ff2_w_out', 'row')]


def _tile(dim, pref, mult=LANES):
    if dim <= pref:
        return dim
    best = None
    for t in range(mult, pref + 1, mult):
        if dim % t == 0:
            best = t
    assert best is not None, (dim, pref, mult)
    return best


def _cparams(sem=None):
    kw = dict(vmem_limit_bytes=VMEM_LIMIT)
    if sem is not None:
        kw['dimension_semantics'] = sem
    return pltpu.CompilerParams(**kw)


def _sigmoid(x):
    return 1.0 / (1.0 + jnp.exp(-x))


_DIMS = {'nn': (((1,), (0,)), ((), ())), 'nt': (((1,), (1,)), ((), ())), 'tn': (((0,), (0,)), ((), ()))}


def _matmul(a, b, *, mode, M, N, K, tm, tn, tk, out_dtype, name, a_spec=None, b_spec=None, comm=None):
    nm, nn, nk = M // tm, N // tn, K // tk
    assert nm * tm == M and nn * tn == N and nk * tk == K, (name, M, N, K, tm, tn, tk)
    a_list = list(a) if isinstance(a, (list, tuple)) else [a]
    b_list = list(b) if isinstance(b, (list, tuple)) else [b]
    if len(a_list) > 1:
        assert mode == 'nt' and tk == K and a_spec is None, name
        a_specs = [pl.BlockSpec((tm, p.shape[1]), lambda i, j, k: (i, 0)) for p in a_list]
    elif a_spec is None:
        a_specs = [pl.BlockSpec((tk, tm), lambda i, j, k: (k, i)) if mode == 'tn'
                   else pl.BlockSpec((tm, tk), lambda i, j, k: (i, k))]
    else:
        a_specs = [a_spec]
    if len(b_list) > 1:
        assert mode == 'tn' and tn == N and b_spec is None, name
        b_specs = [pl.BlockSpec((tk, p.shape[1]), lambda i, j, k: (k, 0)) for p in b_list]
    elif b_spec is None:
        b_specs = [pl.BlockSpec((tn, tk), lambda i, j, k: (j, k)) if mode == 'nt'
                   else pl.BlockSpec((tk, tn), lambda i, j, k: (k, j))]
    else:
        b_specs = [b_spec]
    na, nbb = len(a_list), len(b_list)
    dims = _DIMS[mode]
    assert nk == 1 or out_dtype == F32, name
    ci_specs, co_specs, co_shapes, csems = _comm_specs(comm)
    nci, nco = len(ci_specs), len(co_specs)

    def side_by_side(refs):
        return refs[0][...] if len(refs) == 1 else jnp.concatenate([r[...] for r in refs], axis=1)

    def body(*refs):
        a_refs, b_refs, rest = refs[:na], refs[na:na + nbb], refs[na + nbb:]
        o_ref = rest[nci]
        i, j, k = pl.program_id(0), pl.program_id(1), pl.program_id(2)
        first = jnp.logical_and(jnp.logical_and(i == 0, j == 0), k == 0)
        last = jnp.logical_and(jnp.logical_and(i == nm - 1, j == nn - 1), k == nk - 1)
        at_entry, at_exit = _comm_hooks(comm, first, last, (rest[:nci], rest[nci + 1:nci + 1 + nco], rest[nci + 1 + nco:]))
        at_entry()

        def prod():
            return lax.dot_general(side_by_side(a_refs), side_by_side(b_refs), dims, preferred_element_type=F32)

        if nk == 1:
            o_ref[...] = prod().astype(o_ref.dtype)
        else:
            @pl.when(k == 0)
            def _():
                o_ref[...] = prod()

            @pl.when(k > 0)
            def _():
                o_ref[...] += prod()
        at_exit()

    sem = ("parallel", "parallel", "arbitrary") if comm is None else ("arbitrary",) * 3
    res = pl.pallas_call(
        body, grid=(nm, nn, nk), in_specs=a_specs + b_specs + ci_specs,
        out_specs=[pl.BlockSpec((tm, tn), lambda i, j, k: (i, j))] + co_specs,
        out_shape=[jax.ShapeDtypeStruct((M, N), out_dtype)] + co_shapes, scratch_shapes=csems,
        compiler_params=_cparams(sem), name=name)(*a_list, *b_list, *([] if comm is None else comm.ins))
    return res[0] if comm is None else (res[0], res[1:])


def _grid2_hooks(comm, n0, n1, refs):
    j, i = pl.program_id(0), pl.program_id(1)
    return _comm_hooks(comm, jnp.logical_and(j == 0, i == 0), jnp.logical_and(j == n0 - 1, i == n1 - 1), refs)


def _ffn_in(h, w_in, *, T, D, F, name, comm=None):
    tm, tn = _tile(T, 256), _tile(F, 2816)
    nf, nt = F // tn, T // tm
    ci_specs, co_specs, co_shapes, csems = _comm_specs(comm)
    nci, nco = len(ci_specs), len(co_specs)

    def body(h_ref, wg_ref, wu_ref, *rest):
        jac_ref, a_ref = rest[nci], rest[nci + 1]
        at_entry, at_exit = _grid2_hooks(comm, nf, nt, (rest[:nci], rest[nci + 2:nci + 2 + nco], rest[nci + 2 + nco:]))
        at_entry()
        hh = h_ref[...]
        g = jnp.dot(hh, wg_ref[...], preferred_element_type=F32)
        u = jnp.dot(hh, wu_ref[...], preferred_element_type=F32)
        s = _sigmoid(g)
        sg = g * s
        jac_ref[0] = (u * (s * (1.0 + g * (1.0 - s)))).astype(BF16)
        jac_ref[1] = sg.astype(BF16)
        a_ref[...] = (sg * u).astype(BF16)
        at_exit()

    res = pl.pallas_call(
        body, grid=(nf, nt),
        in_specs=[pl.BlockSpec((tm, D), lambda j, i: (i, 0)),
                  pl.BlockSpec((D, tn), lambda j, i: (0, j)),
                  pl.BlockSpec((D, tn), lambda j, i: (0, nf + j))] + ci_specs,
        out_specs=[pl.BlockSpec((2, tm, tn), lambda j, i: (0, i, j)),
                   pl.BlockSpec((tm, tn), lambda j, i: (i, j))] + co_specs,
        out_shape=[jax.ShapeDtypeStruct((2, T, F), BF16), jax.ShapeDtypeStruct((T, F), BF16)] + co_shapes,
        scratch_shapes=csems,
        compiler_params=_cparams(("parallel", "parallel") if comm is None else ("arbitrary", "arbitrary")),
        name=name)(h, w_in, w_in, *([] if comm is None else comm.ins))
    return (res[0], res[1]) if comm is None else (res[0], res[1], res[2:])


def _ffn_dact(df, w_out, jac, *, T, D, F, name, comm=None):
    tm, tn = _tile(T, 256), _tile(F, 2816)
    nf, nt = F // tn, T // tm
    ci_specs, co_specs, co_shapes, csems = _comm_specs(comm)
    nci, nco = len(ci_specs), len(co_specs)

    def body(df_ref, w_ref, jac_ref, *rest):
        o_ref = rest[nci]
        at_entry, at_exit = _grid2_hooks(comm, nf, nt, (rest[:nci], rest[nci + 1:nci + 1 + nco], rest[nci + 1 + nco:]))
        at_entry()
        da = lax.dot_general(df_ref[...], w_ref[...], _DIMS['nt'], preferred_element_type=F32)
        o_ref[0] = (da * jac_ref[0].astype(F32)).astype(BF16)
        o_ref[1] = (da * jac_ref[1].astype(F32)).astype(BF16)
        at_exit()

    res = pl.pallas_call(
        body, grid=(nf, nt),
        in_specs=[pl.BlockSpec((tm, D), lambda j, i: (i, 0)),
                  pl.BlockSpec((tn, D), lambda j, i: (j, 0)),
                  pl.BlockSpec((2, tm, tn), lambda j, i: (0, i, j))] + ci_specs,
        out_specs=[pl.BlockSpec((2, tm, tn), lambda j, i: (0, i, j))] + co_specs,
        out_shape=[jax.ShapeDtypeStruct((2, T, F), BF16)] + co_shapes, scratch_shapes=csems,
        compiler_params=_cparams(("parallel", "parallel") if comm is None else ("arbitrary", "arbitrary")),
        name=name)(df, w_out, jac, *([] if comm is None else comm.ins))
    return res[0] if comm is None else (res[0], res[1:])


def _rowwise(fn, *, T, tm, name, tiled=(), prev=(), nxt=(), consts=(), out_tiled=(), out_acc=(), scratch=(),
             by_ref=False, comm=None, into=None):
    n = T // tm
    assert n * tm == T and tm % HALO == 0
    hb = tm // HALO
    cols = [a if isinstance(a, tuple) else (a, a.shape[1], 0) for a in tiled]
    tiled = [a for a, _, _ in cols]
    in_specs = [pl.BlockSpec((tm, w), functools.partial(lambda cb, i: (i, cb), cb)) for _, w, cb in cols]
    in_specs += [pl.BlockSpec((HALO, a.shape[1]), lambda i: (jnp.maximum(i * hb - 1, 0), 0)) for a in prev]
    in_specs += [pl.BlockSpec((HALO, a.shape[1]), lambda i: (jnp.minimum((i + 1) * hb, T // HALO - 1), 0))
                 for a in nxt]
    in_specs += [pl.BlockSpec(a.shape, lambda i: (0, 0)) for a in consts]
    out_shape = [jax.ShapeDtypeStruct((T, c), dt) for c, dt in out_tiled]
    out_shape += [jax.ShapeDtypeStruct(s, F32) for s in out_acc]
    out_specs = [pl.BlockSpec((tm, c), lambda i: (i, 0)) for c, _ in out_tiled]
    out_specs += [pl.BlockSpec(s, lambda i: (0, 0)) for s in out_acc]
    nt, npv, nnx, nc, not_, na = len(tiled), len(prev), len(nxt), len(consts), len(out_tiled), len(out_acc)
    ci_specs, co_specs, co_shapes, csems = _comm_specs(comm)
    extra_in, aliases = [], {}
    if into is not None:
        arr, cb = into
        width = out_tiled[0][0]
        out_shape[0] = jax.ShapeDtypeStruct(arr.shape, arr.dtype)
        out_specs[0] = pl.BlockSpec((tm, width), lambda i: (i, cb))
        extra_in = [arr]
        aliases = {nt + npv + nnx + nc: 0}
    n_extra = len(extra_in)

    def body(*refs):
        pos = 0
        groups = []
        for cnt in (nt, npv, nnx, nc, n_extra, len(ci_specs), not_, na, len(co_specs), len(scratch), len(csems)):
            groups.append(refs[pos:pos + cnt])
            pos += cnt
        t_r, p_r, n_r, c_r, _, ci_r, o_r, a_r, co_r, s_r, cs_r = groups
        i = pl.program_id(0)
        at_entry, at_exit = _comm_hooks(comm, i == 0, i == n - 1, (ci_r, co_r, cs_r))
        at_entry()

        @pl.when(i == 0)
        def _():
            for r in a_r:
                r[...] = jnp.zeros_like(r)

        if by_ref:
            fn(i, n, t_r, p_r, n_r, c_r, o_r, a_r, s_r)
        else:
            outs = fn(i, n, [r[...] for r in t_r], [r[...] for r in p_r], [r[...] for r in n_r],
                      [r[...] for r in c_r], a_r, s_r)
            for r, v in zip(o_r, outs):
                r[...] = v.astype(r.dtype)
        at_exit()

    res = pl.pallas_call(
        body, grid=(n,), in_specs=in_specs + [pl.BlockSpec(memory_space=pl.ANY)] * n_extra + ci_specs,
        out_specs=out_specs + co_specs, out_shape=out_shape + co_shapes, scratch_shapes=list(scratch) + csems,
        input_output_aliases=aliases, compiler_params=_cparams(("arbitrary",)), name=name,
    )(*tiled, *prev, *nxt, *consts, *extra_in, *([] if comm is None else comm.ins))
    return res if comm is None else (res[:not_ + na], res[not_ + na:])


def _colsum(v):
    return jnp.sum(v, axis=0, keepdims=True)


def _rowmean(v):
    return jnp.mean(v, axis=-1, keepdims=True)


def _pre_math(xv, g, m, s):
    g_pre, shift, scale = g[2 * s:2 * s + 1], m[3 * s:3 * s + 1], m[3 * s + 1:3 * s + 2]
    r = lax.rsqrt(_rowmean(xv * xv) + RMS_EPS)
    return ((xv * r) * g_pre) * (1.0 + scale) + shift


def _post_math(xv, fv, g, m, s, res_w):
    g_post, gate = g[2 * s + 1:2 * s + 2], m[3 * s + 2:3 * s + 3]
    y = (fv * lax.rsqrt(_rowmean(fv * fv) + RMS_EPS)) * g_post
    return xv + (res_w * (1.0 + gate)) * y


def _pre_fwd(x, gains, mod, *, T, s, name, comm=None):
    def fn(i, n, t, p, nx, c, acc, scr):
        return [_pre_math(t[0], c[0], c[1], s)]

    return _rowwise(fn, T=T, tm=_tile(T, 512, HALO), name=name, tiled=[x], consts=[gains, mod],
                    out_tiled=[(x.shape[1], BF16)], comm=comm)


def _post_pre_fwd(x, f, gains, mod, *, T, s, res_w, name):
    def fn(i, n, t, p, nx, c, acc, scr):
        out = _post_math(t[0], t[1], c[0], c[1], s, res_w)
        return [out, _pre_math(out, c[0], c[1], s + 1)]

    return _rowwise(fn, T=T, tm=_tile(T, 512, HALO), name=name, tiled=[x, f], consts=[gains, mod],
                    out_tiled=[(x.shape[1], F32), (x.shape[1], BF16)])


def _loss_post_bwd(x, f, target, gains, mod, *, T, s, res_w, name):
    D = x.shape[1]

    def fn(i, n, t, p, nx, c, acc, scr):
        (xv, fv, tv), (g, m) = t, c
        err = _post_math(xv, fv, g, m, s, res_w) - tv
        acc[0][...] += _colsum(err * err)
        dout = err * (1.0 / D)
        return [dout, _post_bwd_math(dout, fv, g, m, s, res_w, acc[1:3])]

    return _rowwise(fn, T=T, tm=_tile(T, 512, HALO), name=name, tiled=[x, f, target], consts=[gains, mod],
                    out_tiled=[(D, F32), (D, BF16)], out_acc=[(1, D)] * 3)


def _post_bwd_math(dv, fv, g, m, s, res_w, acc):
    g_post, gate = g[2 * s + 1:2 * s + 2], m[3 * s + 2:3 * s + 3]
    r2 = lax.rsqrt(_rowmean(fv * fv) + RMS_EPS)
    fh = fv * r2
    dy = dv * (res_w * (1.0 + gate))
    acc[0][...] += _colsum(dv * (res_w * (fh * g_post)))
    acc[1][...] += _colsum(dy * fh)
    gy = dy * g_post
    return r2 * (gy - fh * _rowmean(gy * fh))


def _pre_bwd_math(dhv, xv, dv, g, m, s, acc):
    g_pre, scale = g[2 * s:2 * s + 1], m[3 * s + 1:3 * s + 2]
    r = lax.rsqrt(_rowmean(xv * xv) + RMS_EPS)
    nv = xv * r
    acc[0][...] += _colsum(dhv)
    acc[1][...] += _colsum(dhv * (nv * g_pre))
    acc[2][...] += _colsum(dhv * ((1.0 + scale) * nv))
    gn = dhv * (g_pre * (1.0 + scale))
    return r * (gn - nv * _rowmean(gn * nv)) + dv


def _pre_bwd(dh, x, dout, gains, mod, *, T, s, name, comm=None):
    D = x.shape[1]

    def fn(i, n, t, p, nx, c, acc, scr):
        return [_pre_bwd_math(t[0], t[1], t[2], c[0], c[1], s, acc)]

    return _rowwise(fn, T=T, tm=_tile(T, 512, HALO), name=name, tiled=[dh, x, dout], consts=[gains, mod],
                    out_tiled=[(D, F32)], out_acc=[(1, D), (1, D), (1, D)], comm=comm)


def _pre_post_bwd(dh, x, dout, f_prev, gains, mod, *, T, s, res_w_prev, name, comm=None):
    D = x.shape[1]

    def fn(i, n, t, p, nx, c, acc, scr):
        dx = _pre_bwd_math(t[0], t[1], t[2], c[0], c[1], s, acc[0:3])
        return [dx, _post_bwd_math(dx, t[3], c[0], c[1], s - 1, res_w_prev, acc[3:5])]

    return _rowwise(fn, T=T, tm=_tile(T, 512, HALO), name=name, tiled=[dh, x, dout, f_prev], consts=[gains, mod],
                    out_tiled=[(D, F32), (D, BF16)], out_acc=[(1, D)] * 5, comm=comm)


SUBLANES = 8
CONV_CHUNK = 64


def _glu(cvg, C):
    return cvg[:, :C] * _sigmoid(cvg[:, C:])


def _fill_rotations(ext, rot, rows):
    for r in range(SUBLANES):
        rot[r] = ext[pl.ds(r, rows), :]


def _conv_taps(rot, w, r0, rows, off):
    acc = None
    for k in range(CONV_KERNEL):
        a, r = divmod(off(k), SUBLANES)
        term = w[k:k + 1] * rot[r, pl.ds(pl.multiple_of(r0 + a * SUBLANES, SUBLANES), rows), :]
        acc = term if acc is None else acc + term
    return acc


def _causal_off(k):
    return HALO - (CONV_KERNEL - 1) + k


def _conv_norm(rot, cw, cb, r0, rows):
    yc = _conv_taps(rot, cw, r0, rows, _causal_off) + cb
    mu = _rowmean(yc)
    d = yc - mu
    rstd = lax.rsqrt(_rowmean(d * d) + LN_EPS)
    return d * rstd, rstd


def _stage_glu(i, t, p, ext, rot, tm, C):
    ext[pl.ds(0, HALO), :] = jnp.where(i == 0, 0.0, _glu(p[0][...], C))
    ext[pl.ds(HALO, tm), :] = _glu(t[0][...], C)
    ext[pl.ds(HALO + tm, SUBLANES), :] = jnp.zeros((SUBLANES, C), F32)
    _fill_rotations(ext, rot, tm + HALO)


def _conv_scratch(tm, C):
    return [pltpu.VMEM((HALO + tm + SUBLANES, C), F32), pltpu.VMEM((SUBLANES, HALO + tm, C), F32)]


def _conv_fwd(cvg, cw, cvec, *, T, C, name, into=None):
    tm = _tile(T, 512, HALO)
    ch = min(CONV_CHUNK, tm)

    def fn(i, n, t, p, nx, c, o, acc, scr):
        ext, rot = scr
        _stage_glu(i, t, p, ext, rot, tm, C)
        w, vec = c[0][...], c[1][...]

        def chunk(ci, carry):
            r0 = pl.multiple_of(ci * ch, ch)
            yh, _ = _conv_norm(rot, w, vec[0:1], r0, ch)
            zz = yh * vec[1:2] + vec[2:3]
            o[0][pl.ds(r0, ch), :] = (zz * _sigmoid(zz)).astype(BF16)
            return carry

        lax.fori_loop(0, tm // ch, chunk, 0)

    return _rowwise(fn, T=T, tm=tm, name=name, tiled=[cvg], prev=[cvg], consts=[cw, cvec],
                    out_tiled=[(C, BF16)], scratch=_conv_scratch(tm, C), by_ref=True, into=into)[0]


def _conv_bwd1(cvg, duc, cw, cvec, *, T, C, name, comm=None):
    tm = _tile(T, 512, HALO)
    ch = min(CONV_CHUNK, tm)

    def fn(i, n, t, p, nx, c, o, acc, scr):
        ext, rot, w8 = scr

        @pl.when(i == 0)
        def _():
            w8[...] = jnp.zeros_like(w8)

        _stage_glu(i, t, p, ext, rot, tm, C)
        w, vec = c[0][...], c[1][...]
        ln_g = vec[1:2]

        def chunk(ci, carry):
            r0 = pl.multiple_of(ci * ch, ch)
            yh, rstd = _conv_norm(rot, w, vec[0:1], r0, ch)
            zz = yh * ln_g + vec[2:3]
            s = _sigmoid(zz)
            dz = t[1][pl.ds(r0, ch), :] * (s * (1.0 + zz * (1.0 - s)))
            dyh = dz * ln_g
            dyc = rstd * (dyh - _rowmean(dyh) - yh * _rowmean(dyh * yh))
            o[0][pl.ds(r0, ch), :] = dyc
            acc[0][0:1, :] += _colsum(dyc)
            acc[0][1:2, :] += _colsum(dz * yh)
            acc[0][2:3, :] += _colsum(dz)
            for k in range(CONV_KERNEL):
                a, r = divmod(_causal_off(k), SUBLANES)
                prod = dyc * rot[r, pl.ds(pl.multiple_of(r0 + a * SUBLANES, SUBLANES), ch), :]
                part = prod[0:SUBLANES]
                for g in range(1, ch // SUBLANES):
                    part = part + prod[g * SUBLANES:(g + 1) * SUBLANES]
                w8[pl.ds(k * SUBLANES, SUBLANES), :] += part
            return carry

        lax.fori_loop(0, tm // ch, chunk, 0)

        @pl.when(i == n - 1)
        def _():
            for k in range(CONV_KERNEL):
                acc[1][k:k + 1, :] = _colsum(w8[pl.ds(k * SUBLANES, SUBLANES), :])

    return _rowwise(fn, T=T, tm=tm, name=name, tiled=[cvg, duc], prev=[cvg], consts=[cw, cvec],
                    out_tiled=[(C, F32)], out_acc=[(8, C), (HALO, C)],
                    scratch=_conv_scratch(tm, C) + [pltpu.VMEM((HALO * SUBLANES, C), F32)], by_ref=True, comm=comm)


def _conv_bwd2(dyc, cvg, cw, *, T, C, name):
    tm = _tile(T, 512, HALO)
    ch = min(CONV_CHUNK, tm)

    def fn(i, n, t, p, nx, c, o, acc, scr):
        ext, rot = scr
        ext[pl.ds(0, tm), :] = t[0][...]
        ext[pl.ds(tm, HALO), :] = jnp.where(i == n - 1, 0.0, nx[0][...])
        _fill_rotations(ext, rot, tm + HALO - SUBLANES)
        w = c[0][...]

        def chunk(ci, carry):
            r0 = pl.multiple_of(ci * ch, ch)
            dug = _conv_taps(rot, w, r0, ch, lambda k: (CONV_KERNEL - 1) - k)
            cv = t[1][pl.ds(r0, ch), pl.ds(0, C)]
            s = _sigmoid(t[1][pl.ds(r0, ch), pl.ds(C, C)])
            o[0][pl.ds(r0, ch), :] = (dug * s).astype(BF16)
            o[1][pl.ds(r0, ch), :] = (dug * cv * (s * (1.0 - s))).astype(BF16)
            return carry

        lax.fori_loop(0, tm // ch, chunk, 0)

    return _rowwise(fn, T=T, tm=tm, name=name, tiled=[dyc, cvg], nxt=[dyc], consts=[cw],
                    out_tiled=[(C, BF16), (C, BF16)],
                    scratch=[pltpu.VMEM((tm + HALO, C), F32), pltpu.VMEM((SUBLANES, tm + HALO - SUBLANES, C), F32)],
                    by_ref=True)


def _split(v):
    hi = v.astype(BF16)
    return hi, (v - hi.astype(F32)).astype(BF16)


def _dot2(v, m):
    hi, lo = _split(v)
    return jnp.dot(hi, m, preferred_element_type=F32) + jnp.dot(lo, m, preferred_element_type=F32)


def _log_gap(z):
    return -(jnp.maximum(z, 0.0) + jnp.log(1.0 + jnp.exp(-jnp.abs(z))))


def _head_masks():
    lane = lax.broadcasted_iota(jnp.int32, (1, LANES), 1)
    return lane < HEAD_DIM, lane >= HEAD_DIM


LOG_WEIGHT_FLOOR = -110.0
ATTN_BLOCK = 256


def _key_norm_bound(k_ref, masks, T):
    ch = _tile(T, 512)

    def chunk(r, m):
        kk = k_ref[pl.ds(pl.multiple_of(r * ch, ch), ch), :].astype(F32)
        k2 = kk * kk
        return tuple(jnp.maximum(m[h], jnp.max(jnp.sum(jnp.where(masks[h], k2, 0.0), -1, keepdims=True),
                                               axis=0, keepdims=True)) for h in (0, 1))

    m0, m1 = lax.fori_loop(0, T // ch, chunk, (jnp.zeros((1, 1), F32), jnp.zeros((1, 1), F32)))
    row = lax.broadcasted_iota(jnp.int32, (8, LANES), 0)
    return jnp.where(row == 0, jnp.sqrt(m0), jnp.sqrt(m1))


def _score_bound(qh, kn):
    qf = qh.astype(F32)
    return jnp.sqrt(jnp.sum(qf * qf, -1, keepdims=True)) * (kn * 1.01) + 0.01


def _some_weight_left(carries, bounds):
    m = jnp.maximum(jnp.max(carries[0] + bounds[0]), jnp.max(carries[1] + bounds[1]))
    return m > LOG_WEIGHT_FLOOR


def _attn_fwd(qkv, g_attn, *, T, AW, a_cols, name, comm=None):
    P = AW // LANES
    tq = _tile(T, 2 * ATTN_BLOCK)
    tb = tq // 2
    nq = T // tq
    scale = HEAD_DIM ** -0.5
    ci_specs, co_specs, co_shapes, csems = _comm_specs(comm)
    nci, nco = len(ci_specs), len(co_specs)

    def body(q_ref, k_ref, v_ref, g_ref, *rest):
        o_ref, a_ref, kn_ref = rest[nci], rest[nci + 1], rest[nci + 2 + nco]
        at_entry, at_exit = _grid2_hooks(comm, P, nq, (rest[:nci], rest[nci + 2:nci + 2 + nco], rest[nci + 3 + nco:]))
        at_entry()
        i = pl.program_id(1)
        lo_mask, hi_mask = masks = _head_masks()

        @pl.when(i == 0)
        def _():
            kn_ref[...] = _key_norm_bound(k_ref, masks, T)

        rows = lax.broadcasted_iota(jnp.int32, (tb, tb), 0)
        cols = lax.broadcasted_iota(jnp.int32, (tb, tb), 1)
        strict = cols < rows
        everywhere = cols >= 0
        tri = jnp.where(rows >= cols, 1.0, 0.0).astype(BF16)
        qhs, zbs = [], []
        for part in (0, 1):
            q = q_ref[pl.ds(part * tb, tb), :]
            qhs.append([jnp.where(m, q, jnp.zeros_like(q)) * jnp.asarray(scale, BF16) for m in masks])
            zbs.append([_score_bound(qhs[part][h], kn_ref[h:h + 1, 0:1]) for h in (0, 1)])

        def block(kb, part, carry, mask=None):
            st = pl.multiple_of(kb * tb, tb)
            kj = k_ref[pl.ds(st, tb), :]
            vj = v_ref[pl.ds(st, tb), :]
            new = []
            for h in (0, 1):
                acc, c = carry[h]
                z = lax.dot_general(qhs[part][h], kj, _DIMS['nt'], preferred_element_type=F32)
                l = _log_gap(z)
                if mask is not None:
                    l = jnp.where(mask, l, 0.0)
                cum = _dot2(l, tri)
                w = jnp.exp(z + cum + c)
                if mask is not None:
                    w = jnp.where(mask, w, 0.0)
                new.append((acc + _dot2(w, vj), c + cum[:, 0:1]))
            return tuple(new)

        zero = (jnp.zeros((tb, LANES), F32), jnp.zeros((tb, 1), F32))
        carries = []
        for part in (0, 1):
            kb0 = 2 * i + part
            cr = block(kb0, part, (zero, zero), strict)
            cr = block(jnp.maximum(kb0 - 1, 0), part, cr, jnp.logical_and(i > 0, everywhere) if part == 0 else None)
            carries.append(cr)

        def live(st):
            jj, ca, cb = st
            return jnp.logical_and(jj < 2 * i, jnp.logical_or(
                _some_weight_left([ca[0][1], ca[1][1]], zbs[0]), _some_weight_left([cb[0][1], cb[1][1]], zbs[1])))

        def more(st):
            jj, ca, cb = st
            ka = 2 * i - 2 - jj
            ca = block(jnp.maximum(ka, 0), 0, ca, jnp.logical_and(ka >= 0, everywhere))
            cb = block(ka + 1, 1, cb)
            return jj + 1, ca, cb

        _, ca, cb = lax.while_loop(live, more, (jnp.int32(0), carries[0], carries[1]))
        o = jnp.concatenate([jnp.where(lo_mask, c2[0][0], c2[1][0]) for c2 in (ca, cb)], axis=0)
        o2 = o * o
        r0 = lax.rsqrt(jnp.sum(jnp.where(lo_mask, o2, 0.0), -1, keepdims=True) * (1.0 / HEAD_DIM) + RMS_EPS)
        r1 = lax.rsqrt(jnp.sum(jnp.where(hi_mask, o2, 0.0), -1, keepdims=True) * (1.0 / HEAD_DIM) + RMS_EPS)
        o_ref[...] = o
        a_ref[...] = ((o * jnp.where(lo_mask, r0, r1)) * g_ref[...]).astype(BF16)
        at_exit()

    res = pl.pallas_call(
        body, grid=(P, nq),
        in_specs=[pl.BlockSpec((tq, LANES), lambda p, i: (i, p)),
                  pl.BlockSpec((T, LANES), lambda p, i: (0, P + p)),
                  pl.BlockSpec((T, LANES), lambda p, i: (0, 2 * P + p)),
                  pl.BlockSpec((1, LANES), lambda p, i: (0, p))] + ci_specs,
        out_specs=[pl.BlockSpec((tq, LANES), lambda p, i: (i, p)),
                   pl.BlockSpec((tq, LANES), lambda p, i: (i, p))] + co_specs,
        out_shape=[jax.ShapeDtypeStruct((T, AW), F32), jax.ShapeDtypeStruct((T, a_cols), BF16)] + co_shapes,
        scratch_shapes=[pltpu.VMEM((8, LANES), F32)] + csems,
        compiler_params=_cparams(("parallel", "arbitrary") if comm is None else ("arbitrary", "arbitrary")),
        name=name)(qkv, qkv, qkv, g_attn, *([] if comm is None else comm.ins))
    return (res[0], res[1]) if comm is None else (res[0], res[1], res[2:])


def _attn_bwd(qkv, o, da, g_attn, *, T, AW, name):
    P = AW // LANES
    tq = _tile(T, 2 * ATTN_BLOCK)
    tb = tq // 2
    nq, nb = T // tq, T // tb
    scale = HEAD_DIM ** -0.5

    def body(q_ref, k_ref, v_ref, o_ref, da_ref, g_ref, dq_ref, dk_out, dv_out, dg_ref, kn_ref, dk_ref, dv_ref):
        i = pl.program_id(1)
        lo_mask, hi_mask = masks = _head_masks()

        @pl.when(i == 0)
        def _():
            dk_ref[...] = jnp.zeros_like(dk_ref)
            dv_ref[...] = jnp.zeros_like(dv_ref)
            dg_ref[...] = jnp.zeros_like(dg_ref)
            kn_ref[...] = _key_norm_bound(k_ref, masks, T)

        rows = lax.broadcasted_iota(jnp.int32, (tb, tb), 0)
        cols = lax.broadcasted_iota(jnp.int32, (tb, tb), 1)
        strict = cols < rows
        everywhere = cols >= 0
        tri = jnp.where(rows >= cols, 1.0, 0.0).astype(BF16)
        tri_s = jnp.where(rows > cols, 1.0, 0.0).astype(BF16)
        o_all = o_ref[...]
        da = da_ref[...]
        g = g_ref[...]
        o2 = o_all * o_all
        r0 = lax.rsqrt(jnp.sum(jnp.where(lo_mask, o2, 0.0), -1, keepdims=True) * (1.0 / HEAD_DIM) + RMS_EPS)
        r1 = lax.rsqrt(jnp.sum(jnp.where(hi_mask, o2, 0.0), -1, keepdims=True) * (1.0 / HEAD_DIM) + RMS_EPS)
        r = jnp.where(lo_mask, r0, r1)
        oh = o_all * r
        gy = da * g
        gyo = gy * oh
        m0 = jnp.sum(jnp.where(lo_mask, gyo, 0.0), -1, keepdims=True) * (1.0 / HEAD_DIM)
        m1 = jnp.sum(jnp.where(hi_mask, gyo, 0.0), -1, keepdims=True) * (1.0 / HEAD_DIM)
        do_all = r * (gy - oh * jnp.where(lo_mask, m0, m1))
        dg_ref[...] += _colsum(da * oh)

        qhs, zbs, do_bs, deltas, q_ts, do_ts = [], [], [], [], [], []
        for part in (0, 1):
            q = q_ref[pl.ds(part * tb, tb), :]
            o = o_all[part * tb:(part + 1) * tb]
            do = do_all[part * tb:(part + 1) * tb]
            qhs.append([jnp.where(m, q, jnp.zeros_like(q)) * jnp.asarray(scale, BF16) for m in masks])
            zbs.append([_score_bound(qhs[part][h], kn_ref[h:h + 1, 0:1]) for h in (0, 1)])
            do_bs.append([jnp.where(m, do, 0.0).astype(BF16) for m in masks])
            deltas.append([jnp.sum(d.astype(F32) * o, -1, keepdims=True) for d in do_bs[part]])
            q_ts.append([qh.astype(F32).T.astype(BF16) for qh in qhs[part]])
            do_ts.append([d.astype(F32).T.astype(BF16) for d in do_bs[part]])

        def block(kb, part, carry, mask=None):
            masked = mask is not None
            st = pl.multiple_of(kb * tb, tb)
            kj = k_ref[pl.ds(st, tb), :]
            vj = v_ref[pl.ds(st, tb), :]
            new = []
            dk = dv = None
            for h in (0, 1):
                dq, c, gsum = carry[h]
                z = lax.dot_general(qhs[part][h], kj, _DIMS['nt'], preferred_element_type=F32)
                l = _log_gap(z)
                sig = jnp.exp(z + l)
                if masked:
                    l = jnp.where(mask, l, 0.0)
                cum = _dot2(l, tri)
                w = jnp.exp(z + cum + c)
                if masked:
                    w = jnp.where(mask, w, 0.0)
                dp = lax.dot_general(do_bs[part][h], vj, _DIMS['nt'], preferred_element_type=F32)
                pw = w * dp
                after = _dot2(pw, tri_s)
                dz = pw - sig * (deltas[part][h] - gsum - after)
                if masked:
                    dz = jnp.where(mask, dz, 0.0)
                dz_b = dz.astype(BF16)
                dk_h = jnp.dot(q_ts[part][h], dz_b, preferred_element_type=F32)
                dv_h = jnp.dot(do_ts[part][h], w.astype(BF16), preferred_element_type=F32)
                dk = dk_h if dk is None else dk + dk_h
                dv = dv_h if dv is None else dv + dv_h
                dq = dq + jnp.dot(dz_b, kj, preferred_element_type=F32)
                new.append((dq, c + cum[:, 0:1], gsum + (after[:, 0:1] + pw[:, 0:1])))
            dk_ref[kb] += dk
            dv_ref[kb] += dv
            return tuple(new)

        zero1 = jnp.zeros((tb, 1), F32)
        zero = (jnp.zeros((tb, LANES), F32), zero1, zero1)
        carries = []
        for part in (0, 1):
            kb0 = 2 * i + part
            cr = block(kb0, part, (zero, zero), strict)
            cr = block(jnp.maximum(kb0 - 1, 0), part, cr, jnp.logical_and(i > 0, everywhere) if part == 0 else None)
            carries.append(cr)

        def live(st):
            jj, ca, cb = st
            return jnp.logical_and(jj < 2 * i, jnp.logical_or(
                _some_weight_left([ca[0][1], ca[1][1]], zbs[0]), _some_weight_left([cb[0][1], cb[1][1]], zbs[1])))

        def more(st):
            jj, ca, cb = st
            ka = 2 * i - 2 - jj
            ca = block(jnp.maximum(ka, 0), 0, ca, jnp.logical_and(ka >= 0, everywhere))
            cb = block(ka + 1, 1, cb)
            return jj + 1, ca, cb

        _, ca, cb = lax.while_loop(live, more, (jnp.int32(0), carries[0], carries[1]))
        dq_ref[...] = (jnp.concatenate([jnp.where(lo_mask, c2[0][0], c2[1][0]) for c2 in (ca, cb)], axis=0)
                       * scale).astype(BF16)

        @pl.when(i == nq - 1)
        def _():
            def turn(j, carry_):
                st = pl.multiple_of(j * tb, tb)
                dk_out[pl.ds(st, tb), :] = dk_ref[j].T.astype(BF16)
                dv_out[pl.ds(st, tb), :] = dv_ref[j].T.astype(BF16)
                return carry_

            lax.fori_loop(0, nb, turn, 0)

    return pl.pallas_call(
        body, grid=(P, nq),
        in_specs=[pl.BlockSpec((tq, LANES), lambda p, i: (i, p)),
                  pl.BlockSpec((T, LANES), lambda p, i: (0, P + p)),
                  pl.BlockSpec((T, LANES), lambda p, i: (0, 2 * P + p)),
                  pl.BlockSpec((tq, LANES), lambda p, i: (i, p)),
                  pl.BlockSpec((tq, LANES), lambda p, i: (i, p)),
                  pl.BlockSpec((1, LANES), lambda p, i: (0, p))],
        out_specs=[pl.BlockSpec((tq, LANES), lambda p, i: (i, p)),
                   pl.BlockSpec((T, LANES), lambda p, i: (0, p)),
                   pl.BlockSpec((T, LANES), lambda p, i: (0, p)),
                   pl.BlockSpec((1, LANES), lambda p, i: (0, p))],
        out_shape=[jax.ShapeDtypeStruct((T, AW), BF16)] * 3 + [jax.ShapeDtypeStruct((1, AW), F32)],
        scratch_shapes=[pltpu.VMEM((8, LANES), F32), pltpu.VMEM((nb, LANES, tb), F32),
                        pltpu.VMEM((nb, LANES, tb), F32)],
        compiler_params=_cparams(("parallel", "arbitrary")), name=name)(qkv, qkv, qkv, o, da, g_attn)


def _ada_fwd(c_all, w_ada, b_ada, *, name):
    def body(c_ref, w_ref, b_ref, o_ref):
        cv = c_ref[...]
        sc = cv * _sigmoid(cv)
        o_ref[...] = jnp.dot(sc, w_ref[...], preferred_element_type=F32,
                             precision=lax.Precision.HIGHEST) + b_ref[...]

    return pl.pallas_call(body, out_shape=jax.ShapeDtypeStruct((c_all.shape[0], w_ada.shape[1]), F32),
                          compiler_params=_cparams(), name=name)(c_all, w_ada, b_ada)


def _ada_bwd(c_all_t, dmod, *, name):
    def body(c_ref, d_ref, o_ref):
        cv = c_ref[...]
        sc = cv * _sigmoid(cv)
        o_ref[...] = jnp.dot(sc, d_ref[...], preferred_element_type=F32, precision=lax.Precision.HIGHEST)

    return pl.pallas_call(body, out_shape=jax.ShapeDtypeStruct((c_all_t.shape[0], dmod.shape[1]), F32),
                          compiler_params=_cparams(), name=name)(c_all_t, dmod)


def _adamw_update(w_ref, g_ref, m_ref, v_ref, d_ref, nm_ref, nv_ref):
    gv = g_ref[...]
    m2 = ADAM_B1 * m_ref[...] + (1.0 - ADAM_B1) * gv
    v2 = ADAM_B2 * v_ref[...] + (1.0 - ADAM_B2) * jnp.square(gv)
    m_hat = m2 / (1.0 - ADAM_B1 ** ADAM_STEP)
    v_hat = v2 / (1.0 - ADAM_B2 ** ADAM_STEP)
    d_ref[...] = -ADAM_LR * (m_hat / (jnp.sqrt(v_hat) + ADAM_EPS) + ADAM_WD * w_ref[...])
    nm_ref[...] = m2
    nv_ref[...] = v2


def _adamw_many(wgmv, *, name):
    n = len(wgmv[0])

    def body(*refs):
        ins, outs = refs[:4 * n], refs[4 * n:]
        for k in range(n):
            _adamw_update(ins[k], ins[n + k], ins[2 * n + k], ins[3 * n + k], *outs[3 * k:3 * k + 3])

    return pl.pallas_call(
        body, out_shape=[jax.ShapeDtypeStruct(w.shape, F32) for w in wgmv[0] for _ in range(3)],
        compiler_params=_cparams(), name=name)(*wgmv[0], *wgmv[1], *wgmv[2], *wgmv[3])


def _adamw(w, g, m, v, *, name):
    R, C = w.shape
    tr = _tile(R, max(8, (1 << 18) // C), 8)
    body = functools.partial(_adamw_update)

    spec = pl.BlockSpec((tr, C), lambda i: (i, 0))
    return pl.pallas_call(
        body, grid=(R // tr,), in_specs=[spec] * 4, out_specs=[spec] * 3,
        out_shape=[jax.ShapeDtypeStruct((R, C), F32)] * 3,
        compiler_params=_cparams(("parallel",)), name=name)(w, g, m, v)


def _sum_devices(a, *, name):
    def body(a_ref, o_ref):
        s = a_ref[0]
        for d in range(1, a_ref.shape[0]):
            s = s + a_ref[d]
        o_ref[...] = s

    return pl.pallas_call(body, out_shape=jax.ShapeDtypeStruct(a.shape[1:], F32),
                          compiler_params=_cparams(), name=name)(a)


def _place():
    return lax.axis_index("x"), lax.axis_index("y"), lax.axis_index("c")


def _flip(v, bit):
    return 1 - v if bit else v


def _allgather8(blk, *, name):
    R, C = blk.shape

    def body(x_ref, out_ref, send_sems, recv_sems):
        x, y, c = _place()
        me = 4 * x + 2 * y + c
        out_ref[me] = x_ref[...]
        copies = []
        for k in range(1, 8):
            peer = (_flip(x, (k >> 2) & 1), _flip(y, (k >> 1) & 1), _flip(c, k & 1))
            cp = pltpu.make_async_remote_copy(
                src_ref=x_ref, dst_ref=out_ref.at[me], send_sem=send_sems.at[k - 1],
                recv_sem=recv_sems.at[k - 1], device_id=peer, device_id_type=MESH)
            cp.start()
            copies.append(cp)
        for cp in copies:
            cp.wait()

    return pl.pallas_call(
        body, out_shape=jax.ShapeDtypeStruct((8, R, C), F32),
        in_specs=[pl.BlockSpec(memory_space=pltpu.VMEM)], out_specs=pl.BlockSpec(memory_space=pltpu.VMEM),
        scratch_shapes=[pltpu.SemaphoreType.DMA((7,)), pltpu.SemaphoreType.DMA((7,))],
        compiler_params=_cparams(), name=name)(blk)


def _aligned(v, m):
    return v if isinstance(v, int) else pl.multiple_of(v, m)


def _rows_half(ref, half):
    n = ref.shape[0] // 2
    return ref.at[pl.ds(_aligned(half * n, 16), n)]


def _region(ref, kind, slot, half):
    if kind == 'col':
        n, cs = ref.shape[0] // 2, ref.shape[1] // 4
        return ref.at[pl.ds(_aligned(half * n, 16), n), pl.ds(_aligned(slot * cs, LANES), cs)]
    rs = ref.shape[0] // 4
    return ref.at[pl.ds(_aligned(slot * rs + half * (rs // 2), 16), rs // 2)]


def _other_chips(x, y):
    return [(1 - x, y), (x, 1 - y), (1 - x, 1 - y)]


class _Comm:
    def __init__(self, ins, outs, sems, start, finish):
        self.ins, self.outs, self.sems, self.start, self.finish = list(ins), list(outs), list(sems), start, finish


def _comm_specs(comm):
    if comm is None:
        return [], [], [], []
    anyspec = pl.BlockSpec(memory_space=pl.ANY)
    return [anyspec] * len(comm.ins), [anyspec] * len(comm.outs), list(comm.outs), list(comm.sems)


def _comm_hooks(comm, first, last, refs):
    if comm is None:
        return (lambda: None), (lambda: None)

    def at_entry():
        pl.when(first)(lambda: comm.start(*refs))

    def at_exit():
        pl.when(last)(lambda: comm.finish(*refs))

    return at_entry, at_exit


def _comm_alone(comm, *, name):
    ni, no = len(comm.ins), len(comm.outs)

    def body(*refs):
        parts = (refs[:ni], refs[ni:ni + no], refs[ni + no:])
        comm.start(*parts)
        comm.finish(*parts)

    i_specs, o_specs, o_shapes, sems = _comm_specs(comm)
    return pl.pallas_call(body, out_shape=o_shapes, in_specs=i_specs, out_specs=o_specs, scratch_shapes=sems,
                          compiler_params=_cparams(), name=name)(*comm.ins)


def _gather_comm(shards, kinds):
    nw = len(shards)
    full_shapes = []
    for s, kind in zip(shards, kinds):
        full_shapes.append((s.shape[0], 4 * s.shape[1]) if kind == 'col' else (4 * s.shape[0], s.shape[1]))

    def copies(sh, full, sems):
        lsem, ssem, rsem, fssem, frsem = sems
        x, y, c = _place()
        me_slot = 2 * x + y
        chips = _other_chips(x, y)
        local, ici, landed, fwd, passed = [], [], [], [], []
        for w in range(nw):
            for h in (0, 1):
                local.append(pltpu.make_async_copy(_rows_half(sh[w], h), _region(full[w], kinds[w], me_slot, h),
                                                   lsem.at[w, h]))
            for r, (px, py) in enumerate(chips):
                ici.append(pltpu.make_async_remote_copy(
                    src_ref=_rows_half(sh[w], c), dst_ref=_region(full[w], kinds[w], me_slot, c),
                    send_sem=ssem.at[w, r], recv_sem=rsem.at[w, r], device_id=(px, py, c), device_id_type=MESH))
                mine = _region(full[w], kinds[w], 2 * px + py, c)
                landed.append(pltpu.make_async_remote_copy(
                    src_ref=mine, dst_ref=mine, send_sem=ssem.at[w, r], recv_sem=rsem.at[w, r],
                    device_id=(px, py, c), device_id_type=MESH))
                fwd.append(pltpu.make_async_remote_copy(
                    src_ref=mine, dst_ref=mine, send_sem=fssem.at[w, r], recv_sem=frsem.at[w, r],
                    device_id=(x, y, 1 - c), device_id_type=MESH))
                theirs = _region(full[w], kinds[w], 2 * px + py, 1 - c)
                passed.append(pltpu.make_async_remote_copy(
                    src_ref=theirs, dst_ref=theirs, send_sem=fssem.at[w, r], recv_sem=frsem.at[w, r],
                    device_id=(x, y, 1 - c), device_id_type=MESH))
        return local, ici, landed, fwd, passed

    def start(sh, full, sems):
        local, ici, _, _, _ = copies(sh, full, sems)
        for cp in local + ici:
            cp.start()

    def finish(sh, full, sems):
        local, ici, landed, fwd, passed = copies(sh, full, sems)
        for got, cp in zip(landed, fwd):
            got.wait_recv()
            cp.start()
        for got in passed:
            got.wait_recv()
        for cp in ici + fwd:
            cp.wait_send()
        for cp in local:
            cp.wait()

    return _Comm(shards, [jax.ShapeDtypeStruct(s, BF16) for s in full_shapes],
                 [pltpu.SemaphoreType.DMA((nw, 2))] + [pltpu.SemaphoreType.DMA((nw, 3))] * 4, start, finish)


def _exchange_comm(grads, kinds):
    nw = len(grads)

    def copies(g, r1, sems):
        ssem, rsem = sems
        x, y, c = _place()
        out, back = [], []
        for w in range(nw):
            for slot in range(4):
                out.append(pltpu.make_async_remote_copy(
                    src_ref=_region(g[w], kinds[w], slot, 1 - c), dst_ref=_region(r1[w], kinds[w], slot, 1 - c),
                    send_sem=ssem.at[w, slot], recv_sem=rsem.at[w, slot], device_id=(x, y, 1 - c),
                    device_id_type=MESH))
                mine = _region(r1[w], kinds[w], slot, c)
                back.append(pltpu.make_async_remote_copy(
                    src_ref=mine, dst_ref=mine, send_sem=ssem.at[w, slot], recv_sem=rsem.at[w, slot],
                    device_id=(x, y, 1 - c), device_id_type=MESH))
        return out, back

    def start(g, r1, sems):
        for cp in copies(g, r1, sems)[0]:
            cp.start()

    def finish(g, r1, sems):
        out, back = copies(g, r1, sems)
        for got in back:
            got.wait_recv()
        for cp in out:
            cp.wait_send()

    return _Comm(grads, [jax.ShapeDtypeStruct(g.shape, F32) for g in grads],
                 [pltpu.SemaphoreType.DMA((nw, 4))] * 2, start, finish)


def _add_core_halves(g, r1, place, kind, *, name):
    if kind == 'col':
        n, cs = g.shape[0] // 2, g.shape[1] // 4
        tr = _tile(n, 256, 16)
        nt = n // tr
        ispec = pl.BlockSpec((tr, cs), lambda s, t, pr: (pr[0] * nt + t, s))
    else:
        rs, cs = g.shape[0] // 4, g.shape[1]
        n = rs // 2
        tr, nt = n, 1
        ispec = pl.BlockSpec((tr, cs), lambda s, t, pr: (s * 2 + pr[0], 0))

    def body(pr, a_ref, b_ref, o_ref):
        o_ref[...] = (a_ref[...] + b_ref[...]).astype(BF16)

    return pl.pallas_call(
        body,
        grid_spec=pltpu.PrefetchScalarGridSpec(
            num_scalar_prefetch=1, grid=(4, nt), in_specs=[ispec, ispec],
            out_specs=pl.BlockSpec((None, tr, cs), lambda s, t, pr: (s, t, 0))),
        out_shape=jax.ShapeDtypeStruct((4, n, cs), BF16),
        compiler_params=_cparams(("parallel", "parallel")), name=name)(place, g, r1)


def _scatter_comm(hs):
    nw = len(hs)

    def copies(h, r2, sems):
        ssem, rsem = sems
        x, y, c = _place()
        return [pltpu.make_async_remote_copy(
            src_ref=h[w].at[2 * px + py], dst_ref=r2[w].at[r], send_sem=ssem.at[w, r],
            recv_sem=rsem.at[w, r], device_id=(px, py, c), device_id_type=MESH)
            for w in range(nw) for r, (px, py) in enumerate(_other_chips(x, y))]

    def start(h, r2, sems):
        for cp in copies(h, r2, sems):
            cp.start()

    def finish(h, r2, sems):
        for cp in copies(h, r2, sems):
            cp.wait()

    return _Comm(hs, [jax.ShapeDtypeStruct((3,) + a.shape[1:], a.dtype) for a in hs],
                 [pltpu.SemaphoreType.DMA((nw, 3))] * 2, start, finish)


def _sum_owner(hs, r2, place, *, name):
    _, n, cs = hs.shape
    tr = _tile(n, 256, 16)
    nt = n // tr

    def body(pr, h_ref, r_ref, o_ref):
        o_ref[...] = ((h_ref[...].astype(F32) + r_ref[0].astype(F32)) + r_ref[1].astype(F32)) + r_ref[2].astype(F32)

    return pl.pallas_call(
        body,
        grid_spec=pltpu.PrefetchScalarGridSpec(
            num_scalar_prefetch=1, grid=(nt,),
            in_specs=[pl.BlockSpec((None, tr, cs), lambda t, pr: (pr[1], t, 0)),
                      pl.BlockSpec((3, tr, cs), lambda t, pr: (0, t, 0))],
            out_specs=pl.BlockSpec((None, tr, cs), lambda t, pr: (pr[0], t, 0))),
        out_shape=jax.ShapeDtypeStruct((2, n, cs), F32),
        compiler_params=_cparams(("parallel",)), name=name)(place, hs, r2)


def _share_with_sibling(fins, *, name):
    nw = len(fins)

    def body(*refs):
        fin, out = refs[:nw], refs[nw:2 * nw]
        ssem, rsem = refs[2 * nw:]
        x, y, c = _place()
        copies = []
        for w in range(nw):
            cp = pltpu.make_async_remote_copy(
                src_ref=fin[w].at[c], dst_ref=out[w].at[c], send_sem=ssem.at[w], recv_sem=rsem.at[w],
                device_id=(x, y, 1 - c), device_id_type=MESH)
            cp.start()
            copies.append(cp)
        for w in range(nw):
            theirs = out[w].at[1 - c]
            pltpu.make_async_remote_copy(
                src_ref=theirs, dst_ref=theirs, send_sem=ssem.at[w], recv_sem=rsem.at[w],
                device_id=(x, y, 1 - c), device_id_type=MESH).wait_recv()
        for cp in copies:
            cp.wait_send()

    anyspec = pl.BlockSpec(memory_space=pl.ANY)
    return pl.pallas_call(
        body, out_shape=[jax.ShapeDtypeStruct(a.shape, F32) for a in fins],
        in_specs=[anyspec] * nw, out_specs=[anyspec] * nw,
        input_output_aliases={w: w for w in range(nw)},
        scratch_shapes=[pltpu.SemaphoreType.DMA((nw,))] * 2,
        compiler_params=_cparams(), name=name)(*fins)


class _Plan:
    def __init__(self, wfull):
        self.w = dict(wfull)
        self.grads = {}

    def ffn_width(self):
        return self.w['ff1_w_out'].shape[0]

    def comm(self, site):
        return None

    def done(self, site, results):
        pass

    def ready(self, group, names, arrays):
        self.grads.update(zip(names, arrays))


def _riding(plan, site, call):
    comm = plan.comm(site)
    res = call(comm)
    if comm is None:
        return res
    *main, extra = res
    plan.done(site, extra)
    return main[0] if len(main) == 1 else tuple(main)


def _merge_comms(comms):
    if len(comms) == 1:
        return comms[0]

    def parts(refs, field):
        out, pos = [], 0
        for c in comms:
            n = len(getattr(c, field))
            out.append(refs[pos:pos + n])
            pos += n
        return out

    def run(which):
        def fn(ins, outs, sems):
            for c, i, o, s in zip(comms, parts(ins, 'ins'), parts(outs, 'outs'), parts(sems, 'sems')):
                getattr(c, which)(i, o, s)
        return fn

    return _Comm(sum((c.ins for c in comms), []), sum((c.outs for c in comms), []),
                 sum((c.sems for c in comms), []), run('start'), run('finish'))


class _DistPlan(_Plan):
    RIDES = {
        'pre_fwd_ff1': [('gather', ['ff1_w_in'])],
        'ffn_in_ff1': [('gather', ['ff1_w_out', 'w_in_mix', 'w_out_mix'])],
        'attn_fwd': [('gather', ['ff2_w_in', 'ff2_w_out'])],
        'pre_bwd_ff2': [('exchange', ['ff2_w_in', 'ff2_w_out'])],
        'conv_bwd1': [('scatter', ['ff2_w_in', 'ff2_w_out'])],
        'pre_bwd_mix': [('exchange', ['w_in_mix', 'w_out_mix'])],
        'ffn_dact_ff1': [('scatter', ['w_in_mix', 'w_out_mix'])],
        'dw_out_ff1': [('exchange', ['ff1_w_in'])],
        'dh_ff1': [('scatter', ['ff1_w_in']), ('exchange', ['ff1_w_out'])],
        'pre_bwd_ff1': [('scatter', ['ff1_w_out'])],
    }
    AFTER = []

    def __init__(self, shards, place):
        self.shards, self.place, self.kind = shards, place, dict(BIG)
        self.w, self.grads, self.hs, self.fin = {}, {}, {}, {}

    def _make(self, kind, names):
        if kind == 'gather':
            return _gather_comm([self.shards[n] for n in names], [self.kind[n] for n in names])
        if kind == 'exchange':
            return _exchange_comm([self.grads[n] for n in names], [self.kind[n] for n in names])
        return _scatter_comm([self.hs[n] for n in names])

    def _take(self, kind, names, results):
        for n, r in zip(names, results):
            if kind == 'gather':
                self.w[n] = r
            elif kind == 'exchange':
                self.hs[n] = _add_core_halves(self.grads[n], r, self.place, self.kind[n], name=f"grad_core_add_{n}")
            else:
                self.fin[n] = _sum_owner(self.hs[n], r, self.place, name=f"grad_owner_sum_{n}")

    def ffn_width(self):
        return 4 * self.shards['ff1_w_out'].shape[0]

    def comm(self, site):
        rides = self.RIDES.get(site)
        return None if rides is None else _merge_comms([self._make(k, names) for k, names in rides])

    def done(self, site, results):
        pos = 0
        for kind, names in self.RIDES[site]:
            self._take(kind, names, results[pos:pos + len(names)])
            pos += len(names)

    def finish(self):
        for kind, names in self.AFTER:
            self._take(kind, names, _comm_alone(self._make(kind, names), name=f"grad_{kind}_{names[0]}"))
        names = list(self.shards)
        return dict(zip(names, _share_with_sibling([self.fin[n] for n in names], name="grad_share")))


def _local_step(x, target, mod, gains, plan, g_attn, conv_w, cvec):
    T, D = x.shape
    F = plan.ffn_width()
    AW = D // 2
    C = D - AW
    NQKV = 3 * AW
    MIX = NQKV + 2 * C
    tM = _tile(T, 1024)
    tkT = _tile(T, 1024)

    def ffn_fwd(h, tag):
        w_in = plan.w[f"{tag}_w_in"]
        jac, act = _riding(plan, f"ffn_in_{tag}",
                           lambda cm: _ffn_in(h, w_in, T=T, D=D, F=F, name=f"ffn_in_{tag}", comm=cm))
        f = _matmul(act, plan.w[f"{tag}_w_out"], mode='nn', M=T, N=D, K=F, tm=tM, tn=_tile(D, 1024), tk=F,
                    out_dtype=F32, name=f"ffn_out_{tag}")
        return h, jac, act, f

    def ffn_bwd(df, dout, xin, saved, s, tag, prev=None):
        h, jac, act, f = saved
        w_in, w_out = plan.w[f"{tag}_w_in"], plan.w[f"{tag}_w_out"]
        dgu = _riding(plan, f"ffn_dact_{tag}",
                      lambda cm: _ffn_dact(df, w_out, jac, T=T, D=D, F=F, name=f"ffn_dact_{tag}", comm=cm))
        tnf = _tile(F, 2816)
        nf = F // tnf
        dw_in = _matmul(h, dgu, mode='tn', M=D, N=2 * F, K=T, tm=_tile(D, 1024), tn=tnf, tk=tkT, out_dtype=F32,
                        b_spec=pl.BlockSpec((None, tkT, tnf), lambda i, j, k: (j // nf, k, j % nf)),
                        name=f"dw_in_{tag}")
        plan.ready(tag, [f"{tag}_w_in"], [dw_in])
        dw_out = _riding(plan, f"dw_out_{tag}", lambda cm: _matmul(
            act, df, mode='tn', M=F, N=D, K=T, tm=_tile(F, 1408), tn=_tile(D, 1024), tk=tkT, out_dtype=F32,
            name=f"dw_out_{tag}", comm=cm))
        plan.ready(tag, [f"{tag}_w_out"], [dw_out])
        dh = _riding(plan, f"dh_{tag}", lambda cm: _matmul(
            dgu, w_in, mode='nt', M=T, N=D, K=2 * F, tm=tM, tn=_tile(D, 1024), tk=F, out_dtype=F32,
            a_spec=pl.BlockSpec((None, tM, F), lambda i, j, k: (k, i, 0)), name=f"dh_{tag}", comm=cm))
        if prev is None:
            return _riding(plan, f"pre_bwd_{tag}", lambda cm: _pre_bwd(
                dh, xin, dout, gains, mod, T=T, s=s, name=f"pre_bwd_{tag}", comm=cm))
        return _riding(plan, f"pre_bwd_{tag}", lambda cm: _pre_post_bwd(
            dh, xin, dout, prev[0], gains, mod, T=T, s=s, res_w_prev=prev[1], name=f"pre_bwd_{tag}", comm=cm))

    h1 = _riding(plan, "pre_fwd_ff1", lambda cm: _pre_fwd(x, gains, mod, T=T, s=0, name="pre_fwd_ff1", comm=cm))[0]
    s1 = ffn_fwd(h1, "ff1")
    x1, h2 = _post_pre_fwd(x, s1[3], gains, mod, T=T, s=0, res_w=0.5, name="post_fwd_ff1")
    w_in_mix, w_out_mix = plan.w['w_in_mix'], plan.w['w_out_mix']
    tnq = _tile(AW, 512)
    qkv = _matmul(h2, w_in_mix, mode='nn', M=T, N=NQKV, K=D, tm=tM, tn=tnq, tk=D, out_dtype=BF16, name="proj_qkv")
    tnc = _tile(C, 512)
    off = NQKV // tnc
    cvg = _matmul(h2, w_in_mix, mode='nn', M=T, N=2 * C, K=D, tm=tM, tn=tnc, tk=D, out_dtype=F32,
                  b_spec=pl.BlockSpec((D, tnc), lambda i, j, k: (0, off + j)), name="proj_conv")
    o_attn, a_attn = _riding(plan, "attn_fwd", lambda cm: _attn_fwd(
        qkv, g_attn, T=T, AW=AW, a_cols=D, name="attn_fwd", comm=cm))
    mixcat = _conv_fwd(cvg, conv_w, cvec, T=T, C=C, name="conv_fwd", into=(a_attn, AW // C))
    f_mix = _matmul(mixcat, w_out_mix, mode='nn', M=T, N=D, K=D, tm=tM, tn=_tile(D, 1024), tk=D, out_dtype=F32,
                    name="mix_out")
    x2, h3 = _post_pre_fwd(x1, f_mix, gains, mod, T=T, s=1, res_w=1.0, name="post_fwd_mix")

    s3 = ffn_fwd(h3, "ff2")
    dout, df2, sq, dgate2, dgpost2 = _loss_post_bwd(x2, s3[3], target, gains, mod, T=T, s=2, res_w=0.5,
                                                    name="post_fwd_loss")
    dx2, df_mix, dshift2, dscale2, dgpre2, dgate_m, dgpost_m = ffn_bwd(
        df2, dout, x2, s3, 2, "ff2", prev=(f_mix, 1.0))
    dmixcat = _matmul(df_mix, w_out_mix, mode='nt', M=T, N=D, K=D, tm=tM, tn=_tile(D, 1024), tk=D, out_dtype=F32,
                      name="d_mixcat")
    dw_out_mix = _matmul(mixcat, df_mix, mode='tn', M=D, N=D, K=T, tm=_tile(D, 1024), tn=_tile(D, 1024),
                         tk=tkT, out_dtype=F32, name="dw_out_mix")
    dq, dk, dv, dg_attn = _attn_bwd(qkv, o_attn, dmixcat, g_attn, T=T, AW=AW, name="attn_bwd")
    dyc, csum, dconv_w = _riding(plan, "conv_bwd1", lambda cm: _conv_bwd1(
        cvg, (dmixcat, C, AW // C), conv_w, cvec, T=T, C=C, name="conv_bwd1", comm=cm))
    dcv, dcg = _conv_bwd2(dyc, cvg, conv_w, T=T, C=C, name="conv_bwd2")
    dproj = [dq, dk, dv, dcv, dcg]
    dh2 = _matmul(dproj, w_in_mix, mode='nt', M=T, N=D, K=MIX, tm=tM, tn=_tile(D, 1024), tk=MIX, out_dtype=F32,
                  name="dh_mix")
    dw_in_mix = _matmul(h2, dproj, mode='tn', M=D, N=MIX, K=T, tm=_tile(D, 1024), tn=MIX,
                        tk=tkT, out_dtype=F32, name="dw_in_mix")
    plan.ready("mix", ['w_in_mix', 'w_out_mix'], [dw_in_mix, dw_out_mix])
    dx1, df1, dshift_m, dscale_m, dgpre_m, dgate1, dgpost1 = _riding(plan, "pre_bwd_mix", lambda cm: _pre_post_bwd(
        dh2, x1, dx2, s1[3], gains, mod, T=T, s=1, res_w_prev=0.5, name="pre_bwd_mix", comm=cm))

    dx0, dshift1, dscale1, dgpre1 = ffn_bwd(df1, dx1, x, s1, 0, "ff1")

    dgains = [dgpre1, dgpost1, dgpre_m, dgpost_m, dgpre2, dgpost2]
    dmod = [dshift1, dscale1, dgate1, dshift_m, dscale_m, dgate_m, dshift2, dscale2, dgate2]
    return sq, dx0, dgains, dmod, dg_attn, csum, dconv_w


def _pack_rows(pieces, width):
    rows = jnp.concatenate([p.reshape(-1) for p in pieces]).reshape(-1, width)
    pad = (-rows.shape[0]) % 8
    return jnp.pad(rows, ((0, pad), (0, 0)))


def kernel(x, c, w_ada, b_ada, g_pre_ff1, g_post_ff1, ff1_w_in, ff1_w_out, g_pre_mix, g_post_mix, w_in_mix, g_attn_out, conv_w, conv_b, conv_ln_g, conv_ln_b, w_out_mix, g_pre_ff2, g_post_ff2, ff2_w_in, ff2_w_out, loss_target, m_w_ada, m_b_ada, m_g_pre_ff1, m_g_post_ff1, m_ff1_w_in, m_ff1_w_out, m_g_pre_mix, m_g_post_mix, m_w_in_mix, m_g_attn_out, m_conv_w, m_conv_b, m_conv_ln_g, m_conv_ln_b, m_w_out_mix, m_g_pre_ff2, m_g_post_ff2, m_ff2_w_in, m_ff2_w_out, v_w_ada, v_b_ada, v_g_pre_ff1, v_g_post_ff1, v_ff1_w_in, v_ff1_w_out, v_g_pre_mix, v_g_post_mix, v_w_in_mix, v_g_attn_out, v_conv_w, v_conv_b, v_conv_ln_g, v_conv_ln_b, v_w_out_mix, v_g_pre_ff2, v_g_post_ff2, v_ff2_w_in, v_ff2_w_out):
    W = dict(w_ada=w_ada, b_ada=b_ada, g_pre_ff1=g_pre_ff1, g_post_ff1=g_post_ff1, ff1_w_in=ff1_w_in,
             ff1_w_out=ff1_w_out, g_pre_mix=g_pre_mix, g_post_mix=g_post_mix, w_in_mix=w_in_mix,
             g_attn_out=g_attn_out, conv_w=conv_w, conv_b=conv_b, conv_ln_g=conv_ln_g, conv_ln_b=conv_ln_b,
             w_out_mix=w_out_mix, g_pre_ff2=g_pre_ff2, g_post_ff2=g_post_ff2, ff2_w_in=ff2_w_in,
             ff2_w_out=ff2_w_out)
    Mo = dict(w_ada=m_w_ada, b_ada=m_b_ada, g_pre_ff1=m_g_pre_ff1, g_post_ff1=m_g_post_ff1, ff1_w_in=m_ff1_w_in,
              ff1_w_out=m_ff1_w_out, g_pre_mix=m_g_pre_mix, g_post_mix=m_g_post_mix, w_in_mix=m_w_in_mix,
              g_attn_out=m_g_attn_out, conv_w=m_conv_w, conv_b=m_conv_b, conv_ln_g=m_conv_ln_g,
              conv_ln_b=m_conv_ln_b, w_out_mix=m_w_out_mix, g_pre_ff2=m_g_pre_ff2, g_post_ff2=m_g_post_ff2,
              ff2_w_in=m_ff2_w_in, ff2_w_out=m_ff2_w_out)
    Vo = dict(w_ada=v_w_ada, b_ada=v_b_ada, g_pre_ff1=v_g_pre_ff1, g_post_ff1=v_g_post_ff1, ff1_w_in=v_ff1_w_in,
              ff1_w_out=v_ff1_w_out, g_pre_mix=v_g_pre_mix, g_post_mix=v_g_post_mix, w_in_mix=v_w_in_mix,
              g_attn_out=v_g_attn_out, conv_w=v_conv_w, conv_b=v_conv_b, conv_ln_g=v_conv_ln_g,
              conv_ln_b=v_conv_ln_b, w_out_mix=v_w_out_mix, g_pre_ff2=v_g_pre_ff2, g_post_ff2=v_g_post_ff2,
              ff2_w_in=v_ff2_w_in, ff2_w_out=v_ff2_w_out)

    T, D = x.shape[1], x.shape[2]
    AW = D // 2
    C = D - AW
    xi, yi, ci = _place()
    me = 4 * xi + 2 * yi + ci
    chip = 2 * xi + yi
    place = jnp.stack([ci, chip]).astype(jnp.int32)

    c_all = _allgather8(jnp.tile(c, (8, 1)), name="gather_c")[:, 0, :]
    ncol = w_ada.shape[1]
    b_cols = lax.dynamic_index_in_dim(b_ada.reshape(4, ncol), chip, keepdims=True).reshape(1, ncol)
    modp = _ada_fwd(c_all, w_ada, b_cols, name="ada_fwd")
    mod_g = _allgather8(modp, name="gather_mod")
    mod_all = jnp.transpose(mod_g[0::2], (1, 0, 2)).reshape(8, 4 * ncol)
    mod = lax.dynamic_index_in_dim(mod_all, me, keepdims=False).reshape(9, D)

    names = [n for n, _ in BIG]
    plan = _DistPlan({n: W[n].astype(BF16) for n in names}, place)
    cs = conv_w.shape[1]
    cw_all = _allgather8(jnp.pad(conv_w, ((0, HALO - CONV_KERNEL), (0, (-cs) % LANES))), name="gather_conv_w")
    conv_w_full = jnp.transpose(cw_all[0::2, :, :cs], (1, 0, 2)).reshape(HALO, 4 * cs)

    gains = _pack_rows([g_pre_ff1, g_post_ff1, g_pre_mix, g_post_mix, g_pre_ff2, g_post_ff2], D)
    cvec = _pack_rows([conv_b, conv_ln_g, conv_ln_b], C)
    g_attn = g_attn_out.reshape(1, AW)

    sq, dx, dgains, dmod, dg_attn, csum, dconv_w = _local_step(
        x[0], loss_target[0], mod, gains, plan, g_attn, conv_w_full, cvec)

    loss_row = jnp.zeros((1, D), F32).at[0, 0].set(jnp.sum(sq) * (0.5 / D))
    small = _pack_rows(dgains + dmod + [dg_attn, csum[0:3], dconv_w, loss_row], D)
    small_all = _allgather8(small, name="gather_small")
    tot = _sum_devices(small_all, name="sum_small")
    n_g, n_m = 6, 9
    r0 = n_g + n_m
    flat = tot.reshape(-1)
    p = r0 * D
    g_attn_grad = flat[p:p + AW]
    p += AW
    gconv_b, gln_g, gln_b = flat[p:p + C], flat[p + C:p + 2 * C], flat[p + 2 * C:p + 3 * C]
    p += 3 * C
    gconv_w_full = flat[p:p + HALO * C].reshape(HALO, C)[:CONV_KERNEL]
    p += HALO * C
    loss = flat[p]
    gconv_w = lax.dynamic_slice_in_dim(gconv_w_full, chip * cs, cs, axis=1)
    grad_small = {'g_pre_ff1': tot[0], 'g_post_ff1': tot[1], 'g_pre_mix': tot[2], 'g_post_mix': tot[3],
                  'g_pre_ff2': tot[4], 'g_post_ff2': tot[5], 'b_ada': tot[n_g:r0].reshape(-1),
                  'g_attn_out': g_attn_grad.reshape(g_attn_out.shape), 'conv_w': gconv_w, 'conv_b': gconv_b,
                  'conv_ln_g': gln_g, 'conv_ln_b': gln_b}

    dmod_all = small_all[:, n_g:r0, :].reshape(8, 9 * D)
    dmod_cols = lax.dynamic_slice_in_dim(dmod_all, chip * ncol, ncol, axis=1)
    grad_w_ada = _ada_bwd(jnp.transpose(c_all), dmod_cols, name="ada_bwd")

    grads = dict(grad_small)
    grads['w_ada'] = grad_w_ada
    for n, a in plan.finish().items():
        grads[n] = a.reshape(W[n].shape)

    delta, new_m, new_v = {}, {}, {}
    for n in ['w_ada'] + names:
        delta[n], new_m[n], new_v[n] = _adamw(W[n], grads[n], Mo[n], Vo[n], name=f"adamw_{n}")
    smalls = [n for n in WEIGHTS if n not in delta]

    def as2d(a):
        return a if a.ndim == 2 else a.reshape(-1, LANES)

    outs = _adamw_many([[as2d(d[n]) for n in smalls] for d in (W, grads, Mo, Vo)], name="adamw_small")
    for k, n in enumerate(smalls):
        delta[n], new_m[n], new_v[n] = (o.reshape(W[n].shape) for o in outs[3 * k:3 * k + 3])

    return (loss, dx[None], *[grads[n] for n in WEIGHTS], *[delta[n] for n in WEIGHTS],
            *[new_m[n] for n in WEIGHTS], *[new_v[n] for n in WEIGHTS])
```
